```python
import jax
import jax.numpy as jnp
from jax import lax

D_MODEL = 1024
BATCH = 1
SEQ = 16384
DEPTH = 1

GRID_W = 64
CTX_LEN = 256
CHUNK = 128
SGU_GROUPS = 8
SGU_WIDTH = D_MODEL
SGU_GROUP_DIM = SGU_WIDTH // SGU_GROUPS
RET_HEADS = 4
RET_QK_DIM = D_MODEL // RET_HEADS
RET_V_DIM = 2 * RET_QK_DIM
RET_QK_WIDTH = RET_HEADS * RET_QK_DIM
RET_V_WIDTH = RET_HEADS * RET_V_DIM
ROPE_BASE = 10000.0
N_EXPERTS = 32
TOP_K = 4
D_FF = D_MODEL
SWIGLU_LIMIT = 7.0
SWIGLU_ALPHA = 1.702
EPS = 1e-6
V_OFF = SGU_WIDTH
Q_OFF = 2 * SGU_WIDTH
K_OFF = Q_OFF + RET_QK_WIDTH
VR_OFF = K_OFF + RET_QK_WIDTH
GF_OFF = VR_OFF + RET_V_WIDTH
GB_OFF = GF_OFF + RET_V_WIDTH
GA_OFF = GB_OFF + RET_V_WIDTH
GBR_OFF = GA_OFF + D_MODEL
IN_WIDTH = GBR_OFF + D_MODEL
IN_SPLIT_POINTS = (V_OFF, Q_OFF, K_OFF, VR_OFF, GF_OFF, GB_OFF, GA_OFF, GBR_OFF)

kernel_name = 'hybrid_sgu_retention_moe_dit'


def rms_norm(x, w):
    xf = x.astype(jnp.float32)
    y = xf * lax.rsqrt(jnp.mean(xf * xf, axis=-1, keepdims=True) + EPS)
    return (y * w.astype(jnp.float32)).astype(x.dtype)


def layer_norm(x, w, b):
    xf = x.astype(jnp.float32)
    mu = jnp.mean(xf, axis=-1, keepdims=True)
    var = jnp.mean(jnp.square(xf - mu), axis=-1, keepdims=True)
    y = (xf - mu) * lax.rsqrt(var + EPS) * w.astype(jnp.float32) + b.astype(jnp.float32)
    return y.astype(x.dtype)


def head_norm(o):
    of = o.astype(jnp.float32)
    return (of * lax.rsqrt(jnp.mean(of * of, axis=-1, keepdims=True) + EPS)).astype(o.dtype)


def modulate(h, shift, scale):
    return h * (1 + scale) + shift


def sgu(u, v, ln_w, ln_b, w_s, b_s):
    u = jax.nn.gelu(u, approximate=False)
    v = layer_norm(jax.nn.gelu(v, approximate=False), ln_w, ln_b)
    B, T, _ = v.shape
    vc = v.reshape(B, T // CHUNK, CHUNK, SGU_GROUPS, SGU_GROUP_DIM)
    mixed = jnp.einsum('gpq,bnqgc->bnpgc', w_s, vc) + b_s.T[:, :, None]
    return u * mixed.reshape(B, T, SGU_WIDTH)


def split_heads(t, d):
    B, T, _ = t.shape
    return t.reshape(B, T, RET_HEADS, d).transpose(0, 2, 1, 3)


def merge_heads(t):
    B, H, T, d = t.shape
    return t.transpose(0, 2, 1, 3).reshape(B, T, H * d)


def ret_heads(q, k, v):
    return (split_heads(q, RET_QK_DIM),
            split_heads(k, RET_QK_DIM) * (RET_QK_DIM ** -0.5),
            split_heads(v, RET_V_DIM))


def log_decay(a):
    return jnp.log1p(-jnp.exp2(a.astype(jnp.float32)))


def rope_tables(T):
    rows = T // GRID_W
    row = jnp.repeat(jnp.arange(rows, dtype=jnp.float32), GRID_W)
    col = jnp.tile(jnp.arange(GRID_W, dtype=jnp.float32), rows)
    n_freq = RET_QK_DIM // 4
    inv = ROPE_BASE ** (-jnp.arange(n_freq, dtype=jnp.float32) / n_freq)
    ang = jnp.concatenate([row[:, None] * inv, col[:, None] * inv], axis=-1)
    return jnp.cos(ang), jnp.sin(ang)


def apply_rope(x, cos, sin):
    half = x.shape[-1] // 2
    x1, x2 = x[..., :half], x[..., half:]
    cos = cos.astype(x.dtype)
    sin = sin.astype(x.dtype)
    return jnp.concatenate([x1 * cos - x2 * sin, x1 * sin + x2 * cos], axis=-1)


def context_state(k, v, lg, reverse):
    L = k.shape[2]
    j = jnp.arange(L, dtype=jnp.float32)
    power = j if reverse else (L - 1 - j)
    w = jnp.exp(power[None, :] * lg[:, None]).astype(k.dtype)
    return jnp.einsum('hl,bhld,bhlv->bhdv', w, k, v)


def retention_scan(q, k, v, lg, s0):
    B, H, T, dk = q.shape
    dv = v.shape[-1]
    n = T // CHUNK
    to_chunks = lambda t: jnp.moveaxis(t.reshape(B, H, n, CHUNK, t.shape[-1]), 2, 0)
    idx = jnp.arange(CHUNK, dtype=jnp.float32)
    diff = idx[:, None] - idx[None, :]
    mask = diff >= 0
    intra = jnp.where(mask[None], jnp.exp(jnp.where(mask, diff, 0.0)[None] * lg[:, None, None]), 0.0).astype(q.dtype)
    q_decay = jnp.exp((idx[None, :] + 1) * lg[:, None]).astype(q.dtype)[:, :, None]
    k_decay = jnp.exp((CHUNK - 1 - idx[None, :]) * lg[:, None]).astype(q.dtype)[:, :, None]
    chunk_decay = jnp.exp(CHUNK * lg).astype(q.dtype)[:, None, None]

    def step(s, inp):
        qc, kc, vc = inp
        scores = jnp.einsum('bhid,bhjd->bhij', qc, kc) * intra
        inner = jnp.einsum('bhij,bhjv->bhiv', scores, vc)
        cross = jnp.einsum('bhid,bhdv->bhiv', qc, s) * q_decay
        s = s * chunk_decay + jnp.einsum('bhjd,bhjv->bhdv', kc * k_decay, vc)
        return s, inner + cross

    _, out = lax.scan(step, s0.astype(q.dtype), (to_chunks(q), to_chunks(k), to_chunks(v)))
    return jnp.moveaxis(out, 0, 2).reshape(B, H, T, dv)


def mix_branches(u, v, q, k, vr, gf, gb, ga, gbr, s_f0, s_b0, lg_f, lg_b,
                 ln_w, ln_b, w_s, b_s, w_pa, w_pb, w_o):
    ya = sgu(u, v, ln_w, ln_b, w_s, b_s)
    o_f = retention_scan(q, k, vr, lg_f, s_f0)
    o_b = jnp.flip(retention_scan(jnp.flip(q, 2), jnp.flip(k, 2), jnp.flip(vr, 2), lg_b, s_b0), 2)
    yb = jax.nn.silu(gf) * merge_heads(head_norm(o_f)) + jax.nn.silu(gb) * merge_heads(head_norm(o_b))
    y = jax.nn.sigmoid(ga) * (ya @ w_pa) + jax.nn.sigmoid(gbr) * (yb @ w_pb)
    return y @ w_o


def moe(h, router_w, router_b, w1, b1, w2, b2):
    B, T, D = h.shape
    xt = h.reshape(B * T, D)
    logits = (xt @ router_w + router_b).astype(jnp.float32)
    top_val, top_idx = lax.top_k(logits, TOP_K)
    probs = jax.nn.softmax(top_val, axis=-1)
    combine = jnp.sum(jax.nn.one_hot(top_idx, N_EXPERTS, dtype=jnp.float32) * probs[..., None], axis=1).astype(h.dtype)
    out = jnp.zeros_like(xt)
    for e in range(N_EXPERTS):
        hh = xt @ w1[e] + b1[e]
        gate = jnp.minimum(hh[:, :D_FF], SWIGLU_LIMIT)
        up = jnp.clip(hh[:, D_FF:], -SWIGLU_LIMIT, SWIGLU_LIMIT)
        act = (up + 1) * gate * jax.nn.sigmoid(SWIGLU_ALPHA * gate)
        out = out + combine[:, e:e + 1] * (act @ w2[e] + b2[e])
    return out.reshape(B, T, D)


def setup_inputs(seed: int = 0) -> dict:
    key = jax.random.key(seed)
    ks = jax.random.split(key, 32)
    f32 = jnp.float32
    nrm = lambda k, shape, s: s * jax.random.normal(k, shape, f32)
    L = DEPTH
    return {
        'x': nrm(ks[0], (BATCH, SEQ, D_MODEL), 1.0),
        'c': nrm(ks[1], (BATCH, D_MODEL), 1.0),
        'ctx': nrm(ks[2], (BATCH, CTX_LEN, D_MODEL), 1.0),
        'c_ctx': nrm(ks[3], (D_MODEL,), 1.0),
        'w_mod': nrm(ks[4], (L, D_MODEL, 6 * D_MODEL), 0.5 * D_MODEL ** -0.5),
        'b_mod': nrm(ks[5], (L, 6 * D_MODEL), 0.02),
        'norm1_w': 1.0 + nrm(ks[6], (L, D_MODEL), 0.02),
        'norm2_w': 1.0 + nrm(ks[7], (L, D_MODEL), 0.02),
        'w_in': nrm(ks[8], (L, D_MODEL, IN_WIDTH), D_MODEL ** -0.5),
        'sgu_ln_w': 1.0 + nrm(ks[9], (L, SGU_WIDTH), 0.02),
        'sgu_ln_b': nrm(ks[10], (L, SGU_WIDTH), 0.02),
        'sgu_w': nrm(ks[11], (L, SGU_GROUPS, CHUNK, CHUNK), CHUNK ** -0.5),
        'sgu_b': 1.0 + nrm(ks[12], (L, SGU_GROUPS, CHUNK), 0.01),
        'ret_decay_fwd': -5.0 - jnp.arange(RET_HEADS, dtype=f32) + nrm(ks[13], (L, RET_HEADS), 0.1),
        'ret_decay_bwd': -5.0 - jnp.arange(RET_HEADS, dtype=f32) + nrm(ks[14], (L, RET_HEADS), 0.1),
        'w_proj_a': nrm(ks[15], (L, SGU_WIDTH, D_MODEL), SGU_WIDTH ** -0.5),
        'w_proj_b': nrm(ks[16], (L, RET_V_WIDTH, D_MODEL), RET_V_WIDTH ** -0.5),
        'w_out': nrm(ks[17], (L, D_MODEL, D_MODEL), D_MODEL ** -0.5),
        'router_w': nrm(ks[18], (L, D_MODEL, N_EXPERTS), D_MODEL ** -0.5),
        'router_b': nrm(ks[19], (L, N_EXPERTS), 0.01),
        'moe_w1': nrm(ks[20], (L, N_EXPERTS, D_MODEL, 2 * D_FF), D_MODEL ** -0.5),
        'moe_b1': nrm(ks[21], (L, N_EXPERTS, 2 * D_FF), 0.01),
        'moe_w2': nrm(ks[22], (L, N_EXPERTS, D_FF, D_MODEL), D_FF ** -0.5),
        'moe_b2': nrm(ks[23], (L, N_EXPERTS, D_MODEL), 0.01),
        'final_norm_w': 1.0 + nrm(ks[24], (D_MODEL,), 0.02),
    }


def reference(x, c, ctx, c_ctx, w_mod, b_mod, norm1_w, norm2_w, w_in, sgu_ln_w, sgu_ln_b,
              sgu_w, sgu_b, ret_decay_fwd, ret_decay_bwd, w_proj_a, w_proj_b, w_out,
              router_w, router_b, moe_w1, moe_b1, moe_w2, moe_b2, final_norm_w):
    T = x.shape[1]
    cos, sin = rope_tables(T)
    for l in range(DEPTH):
        last = l == DEPTH - 1
        lg_f = log_decay(ret_decay_fwd[l])
        lg_b = log_decay(ret_decay_bwd[l])
        mod = (jax.nn.silu(c) @ w_mod[l] + b_mod[l])[:, None, :]
        sh1, sc1, g1, sh2, sc2, g2 = jnp.split(mod, 6, axis=-1)
        modc = jax.nn.silu(c_ctx) @ w_mod[l] + b_mod[l]
        sh1c, sc1c, g1c, sh2c, sc2c, g2c = jnp.split(modc, 6, axis=-1)

        hc = modulate(rms_norm(ctx, norm1_w[l]), sh1c, sc1c)
        qc, kc, vc = jnp.split(hc @ w_in[l][:, Q_OFF:GF_OFF], (K_OFF - Q_OFF, VR_OFF - Q_OFF), axis=-1)
        qch, kch, vch = ret_heads(qc, kc, vc)
        s_f0 = context_state(kch, vch, lg_f, False)
        s_b0 = context_state(kch, vch, lg_b, True)

        h = modulate(rms_norm(x, norm1_w[l]), sh1, sc1)
        u, v, q, k, vr, gf, gb, ga, gbr = jnp.split(h @ w_in[l], IN_SPLIT_POINTS, axis=-1)
        qh, kh, vh = ret_heads(q, k, vr)
        qh = apply_rope(qh, cos, sin)
        kh = apply_rope(kh, cos, sin)
        y = mix_branches(u, v, qh, kh, vh, gf, gb, ga, gbr, s_f0, s_b0, lg_f, lg_b,
                         sgu_ln_w[l], sgu_ln_b[l], sgu_w[l], sgu_b[l], w_proj_a[l], w_proj_b[l], w_out[l])

        if not last:
            uc, vac = jnp.split(hc @ w_in[l][:, :Q_OFF], 2, axis=-1)
            gfc, gbc, gac, gbrc = jnp.split(hc @ w_in[l][:, GF_OFF:],
                                            (RET_V_WIDTH, 2 * RET_V_WIDTH, 2 * RET_V_WIDTH + D_MODEL), axis=-1)
            zero = jnp.zeros_like(s_f0)
            yc = mix_branches(uc, vac, qch, kch, vch, gfc, gbc, gac, gbrc, zero, zero, lg_f, lg_b,
                              sgu_ln_w[l], sgu_ln_b[l], sgu_w[l], sgu_b[l], w_proj_a[l], w_proj_b[l], w_out[l])
            ctx_new = ctx + g1c * yc
            hc2 = modulate(rms_norm(ctx_new, norm2_w[l]), sh2c, sc2c)
            ctx_new = ctx_new + g2c * moe(hc2, router_w[l], router_b[l], moe_w1[l], moe_b1[l], moe_w2[l], moe_b2[l])

        x = x + g1 * y
        h2 = modulate(rms_norm(x, norm2_w[l]), sh2, sc2)
        x = x + g2 * moe(h2, router_w[l], router_b[l], moe_w1[l], moe_b1[l], moe_w2[l], moe_b2[l])

        if not last:
            ctx = ctx_new
    return rms_norm(x, final_norm_w)
```

```python
import functools

import jax
import jax.numpy as jnp
from jax import lax
from jax.experimental import pallas as pl
from jax.experimental.pallas import tpu as pltpu

D_MODEL = 1024
GRID_W = 64
SGU_CHUNK = 128
SGU_GROUPS = 8
HEADS = 4
DK = D_MODEL // HEADS
DV = 2 * DK
ROPE_BASE = 10000.0
N_EXPERTS = 32
TOP_K = 4
D_FF = D_MODEL
SWIGLU_LIMIT = 7.0
SWIGLU_ALPHA = 1.702
EPS = 1e-6
IN_WIDTH = 12 * D_MODEL
COL_U, COL_V, COL_Q, COL_K, COL_VR, COL_GF, COL_GB, COL_GA, COL_GBR = 0, 1, 2, 3, 4, 6, 8, 10, 11

RET_CHUNK = 256
VMEM_LIMIT = 56 * 1024 * 1024

F32 = jnp.float32
BF16 = jnp.bfloat16


def _params(sem):
    return pltpu.CompilerParams(dimension_semantics=sem, vmem_limit_bytes=VMEM_LIMIT)


def _dot(a, b):
    return jnp.dot(a, b, preferred_element_type=F32)


def _dot_nt(a, b):
    return lax.dot_general(a, b, (((1,), (1,)), ((), ())), preferred_element_type=F32)


def _dot_tn(a, b):
    return lax.dot_general(a, b, (((0,), (0,)), ((), ())), preferred_element_type=F32)


def _rms(x, w):
    return x * lax.rsqrt(jnp.mean(x * x, axis=-1, keepdims=True) + EPS) * w


def _gelu(x):
    return 0.5 * x * (1.0 + lax.erf(x * (2.0 ** -0.5)))


def _mod_slice(mod_ref, row, k):
    return mod_ref[row:row + 1, k * D_MODEL:(k + 1) * D_MODEL]


def _mod_kernel(cc_ref, w_ref, b_ref, o_ref):
    s = cc_ref[...]
    s = s * jax.nn.sigmoid(s)
    w = w_ref[...]
    r0 = jnp.sum(s[:, 0:1] * w, axis=0, keepdims=True)
    r1 = jnp.sum(s[:, 1:2] * w, axis=0, keepdims=True)
    o_ref[...] = jnp.concatenate([r0, r1], axis=0) + b_ref[...]


def _mod_call(cc, w_mod, b_mod):
    bn = 1536
    n = w_mod.shape[1]
    return pl.pallas_call(
        _mod_kernel,
        grid=(n // bn,),
        in_specs=[
            pl.BlockSpec((D_MODEL, 2), lambda j: (0, 0)),
            pl.BlockSpec((D_MODEL, bn), lambda j: (0, j)),
            pl.BlockSpec((1, bn), lambda j: (0, j)),
        ],
        out_specs=pl.BlockSpec((2, bn), lambda j: (0, j)),
        out_shape=jax.ShapeDtypeStruct((2, n), F32),
        compiler_params=_params(("arbitrary",)),
        name="mod_vectors",
    )(cc, w_mod, b_mod)


def _ctx_kernel(ctx_ref, nw_ref, mod_ref, w_ref, dec_ref, s0_ref):
    L = ctx_ref.shape[0]
    sh = _mod_slice(mod_ref, 1, 0)
    sc = _mod_slice(mod_ref, 1, 1)
    hc = (_rms(ctx_ref[...], nw_ref[...]) * (1.0 + sc) + sh).astype(BF16)
    kv = _dot(hc, w_ref[...])
    lg = jnp.log1p(-jnp.exp2(dec_ref[...]))
    pos = lax.broadcasted_iota(jnp.int32, (L, 1), 0).astype(F32)
    for h in range(HEADS):
        k = kv[:, h * DK:(h + 1) * DK] * (DK ** -0.5)
        v = kv[:, HEADS * DK + h * DV:HEADS * DK + (h + 1) * DV].astype(BF16)
        wf = jnp.exp((L - 1.0 - pos) * lg[0:1, h:h + 1])
        wb = jnp.exp(pos * lg[1:2, h:h + 1])
        s0_ref[0, h] = _dot_tn((k * wf).astype(BF16), v)
        s0_ref[1, h] = _dot_tn((k * wb).astype(BF16), v)


def _ctx_call(ctx2d, nw, mod, w_in_bf, dec):
    L = ctx2d.shape[0]
    wcols = HEADS * DK + HEADS * DV
    return pl.pallas_call(
        _ctx_kernel,
        grid=(1,),
        in_specs=[
            pl.BlockSpec((L, D_MODEL), lambda i: (0, 0)),
            pl.BlockSpec((1, D_MODEL), lambda i: (0, 0)),
            pl.BlockSpec(mod.shape, lambda i: (0, 0)),
            pl.BlockSpec((D_MODEL, wcols), lambda i: (0, COL_K * D_MODEL // wcols)),
            pl.BlockSpec(dec.shape, lambda i: (0, 0)),
        ],
        out_specs=pl.BlockSpec((2, HEADS, DK, DV), lambda i: (0, 0, 0, 0)),
        out_shape=jax.ShapeDtypeStruct((2, HEADS, DK, DV), F32),
        compiler_params=_params(("arbitrary",)),
        name="ctx_states",
    )(ctx2d, nw, mod, w_in_bf, dec)


def _inproj_kernel(x_ref, nw_ref, mod_ref, w_ref, cos_ref, sin_ref, lnw_ref, lnb_ref, o_ref, h_scr):
    j = pl.program_id(1)

    @pl.when(j == 0)
    def _():
        sh = _mod_slice(mod_ref, 0, 0)
        sc = _mod_slice(mod_ref, 0, 1)
        h_scr[...] = (_rms(x_ref[...], nw_ref[...]) * (1.0 + sc) + sh).astype(BF16)

    acc = _dot(h_scr[...], w_ref[...])

    @pl.when(j == COL_U)
    def _():
        o_ref[...] = _gelu(acc).astype(BF16)

    @pl.when(j == COL_V)
    def _():
        g = _gelu(acc)
        mu = jnp.mean(g, axis=-1, keepdims=True)
        d = g - mu
        var = jnp.mean(d * d, axis=-1, keepdims=True)
        o_ref[...] = (d * lax.rsqrt(var + EPS) * lnw_ref[...] + lnb_ref[...]).astype(BF16)

    def rope(scale):
        cos = cos_ref[...]
        sin = sin_ref[...]
        half = DK // 2
        for h in range(HEADS):
            x1 = acc[:, h * DK:h * DK + half]
            x2 = acc[:, h * DK + half:(h + 1) * DK]
            o_ref[:, h * DK:h * DK + half] = ((x1 * cos - x2 * sin) * scale).astype(BF16)
            o_ref[:, h * DK + half:(h + 1) * DK] = ((x1 * sin + x2 * cos) * scale).astype(BF16)

    @pl.when(j == COL_Q)
    def _():
        rope(1.0)

    @pl.when(j == COL_K)
    def _():
        rope(DK ** -0.5)

    @pl.when((j >= COL_VR) & (j < COL_GF))
    def _():
        o_ref[...] = acc.astype(BF16)

    @pl.when((j >= COL_GF) & (j < COL_GA))
    def _():
        o_ref[...] = (acc * jax.nn.sigmoid(acc)).astype(BF16)

    @pl.when(j >= COL_GA)
    def _():
        o_ref[...] = jax.nn.sigmoid(acc).astype(BF16)


def _inproj_call(x2d, nw, mod, w_in_bf, cos, sin, lnw, lnb):
    T = x2d.shape[0]
    tm = 1024
    half = DK // 2
    return pl.pallas_call(
        _inproj_kernel,
        grid=(T // tm, IN_WIDTH // D_MODEL),
        in_specs=[
            pl.BlockSpec((tm, D_MODEL), lambda i, j: (i, 0)),
            pl.BlockSpec((1, D_MODEL), lambda i, j: (0, 0)),
            pl.BlockSpec(mod.shape, lambda i, j: (0, 0)),
            pl.BlockSpec((D_MODEL, D_MODEL), lambda i, j: (0, j)),
            pl.BlockSpec((tm, half), lambda i, j: (i, 0)),
            pl.BlockSpec((tm, half), lambda i, j: (i, 0)),
            pl.BlockSpec((1, D_MODEL), lambda i, j: (0, 0)),
            pl.BlockSpec((1, D_MODEL), lambda i, j: (0, 0)),
        ],
        out_specs=pl.BlockSpec((tm, D_MODEL), lambda i, j: (i, j)),
        out_shape=jax.ShapeDtypeStruct((T, IN_WIDTH), BF16),
        scratch_shapes=[pltpu.VMEM((tm, D_MODEL), BF16)],
        compiler_params=_params(("arbitrary", "arbitrary")),
        name="in_proj",
    )(x2d, nw, mod, w_in_bf, cos, sin, lnw, lnb)


def _sgu_kernel(u_ref, v_ref, ga_ref, ws_ref, bt_ref, wpa_ref, o_ref, ya_scr):
    tm = u_ref.shape[0]
    gd = D_MODEL // SGU_GROUPS
    for n in range(tm // SGU_CHUNK):
        rows = slice(n * SGU_CHUNK, (n + 1) * SGU_CHUNK)
        for g in range(SGU_GROUPS):
            cols = slice(g * gd, (g + 1) * gd)
            mixed = _dot(ws_ref[g], v_ref[rows, cols]) + bt_ref[:, g:g + 1]
            ya_scr[rows, cols] = (u_ref[rows, cols].astype(F32) * mixed).astype(BF16)
    pa = _dot(ya_scr[...], wpa_ref[...])
    o_ref[...] = (ga_ref[...].astype(F32) * pa).astype(BF16)


def _sgu_call(proj, ws_bf, bt, wpa_bf):
    T = proj.shape[0]
    tm = 512
    return pl.pallas_call(
        _sgu_kernel,
        grid=(T // tm,),
        in_specs=[
            pl.BlockSpec((tm, D_MODEL), lambda i: (i, COL_U)),
            pl.BlockSpec((tm, D_MODEL), lambda i: (i, COL_V)),
            pl.BlockSpec((tm, D_MODEL), lambda i: (i, COL_GA)),
            pl.BlockSpec(ws_bf.shape, lambda i: (0, 0, 0)),
            pl.BlockSpec(bt.shape, lambda i: (0, 0)),
            pl.BlockSpec((D_MODEL, D_MODEL), lambda i: (0, 0)),
        ],
        out_specs=pl.BlockSpec((tm, D_MODEL), lambda i: (i, 0)),
        out_shape=jax.ShapeDtypeStruct((T, D_MODEL), BF16),
        scratch_shapes=[pltpu.VMEM((tm, D_MODEL), BF16)],
        compiler_params=_params(("arbitrary",)),
        name="sgu_proj_a",
    )(proj, proj, proj, ws_bf, bt, wpa_bf)


def _ret_kernel(qf_ref, kf_ref, vf_ref, gf_ref, qb_ref, kb_ref, vb_ref, gb_ref, s0_ref, dec_ref,
                yf_ref, yb_ref, s_scr, intra_scr, qd_scr, kd_scr, cd_scr):
    C = RET_CHUNK
    i = pl.program_id(0)

    @pl.when(i == 0)
    def _():
        lg = jnp.log1p(-jnp.exp2(dec_ref[...]))
        r = lax.broadcasted_iota(jnp.int32, (C, C), 0).astype(F32)
        c = lax.broadcasted_iota(jnp.int32, (C, C), 1).astype(F32)
        pos = lax.broadcasted_iota(jnp.int32, (C, 1), 0).astype(F32)
        for h in range(HEADS):
            lf = lg[0:1, h:h + 1]
            lb = lg[1:2, h:h + 1]
            intra_scr[0, h] = jnp.where(r >= c, jnp.exp(jnp.maximum(r - c, 0.0) * lf), 0.0)
            intra_scr[1, h] = jnp.where(c >= r, jnp.exp(jnp.maximum(c - r, 0.0) * lb), 0.0)
            qd_scr[0, h] = jnp.exp((pos + 1.0) * lf)
            qd_scr[1, h] = jnp.exp((C - pos) * lb)
            kd_scr[0, h] = jnp.exp((C - 1.0 - pos) * lf)
            kd_scr[1, h] = jnp.exp(pos * lb)
            cd_scr[0, h] = jnp.exp(C * lf)
            cd_scr[1, h] = jnp.exp(C * lb)
        s_scr[...] = s0_ref[...]

    dirs = ((qf_ref, kf_ref, vf_ref, gf_ref, yf_ref), (qb_ref, kb_ref, vb_ref, gb_ref, yb_ref))
    for d, (q_ref, k_ref, v_ref, g_ref, y_ref) in enumerate(dirs):
        for h in range(HEADS):
            q = q_ref[:, h * DK:(h + 1) * DK]
            k = k_ref[:, h * DK:(h + 1) * DK]
            v = v_ref[:, h * DV:(h + 1) * DV]
            s = s_scr[d, h]
            a = (_dot_nt(q, k) * intra_scr[d, h]).astype(BF16)
            o = _dot(a, v) + _dot(q, s.astype(BF16)) * qd_scr[d, h]
            hn = o * lax.rsqrt(jnp.mean(o * o, axis=-1, keepdims=True) + EPS)
            y_ref[:, h * DV:(h + 1) * DV] = (g_ref[:, h * DV:(h + 1) * DV].astype(F32) * hn).astype(BF16)
            kd = (k.astype(F32) * kd_scr[d, h]).astype(BF16)
            s_scr[d, h] = s * cd_scr[d, h] + _dot_tn(kd, v)


def _ret_call(proj, s0, dec):
    T = proj.shape[0]
    C = RET_CHUNK
    n = T // C
    qk_w = HEADS * DK
    v_w = HEADS * DV
    fwd = lambda col: (lambda i: (i, col))
    bwd = lambda col: (lambda i: (n - 1 - i, col))
    return pl.pallas_call(
        _ret_kernel,
        grid=(n,),
        in_specs=[
            pl.BlockSpec((C, qk_w), fwd(COL_Q)),
            pl.BlockSpec((C, qk_w), fwd(COL_K)),
            pl.BlockSpec((C, v_w), fwd(COL_VR // 2)),
            pl.BlockSpec((C, v_w), fwd(COL_GF // 2)),
            pl.BlockSpec((C, qk_w), bwd(COL_Q)),
            pl.BlockSpec((C, qk_w), bwd(COL_K)),
            pl.BlockSpec((C, v_w), bwd(COL_VR // 2)),
            pl.BlockSpec((C, v_w), bwd(COL_GB // 2)),
            pl.BlockSpec(s0.shape, lambda i: (0, 0, 0, 0)),
            pl.BlockSpec(dec.shape, lambda i: (0, 0)),
        ],
        out_specs=[
            pl.BlockSpec((C, v_w), lambda i: (i, 0)),
            pl.BlockSpec((C, v_w), lambda i: (n - 1 - i, 0)),
        ],
        out_shape=[jax.ShapeDtypeStruct((T, v_w), BF16), jax.ShapeDtypeStruct((T, v_w), BF16)],
        scratch_shapes=[
            pltpu.VMEM((2, HEADS, DK, DV), F32),
            pltpu.VMEM((2, HEADS, C, C), F32),
            pltpu.VMEM((2, HEADS, C, 1), F32),
            pltpu.VMEM((2, HEADS, C, 1), F32),
            pltpu.VMEM((2, HEADS, 1, 1), F32),
        ],
        compiler_params=_params(("arbitrary",)),
        name="retention",
    )(proj, proj, proj, proj, proj, proj, proj, proj, s0, dec)


def _merge_kernel(yf_ref, yb_ref, za_ref, gbr_ref, x_ref, mod_ref, nw_ref, wpb_ref, wo_ref, rw_ref, rb_ref,
                  x1_ref, h2_ref, lg_ref):
    yb = yf_ref[...] + yb_ref[...]
    pb = _dot(yb, wpb_ref[...])
    y = za_ref[...].astype(F32) + gbr_ref[...].astype(F32) * pb
    yo = _dot(y.astype(BF16), wo_ref[...])
    x1 = x_ref[...] + _mod_slice(mod_ref, 0, 2) * yo
    x1_ref[...] = x1
    h2 = _rms(x1, nw_ref[...]) * (1.0 + _mod_slice(mod_ref, 0, 4)) + _mod_slice(mod_ref, 0, 3)
    h2_ref[...] = h2.astype(BF16)
    lg_ref[...] = jnp.dot(h2, rw_ref[...], precision=lax.Precision.HIGHEST,
                          preferred_element_type=F32) + rb_ref[...]


def _merge_call(yf, yb, za, proj, x2d, mod, nw2, wpb_bf, wo_bf, rw, rb):
    T = x2d.shape[0]
    tm = 512
    row = lambda i: (i, 0)
    fixed = lambda i: (0, 0)
    return pl.pallas_call(
        _merge_kernel,
        grid=(T // tm,),
        in_specs=[
            pl.BlockSpec((tm, HEADS * DV), row),
            pl.BlockSpec((tm, HEADS * DV), row),
            pl.BlockSpec((tm, D_MODEL), row),
            pl.BlockSpec((tm, D_MODEL), lambda i: (i, COL_GBR)),
            pl.BlockSpec((tm, D_MODEL), row),
            pl.BlockSpec(mod.shape, fixed),
            pl.BlockSpec((1, D_MODEL), fixed),
            pl.BlockSpec(wpb_bf.shape, fixed),
            pl.BlockSpec(wo_bf.shape, fixed),
            pl.BlockSpec(rw.shape, fixed),
            pl.BlockSpec(rb.shape, fixed),
        ],
        out_specs=[
            pl.BlockSpec((tm, D_MODEL), row),
            pl.BlockSpec((tm, D_MODEL), row),
            pl.BlockSpec((tm, N_EXPERTS), row),
        ],
        out_shape=[
            jax.ShapeDtypeStruct((T, D_MODEL), F32),
            jax.ShapeDtypeStruct((T, D_MODEL), BF16),
            jax.ShapeDtypeStruct((T, N_EXPERTS), F32),
        ],
        compiler_params=_params(("arbitrary",)),
        name="merge_router",
    )(yf, yb, za, proj, x2d, mod, nw2, wpb_bf, wo_bf, rw, rb)


def _combine_weights(logits):
    lane = lax.broadcasted_iota(jnp.int32, logits.shape, 1)
    work = logits
    vals, hots = [], []
    for _ in range(TOP_K):
        m = jnp.max(work, axis=1, keepdims=True)
        first = jnp.min(jnp.where(work == m, lane, N_EXPERTS), axis=1, keepdims=True)
        hot = lane == first
        vals.append(m)
        hots.append(hot)
        work = jnp.where(hot, -jnp.inf, work)
    exps = [jnp.exp(v - vals[0]) for v in vals]
    inv = 1.0 / functools.reduce(lambda a, b: a + b, exps)
    comb = jnp.zeros(logits.shape, F32)
    for e, hot in zip(exps, hots):
        comb = comb + jnp.where(hot, e * inv, 0.0)
    return comb


def _moe_kernel(lg_ref, h2_ref, x1_ref, mod_ref, fw_ref, w1_ref, b1_ref, w2_ref, b2_ref, o_ref, acc_scr, comb_scr):
    e = pl.program_id(1)

    @pl.when(e == 0)
    def _():
        comb_scr[...] = _combine_weights(lg_ref[...])
        acc_scr[...] = jnp.zeros_like(acc_scr)

    hh = _dot(h2_ref[...], w1_ref[...]) + b1_ref[...]
    gate = jnp.minimum(hh[:, :D_FF], SWIGLU_LIMIT)
    up = jnp.clip(hh[:, D_FF:], -SWIGLU_LIMIT, SWIGLU_LIMIT)
    act = (up + 1.0) * gate * jax.nn.sigmoid(SWIGLU_ALPHA * gate)
    y = _dot(act.astype(BF16), w2_ref[...]) + b2_ref[...]
    lane = lax.broadcasted_iota(jnp.int32, comb_scr.shape, 1)
    ce = jnp.sum(jnp.where(lane == e, comb_scr[...], 0.0), axis=1, keepdims=True)
    acc_scr[...] += ce * y

    @pl.when(e == N_EXPERTS - 1)
    def _():
        x2 = x1_ref[...] + _mod_slice(mod_ref, 0, 5) * acc_scr[...]
        o_ref[...] = _rms(x2, fw_ref[...])


def _moe_call(logits, h2, x1, mod, fw, w1_bf, b1, w2_bf, b2):
    T = x1.shape[0]
    tm = 512
    row = lambda i, e: (i, 0)
    fixed = lambda i, e: (0, 0)
    per_e = lambda i, e: (e, 0, 0)
    return pl.pallas_call(
        _moe_kernel,
        grid=(T // tm, N_EXPERTS),
        in_specs=[
            pl.BlockSpec((tm, N_EXPERTS), row),
            pl.BlockSpec((tm, D_MODEL), row),
            pl.BlockSpec((tm, D_MODEL), row),
            pl.BlockSpec(mod.shape, fixed),
            pl.BlockSpec((1, D_MODEL), fixed),
            pl.BlockSpec((None, D_MODEL, 2 * D_FF), per_e),
            pl.BlockSpec((None, 1, 2 * D_FF), per_e),
            pl.BlockSpec((None, D_FF, D_MODEL), per_e),
            pl.BlockSpec((None, 1, D_MODEL), per_e),
        ],
        out_specs=pl.BlockSpec((tm, D_MODEL), row),
        out_shape=jax.ShapeDtypeStruct((T, D_MODEL), F32),
        scratch_shapes=[pltpu.VMEM((tm, D_MODEL), F32), pltpu.VMEM((tm, N_EXPERTS), F32)],
        compiler_params=_params(("arbitrary", "arbitrary")),
        name="moe_dense",
    )(logits, h2, x1, mod, fw, w1_bf, b1, w2_bf, b2)


def _rope_tables(T):
    rows = T // GRID_W
    row = jnp.repeat(jnp.arange(rows, dtype=F32), GRID_W)
    col = jnp.tile(jnp.arange(GRID_W, dtype=F32), rows)
    n_freq = DK // 4
    inv = ROPE_BASE ** (-jnp.arange(n_freq, dtype=F32) / n_freq)
    ang = jnp.concatenate([row[:, None] * inv, col[:, None] * inv], axis=-1)
    return jnp.cos(ang), jnp.sin(ang)


def kernel(x, c, ctx, c_ctx, w_mod, b_mod, norm1_w, norm2_w, w_in, sgu_ln_w, sgu_ln_b, sgu_w, sgu_b,
           ret_decay_fwd, ret_decay_bwd, w_proj_a, w_proj_b, w_out, router_w, router_b,
           moe_w1, moe_b1, moe_w2, moe_b2, final_norm_w):
    B, T, D = x.shape
    assert B == 1 and D == D_MODEL and w_mod.shape[0] == 1 and T % 1024 == 0
    x2d = x.reshape(T, D)
    cc = jnp.stack([c.reshape(D), c_ctx.reshape(D)], axis=1)
    mod = _mod_call(cc, w_mod[0], b_mod)
    dec = jnp.stack([ret_decay_fwd[0], ret_decay_bwd[0]], axis=0)
    w_in_bf = w_in[0].astype(BF16)
    s0 = _ctx_call(ctx.reshape(ctx.shape[1], D), norm1_w, mod, w_in_bf, dec)
    cos, sin = _rope_tables(T)
    proj = _inproj_call(x2d, norm1_w, mod, w_in_bf, cos, sin, sgu_ln_w, sgu_ln_b)
    za = _sgu_call(proj, sgu_w[0].astype(BF16), sgu_b[0].T, w_proj_a[0].astype(BF16))
    yf, yb = _ret_call(proj, s0, dec)
    x1, h2, logits = _merge_call(yf, yb, za, proj, x2d, mod, norm2_w, w_proj_b[0].astype(BF16),
                                 w_out[0].astype(BF16), router_w[0], router_b)
    out = _moe_call(logits, h2, x1, mod, final_norm_w.reshape(1, D), moe_w1[0].astype(BF16), moe_b1[0][:, None, :],
                    moe_w2[0].astype(BF16), moe_b2[0][:, None, :])
    return out.reshape(B, T, D)
```

```python
import functools

import jax
import jax.numpy as jnp
from jax import lax
from jax.experimental import pallas as pl
from jax.experimental.pallas import tpu as pltpu

D_MODEL = 1024
GRID_W = 64
SGU_CHUNK = 128
SGU_GROUPS = 8
HEADS = 4
DK = D_MODEL // HEADS
DV = 2 * DK
ROPE_BASE = 10000.0
N_EXPERTS = 32
TOP_K = 4
D_FF = D_MODEL
SWIGLU_LIMIT = 7.0
SWIGLU_ALPHA = 1.702
EPS = 1e-6
IN_WIDTH = 12 * D_MODEL
COL_U, COL_V, COL_Q, COL_K, COL_VR, COL_GF, COL_GB, COL_GA, COL_GBR = 0, 1, 2, 3, 4, 6, 8, 10, 11

RET_CHUNK = 256
VMEM_LIMIT = 56 * 1024 * 1024

MOE_TT = 512
MOE_CH = 16
MOE_TM = 512
MOE_CPT = MOE_TM // MOE_CH
MOE_MAXC = TOP_K * MOE_TT // MOE_CH + N_EXPERTS
MOE_RLOC = MOE_MAXC * MOE_CH
MOE_RB = 256

F32 = jnp.float32
BF16 = jnp.bfloat16


def _params(sem):
    return pltpu.CompilerParams(dimension_semantics=sem, vmem_limit_bytes=VMEM_LIMIT)


def _dot(a, b):
    return jnp.dot(a, b, preferred_element_type=F32)


def _dot_nt(a, b):
    return lax.dot_general(a, b, (((1,), (1,)), ((), ())), preferred_element_type=F32)


def _dot_tn(a, b):
    return lax.dot_general(a, b, (((0,), (0,)), ((), ())), preferred_element_type=F32)


def _rms(x, w):
    return x * lax.rsqrt(jnp.mean(x * x, axis=-1, keepdims=True) + EPS) * w


def _gelu(x):
    return 0.5 * x * (1.0 + lax.erf(x * (2.0 ** -0.5)))


def _mod_slice(mod_ref, row, k):
    return mod_ref[row:row + 1, k * D_MODEL:(k + 1) * D_MODEL]


def _mod_kernel(cc_ref, w_ref, b_ref, o_ref):
    s = cc_ref[...]
    s = s * jax.nn.sigmoid(s)
    w = w_ref[...]
    r0 = jnp.sum(s[:, 0:1] * w, axis=0, keepdims=True)
    r1 = jnp.sum(s[:, 1:2] * w, axis=0, keepdims=True)
    o_ref[...] = jnp.concatenate([r0, r1], axis=0) + b_ref[...]


def _mod_call(cc, w_mod, b_mod):
    bn = 1536
    n = w_mod.shape[1]
    return pl.pallas_call(
        _mod_kernel,
        grid=(n // bn,),
        in_specs=[
            pl.BlockSpec((D_MODEL, 2), lambda j: (0, 0)),
            pl.BlockSpec((D_MODEL, bn), lambda j: (0, j)),
            pl.BlockSpec((1, bn), lambda j: (0, j)),
        ],
        out_specs=pl.BlockSpec((2, bn), lambda j: (0, j)),
        out_shape=jax.ShapeDtypeStruct((2, n), F32),
        compiler_params=_params(("arbitrary",)),
        name="mod_vectors",
    )(cc, w_mod, b_mod)


def _ctx_kernel(ctx_ref, nw_ref, mod_ref, w_ref, dec_ref, s0_ref):
    L = ctx_ref.shape[0]
    sh = _mod_slice(mod_ref, 1, 0)
    sc = _mod_slice(mod_ref, 1, 1)
    hc = (_rms(ctx_ref[...], nw_ref[...]) * (1.0 + sc) + sh).astype(BF16)
    kv = _dot(hc, w_ref[...])
    lg = jnp.log1p(-jnp.exp2(dec_ref[...]))
    pos = lax.broadcasted_iota(jnp.int32, (L, 1), 0).astype(F32)
    for h in range(HEADS):
        k = kv[:, h * DK:(h + 1) * DK] * (DK ** -0.5)
        v = kv[:, HEADS * DK + h * DV:HEADS * DK + (h + 1) * DV].astype(BF16)
        wf = jnp.exp((L - 1.0 - pos) * lg[0:1, h:h + 1])
        wb = jnp.exp(pos * lg[1:2, h:h + 1])
        s0_ref[0, h] = _dot_tn((k * wf).astype(BF16), v)
        s0_ref[1, h] = _dot_tn((k * wb).astype(BF16), v)


def _ctx_call(ctx2d, nw, mod, w_in_bf, dec):
    L = ctx2d.shape[0]
    wcols = HEADS * DK + HEADS * DV
    return pl.pallas_call(
        _ctx_kernel,
        grid=(1,),
        in_specs=[
            pl.BlockSpec((L, D_MODEL), lambda i: (0, 0)),
            pl.BlockSpec((1, D_MODEL), lambda i: (0, 0)),
            pl.BlockSpec(mod.shape, lambda i: (0, 0)),
            pl.BlockSpec((D_MODEL, wcols), lambda i: (0, COL_K * D_MODEL // wcols)),
            pl.BlockSpec(dec.shape, lambda i: (0, 0)),
        ],
        out_specs=pl.BlockSpec((2, HEADS, DK, DV), lambda i: (0, 0, 0, 0)),
        out_shape=jax.ShapeDtypeStruct((2, HEADS, DK, DV), F32),
        compiler_params=_params(("arbitrary",)),
        name="ctx_states",
    )(ctx2d, nw, mod, w_in_bf, dec)


def _inproj_kernel(x_ref, nw_ref, mod_ref, w_ref, cos_ref, sin_ref, lnw_ref, lnb_ref, o_ref, h_scr):
    j = pl.program_id(1)

    @pl.when(j == 0)
    def _():
        sh = _mod_slice(mod_ref, 0, 0)
        sc = _mod_slice(mod_ref, 0, 1)
        h_scr[...] = (_rms(x_ref[...], nw_ref[...]) * (1.0 + sc) + sh).astype(BF16)

    acc = _dot(h_scr[...], w_ref[...])

    @pl.when(j == COL_U)
    def _():
        o_ref[...] = _gelu(acc).astype(BF16)

    @pl.when(j == COL_V)
    def _():
        g = _gelu(acc)
        mu = jnp.mean(g, axis=-1, keepdims=True)
        d = g - mu
        var = jnp.mean(d * d, axis=-1, keepdims=True)
        o_ref[...] = (d * lax.rsqrt(var + EPS) * lnw_ref[...] + lnb_ref[...]).astype(BF16)

    def rope(scale):
        cos = cos_ref[...]
        sin = sin_ref[...]
        half = DK // 2
        for h in range(HEADS):
            x1 = acc[:, h * DK:h * DK + half]
            x2 = acc[:, h * DK + half:(h + 1) * DK]
            o_ref[:, h * DK:h * DK + half] = ((x1 * cos - x2 * sin) * scale).astype(BF16)
            o_ref[:, h * DK + half:(h + 1) * DK] = ((x1 * sin + x2 * cos) * scale).astype(BF16)

    @pl.when(j == COL_Q)
    def _():
        rope(1.0)

    @pl.when(j == COL_K)
    def _():
        rope(DK ** -0.5)

    @pl.when((j >= COL_VR) & (j < COL_GF))
    def _():
        o_ref[...] = acc.astype(BF16)

    @pl.when((j >= COL_GF) & (j < COL_GA))
    def _():
        o_ref[...] = (acc * jax.nn.sigmoid(acc)).astype(BF16)

    @pl.when(j >= COL_GA)
    def _():
        o_ref[...] = jax.nn.sigmoid(acc).astype(BF16)


def _inproj_call(x2d, nw, mod, w_in_bf, cos, sin, lnw, lnb):
    T = x2d.shape[0]
    tm = 1024
    half = DK // 2
    return pl.pallas_call(
        _inproj_kernel,
        grid=(T // tm, IN_WIDTH // D_MODEL),
        in_specs=[
            pl.BlockSpec((tm, D_MODEL), lambda i, j: (i, 0)),
            pl.BlockSpec((1, D_MODEL), lambda i, j: (0, 0)),
            pl.BlockSpec(mod.shape, lambda i, j: (0, 0)),
            pl.BlockSpec((D_MODEL, D_MODEL), lambda i, j: (0, j)),
            pl.BlockSpec((tm, half), lambda i, j: (i, 0)),
            pl.BlockSpec((tm, half), lambda i, j: (i, 0)),
            pl.BlockSpec((1, D_MODEL), lambda i, j: (0, 0)),
            pl.BlockSpec((1, D_MODEL), lambda i, j: (0, 0)),
        ],
        out_specs=pl.BlockSpec((tm, D_MODEL), lambda i, j: (i, j)),
        out_shape=jax.ShapeDtypeStruct((T, IN_WIDTH), BF16),
        scratch_shapes=[pltpu.VMEM((tm, D_MODEL), BF16)],
        compiler_params=_params(("arbitrary", "arbitrary")),
        name="in_proj",
    )(x2d, nw, mod, w_in_bf, cos, sin, lnw, lnb)


def _sgu_kernel(u_ref, v_ref, ga_ref, ws_ref, bt_ref, wpa_ref, o_ref, ya_scr):
    tm = u_ref.shape[0]
    gd = D_MODEL // SGU_GROUPS
    for n in range(tm // SGU_CHUNK):
        rows = slice(n * SGU_CHUNK, (n + 1) * SGU_CHUNK)
        for g in range(SGU_GROUPS):
            cols = slice(g * gd, (g + 1) * gd)
            mixed = _dot(ws_ref[g], v_ref[rows, cols]) + bt_ref[:, g:g + 1]
            ya_scr[rows, cols] = (u_ref[rows, cols].astype(F32) * mixed).astype(BF16)
    pa = _dot(ya_scr[...], wpa_ref[...])
    o_ref[...] = (ga_ref[...].astype(F32) * pa).astype(BF16)


def _sgu_call(proj, ws_bf, bt, wpa_bf):
    T = proj.shape[0]
    tm = 512
    return pl.pallas_call(
        _sgu_kernel,
        grid=(T // tm,),
        in_specs=[
            pl.BlockSpec((tm, D_MODEL), lambda i: (i, COL_U)),
            pl.BlockSpec((tm, D_MODEL), lambda i: (i, COL_V)),
            pl.BlockSpec((tm, D_MODEL), lambda i: (i, COL_GA)),
            pl.BlockSpec(ws_bf.shape, lambda i: (0, 0, 0)),
            pl.BlockSpec(bt.shape, lambda i: (0, 0)),
            pl.BlockSpec((D_MODEL, D_MODEL), lambda i: (0, 0)),
        ],
        out_specs=pl.BlockSpec((tm, D_MODEL), lambda i: (i, 0)),
        out_shape=jax.ShapeDtypeStruct((T, D_MODEL), BF16),
        scratch_shapes=[pltpu.VMEM((tm, D_MODEL), BF16)],
        compiler_params=_params(("arbitrary",)),
        name="sgu_proj_a",
    )(proj, proj, proj, ws_bf, bt, wpa_bf)


def _ret_kernel(qf_ref, kf_ref, vf_ref, gf_ref, qb_ref, kb_ref, vb_ref, gb_ref, s0_ref, dec_ref,
                yf_ref, yb_ref, s_scr, intra_scr, qd_scr, kd_scr, cd_scr):
    C = RET_CHUNK
    i = pl.program_id(0)

    @pl.when(i == 0)
    def _():
        lg = jnp.log1p(-jnp.exp2(dec_ref[...]))
        r = lax.broadcasted_iota(jnp.int32, (C, C), 0).astype(F32)
        c = lax.broadcasted_iota(jnp.int32, (C, C), 1).astype(F32)
        pos = lax.broadcasted_iota(jnp.int32, (C, 1), 0).astype(F32)
        for h in range(HEADS):
            lf = lg[0:1, h:h + 1]
            lb = lg[1:2, h:h + 1]
            intra_scr[0, h] = jnp.where(r >= c, jnp.exp(jnp.maximum(r - c, 0.0) * lf), 0.0)
            intra_scr[1, h] = jnp.where(c >= r, jnp.exp(jnp.maximum(c - r, 0.0) * lb), 0.0)
            qd_scr[0, h] = jnp.exp((pos + 1.0) * lf)
            qd_scr[1, h] = jnp.exp((C - pos) * lb)
            kd_scr[0, h] = jnp.exp((C - 1.0 - pos) * lf)
            kd_scr[1, h] = jnp.exp(pos * lb)
            cd_scr[0, h] = jnp.exp(C * lf)
            cd_scr[1, h] = jnp.exp(C * lb)
        s_scr[...] = s0_ref[...]

    dirs = ((qf_ref, kf_ref, vf_ref, gf_ref, yf_ref), (qb_ref, kb_ref, vb_ref, gb_ref, yb_ref))
    for d, (q_ref, k_ref, v_ref, g_ref, y_ref) in enumerate(dirs):
        for h in range(HEADS):
            q = q_ref[:, h * DK:(h + 1) * DK]
            k = k_ref[:, h * DK:(h + 1) * DK]
            v = v_ref[:, h * DV:(h + 1) * DV]
            s = s_scr[d, h]
            a = (_dot_nt(q, k) * intra_scr[d, h]).astype(BF16)
            o = _dot(a, v) + _dot(q, s.astype(BF16)) * qd_scr[d, h]
            hn = o * lax.rsqrt(jnp.mean(o * o, axis=-1, keepdims=True) + EPS)
            y_ref[:, h * DV:(h + 1) * DV] = (g_ref[:, h * DV:(h + 1) * DV].astype(F32) * hn).astype(BF16)
            kd = (k.astype(F32) * kd_scr[d, h]).astype(BF16)
            s_scr[d, h] = s * cd_scr[d, h] + _dot_tn(kd, v)


def _ret_call(proj, s0, dec):
    T = proj.shape[0]
    C = RET_CHUNK
    n = T // C
    qk_w = HEADS * DK
    v_w = HEADS * DV
    fwd = lambda col: (lambda i: (i, col))
    bwd = lambda col: (lambda i: (n - 1 - i, col))
    return pl.pallas_call(
        _ret_kernel,
        grid=(n,),
        in_specs=[
            pl.BlockSpec((C, qk_w), fwd(COL_Q)),
            pl.BlockSpec((C, qk_w), fwd(COL_K)),
            pl.BlockSpec((C, v_w), fwd(COL_VR // 2)),
            pl.BlockSpec((C, v_w), fwd(COL_GF // 2)),
            pl.BlockSpec((C, qk_w), bwd(COL_Q)),
            pl.BlockSpec((C, qk_w), bwd(COL_K)),
            pl.BlockSpec((C, v_w), bwd(COL_VR // 2)),
            pl.BlockSpec((C, v_w), bwd(COL_GB // 2)),
            pl.BlockSpec(s0.shape, lambda i: (0, 0, 0, 0)),
            pl.BlockSpec(dec.shape, lambda i: (0, 0)),
        ],
        out_specs=[
            pl.BlockSpec((C, v_w), lambda i: (i, 0)),
            pl.BlockSpec((C, v_w), lambda i: (n - 1 - i, 0)),
        ],
        out_shape=[jax.ShapeDtypeStruct((T, v_w), BF16), jax.ShapeDtypeStruct((T, v_w), BF16)],
        scratch_shapes=[
            pltpu.VMEM((2, HEADS, DK, DV), F32),
            pltpu.VMEM((2, HEADS, C, C), F32),
            pltpu.VMEM((2, HEADS, C, 1), F32),
            pltpu.VMEM((2, HEADS, C, 1), F32),
            pltpu.VMEM((2, HEADS, 1, 1), F32),
        ],
        compiler_params=_params(("arbitrary",)),
        name="retention",
    )(proj, proj, proj, proj, proj, proj, proj, proj, s0, dec)


def _merge_kernel(yf_ref, yb_ref, za_ref, gbr_ref, x_ref, mod_ref, nw_ref, wpb_ref, wo_ref, rw_ref, rb_ref,
                  x1_ref, h2_ref, lg_ref):
    yb = yf_ref[...] + yb_ref[...]
    pb = _dot(yb, wpb_ref[...])
    y = za_ref[...].astype(F32) + gbr_ref[...].astype(F32) * pb
    yo = _dot(y.astype(BF16), wo_ref[...])
    x1 = x_ref[...] + _mod_slice(mod_ref, 0, 2) * yo
    x1_ref[...] = x1
    h2 = _rms(x1, nw_ref[...]) * (1.0 + _mod_slice(mod_ref, 0, 4)) + _mod_slice(mod_ref, 0, 3)
    h2_ref[...] = h2.astype(BF16)
    lg_ref[...] = lax.dot_general(rw_ref[...], h2, (((1,), (1,)), ((), ())), precision=lax.Precision.HIGHEST,
                                  preferred_element_type=F32) + rb_ref[...]


def _merge_call(yf, yb, za, proj, x2d, mod, nw2, wpb_bf, wo_bf, rw, rb):
    T = x2d.shape[0]
    tm = 512
    row = lambda i: (i, 0)
    fixed = lambda i: (0, 0)
    return pl.pallas_call(
        _merge_kernel,
        grid=(T // tm,),
        in_specs=[
            pl.BlockSpec((tm, HEADS * DV), row),
            pl.BlockSpec((tm, HEADS * DV), row),
            pl.BlockSpec((tm, D_MODEL), row),
            pl.BlockSpec((tm, D_MODEL), lambda i: (i, COL_GBR)),
            pl.BlockSpec((tm, D_MODEL), row),
            pl.BlockSpec(mod.shape, fixed),
            pl.BlockSpec((1, D_MODEL), fixed),
            pl.BlockSpec(wpb_bf.shape, fixed),
            pl.BlockSpec(wo_bf.shape, fixed),
            pl.BlockSpec(rw.shape, fixed),
            pl.BlockSpec(rb.shape, fixed),
        ],
        out_specs=[
            pl.BlockSpec((tm, D_MODEL), row),
            pl.BlockSpec((tm, D_MODEL), row),
            pl.BlockSpec((N_EXPERTS, tm), lambda i: (0, i)),
        ],
        out_shape=[
            jax.ShapeDtypeStruct((T, D_MODEL), F32),
            jax.ShapeDtypeStruct((T, D_MODEL), BF16),
            jax.ShapeDtypeStruct((N_EXPERTS, T), F32),
        ],
        compiler_params=_params(("arbitrary",)),
        name="merge_router",
    )(yf, yb, za, proj, x2d, mod, nw2, wpb_bf, wo_bf, rw, rb)


def _route_kernel(lg_ref, pos_ref, p_ref, nch_ref, su_scr):
    E, tt = lg_ref.shape

    @pl.when(pl.program_id(0) == 0)
    def _():
        r = lax.broadcasted_iota(jnp.int32, (tt, tt), 0)
        c = lax.broadcasted_iota(jnp.int32, (tt, tt), 1)
        su_scr[...] = jnp.where(r < c, 1.0, 0.0).astype(BF16)

    sub = lax.broadcasted_iota(jnp.int32, (E, tt), 0)
    work = lg_ref[...]
    vals, hots = [], []
    for _ in range(TOP_K):
        m = jnp.max(work, axis=0, keepdims=True)
        first = jnp.min(jnp.where(work == m, sub, E), axis=0, keepdims=True)
        hot = sub == first
        vals.append(m)
        hots.append(hot)
        work = jnp.where(hot, -jnp.inf, work)
    exps = [jnp.exp(v - vals[0]) for v in vals]
    inv = 1.0 / functools.reduce(lambda a, b: a + b, exps)

    member = functools.reduce(lambda a, b: a + b, [jnp.where(h, 1.0, 0.0) for h in hots])
    cnt = jnp.sum(member, axis=1, keepdims=True)
    nch = jnp.floor((cnt + (MOE_CH - 1.0)) * (1.0 / MOE_CH))
    nch_b = jnp.broadcast_to(nch, (E, 128))
    rank = _dot(member.astype(BF16), su_scr[...])
    er = lax.broadcasted_iota(jnp.int32, (E, E), 0)
    ec = lax.broadcasted_iota(jnp.int32, (E, E), 1)
    first_chunk = _dot(jnp.where(ec < er, 1.0, 0.0).astype(BF16), nch_b.astype(BF16))[:, 0:1]
    base = first_chunk * MOE_CH + rank
    for k in range(TOP_K):
        pos_ref[k:k + 1, :] = jnp.sum(jnp.where(hots[k], base, 0.0), axis=0, keepdims=True).astype(jnp.int32)
        p_ref[k:k + 1, :] = exps[k] * inv
    nch_ref[0] = nch_b.astype(jnp.int32)


def _route_call(logits_t):
    E, T = logits_t.shape
    tt = MOE_TT
    return pl.pallas_call(
        _route_kernel,
        grid=(T // tt,),
        in_specs=[pl.BlockSpec((E, tt), lambda j: (0, j))],
        out_specs=[
            pl.BlockSpec((TOP_K, tt), lambda j: (0, j)),
            pl.BlockSpec((TOP_K, tt), lambda j: (0, j)),
            pl.BlockSpec((1, E, 128), lambda j: (j, 0, 0)),
        ],
        out_shape=[
            jax.ShapeDtypeStruct((TOP_K, T), jnp.int32),
            jax.ShapeDtypeStruct((TOP_K, T), F32),
            jax.ShapeDtypeStruct((T // tt, E, 128), jnp.int32),
        ],
        scratch_shapes=[pltpu.VMEM((tt, tt), BF16)],
        compiler_params=_params(("arbitrary",)),
        name="moe_route",
    )(logits_t)


def _slot_tiles_max(n_tiles):
    return -(-(n_tiles * MOE_MAXC + N_EXPERTS * (MOE_CPT - 1)) // MOE_CPT)


def _routing_tables(nch):
    n_tiles, E = nch.shape

    def cumsum_last(a):
        m = a.shape[-1]
        keep = jnp.arange(m)[:, None] <= jnp.arange(m)[None, :]
        return jnp.sum(jnp.where(keep, a[..., :, None], 0), axis=-2)

    tot = jnp.sum(nch, axis=0)
    seg = (tot + MOE_CPT - 1) // MOE_CPT * MOE_CPT
    seg_end = cumsum_last(seg)
    seg_start = seg_end - seg
    gstart = seg_start[None, :] + cumsum_last(nch.T).T - nch
    l_end = cumsum_last(nch)
    lstart = l_end - nch
    c = jnp.arange(MOE_MAXC, dtype=jnp.int32)[None, :, None]
    owner = (lstart[:, None, :] <= c) & (c < l_end[:, None, :])
    dst = jnp.sum(jnp.where(owner, gstart[:, None, :] + c - lstart[:, None, :], 0), axis=-1)
    dst = dst.astype(jnp.int32).reshape(-1)
    ncht = l_end[:, -1].astype(jnp.int32)
    nt_max = _slot_tiles_max(n_tiles)
    n_used = (seg_end[-1] // MOE_CPT).astype(jnp.int32)
    tile_blk = jnp.minimum(jnp.arange(nt_max, dtype=jnp.int32), n_used - 1)
    tile_exp = jnp.sum(seg_end[None, :] <= (tile_blk * MOE_CPT)[:, None], axis=1)
    tile_exp = jnp.minimum(tile_exp, E - 1).astype(jnp.int32)
    zstart = (seg_start + tot).astype(jnp.int32)
    zcount = (seg - tot).astype(jnp.int32)
    return dst, ncht, zstart, zcount, tile_exp, tile_blk, n_used.reshape(1)


def _chunk_rows(c):
    return pl.ds(pl.multiple_of(c * MOE_CH, MOE_CH), MOE_CH)


def _dispatch_kernel(dst_ref, ncht_ref, zstart_ref, zcount_ref, nu_ref, h2_ref, pos_ref, xs_hbm, xl_scr, zero_scr,
                     sems, zsem):
    j = pl.program_id(0)
    n = pl.num_programs(0)
    slot = j % 2
    tt = h2_ref.shape[0]
    nt_max = xs_hbm.shape[0] // MOE_TM

    def chunk_copy(tile, slot_, c):
        d = dst_ref[tile * MOE_MAXC + c]
        return pltpu.make_async_copy(xl_scr.at[slot_, _chunk_rows(c)], xs_hbm.at[_chunk_rows(d)], sems.at[slot_])

    def zero_chunk_copy(d):
        return pltpu.make_async_copy(zero_scr.at[pl.ds(0, MOE_CH)], xs_hbm.at[_chunk_rows(d)], zsem.at[0])

    def zero_tile_copy(t):
        rows = pl.ds(pl.multiple_of(t * MOE_TM, MOE_TM), MOE_TM)
        return pltpu.make_async_copy(zero_scr, xs_hbm.at[rows], zsem.at[0])

    def zero_fill(op):
        def expert(e, carry):
            def chunk(r, carry_):
                op(zero_chunk_copy(zstart_ref[e] + r))
                return carry_
            return lax.fori_loop(0, zcount_ref[e], chunk, carry)
        lax.fori_loop(0, N_EXPERTS, expert, 0)

        def tile(t, carry):
            op(zero_tile_copy(t))
            return carry
        lax.fori_loop(nu_ref[0], nt_max, tile, 0)

    def wait_tile(tile, slot_):
        def body(c, carry):
            chunk_copy(tile, slot_, c).wait()
            return carry
        lax.fori_loop(0, ncht_ref[tile], body, 0)

    @pl.when(j == 0)
    def _():
        zero_scr[...] = jnp.zeros_like(zero_scr)
        zero_fill(lambda cp: cp.start())

    @pl.when(j >= 2)
    def _():
        wait_tile(j - 2, slot)

    pos = pos_ref[...]
    h2 = h2_ref[...]
    for rb in range(MOE_RLOC // MOE_RB):
        io = lax.broadcasted_iota(jnp.int32, (MOE_RB, tt), 0) + rb * MOE_RB
        onehot = jnp.zeros((MOE_RB, tt), F32)
        for k in range(TOP_K):
            onehot = jnp.where(io == pos[k:k + 1, :], 1.0, onehot)
        xl_scr[slot, rb * MOE_RB:(rb + 1) * MOE_RB, :] = _dot(onehot.astype(BF16), h2).astype(BF16)

    def start(c, carry):
        chunk_copy(j, slot, c).start()
        return carry
    lax.fori_loop(0, ncht_ref[j], start, 0)

    @pl.when(j == n - 1)
    def _():
        @pl.when(j >= 1)
        def _():
            wait_tile(j - 1, 1 - slot)
        wait_tile(j, slot)
        zero_fill(lambda cp: cp.wait())


def _dispatch_call(tables, h2, pos_t, n_slots):
    T = h2.shape[0]
    tt = MOE_TT
    dst, ncht, zstart, zcount, n_used = tables
    grid_spec = pltpu.PrefetchScalarGridSpec(
        num_scalar_prefetch=5,
        grid=(T // tt,),
        in_specs=[
            pl.BlockSpec((tt, D_MODEL), lambda j, *_: (j, 0)),
            pl.BlockSpec((TOP_K, tt), lambda j, *_: (0, j)),
        ],
        out_specs=pl.BlockSpec(memory_space=pl.ANY),
        scratch_shapes=[
            pltpu.VMEM((2, MOE_RLOC, D_MODEL), BF16),
            pltpu.VMEM((MOE_TM, D_MODEL), BF16),
            pltpu.SemaphoreType.DMA((2,)),
            pltpu.SemaphoreType.DMA((1,)),
        ],
    )
    return pl.pallas_call(
        _dispatch_kernel,
        grid_spec=grid_spec,
        out_shape=jax.ShapeDtypeStruct((n_slots, D_MODEL), BF16),
        compiler_params=_params(("arbitrary",)),
        name="moe_dispatch",
    )(dst, ncht, zstart, zcount, n_used, h2, pos_t)


def _experts_kernel(te_ref, tb_ref, nu_ref, xs_ref, w1_ref, b1_ref, w2_ref, b2_ref, y_ref, w1_scr, w2_scr):
    i = pl.program_id(0)
    valid = i < nu_ref[0]
    new_expert = (i == 0) | (te_ref[i] != te_ref[jnp.maximum(i - 1, 0)])

    @pl.when(valid & new_expert)
    def _():
        w1_scr[...] = w1_ref[...].astype(BF16)
        w2_scr[...] = w2_ref[...].astype(BF16)

    @pl.when(valid)
    def _():
        hh = _dot(xs_ref[...], w1_scr[...]) + b1_ref[...]
        gate = jnp.minimum(hh[:, :D_FF], SWIGLU_LIMIT)
        up = jnp.clip(hh[:, D_FF:], -SWIGLU_LIMIT, SWIGLU_LIMIT)
        act = (up + 1.0) * gate * jax.nn.sigmoid(SWIGLU_ALPHA * gate)
        y_ref[...] = (_dot(act.astype(BF16), w2_scr[...]) + b2_ref[...]).astype(BF16)

    @pl.when(jnp.logical_not(valid))
    def _():
        y_ref[...] = jnp.zeros_like(y_ref)


def _experts_call(tile_exp, tile_blk, n_used, xs, w1, b1, w2, b2):
    nt_max = tile_exp.shape[0]
    slot_tile = lambda i, te, tb, nu: (tb[i], 0)
    per_e = lambda i, te, tb, nu: (te[i], 0, 0)
    grid_spec = pltpu.PrefetchScalarGridSpec(
        num_scalar_prefetch=3,
        grid=(nt_max,),
        in_specs=[
            pl.BlockSpec((MOE_TM, D_MODEL), slot_tile),
            pl.BlockSpec((None, D_MODEL, 2 * D_FF), per_e),
            pl.BlockSpec((None, 1, 2 * D_FF), per_e),
            pl.BlockSpec((None, D_FF, D_MODEL), per_e),
            pl.BlockSpec((None, 1, D_MODEL), per_e),
        ],
        out_specs=pl.BlockSpec((MOE_TM, D_MODEL), lambda i, te, tb, nu: (i, 0)),
        scratch_shapes=[pltpu.VMEM((D_MODEL, 2 * D_FF), BF16), pltpu.VMEM((D_FF, D_MODEL), BF16)],
    )
    return pl.pallas_call(
        _experts_kernel,
        grid_spec=grid_spec,
        out_shape=jax.ShapeDtypeStruct(xs.shape, BF16),
        compiler_params=_params(("arbitrary",)),
        name="moe_experts",
    )(tile_exp, tile_blk, n_used, xs, w1, b1, w2, b2)


def _combine_kernel(dst_ref, ncht_ref, pos_ref, p_ref, x1_ref, mod_ref, fw_ref, y_hbm, o_ref, yl_scr, w_scr, sems):
    j = pl.program_id(0)
    n = pl.num_programs(0)
    slot = j % 2
    tt = x1_ref.shape[0]

    def chunk_copy(tile, slot_, c):
        d = dst_ref[tile * MOE_MAXC + c]
        return pltpu.make_async_copy(y_hbm.at[_chunk_rows(d)], yl_scr.at[slot_, _chunk_rows(c)], sems.at[slot_])

    def start_tile(tile, slot_):
        def body(c, carry):
            chunk_copy(tile, slot_, c).start()
            return carry
        lax.fori_loop(0, ncht_ref[tile], body, 0)

    @pl.when(j == 0)
    def _():
        yl_scr[...] = jnp.zeros_like(yl_scr)
        start_tile(0, 0)

    @pl.when(j + 1 < n)
    def _():
        start_tile(j + 1, 1 - slot)

    def wait(c, carry):
        chunk_copy(j, slot, c).wait()
        return carry
    lax.fori_loop(0, ncht_ref[j], wait, 0)

    pos = pos_ref[...]
    p = p_ref[...]
    for cb in range(MOE_RLOC // MOE_RB):
        io = lax.broadcasted_iota(jnp.int32, (tt, MOE_RB), 1) + cb * MOE_RB
        w = jnp.zeros((tt, MOE_RB), F32)
        for k in range(TOP_K):
            w = jnp.where(io == pos[:, k:k + 1], p[:, k:k + 1], w)
        w_scr[:, cb * MOE_RB:(cb + 1) * MOE_RB] = w.astype(BF16)
    moe = _dot(w_scr[...], yl_scr[slot])
    x2 = x1_ref[...] + _mod_slice(mod_ref, 0, 5) * moe
    o_ref[...] = _rms(x2, fw_ref[...])


def _combine_call(dst, ncht, pos, p, x1, mod, fw, y):
    T = x1.shape[0]
    tt = MOE_TT
    row = lambda j, *_: (j, 0)
    fixed = lambda j, *_: (0, 0)
    grid_spec = pltpu.PrefetchScalarGridSpec(
        num_scalar_prefetch=2,
        grid=(T // tt,),
        in_specs=[
            pl.BlockSpec((tt, TOP_K), row),
            pl.BlockSpec((tt, TOP_K), row),
            pl.BlockSpec((tt, D_MODEL), row),
            pl.BlockSpec(mod.shape, fixed),
            pl.BlockSpec((1, D_MODEL), fixed),
            pl.BlockSpec(memory_space=pl.ANY),
        ],
        out_specs=pl.BlockSpec((tt, D_MODEL), row),
        scratch_shapes=[
            pltpu.VMEM((2, MOE_RLOC, D_MODEL), BF16),
            pltpu.VMEM((tt, MOE_RLOC), BF16),
            pltpu.SemaphoreType.DMA((2,)),
        ],
    )
    return pl.pallas_call(
        _combine_kernel,
        grid_spec=grid_spec,
        out_shape=jax.ShapeDtypeStruct((T, D_MODEL), F32),
        compiler_params=_params(("arbitrary",)),
        name="moe_combine",
    )(dst, ncht, pos, p, x1, mod, fw, y)


def _rope_tables(T):
    rows = T // GRID_W
    row = jnp.repeat(jnp.arange(rows, dtype=F32), GRID_W)
    col = jnp.tile(jnp.arange(GRID_W, dtype=F32), rows)
    n_freq = DK // 4
    inv = ROPE_BASE ** (-jnp.arange(n_freq, dtype=F32) / n_freq)
    ang = jnp.concatenate([row[:, None] * inv, col[:, None] * inv], axis=-1)
    return jnp.cos(ang), jnp.sin(ang)


def kernel(x, c, ctx, c_ctx, w_mod, b_mod, norm1_w, norm2_w, w_in, sgu_ln_w, sgu_ln_b, sgu_w, sgu_b,
           ret_decay_fwd, ret_decay_bwd, w_proj_a, w_proj_b, w_out, router_w, router_b,
           moe_w1, moe_b1, moe_w2, moe_b2, final_norm_w):
    B, T, D = x.shape
    assert B == 1 and D == D_MODEL and w_mod.shape[0] == 1 and T % 1024 == 0
    x2d = x.reshape(T, D)
    cc = jnp.stack([c.reshape(D), c_ctx.reshape(D)], axis=1)
    mod = _mod_call(cc, w_mod[0], b_mod)
    dec = jnp.stack([ret_decay_fwd[0], ret_decay_bwd[0]], axis=0)
    w_in_bf = w_in[0].astype(BF16)
    s0 = _ctx_call(ctx.reshape(ctx.shape[1], D), norm1_w, mod, w_in_bf, dec)
    cos, sin = _rope_tables(T)
    proj = _inproj_call(x2d, norm1_w, mod, w_in_bf, cos, sin, sgu_ln_w, sgu_ln_b)
    za = _sgu_call(proj, sgu_w[0].astype(BF16), sgu_b[0].T, w_proj_a[0].astype(BF16))
    yf, yb = _ret_call(proj, s0, dec)
    x1, h2, logits_t = _merge_call(yf, yb, za, proj, x2d, mod, norm2_w, w_proj_b[0].astype(BF16),
                                   w_out[0].astype(BF16), router_w[0].T, router_b.reshape(N_EXPERTS, 1))
    pos_t, p_t, nch = _route_call(logits_t)
    dst, ncht, zstart, zcount, tile_exp, tile_blk, n_used = _routing_tables(nch[:, :, 0])
    n_slots = tile_exp.shape[0] * MOE_TM
    xs = _dispatch_call((dst, ncht, zstart, zcount, n_used), h2, pos_t, n_slots)
    y = _experts_call(tile_exp, tile_blk, n_used, xs, moe_w1[0], moe_b1[0][:, None, :], moe_w2[0],
                      moe_b2[0][:, None, :])
    out = _combine_call(dst, ncht, pos_t.T, p_t.T, x1, mod, final_norm_w.reshape(1, D), y)
    return out.reshape(B, T, D)
```

```python
import functools

import jax
import jax.numpy as jnp
from jax import lax
from jax.experimental import pallas as pl
from jax.experimental.pallas import tpu as pltpu

D_MODEL = 1024
GRID_W = 64
SGU_CHUNK = 128
SGU_GROUPS = 8
HEADS = 4
DK = D_MODEL // HEADS
DV = 2 * DK
ROPE_BASE = 10000.0
N_EXPERTS = 32
TOP_K = 4
D_FF = D_MODEL
SWIGLU_LIMIT = 7.0
SWIGLU_ALPHA = 1.702
EPS = 1e-6
IN_WIDTH = 12 * D_MODEL
COL_U, COL_V, COL_Q, COL_K, COL_VR, COL_GF, COL_GB, COL_GA, COL_GBR = 0, 1, 2, 3, 4, 6, 8, 10, 11

RET_CHUNK = 256
VMEM_LIMIT = 56 * 1024 * 1024

MOE_TT = 512
MOE_CH = 16
MOE_TM = 512
MOE_CPT = MOE_TM // MOE_CH
MOE_MAXC = TOP_K * MOE_TT // MOE_CH + N_EXPERTS
MOE_RLOC = MOE_MAXC * MOE_CH
MOE_RB = 256

F32 = jnp.float32
BF16 = jnp.bfloat16


def _params(sem):
    return pltpu.CompilerParams(dimension_semantics=sem, vmem_limit_bytes=VMEM_LIMIT)


def _dot(a, b):
    return jnp.dot(a, b, preferred_element_type=F32)


def _dot_nt(a, b):
    return lax.dot_general(a, b, (((1,), (1,)), ((), ())), preferred_element_type=F32)


def _dot_tn(a, b):
    return lax.dot_general(a, b, (((0,), (0,)), ((), ())), preferred_element_type=F32)


def _rms(x, w):
    return x * lax.rsqrt(jnp.mean(x * x, axis=-1, keepdims=True) + EPS) * w


def _gelu(x):
    return 0.5 * x * (1.0 + lax.erf(x * (2.0 ** -0.5)))


def _mod_slice(mod_ref, row, k):
    return mod_ref[row:row + 1, k * D_MODEL:(k + 1) * D_MODEL]


def _mod_kernel(cc_ref, w_ref, b_ref, o_ref):
    s = cc_ref[...]
    s = s * jax.nn.sigmoid(s)
    w = w_ref[...]
    r0 = jnp.sum(s[:, 0:1] * w, axis=0, keepdims=True)
    r1 = jnp.sum(s[:, 1:2] * w, axis=0, keepdims=True)
    o_ref[...] = jnp.concatenate([r0, r1], axis=0) + b_ref[...]


def _mod_call(cc, w_mod, b_mod):
    bn = 1536
    n = w_mod.shape[1]
    return pl.pallas_call(
        _mod_kernel,
        grid=(n // bn,),
        in_specs=[
            pl.BlockSpec((D_MODEL, 2), lambda j: (0, 0)),
            pl.BlockSpec((D_MODEL, bn), lambda j: (0, j)),
            pl.BlockSpec((1, bn), lambda j: (0, j)),
        ],
        out_specs=pl.BlockSpec((2, bn), lambda j: (0, j)),
        out_shape=jax.ShapeDtypeStruct((2, n), F32),
        compiler_params=_params(("arbitrary",)),
        name="mod_vectors",
    )(cc, w_mod, b_mod)


def _ctx_kernel(ctx_ref, nw_ref, mod_ref, w_ref, dec_ref, s0_ref):
    L = ctx_ref.shape[0]
    sh = _mod_slice(mod_ref, 1, 0)
    sc = _mod_slice(mod_ref, 1, 1)
    hc = (_rms(ctx_ref[...], nw_ref[...]) * (1.0 + sc) + sh).astype(BF16)
    kv = _dot(hc, w_ref[...])
    lg = jnp.log1p(-jnp.exp2(dec_ref[...]))
    pos = lax.broadcasted_iota(jnp.int32, (L, 1), 0).astype(F32)
    for h in range(HEADS):
        k = kv[:, h * DK:(h + 1) * DK] * (DK ** -0.5)
        v = kv[:, HEADS * DK + h * DV:HEADS * DK + (h + 1) * DV].astype(BF16)
        wf = jnp.exp((L - 1.0 - pos) * lg[0:1, h:h + 1])
        wb = jnp.exp(pos * lg[1:2, h:h + 1])
        s0_ref[0, h] = _dot_tn((k * wf).astype(BF16), v)
        s0_ref[1, h] = _dot_tn((k * wb).astype(BF16), v)


def _ctx_call(ctx2d, nw, mod, w_in_bf, dec):
    L = ctx2d.shape[0]
    wcols = HEADS * DK + HEADS * DV
    return pl.pallas_call(
        _ctx_kernel,
        grid=(1,),
        in_specs=[
            pl.BlockSpec((L, D_MODEL), lambda i: (0, 0)),
            pl.BlockSpec((1, D_MODEL), lambda i: (0, 0)),
            pl.BlockSpec(mod.shape, lambda i: (0, 0)),
            pl.BlockSpec((D_MODEL, wcols), lambda i: (0, COL_K * D_MODEL // wcols)),
            pl.BlockSpec(dec.shape, lambda i: (0, 0)),
        ],
        out_specs=pl.BlockSpec((2, HEADS, DK, DV), lambda i: (0, 0, 0, 0)),
        out_shape=jax.ShapeDtypeStruct((2, HEADS, DK, DV), F32),
        compiler_params=_params(("arbitrary",)),
        name="ctx_states",
    )(ctx2d, nw, mod, w_in_bf, dec)


def _inproj_kernel(x_ref, nw_ref, mod_ref, w_ref, cos_ref, sin_ref, lnw_ref, lnb_ref, o_ref, h_scr, g_scr):
    j = pl.program_id(1)

    @pl.when(j == 0)
    def _():
        sh = _mod_slice(mod_ref, 0, 0)
        sc = _mod_slice(mod_ref, 0, 1)
        h_scr[...] = (_rms(x_ref[...], nw_ref[...]) * (1.0 + sc) + sh).astype(BF16)

    def pieces(emit):
        for s in range(D_MODEL // DK):
            cols = slice(s * DK, (s + 1) * DK)
            emit(cols, _dot(h_scr[...], w_ref[:, cols]))

    def store(fn):
        def emit(cols, acc):
            o_ref[:, cols] = fn(acc).astype(BF16)
        return emit

    @pl.when(j == COL_U)
    def _():
        pieces(store(_gelu))

    @pl.when(j == COL_V)
    def _():
        def emit(cols, acc):
            g_scr[:, cols] = _gelu(acc)
        pieces(emit)
        g = g_scr[...]
        mu = jnp.mean(g, axis=-1, keepdims=True)
        d = g - mu
        var = jnp.mean(d * d, axis=-1, keepdims=True)
        o_ref[...] = (d * lax.rsqrt(var + EPS) * lnw_ref[...] + lnb_ref[...]).astype(BF16)

    def rope(scale):
        half = DK // 2

        def emit(cols, acc):
            cos = cos_ref[...]
            sin = sin_ref[...]
            x1 = acc[:, :half]
            x2 = acc[:, half:]
            o_ref[:, cols.start:cols.start + half] = ((x1 * cos - x2 * sin) * scale).astype(BF16)
            o_ref[:, cols.start + half:cols.stop] = ((x1 * sin + x2 * cos) * scale).astype(BF16)
        pieces(emit)

    @pl.when(j == COL_Q)
    def _():
        rope(1.0)

    @pl.when(j == COL_K)
    def _():
        rope(DK ** -0.5)

    @pl.when((j >= COL_VR) & (j < COL_GF))
    def _():
        pieces(store(lambda a: a))

    @pl.when((j >= COL_GF) & (j < COL_GA))
    def _():
        pieces(store(lambda a: a * jax.nn.sigmoid(a)))

    @pl.when(j >= COL_GA)
    def _():
        pieces(store(jax.nn.sigmoid))


def _inproj_call(x2d, nw, mod, w_in_bf, cos, sin, lnw, lnb):
    T = x2d.shape[0]
    tm = 1024
    half = DK // 2
    return pl.pallas_call(
        _inproj_kernel,
        grid=(T // tm, IN_WIDTH // D_MODEL),
        in_specs=[
            pl.BlockSpec((tm, D_MODEL), lambda i, j: (i, 0)),
            pl.BlockSpec((1, D_MODEL), lambda i, j: (0, 0)),
            pl.BlockSpec(mod.shape, lambda i, j: (0, 0)),
            pl.BlockSpec((D_MODEL, D_MODEL), lambda i, j: (0, j)),
            pl.BlockSpec((tm, half), lambda i, j: (i, 0)),
            pl.BlockSpec((tm, half), lambda i, j: (i, 0)),
            pl.BlockSpec((1, D_MODEL), lambda i, j: (0, 0)),
            pl.BlockSpec((1, D_MODEL), lambda i, j: (0, 0)),
        ],
        out_specs=pl.BlockSpec((tm, D_MODEL), lambda i, j: (i, j)),
        out_shape=jax.ShapeDtypeStruct((T, IN_WIDTH), BF16),
        scratch_shapes=[pltpu.VMEM((tm, D_MODEL), BF16), pltpu.VMEM((tm, D_MODEL), F32)],
        compiler_params=_params(("arbitrary", "arbitrary")),
        name="in_proj",
    )(x2d, nw, mod, w_in_bf, cos, sin, lnw, lnb)


def _sgu_kernel(u_ref, v_ref, ga_ref, ws_ref, bt_ref, wpa_ref, o_ref, ya_scr):
    tm = u_ref.shape[0]
    gd = D_MODEL // SGU_GROUPS
    for n in range(tm // SGU_CHUNK):
        rows = slice(n * SGU_CHUNK, (n + 1) * SGU_CHUNK)
        for g in range(SGU_GROUPS):
            cols = slice(g * gd, (g + 1) * gd)
            mixed = _dot(ws_ref[g], v_ref[rows, cols]) + bt_ref[:, g:g + 1]
            ya_scr[rows, cols] = (u_ref[rows, cols].astype(F32) * mixed).astype(BF16)
    pa = _dot(ya_scr[...], wpa_ref[...])
    o_ref[...] = (ga_ref[...].astype(F32) * pa).astype(BF16)


def _sgu_call(proj, ws_bf, bt, wpa_bf):
    T = proj.shape[0]
    tm = 512
    return pl.pallas_call(
        _sgu_kernel,
        grid=(T // tm,),
        in_specs=[
            pl.BlockSpec((tm, D_MODEL), lambda i: (i, COL_U)),
            pl.BlockSpec((tm, D_MODEL), lambda i: (i, COL_V)),
            pl.BlockSpec((tm, D_MODEL), lambda i: (i, COL_GA)),
            pl.BlockSpec(ws_bf.shape, lambda i: (0, 0, 0)),
            pl.BlockSpec(bt.shape, lambda i: (0, 0)),
            pl.BlockSpec((D_MODEL, D_MODEL), lambda i: (0, 0)),
        ],
        out_specs=pl.BlockSpec((tm, D_MODEL), lambda i: (i, 0)),
        out_shape=jax.ShapeDtypeStruct((T, D_MODEL), BF16),
        scratch_shapes=[pltpu.VMEM((tm, D_MODEL), BF16)],
        compiler_params=_params(("arbitrary",)),
        name="sgu_proj_a",
    )(proj, proj, proj, ws_bf, bt, wpa_bf)


def _ret_kernel(qf_ref, kf_ref, vf_ref, gf_ref, qb_ref, kb_ref, vb_ref, gb_ref, s0_ref, dec_ref,
                yf_ref, yb_ref, s_scr, intra_scr, qd_scr, kd_scr, cd_scr):
    C = RET_CHUNK
    i = pl.program_id(0)

    @pl.when(i == 0)
    def _():
        lg = jnp.log1p(-jnp.exp2(dec_ref[...]))
        r = lax.broadcasted_iota(jnp.int32, (C, C), 0).astype(F32)
        c = lax.broadcasted_iota(jnp.int32, (C, C), 1).astype(F32)
        pos = lax.broadcasted_iota(jnp.int32, (C, 1), 0).astype(F32)
        for h in range(HEADS):
            lf = lg[0:1, h:h + 1]
            lb = lg[1:2, h:h + 1]
            intra_scr[0, h] = jnp.where(r >= c, jnp.exp(jnp.maximum(r - c, 0.0) * lf), 0.0)
            intra_scr[1, h] = jnp.where(c >= r, jnp.exp(jnp.maximum(c - r, 0.0) * lb), 0.0)
            qd_scr[0, h] = jnp.exp((pos + 1.0) * lf)
            qd_scr[1, h] = jnp.exp((C - pos) * lb)
            kd_scr[0, h] = jnp.exp((C - 1.0 - pos) * lf)
            kd_scr[1, h] = jnp.exp(pos * lb)
            cd_scr[0, h] = jnp.exp(C * lf)
            cd_scr[1, h] = jnp.exp(C * lb)
        s_scr[...] = s0_ref[...]

    dirs = ((qf_ref, kf_ref, vf_ref, gf_ref, yf_ref), (qb_ref, kb_ref, vb_ref, gb_ref, yb_ref))
    for d, (q_ref, k_ref, v_ref, g_ref, y_ref) in enumerate(dirs):
        for h in range(HEADS):
            q = q_ref[:, h * DK:(h + 1) * DK]
            k = k_ref[:, h * DK:(h + 1) * DK]
            v = v_ref[:, h * DV:(h + 1) * DV]
            s = s_scr[d, h]
            a = (_dot_nt(q, k) * intra_scr[d, h]).astype(BF16)
            o = _dot(a, v) + _dot(q, s.astype(BF16)) * qd_scr[d, h]
            hn = o * lax.rsqrt(jnp.mean(o * o, axis=-1, keepdims=True) + EPS)
            y_ref[:, h * DV:(h + 1) * DV] = (g_ref[:, h * DV:(h + 1) * DV].astype(F32) * hn).astype(BF16)
            kd = (k.astype(F32) * kd_scr[d, h]).astype(BF16)
            s_scr[d, h] = s * cd_scr[d, h] + _dot_tn(kd, v)


def _ret_call(proj, s0, dec):
    T = proj.shape[0]
    C = RET_CHUNK
    n = T // C
    qk_w = HEADS * DK
    v_w = HEADS * DV
    fwd = lambda col: (lambda i: (i, col))
    bwd = lambda col: (lambda i: (n - 1 - i, col))
    return pl.pallas_call(
        _ret_kernel,
        grid=(n,),
        in_specs=[
            pl.BlockSpec((C, qk_w), fwd(COL_Q)),
            pl.BlockSpec((C, qk_w), fwd(COL_K)),
            pl.BlockSpec((C, v_w), fwd(COL_VR // 2)),
            pl.BlockSpec((C, v_w), fwd(COL_GF // 2)),
            pl.BlockSpec((C, qk_w), bwd(COL_Q)),
            pl.BlockSpec((C, qk_w), bwd(COL_K)),
            pl.BlockSpec((C, v_w), bwd(COL_VR // 2)),
            pl.BlockSpec((C, v_w), bwd(COL_GB // 2)),
            pl.BlockSpec(s0.shape, lambda i: (0, 0, 0, 0)),
            pl.BlockSpec(dec.shape, lambda i: (0, 0)),
        ],
        out_specs=[
            pl.BlockSpec((C, v_w), lambda i: (i, 0)),
            pl.BlockSpec((C, v_w), lambda i: (n - 1 - i, 0)),
        ],
        out_shape=[jax.ShapeDtypeStruct((T, v_w), BF16), jax.ShapeDtypeStruct((T, v_w), BF16)],
        scratch_shapes=[
            pltpu.VMEM((2, HEADS, DK, DV), F32),
            pltpu.VMEM((2, HEADS, C, C), F32),
            pltpu.VMEM((2, HEADS, C, 1), F32),
            pltpu.VMEM((2, HEADS, C, 1), F32),
            pltpu.VMEM((2, HEADS, 1, 1), F32),
        ],
        compiler_params=_params(("arbitrary",)),
        name="retention",
    )(proj, proj, proj, proj, proj, proj, proj, proj, s0, dec)


def _merge_kernel(yf_ref, yb_ref, za_ref, gbr_ref, x_ref, mod_ref, nw_ref, wpb_ref, wo_ref, rw_ref, rb_ref,
                  x1_ref, h2_ref, lg_ref):
    yb = yf_ref[...] + yb_ref[...]
    pb = _dot(yb, wpb_ref[...])
    y = za_ref[...].astype(F32) + gbr_ref[...].astype(F32) * pb
    yo = _dot(y.astype(BF16), wo_ref[...])
    x1 = x_ref[...] + _mod_slice(mod_ref, 0, 2) * yo
    x1_ref[...] = x1
    h2 = _rms(x1, nw_ref[...]) * (1.0 + _mod_slice(mod_ref, 0, 4)) + _mod_slice(mod_ref, 0, 3)
    h2_hi = h2.astype(BF16)
    h2_ref[...] = h2_hi
    h2_lo = (h2 - h2_hi.astype(F32)).astype(BF16)
    rw = rw_ref[...]
    rw_hi = rw.astype(BF16)
    rw_lo = (rw - rw_hi.astype(F32)).astype(BF16)
    both = _dot_nt(jnp.concatenate([rw_hi, rw_lo], axis=0), h2_hi)
    lg_ref[...] = both[:N_EXPERTS] + both[N_EXPERTS:] + _dot_nt(rw_hi, h2_lo) + rb_ref[...]


def _merge_call(yf, yb, za, proj, x2d, mod, nw2, wpb_bf, wo_bf, rw, rb):
    T = x2d.shape[0]
    tm = 512
    row = lambda i: (i, 0)
    fixed = lambda i: (0, 0)
    return pl.pallas_call(
        _merge_kernel,
        grid=(T // tm,),
        in_specs=[
            pl.BlockSpec((tm, HEADS * DV), row),
            pl.BlockSpec((tm, HEADS * DV), row),
            pl.BlockSpec((tm, D_MODEL), row),
            pl.BlockSpec((tm, D_MODEL), lambda i: (i, COL_GBR)),
            pl.BlockSpec((tm, D_MODEL), row),
            pl.BlockSpec(mod.shape, fixed),
            pl.BlockSpec((1, D_MODEL), fixed),
            pl.BlockSpec(wpb_bf.shape, fixed),
            pl.BlockSpec(wo_bf.shape, fixed),
            pl.BlockSpec(rw.shape, fixed),
            pl.BlockSpec(rb.shape, fixed),
        ],
        out_specs=[
            pl.BlockSpec((tm, D_MODEL), row),
            pl.BlockSpec((tm, D_MODEL), row),
            pl.BlockSpec((N_EXPERTS, tm), lambda i: (0, i)),
        ],
        out_shape=[
            jax.ShapeDtypeStruct((T, D_MODEL), F32),
            jax.ShapeDtypeStruct((T, D_MODEL), BF16),
            jax.ShapeDtypeStruct((N_EXPERTS, T), F32),
        ],
        compiler_params=_params(("arbitrary",)),
        name="merge_router",
    )(yf, yb, za, proj, x2d, mod, nw2, wpb_bf, wo_bf, rw, rb)


def _route_kernel(lg_ref, pos_ref, p_ref, nch_ref, su_scr):
    E, tt = lg_ref.shape

    @pl.when(pl.program_id(0) == 0)
    def _():
        r = lax.broadcasted_iota(jnp.int32, (tt, tt), 0)
        c = lax.broadcasted_iota(jnp.int32, (tt, tt), 1)
        su_scr[...] = jnp.where(r < c, 1.0, 0.0).astype(BF16)

    sub = lax.broadcasted_iota(jnp.int32, (E, tt), 0)
    work = lg_ref[...]
    vals, hots = [], []
    for _ in range(TOP_K):
        m = jnp.max(work, axis=0, keepdims=True)
        first = jnp.min(jnp.where(work == m, sub, E), axis=0, keepdims=True)
        hot = sub == first
        vals.append(m)
        hots.append(hot)
        work = jnp.where(hot, -jnp.inf, work)
    exps = [jnp.exp(v - vals[0]) for v in vals]
    inv = 1.0 / functools.reduce(lambda a, b: a + b, exps)

    member = functools.reduce(lambda a, b: a + b, [jnp.where(h, 1.0, 0.0) for h in hots])
    cnt = jnp.sum(member, axis=1, keepdims=True)
    nch = jnp.floor((cnt + (MOE_CH - 1.0)) * (1.0 / MOE_CH))
    nch_b = jnp.broadcast_to(nch, (E, 128))
    rank = _dot(member.astype(BF16), su_scr[...])
    er = lax.broadcasted_iota(jnp.int32, (E, E), 0)
    ec = lax.broadcasted_iota(jnp.int32, (E, E), 1)
    first_chunk = _dot(jnp.where(ec < er, 1.0, 0.0).astype(BF16), nch_b.astype(BF16))[:, 0:1]
    base = first_chunk * MOE_CH + rank
    for k in range(TOP_K):
        pos_ref[k:k + 1, :] = jnp.sum(jnp.where(hots[k], base, 0.0), axis=0, keepdims=True).astype(jnp.int32)
        p_ref[k:k + 1, :] = exps[k] * inv
    nch_ref[0] = nch_b.astype(jnp.int32)


def _route_call(logits_t):
    E, T = logits_t.shape
    tt = MOE_TT
    return pl.pallas_call(
        _route_kernel,
        grid=(T // tt,),
        in_specs=[pl.BlockSpec((E, tt), lambda j: (0, j))],
        out_specs=[
            pl.BlockSpec((TOP_K, tt), lambda j: (0, j)),
            pl.BlockSpec((TOP_K, tt), lambda j: (0, j)),
            pl.BlockSpec((1, E, 128), lambda j: (j, 0, 0)),
        ],
        out_shape=[
            jax.ShapeDtypeStruct((TOP_K, T), jnp.int32),
            jax.ShapeDtypeStruct((TOP_K, T), F32),
            jax.ShapeDtypeStruct((T // tt, E, 128), jnp.int32),
        ],
        scratch_shapes=[pltpu.VMEM((tt, tt), BF16)],
        compiler_params=_params(("arbitrary",)),
        name="moe_route",
    )(logits_t)


def _slot_tiles_max(n_tiles):
    return -(-(n_tiles * MOE_MAXC + N_EXPERTS * (MOE_CPT - 1)) // MOE_CPT)


def _routing_tables(nch):
    n_tiles, E = nch.shape

    def cumsum_last(a):
        m = a.shape[-1]
        keep = jnp.arange(m)[:, None] <= jnp.arange(m)[None, :]
        return jnp.sum(jnp.where(keep, a[..., :, None], 0), axis=-2)

    tot = jnp.sum(nch, axis=0)
    seg = (tot + MOE_CPT - 1) // MOE_CPT * MOE_CPT
    seg_end = cumsum_last(seg)
    seg_start = seg_end - seg
    gstart = seg_start[None, :] + cumsum_last(nch.T).T - nch
    l_end = cumsum_last(nch)
    lstart = l_end - nch
    c = jnp.arange(MOE_MAXC, dtype=jnp.int32)[None, :, None]
    owner = (lstart[:, None, :] <= c) & (c < l_end[:, None, :])
    dst = jnp.sum(jnp.where(owner, gstart[:, None, :] + c - lstart[:, None, :], 0), axis=-1)
    dst = dst.astype(jnp.int32).reshape(-1)
    ncht = l_end[:, -1].astype(jnp.int32)
    nt_max = _slot_tiles_max(n_tiles)
    n_used = (seg_end[-1] // MOE_CPT).astype(jnp.int32)
    tile_blk = jnp.minimum(jnp.arange(nt_max, dtype=jnp.int32), n_used - 1)
    tile_exp = jnp.sum(seg_end[None, :] <= (tile_blk * MOE_CPT)[:, None], axis=1)
    tile_exp = jnp.minimum(tile_exp, E - 1).astype(jnp.int32)
    experts = jnp.arange(E, dtype=jnp.int32)
    own = tile_exp[:, None] == experts[None, :]
    end_tile = jnp.sum(jnp.where(own, (seg_end // MOE_CPT)[None, :], 0), axis=1)
    follows = ((seg_start // MOE_CPT)[None, :] == end_tile[:, None]) & (seg > 0)[None, :]
    next_exp = jnp.where(jnp.any(follows, axis=1), jnp.sum(jnp.where(follows, experts[None, :], 0), axis=1), -1)
    next_exp = next_exp.astype(jnp.int32)
    zstart = (seg_start + tot).astype(jnp.int32)
    zcount = (seg - tot).astype(jnp.int32)
    return dst, ncht, zstart, zcount, tile_exp, tile_blk, n_used.reshape(1), next_exp


def _chunk_rows(c):
    return pl.ds(pl.multiple_of(c * MOE_CH, MOE_CH), MOE_CH)


def _dispatch_kernel(dst_ref, ncht_ref, zstart_ref, zcount_ref, nu_ref, h2_ref, pos_ref, xs_hbm, xl_scr, zero_scr,
                     sems, zsem):
    j = pl.program_id(0)
    n = pl.num_programs(0)
    slot = j % 2
    tt = h2_ref.shape[0]
    nt_max = xs_hbm.shape[0] // MOE_TM

    def chunk_copy(tile, slot_, c):
        d = dst_ref[tile * MOE_MAXC + c]
        return pltpu.make_async_copy(xl_scr.at[slot_, _chunk_rows(c)], xs_hbm.at[_chunk_rows(d)], sems.at[slot_])

    def zero_chunk_copy(d):
        return pltpu.make_async_copy(zero_scr.at[pl.ds(0, MOE_CH)], xs_hbm.at[_chunk_rows(d)], zsem.at[0])

    def zero_tile_copy(t):
        rows = pl.ds(pl.multiple_of(t * MOE_TM, MOE_TM), MOE_TM)
        return pltpu.make_async_copy(zero_scr, xs_hbm.at[rows], zsem.at[0])

    def zero_fill(op):
        def expert(e, carry):
            def chunk(r, carry_):
                op(zero_chunk_copy(zstart_ref[e] + r))
                return carry_
            return lax.fori_loop(0, zcount_ref[e], chunk, carry)
        lax.fori_loop(0, N_EXPERTS, expert, 0)

        def tile(t, carry):
            op(zero_tile_copy(t))
            return carry
        lax.fori_loop(nu_ref[0], nt_max, tile, 0)

    def wait_tile(tile, slot_):
        def body(c, carry):
            chunk_copy(tile, slot_, c).wait()
            return carry
        lax.fori_loop(0, ncht_ref[tile], body, 0)

    @pl.when(j == 0)
    def _():
        zero_scr[...] = jnp.zeros_like(zero_scr)
        zero_fill(lambda cp: cp.start())

    @pl.when(j >= 2)
    def _():
        wait_tile(j - 2, slot)

    pos = pos_ref[...]
    h2 = h2_ref[...]
    for rb in range(MOE_RLOC // MOE_RB):
        io = lax.broadcasted_iota(jnp.int32, (MOE_RB, tt), 0) + rb * MOE_RB
        onehot = jnp.zeros((MOE_RB, tt), F32)
        for k in range(TOP_K):
            onehot = jnp.where(io == pos[k:k + 1, :], 1.0, onehot)
        xl_scr[slot, rb * MOE_RB:(rb + 1) * MOE_RB, :] = _dot(onehot.astype(BF16), h2).astype(BF16)

    def start(c, carry):
        chunk_copy(j, slot, c).start()
        return carry
    lax.fori_loop(0, ncht_ref[j], start, 0)

    @pl.when(j == n - 1)
    def _():
        @pl.when(j >= 1)
        def _():
            wait_tile(j - 1, 1 - slot)
        wait_tile(j, slot)
        zero_fill(lambda cp: cp.wait())


def _dispatch_call(tables, h2, pos_t, n_slots):
    T = h2.shape[0]
    tt = MOE_TT
    dst, ncht, zstart, zcount, n_used = tables
    grid_spec = pltpu.PrefetchScalarGridSpec(
        num_scalar_prefetch=5,
        grid=(T // tt,),
        in_specs=[
            pl.BlockSpec((tt, D_MODEL), lambda j, *_: (j, 0)),
            pl.BlockSpec((TOP_K, tt), lambda j, *_: (0, j)),
        ],
        out_specs=pl.BlockSpec(memory_space=pl.ANY),
        scratch_shapes=[
            pltpu.VMEM((2, MOE_RLOC, D_MODEL), BF16),
            pltpu.VMEM((MOE_TM, D_MODEL), BF16),
            pltpu.SemaphoreType.DMA((2,)),
            pltpu.SemaphoreType.DMA((1,)),
        ],
    )
    return pl.pallas_call(
        _dispatch_kernel,
        grid_spec=grid_spec,
        out_shape=jax.ShapeDtypeStruct((n_slots, D_MODEL), BF16),
        compiler_params=_params(("arbitrary",)),
        name="moe_dispatch",
    )(dst, ncht, zstart, zcount, n_used, h2, pos_t)


def _experts_kernel(te_ref, tb_ref, nu_ref, nx_ref, xs_ref, w1_hbm, b1_ref, w2_hbm, b2_ref, y_ref,
                    w1_f32, w2_f32, w1_scr, w2_scr, sems):
    i = pl.program_id(0)
    valid = i < nu_ref[0]
    new_expert = (i == 0) | (te_ref[i] != te_ref[jnp.maximum(i - 1, 0)])

    def fetch(e):
        return (pltpu.make_async_copy(w1_hbm.at[e], w1_f32, sems.at[0]),
                pltpu.make_async_copy(w2_hbm.at[e], w2_f32, sems.at[1]))

    @pl.when(i == 0)
    def _():
        for cp in fetch(te_ref[0]):
            cp.start()

    @pl.when(valid & new_expert)
    def _():
        for cp in fetch(te_ref[i]):
            cp.wait()
        w1_scr[...] = w1_f32[...].astype(BF16)
        w2_scr[...] = w2_f32[...].astype(BF16)

        @pl.when(nx_ref[i] >= 0)
        def _():
            for cp in fetch(nx_ref[i]):
                cp.start()

    @pl.when(valid)
    def _():
        hh = _dot(xs_ref[...], w1_scr[...]) + b1_ref[...]
        gate = jnp.minimum(hh[:, :D_FF], SWIGLU_LIMIT)
        up = jnp.clip(hh[:, D_FF:], -SWIGLU_LIMIT, SWIGLU_LIMIT)
        act = (up + 1.0) * gate * jax.nn.sigmoid(SWIGLU_ALPHA * gate)
        y_ref[...] = (_dot(act.astype(BF16), w2_scr[...]) + b2_ref[...]).astype(BF16)

    @pl.when(jnp.logical_not(valid))
    def _():
        y_ref[...] = jnp.zeros_like(y_ref)


def _experts_call(tile_exp, tile_blk, n_used, next_exp, xs, w1, b1, w2, b2):
    nt_max = tile_exp.shape[0]
    slot_tile = lambda i, te, tb, nu, nx: (tb[i], 0)
    per_e = lambda i, te, tb, nu, nx: (te[i], 0, 0)
    grid_spec = pltpu.PrefetchScalarGridSpec(
        num_scalar_prefetch=4,
        grid=(nt_max,),
        in_specs=[
            pl.BlockSpec((MOE_TM, D_MODEL), slot_tile),
            pl.BlockSpec(memory_space=pl.ANY),
            pl.BlockSpec((None, 1, 2 * D_FF), per_e),
            pl.BlockSpec(memory_space=pl.ANY),
            pl.BlockSpec((None, 1, D_MODEL), per_e),
        ],
        out_specs=pl.BlockSpec((MOE_TM, D_MODEL), lambda i, te, tb, nu, nx: (i, 0)),
        scratch_shapes=[
            pltpu.VMEM((D_MODEL, 2 * D_FF), F32),
            pltpu.VMEM((D_FF, D_MODEL), F32),
            pltpu.VMEM((D_MODEL, 2 * D_FF), BF16),
            pltpu.VMEM((D_FF, D_MODEL), BF16),
            pltpu.SemaphoreType.DMA((2,)),
        ],
    )
    return pl.pallas_call(
        _experts_kernel,
        grid_spec=grid_spec,
        out_shape=jax.ShapeDtypeStruct(xs.shape, BF16),
        compiler_params=_params(("arbitrary",)),
        name="moe_experts",
    )(tile_exp, tile_blk, n_used, next_exp, xs, w1, b1, w2, b2)


def _combine_kernel(dst_ref, ncht_ref, pos_ref, p_ref, x1_ref, mod_ref, fw_ref, y_hbm, o_ref, yl_scr, w_scr, sems):
    j = pl.program_id(0)
    n = pl.num_programs(0)
    slot = j % 2
    tt = x1_ref.shape[0]

    def chunk_copy(tile, slot_, c):
        d = dst_ref[tile * MOE_MAXC + c]
        return pltpu.make_async_copy(y_hbm.at[_chunk_rows(d)], yl_scr.at[slot_, _chunk_rows(c)], sems.at[slot_])

    def start_tile(tile, slot_):
        def body(c, carry):
            chunk_copy(tile, slot_, c).start()
            return carry
        lax.fori_loop(0, ncht_ref[tile], body, 0)

    @pl.when(j == 0)
    def _():
        yl_scr[...] = jnp.zeros_like(yl_scr)
        start_tile(0, 0)

    @pl.when(j + 1 < n)
    def _():
        start_tile(j + 1, 1 - slot)

    def wait(c, carry):
        chunk_copy(j, slot, c).wait()
        return carry
    lax.fori_loop(0, ncht_ref[j], wait, 0)

    pos = pos_ref[...]
    p = p_ref[...]
    for cb in range(MOE_RLOC // MOE_RB):
        io = lax.broadcasted_iota(jnp.int32, (tt, MOE_RB), 1) + cb * MOE_RB
        w = jnp.zeros((tt, MOE_RB), F32)
        for k in range(TOP_K):
            w = jnp.where(io == pos[:, k:k + 1], p[:, k:k + 1], w)
        w_scr[:, cb * MOE_RB:(cb + 1) * MOE_RB] = w.astype(BF16)
    moe = _dot(w_scr[...], yl_scr[slot])
    x2 = x1_ref[...] + _mod_slice(mod_ref, 0, 5) * moe
    o_ref[...] = _rms(x2, fw_ref[...])


def _combine_call(dst, ncht, pos, p, x1, mod, fw, y):
    T = x1.shape[0]
    tt = MOE_TT
    row = lambda j, *_: (j, 0)
    fixed = lambda j, *_: (0, 0)
    grid_spec = pltpu.PrefetchScalarGridSpec(
        num_scalar_prefetch=2,
        grid=(T // tt,),
        in_specs=[
            pl.BlockSpec((tt, TOP_K), row),
            pl.BlockSpec((tt, TOP_K), row),
            pl.BlockSpec((tt, D_MODEL), row),
            pl.BlockSpec(mod.shape, fixed),
            pl.BlockSpec((1, D_MODEL), fixed),
            pl.BlockSpec(memory_space=pl.ANY),
        ],
        out_specs=pl.BlockSpec((tt, D_MODEL), row),
        scratch_shapes=[
            pltpu.VMEM((2, MOE_RLOC, D_MODEL), BF16),
            pltpu.VMEM((tt, MOE_RLOC), BF16),
            pltpu.SemaphoreType.DMA((2,)),
        ],
    )
    return pl.pallas_call(
        _combine_kernel,
        grid_spec=grid_spec,
        out_shape=jax.ShapeDtypeStruct((T, D_MODEL), F32),
        compiler_params=_params(("arbitrary",)),
        name="moe_combine",
    )(dst, ncht, pos, p, x1, mod, fw, y)


def _rope_tables(T):
    rows = T // GRID_W
    n_freq = DK // 4
    inv = ROPE_BASE ** (-jnp.arange(n_freq, dtype=F32) / n_freq)
    row_ang = jnp.arange(rows, dtype=F32)[:, None] * inv
    col_ang = jnp.arange(GRID_W, dtype=F32)[:, None] * inv
    small = lax.optimization_barrier((jnp.cos(row_ang), jnp.sin(row_ang), jnp.cos(col_ang), jnp.sin(col_ang)))
    cr, sr, cc, sc = small
    expand = lambda r, c: jnp.concatenate([jnp.repeat(r, GRID_W, axis=0), jnp.tile(c, (rows, 1))], axis=-1)
    return expand(cr, cc), expand(sr, sc)


def kernel(x, c, ctx, c_ctx, w_mod, b_mod, norm1_w, norm2_w, w_in, sgu_ln_w, sgu_ln_b, sgu_w, sgu_b,
           ret_decay_fwd, ret_decay_bwd, w_proj_a, w_proj_b, w_out, router_w, router_b,
           moe_w1, moe_b1, moe_w2, moe_b2, final_norm_w):
    B, T, D = x.shape
    assert B == 1 and D == D_MODEL and w_mod.shape[0] == 1 and T % 1024 == 0
    x2d = x.reshape(T, D)
    cc = jnp.stack([c.reshape(D), c_ctx.reshape(D)], axis=1)
    mod = _mod_call(cc, w_mod[0], b_mod)
    dec = jnp.stack([ret_decay_fwd[0], ret_decay_bwd[0]], axis=0)
    w_in_bf = w_in[0].astype(BF16)
    s0 = _ctx_call(ctx.reshape(ctx.shape[1], D), norm1_w, mod, w_in_bf, dec)
    cos, sin = _rope_tables(T)
    proj = _inproj_call(x2d, norm1_w, mod, w_in_bf, cos, sin, sgu_ln_w, sgu_ln_b)
    za = _sgu_call(proj, sgu_w[0].astype(BF16), sgu_b[0].T, w_proj_a[0].astype(BF16))
    yf, yb = _ret_call(proj, s0, dec)
    x1, h2, logits_t = _merge_call(yf, yb, za, proj, x2d, mod, norm2_w, w_proj_b[0].astype(BF16),
                                   w_out[0].astype(BF16), router_w[0].T, router_b.reshape(N_EXPERTS, 1))
    pos_t, p_t, nch = _route_call(logits_t)
    dst, ncht, zstart, zcount, tile_exp, tile_blk, n_used, next_exp = _routing_tables(nch[:, :, 0])
    n_slots = tile_exp.shape[0] * MOE_TM
    xs = _dispatch_call((dst, ncht, zstart, zcount, n_used), h2, pos_t, n_slots)
    y = _experts_call(tile_exp, tile_blk, n_used, next_exp, xs, moe_w1[0], moe_b1[0][:, None, :], moe_w2[0],
                      moe_b2[0][:, None, :])
    out = _combine_call(dst, ncht, pos_t.T, p_t.T, x1, mod, final_norm_w.reshape(1, D), y)
    return out.reshape(B, T, D)
```

```python
import functools

import jax
import jax.numpy as jnp
from jax import lax
from jax.experimental import pallas as pl
from jax.experimental.pallas import tpu as pltpu

D_MODEL = 1024
GRID_W = 64
SGU_CHUNK = 128
SGU_GROUPS = 8
HEADS = 4
DK = D_MODEL // HEADS
DV = 2 * DK
ROPE_BASE = 10000.0
N_EXPERTS = 32
TOP_K = 4
D_FF = D_MODEL
SWIGLU_LIMIT = 7.0
SWIGLU_ALPHA = 1.702
EPS = 1e-6
IN_WIDTH = 12 * D_MODEL
COL_U, COL_V, COL_Q, COL_K, COL_VR, COL_GF, COL_GB, COL_GA, COL_GBR = 0, 1, 2, 3, 4, 6, 8, 10, 11

RET_CHUNK = 256
VMEM_LIMIT = 56 * 1024 * 1024

MOE_TT = 512
MOE_CH = 16
MOE_TM = 512
MOE_CPT = MOE_TM // MOE_CH
MOE_MAXC = TOP_K * MOE_TT // MOE_CH + N_EXPERTS
MOE_RLOC = MOE_MAXC * MOE_CH
MOE_RB = 256
MOE_RUN = 4
MOE_MAXQ = MOE_MAXC // MOE_RUN
MOE_MAXS = (MOE_RUN - 1) * N_EXPERTS
MOE_CPROW = 2 * (MOE_MAXQ + MOE_MAXS)

F32 = jnp.float32
BF16 = jnp.bfloat16


def _params(sem):
    return pltpu.CompilerParams(dimension_semantics=sem, vmem_limit_bytes=VMEM_LIMIT)


def _dot(a, b):
    return jnp.dot(a, b, preferred_element_type=F32)


def _dot_nt(a, b):
    return lax.dot_general(a, b, (((1,), (1,)), ((), ())), preferred_element_type=F32)


def _dot_tn(a, b):
    return lax.dot_general(a, b, (((0,), (0,)), ((), ())), preferred_element_type=F32)


def _rms(x, w):
    return x * lax.rsqrt(jnp.mean(x * x, axis=-1, keepdims=True) + EPS) * w


def _gelu(x):
    return 0.5 * x * (1.0 + lax.erf(x * (2.0 ** -0.5)))


def _mod_slice(mod_ref, row, k):
    return mod_ref[row:row + 1, k * D_MODEL:(k + 1) * D_MODEL]


def _mod_kernel(cc_ref, w_ref, b_ref, o_ref):
    s = cc_ref[...]
    s = s * jax.nn.sigmoid(s)
    w = w_ref[...]
    r0 = jnp.sum(s[:, 0:1] * w, axis=0, keepdims=True)
    r1 = jnp.sum(s[:, 1:2] * w, axis=0, keepdims=True)
    o_ref[...] = jnp.concatenate([r0, r1], axis=0) + b_ref[...]


def _mod_call(cc, w_mod, b_mod):
    bn = 1536
    n = w_mod.shape[1]
    return pl.pallas_call(
        _mod_kernel,
        grid=(n // bn,),
        in_specs=[
            pl.BlockSpec((D_MODEL, 2), lambda j: (0, 0)),
            pl.BlockSpec((D_MODEL, bn), lambda j: (0, j)),
            pl.BlockSpec((1, bn), lambda j: (0, j)),
        ],
        out_specs=pl.BlockSpec((2, bn), lambda j: (0, j)),
        out_shape=jax.ShapeDtypeStruct((2, n), F32),
        compiler_params=_params(("arbitrary",)),
        name="mod_vectors",
    )(cc, w_mod, b_mod)


def _ctx_kernel(ctx_ref, nw_ref, mod_ref, w_ref, dec_ref, s0_ref):
    L = ctx_ref.shape[0]
    sh = _mod_slice(mod_ref, 1, 0)
    sc = _mod_slice(mod_ref, 1, 1)
    hc = (_rms(ctx_ref[...], nw_ref[...]) * (1.0 + sc) + sh).astype(BF16)
    kv = _dot(hc, w_ref[...])
    lg = jnp.log1p(-jnp.exp2(dec_ref[...]))
    pos = lax.broadcasted_iota(jnp.int32, (L, 1), 0).astype(F32)
    for h in range(HEADS):
        k = kv[:, h * DK:(h + 1) * DK] * (DK ** -0.5)
        v = kv[:, HEADS * DK + h * DV:HEADS * DK + (h + 1) * DV].astype(BF16)
        wf = jnp.exp((L - 1.0 - pos) * lg[0:1, h:h + 1])
        wb = jnp.exp(pos * lg[1:2, h:h + 1])
        s0_ref[0, h] = _dot_tn((k * wf).astype(BF16), v)
        s0_ref[1, h] = _dot_tn((k * wb).astype(BF16), v)


def _ctx_call(ctx2d, nw, mod, w_in_bf, dec):
    L = ctx2d.shape[0]
    wcols = HEADS * DK + HEADS * DV
    return pl.pallas_call(
        _ctx_kernel,
        grid=(1,),
        in_specs=[
            pl.BlockSpec((L, D_MODEL), lambda i: (0, 0)),
            pl.BlockSpec((1, D_MODEL), lambda i: (0, 0)),
            pl.BlockSpec(mod.shape, lambda i: (0, 0)),
            pl.BlockSpec((D_MODEL, wcols), lambda i: (0, COL_K * D_MODEL // wcols)),
            pl.BlockSpec(dec.shape, lambda i: (0, 0)),
        ],
        out_specs=pl.BlockSpec((2, HEADS, DK, DV), lambda i: (0, 0, 0, 0)),
        out_shape=jax.ShapeDtypeStruct((2, HEADS, DK, DV), F32),
        compiler_params=_params(("arbitrary",)),
        name="ctx_states",
    )(ctx2d, nw, mod, w_in_bf, dec)


def _inproj_kernel(x_ref, nw_ref, mod_ref, w_ref, rot_row_ref, rot_col_ref, lnw_ref, lnb_ref, o_ref,
                   h_scr, g_scr, cos_ref, sin_ref):
    j = pl.program_id(1)

    @pl.when(j == COL_Q)
    def _():
        for r in range(h_scr.shape[0] // GRID_W):
            tok = slice(r * GRID_W, (r + 1) * GRID_W)
            cos_ref[tok, :] = rot_row_ref[0, r:r + 1, :] + rot_col_ref[0]
            sin_ref[tok, :] = rot_row_ref[1, r:r + 1, :] + rot_col_ref[1]

    @pl.when(j == 0)
    def _():
        sh = _mod_slice(mod_ref, 0, 0)
        sc = _mod_slice(mod_ref, 0, 1)
        h_scr[...] = (_rms(x_ref[...], nw_ref[...]) * (1.0 + sc) + sh).astype(BF16)

    tm = h_scr.shape[0]

    def pieces(emit, rsplit=1):
        for s in range(D_MODEL // DK):
            cols = slice(s * DK, (s + 1) * DK)
            for r in range(rsplit):
                rows = slice(r * tm // rsplit, (r + 1) * tm // rsplit)
                emit(rows, cols, _dot(h_scr[rows, :], w_ref[:, cols]))

    def store(fn):
        def emit(rows, cols, acc):
            o_ref[rows, cols] = fn(acc).astype(BF16)
        return emit

    @pl.when(j == COL_U)
    def _():
        pieces(store(_gelu))

    @pl.when(j == COL_V)
    def _():
        def emit(rows, cols, acc):
            g_scr[rows, cols] = _gelu(acc)
        pieces(emit)
        g = g_scr[...]
        mu = jnp.mean(g, axis=-1, keepdims=True)
        d = g - mu
        var = jnp.mean(d * d, axis=-1, keepdims=True)
        o_ref[...] = (d * lax.rsqrt(var + EPS) * lnw_ref[...] + lnb_ref[...]).astype(BF16)

    def rope(scale):
        half = DK // 2

        def emit(rows, cols, acc):
            cos = cos_ref[rows, :]
            sin = sin_ref[rows, :]
            x1 = acc[:, :half]
            x2 = acc[:, half:]
            o_ref[rows, cols.start:cols.start + half] = ((x1 * cos - x2 * sin) * scale).astype(BF16)
            o_ref[rows, cols.start + half:cols.stop] = ((x1 * sin + x2 * cos) * scale).astype(BF16)
        pieces(emit)

    @pl.when(j == COL_Q)
    def _():
        rope(1.0)

    @pl.when(j == COL_K)
    def _():
        rope(DK ** -0.5)

    @pl.when((j >= COL_VR) & (j < COL_GF))
    def _():
        pieces(store(lambda a: a))

    @pl.when((j >= COL_GF) & (j < COL_GA))
    def _():
        pieces(store(lambda a: a * jax.nn.sigmoid(a)), rsplit=2)

    @pl.when(j >= COL_GA)
    def _():
        pieces(store(jax.nn.sigmoid), rsplit=2)


def _inproj_call(x2d, nw, mod, w_in_bf, rot_row, rot_col, lnw, lnb):
    T = x2d.shape[0]
    tm = 1024
    half = DK // 2
    return pl.pallas_call(
        _inproj_kernel,
        grid=(T // tm, IN_WIDTH // D_MODEL),
        in_specs=[
            pl.BlockSpec((tm, D_MODEL), lambda i, j: (i, 0)),
            pl.BlockSpec((1, D_MODEL), lambda i, j: (0, 0)),
            pl.BlockSpec(mod.shape, lambda i, j: (0, 0)),
            pl.BlockSpec((D_MODEL, D_MODEL), lambda i, j: (0, j)),
            pl.BlockSpec((2, tm // GRID_W, half), lambda i, j: (0, i, 0)),
            pl.BlockSpec((2, GRID_W, half), lambda i, j: (0, 0, 0)),
            pl.BlockSpec((1, D_MODEL), lambda i, j: (0, 0)),
            pl.BlockSpec((1, D_MODEL), lambda i, j: (0, 0)),
        ],
        out_specs=pl.BlockSpec((tm, D_MODEL), lambda i, j: (i, j)),
        out_shape=jax.ShapeDtypeStruct((T, IN_WIDTH), BF16),
        scratch_shapes=[
            pltpu.VMEM((tm, D_MODEL), BF16),
            pltpu.VMEM((tm, D_MODEL), F32),
            pltpu.VMEM((tm, half), F32),
            pltpu.VMEM((tm, half), F32),
        ],
        compiler_params=_params(("arbitrary", "arbitrary")),
        name="in_proj",
    )(x2d, nw, mod, w_in_bf, rot_row, rot_col, lnw, lnb)


def _sgu_kernel(u_ref, v_ref, ga_ref, ws_ref, bt_ref, wpa_ref, o_ref, ya_scr):
    tm = u_ref.shape[0]
    gd = D_MODEL // SGU_GROUPS
    for n in range(tm // SGU_CHUNK):
        rows = slice(n * SGU_CHUNK, (n + 1) * SGU_CHUNK)
        for g in range(SGU_GROUPS):
            cols = slice(g * gd, (g + 1) * gd)
            mixed = _dot(ws_ref[g], v_ref[rows, cols]) + bt_ref[:, g:g + 1]
            ya_scr[rows, cols] = (u_ref[rows, cols].astype(F32) * mixed).astype(BF16)
    pa = _dot(ya_scr[...], wpa_ref[...])
    o_ref[...] = (ga_ref[...].astype(F32) * pa).astype(BF16)


def _sgu_call(proj, ws_bf, bt, wpa_bf):
    T = proj.shape[0]
    tm = 512
    return pl.pallas_call(
        _sgu_kernel,
        grid=(T // tm,),
        in_specs=[
            pl.BlockSpec((tm, D_MODEL), lambda i: (i, COL_U)),
            pl.BlockSpec((tm, D_MODEL), lambda i: (i, COL_V)),
            pl.BlockSpec((tm, D_MODEL), lambda i: (i, COL_GA)),
            pl.BlockSpec(ws_bf.shape, lambda i: (0, 0, 0)),
            pl.BlockSpec(bt.shape, lambda i: (0, 0)),
            pl.BlockSpec((D_MODEL, D_MODEL), lambda i: (0, 0)),
        ],
        out_specs=pl.BlockSpec((tm, D_MODEL), lambda i: (i, 0)),
        out_shape=jax.ShapeDtypeStruct((T, D_MODEL), BF16),
        scratch_shapes=[pltpu.VMEM((tm, D_MODEL), BF16)],
        compiler_params=_params(("arbitrary",)),
        name="sgu_proj_a",
    )(proj, proj, proj, ws_bf, bt, wpa_bf)


def _ret_kernel(qf_ref, kf_ref, vf_ref, gf_ref, qb_ref, kb_ref, vb_ref, gb_ref, s0_ref, dec_ref,
                yf_ref, yb_ref, s_scr, intra_scr, qd_scr, kd_scr, cd_scr):
    C = RET_CHUNK
    i = pl.program_id(0)

    @pl.when(i == 0)
    def _():
        lg = jnp.log1p(-jnp.exp2(dec_ref[...]))
        r = lax.broadcasted_iota(jnp.int32, (C, C), 0).astype(F32)
        c = lax.broadcasted_iota(jnp.int32, (C, C), 1).astype(F32)
        pos = lax.broadcasted_iota(jnp.int32, (C, 1), 0).astype(F32)
        for h in range(HEADS):
            lf = lg[0:1, h:h + 1]
            lb = lg[1:2, h:h + 1]
            intra_scr[0, h] = jnp.where(r >= c, jnp.exp(jnp.maximum(r - c, 0.0) * lf), 0.0)
            intra_scr[1, h] = jnp.where(c >= r, jnp.exp(jnp.maximum(c - r, 0.0) * lb), 0.0)
            qd_scr[0, h] = jnp.exp((pos + 1.0) * lf)
            qd_scr[1, h] = jnp.exp((C - pos) * lb)
            kd_scr[0, h] = jnp.exp((C - 1.0 - pos) * lf)
            kd_scr[1, h] = jnp.exp(pos * lb)
            cd_scr[0, h] = jnp.exp(C * lf)
            cd_scr[1, h] = jnp.exp(C * lb)
        s_scr[...] = s0_ref[...]

    dirs = ((qf_ref, kf_ref, vf_ref, gf_ref, yf_ref), (qb_ref, kb_ref, vb_ref, gb_ref, yb_ref))
    for d, (q_ref, k_ref, v_ref, g_ref, y_ref) in enumerate(dirs):
        for h in range(HEADS):
            q = q_ref[:, h * DK:(h + 1) * DK]
            k = k_ref[:, h * DK:(h + 1) * DK]
            v = v_ref[:, h * DV:(h + 1) * DV]
            s = s_scr[d, h]
            a = (_dot_nt(q, k) * intra_scr[d, h]).astype(BF16)
            o = _dot(a, v) + _dot(q, s.astype(BF16)) * qd_scr[d, h]
            hn = o * lax.rsqrt(jnp.mean(o * o, axis=-1, keepdims=True) + EPS)
            y_ref[:, h * DV:(h + 1) * DV] = (g_ref[:, h * DV:(h + 1) * DV].astype(F32) * hn).astype(BF16)
            kd = (k.astype(F32) * kd_scr[d, h]).astype(BF16)
            s_scr[d, h] = s * cd_scr[d, h] + _dot_tn(kd, v)


def _ret_call(proj, s0, dec):
    T = proj.shape[0]
    C = RET_CHUNK
    n = T // C
    qk_w = HEADS * DK
    v_w = HEADS * DV
    fwd = lambda col: (lambda i: (i, col))
    bwd = lambda col: (lambda i: (n - 1 - i, col))
    return pl.pallas_call(
        _ret_kernel,
        grid=(n,),
        in_specs=[
            pl.BlockSpec((C, qk_w), fwd(COL_Q)),
            pl.BlockSpec((C, qk_w), fwd(COL_K)),
            pl.BlockSpec((C, v_w), fwd(COL_VR // 2)),
            pl.BlockSpec((C, v_w), fwd(COL_GF // 2)),
            pl.BlockSpec((C, qk_w), bwd(COL_Q)),
            pl.BlockSpec((C, qk_w), bwd(COL_K)),
            pl.BlockSpec((C, v_w), bwd(COL_VR // 2)),
            pl.BlockSpec((C, v_w), bwd(COL_GB // 2)),
            pl.BlockSpec(s0.shape, lambda i: (0, 0, 0, 0)),
            pl.BlockSpec(dec.shape, lambda i: (0, 0)),
        ],
        out_specs=[
            pl.BlockSpec((C, v_w), lambda i: (i, 0)),
            pl.BlockSpec((C, v_w), lambda i: (n - 1 - i, 0)),
        ],
        out_shape=[jax.ShapeDtypeStruct((T, v_w), BF16), jax.ShapeDtypeStruct((T, v_w), BF16)],
        scratch_shapes=[
            pltpu.VMEM((2, HEADS, DK, DV), F32),
            pltpu.VMEM((2, HEADS, C, C), F32),
            pltpu.VMEM((2, HEADS, C, 1), F32),
            pltpu.VMEM((2, HEADS, C, 1), F32),
            pltpu.VMEM((2, HEADS, 1, 1), F32),
        ],
        compiler_params=_params(("arbitrary",)),
        name="retention",
    )(proj, proj, proj, proj, proj, proj, proj, proj, s0, dec)


def _merge_kernel(yf_ref, yb_ref, za_ref, gbr_ref, x_ref, mod_ref, nw_ref, wpb_ref, wo_ref, rw_ref, rb_ref,
                  x1_ref, h2_ref, lg_ref):
    yb = yf_ref[...] + yb_ref[...]
    pb = _dot(yb, wpb_ref[...])
    y = za_ref[...].astype(F32) + gbr_ref[...].astype(F32) * pb
    yo = _dot(y.astype(BF16), wo_ref[...])
    x1 = x_ref[...] + _mod_slice(mod_ref, 0, 2) * yo
    x1_ref[...] = x1
    h2 = _rms(x1, nw_ref[...]) * (1.0 + _mod_slice(mod_ref, 0, 4)) + _mod_slice(mod_ref, 0, 3)
    h2_hi = h2.astype(BF16)
    h2_ref[...] = h2_hi
    h2_lo = (h2 - h2_hi.astype(F32)).astype(BF16)
    rw = rw_ref[...]
    rw_hi = rw.astype(BF16)
    rw_lo = (rw - rw_hi.astype(F32)).astype(BF16)
    both = _dot_nt(jnp.concatenate([rw_hi, rw_lo], axis=0), h2_hi)
    lg_ref[...] = both[:N_EXPERTS] + both[N_EXPERTS:] + _dot_nt(rw_hi, h2_lo) + rb_ref[...]


def _merge_call(yf, yb, za, proj, x2d, mod, nw2, wpb_bf, wo_bf, rw, rb):
    T = x2d.shape[0]
    tm = 512
    row = lambda i: (i, 0)
    fixed = lambda i: (0, 0)
    return pl.pallas_call(
        _merge_kernel,
        grid=(T // tm,),
        in_specs=[
            pl.BlockSpec((tm, HEADS * DV), row),
            pl.BlockSpec((tm, HEADS * DV), row),
            pl.BlockSpec((tm, D_MODEL), row),
            pl.BlockSpec((tm, D_MODEL), lambda i: (i, COL_GBR)),
            pl.BlockSpec((tm, D_MODEL), row),
            pl.BlockSpec(mod.shape, fixed),
            pl.BlockSpec((1, D_MODEL), fixed),
            pl.BlockSpec(wpb_bf.shape, fixed),
            pl.BlockSpec(wo_bf.shape, fixed),
            pl.BlockSpec(rw.shape, fixed),
            pl.BlockSpec(rb.shape, fixed),
        ],
        out_specs=[
            pl.BlockSpec((tm, D_MODEL), row),
            pl.BlockSpec((tm, D_MODEL), row),
            pl.BlockSpec((N_EXPERTS, tm), lambda i: (0, i)),
        ],
        out_shape=[
            jax.ShapeDtypeStruct((T, D_MODEL), F32),
            jax.ShapeDtypeStruct((T, D_MODEL), BF16),
            jax.ShapeDtypeStruct((N_EXPERTS, T), F32),
        ],
        compiler_params=_params(("arbitrary",)),
        name="merge_router",
    )(yf, yb, za, proj, x2d, mod, nw2, wpb_bf, wo_bf, rw, rb)


def _route_kernel(lg_ref, pos_ref, p_ref, nch_ref, su_scr):
    E, tt = lg_ref.shape

    @pl.when(pl.program_id(0) == 0)
    def _():
        r = lax.broadcasted_iota(jnp.int32, (tt, tt), 0)
        c = lax.broadcasted_iota(jnp.int32, (tt, tt), 1)
        su_scr[...] = jnp.where(r < c, 1.0, 0.0).astype(BF16)

    sub = lax.broadcasted_iota(jnp.int32, (E, tt), 0)
    work = lg_ref[...]
    vals, hots = [], []
    for _ in range(TOP_K):
        m = jnp.max(work, axis=0, keepdims=True)
        first = jnp.min(jnp.where(work == m, sub, E), axis=0, keepdims=True)
        hot = sub == first
        vals.append(m)
        hots.append(hot)
        work = jnp.where(hot, -jnp.inf, work)
    exps = [jnp.exp(v - vals[0]) for v in vals]
    inv = 1.0 / functools.reduce(lambda a, b: a + b, exps)

    member = functools.reduce(lambda a, b: a + b, [jnp.where(h, 1.0, 0.0) for h in hots])
    cnt = jnp.sum(member, axis=1, keepdims=True)
    nch = jnp.floor((cnt + (MOE_CH - 1.0)) * (1.0 / MOE_CH))
    nch_b = jnp.broadcast_to(nch, (E, 128))
    rank = _dot(member.astype(BF16), su_scr[...])
    er = lax.broadcasted_iota(jnp.int32, (E, E), 0)
    ec = lax.broadcasted_iota(jnp.int32, (E, E), 1)
    first_chunk = _dot(jnp.where(ec < er, 1.0, 0.0).astype(BF16), nch_b.astype(BF16))[:, 0:1]
    base = first_chunk * MOE_CH + rank
    for k in range(TOP_K):
        pos_ref[k:k + 1, :] = jnp.sum(jnp.where(hots[k], base, 0.0), axis=0, keepdims=True).astype(jnp.int32)
        p_ref[k:k + 1, :] = exps[k] * inv
    nch_ref[0] = nch_b.astype(jnp.int32)


def _route_call(logits_t):
    E, T = logits_t.shape
    tt = MOE_TT
    return pl.pallas_call(
        _route_kernel,
        grid=(T // tt,),
        in_specs=[pl.BlockSpec((E, tt), lambda j: (0, j))],
        out_specs=[
            pl.BlockSpec((TOP_K, tt), lambda j: (0, j)),
            pl.BlockSpec((TOP_K, tt), lambda j: (0, j)),
            pl.BlockSpec((1, E, 128), lambda j: (j, 0, 0)),
        ],
        out_shape=[
            jax.ShapeDtypeStruct((TOP_K, T), jnp.int32),
            jax.ShapeDtypeStruct((TOP_K, T), F32),
            jax.ShapeDtypeStruct((T // tt, E, 128), jnp.int32),
        ],
        scratch_shapes=[pltpu.VMEM((tt, tt), BF16)],
        compiler_params=_params(("arbitrary",)),
        name="moe_route",
    )(logits_t)


def _slot_tiles_max(n_tiles):
    return -(-(n_tiles * MOE_MAXC + N_EXPERTS * (MOE_CPT - 1)) // MOE_CPT)


def _routing_tables(nch):
    n_tiles, E = nch.shape

    def cumsum_last(a):
        m = a.shape[-1]
        keep = jnp.arange(m)[:, None] <= jnp.arange(m)[None, :]
        return jnp.sum(jnp.where(keep, a[..., :, None], 0), axis=-2)

    tot = jnp.sum(nch, axis=0)
    seg = (tot + MOE_CPT - 1) // MOE_CPT * MOE_CPT
    seg_end = cumsum_last(seg)
    seg_start = seg_end - seg
    gstart = seg_start[None, :] + cumsum_last(nch.T).T - nch
    l_end = cumsum_last(nch)
    lstart = l_end - nch

    def copy_list(count, first_local, first_global, step, max_len):
        end = cumsum_last(count)
        start = end - count
        i = jnp.arange(max_len, dtype=jnp.int32)[None, :, None]
        owner = (start[:, None, :] <= i) & (i < end[:, None, :])
        off = (i - start[:, None, :]) * step
        src = jnp.sum(jnp.where(owner, first_local[:, None, :] + off, 0), axis=-1)
        dst = jnp.sum(jnp.where(owner, first_global[:, None, :] + off, 0), axis=-1)
        return src, dst, end[:, -1]

    runs = nch // MOE_RUN
    rsrc, rdst, nruns = copy_list(runs, lstart, gstart, MOE_RUN, MOE_MAXQ)
    ssrc, sdst, nsingles = copy_list(nch - runs * MOE_RUN, lstart + runs * MOE_RUN, gstart + runs * MOE_RUN, 1,
                                     MOE_MAXS)
    copies = jnp.concatenate([rsrc, rdst, ssrc, sdst], axis=1).astype(jnp.int32).reshape(-1)
    ncopies = jnp.stack([nruns, nsingles], axis=1).astype(jnp.int32).reshape(-1)
    nt_max = _slot_tiles_max(n_tiles)
    n_used = (seg_end[-1] // MOE_CPT).astype(jnp.int32)
    tile_blk = jnp.minimum(jnp.arange(nt_max, dtype=jnp.int32), n_used - 1)
    tile_exp = jnp.sum(seg_end[None, :] <= (tile_blk * MOE_CPT)[:, None], axis=1)
    tile_exp = jnp.minimum(tile_exp, E - 1).astype(jnp.int32)
    experts = jnp.arange(E, dtype=jnp.int32)
    own = tile_exp[:, None] == experts[None, :]
    end_tile = jnp.sum(jnp.where(own, (seg_end // MOE_CPT)[None, :], 0), axis=1)
    follows = ((seg_start // MOE_CPT)[None, :] == end_tile[:, None]) & (seg > 0)[None, :]
    next_exp = jnp.where(jnp.any(follows, axis=1), jnp.sum(jnp.where(follows, experts[None, :], 0), axis=1), -1)
    next_exp = next_exp.astype(jnp.int32)
    zstart = (seg_start + tot).astype(jnp.int32)
    zcount = (seg - tot).astype(jnp.int32)
    return copies, ncopies, zstart, zcount, tile_exp, tile_blk, n_used.reshape(1), next_exp


def _chunk_rows(c, nchunks=1):
    return pl.ds(pl.multiple_of(c * MOE_CH, MOE_CH), nchunks * MOE_CH)


def _for_tile_copies(cp_ref, cn_ref, tile, make, op):
    base = tile * MOE_CPROW

    def run(i, carry):
        op(make(cp_ref[base + i], cp_ref[base + MOE_MAXQ + i], MOE_RUN))
        return carry
    lax.fori_loop(0, cn_ref[2 * tile], run, 0)

    def single(i, carry):
        op(make(cp_ref[base + 2 * MOE_MAXQ + i], cp_ref[base + 2 * MOE_MAXQ + MOE_MAXS + i], 1))
        return carry
    lax.fori_loop(0, cn_ref[2 * tile + 1], single, 0)


def _dispatch_kernel(cp_ref, cn_ref, zstart_ref, zcount_ref, nu_ref, h2_ref, pos_ref, xs_hbm, xl_scr, zero_scr,
                     sems, zsem):
    j = pl.program_id(0)
    n = pl.num_programs(0)
    slot = j % 2
    tt = h2_ref.shape[0]
    nt_max = xs_hbm.shape[0] // MOE_TM

    def tile_copies(tile, slot_, op):
        def make(local, glob, nchunks):
            return pltpu.make_async_copy(xl_scr.at[slot_, _chunk_rows(local, nchunks)],
                                         xs_hbm.at[_chunk_rows(glob, nchunks)], sems.at[slot_])
        _for_tile_copies(cp_ref, cn_ref, tile, make, op)

    def zero_chunk_copy(d):
        return pltpu.make_async_copy(zero_scr.at[pl.ds(0, MOE_CH)], xs_hbm.at[_chunk_rows(d)], zsem.at[0])

    def zero_tile_copy(t):
        rows = pl.ds(pl.multiple_of(t * MOE_TM, MOE_TM), MOE_TM)
        return pltpu.make_async_copy(zero_scr, xs_hbm.at[rows], zsem.at[0])

    def zero_fill(op):
        def expert(e, carry):
            def chunk(r, carry_):
                op(zero_chunk_copy(zstart_ref[e] + r))
                return carry_
            return lax.fori_loop(0, zcount_ref[e], chunk, carry)
        lax.fori_loop(0, N_EXPERTS, expert, 0)

        def tile(t, carry):
            op(zero_tile_copy(t))
            return carry
        lax.fori_loop(nu_ref[0], nt_max, tile, 0)

    def wait_tile(tile, slot_):
        tile_copies(tile, slot_, lambda cp: cp.wait())

    @pl.when(j == 0)
    def _():
        zero_scr[...] = jnp.zeros_like(zero_scr)
        zero_fill(lambda cp: cp.start())

    @pl.when(j >= 2)
    def _():
        wait_tile(j - 2, slot)

    pos = pos_ref[...]
    h2 = h2_ref[...]
    for rb in range(MOE_RLOC // MOE_RB):
        io = lax.broadcasted_iota(jnp.int32, (MOE_RB, tt), 0) + rb * MOE_RB
        onehot = jnp.zeros((MOE_RB, tt), F32)
        for k in range(TOP_K):
            onehot = jnp.where(io == pos[k:k + 1, :], 1.0, onehot)
        xl_scr[slot, rb * MOE_RB:(rb + 1) * MOE_RB, :] = _dot(onehot.astype(BF16), h2).astype(BF16)

    tile_copies(j, slot, lambda cp: cp.start())

    @pl.when(j == n - 1)
    def _():
        @pl.when(j >= 1)
        def _():
            wait_tile(j - 1, 1 - slot)
        wait_tile(j, slot)
        zero_fill(lambda cp: cp.wait())


def _dispatch_call(tables, h2, pos_t, n_slots):
    T = h2.shape[0]
    tt = MOE_TT
    copies, ncopies, zstart, zcount, n_used = tables
    grid_spec = pltpu.PrefetchScalarGridSpec(
        num_scalar_prefetch=5,
        grid=(T // tt,),
        in_specs=[
            pl.BlockSpec((tt, D_MODEL), lambda j, *_: (j, 0)),
            pl.BlockSpec((TOP_K, tt), lambda j, *_: (0, j)),
        ],
        out_specs=pl.BlockSpec(memory_space=pl.ANY),
        scratch_shapes=[
            pltpu.VMEM((2, MOE_RLOC, D_MODEL), BF16),
            pltpu.VMEM((MOE_TM, D_MODEL), BF16),
            pltpu.SemaphoreType.DMA((2,)),
            pltpu.SemaphoreType.DMA((1,)),
        ],
    )
    return pl.pallas_call(
        _dispatch_kernel,
        grid_spec=grid_spec,
        out_shape=jax.ShapeDtypeStruct((n_slots, D_MODEL), BF16),
        compiler_params=_params(("arbitrary",)),
        name="moe_dispatch",
    )(copies, ncopies, zstart, zcount, n_used, h2, pos_t)


def _experts_kernel(te_ref, tb_ref, nu_ref, nx_ref, xs_ref, w1_hbm, b1_ref, w2_hbm, b2_ref, y_ref,
                    w1_f32, w2_f32, w1_scr, w2_scr, sems):
    i = pl.program_id(0)
    valid = i < nu_ref[0]
    new_expert = (i == 0) | (te_ref[i] != te_ref[jnp.maximum(i - 1, 0)])

    def fetch(e):
        return (pltpu.make_async_copy(w1_hbm.at[e], w1_f32, sems.at[0]),
                pltpu.make_async_copy(w2_hbm.at[e], w2_f32, sems.at[1]))

    @pl.when(i == 0)
    def _():
        for cp in fetch(te_ref[0]):
            cp.start()

    @pl.when(valid & new_expert)
    def _():
        for cp in fetch(te_ref[i]):
            cp.wait()
        w1_scr[...] = w1_f32[...].astype(BF16)
        w2_scr[...] = w2_f32[...].astype(BF16)

        @pl.when(nx_ref[i] >= 0)
        def _():
            for cp in fetch(nx_ref[i]):
                cp.start()

    @pl.when(valid)
    def _():
        hh = _dot(xs_ref[...], w1_scr[...]) + b1_ref[...]
        gate = jnp.minimum(hh[:, :D_FF], SWIGLU_LIMIT)
        up = jnp.clip(hh[:, D_FF:], -SWIGLU_LIMIT, SWIGLU_LIMIT)
        act = (up + 1.0) * gate * jax.nn.sigmoid(SWIGLU_ALPHA * gate)
        y_ref[...] = (_dot(act.astype(BF16), w2_scr[...]) + b2_ref[...]).astype(BF16)

    @pl.when(jnp.logical_not(valid))
    def _():
        y_ref[...] = jnp.zeros_like(y_ref)


def _experts_call(tile_exp, tile_blk, n_used, next_exp, xs, w1, b1, w2, b2):
    nt_max = tile_exp.shape[0]
    slot_tile = lambda i, te, tb, nu, nx: (tb[i], 0)
    per_e = lambda i, te, tb, nu, nx: (te[i], 0, 0)
    grid_spec = pltpu.PrefetchScalarGridSpec(
        num_scalar_prefetch=4,
        grid=(nt_max,),
        in_specs=[
            pl.BlockSpec((MOE_TM, D_MODEL), slot_tile),
            pl.BlockSpec(memory_space=pl.ANY),
            pl.BlockSpec((None, 1, 2 * D_FF), per_e),
            pl.BlockSpec(memory_space=pl.ANY),
            pl.BlockSpec((None, 1, D_MODEL), per_e),
        ],
        out_specs=pl.BlockSpec((MOE_TM, D_MODEL), lambda i, te, tb, nu, nx: (i, 0)),
        scratch_shapes=[
            pltpu.VMEM((D_MODEL, 2 * D_FF), F32),
            pltpu.VMEM((D_FF, D_MODEL), F32),
            pltpu.VMEM((D_MODEL, 2 * D_FF), BF16),
            pltpu.VMEM((D_FF, D_MODEL), BF16),
            pltpu.SemaphoreType.DMA((2,)),
        ],
    )
    return pl.pallas_call(
        _experts_kernel,
        grid_spec=grid_spec,
        out_shape=jax.ShapeDtypeStruct(xs.shape, BF16),
        compiler_params=_params(("arbitrary",)),
        name="moe_experts",
    )(tile_exp, tile_blk, n_used, next_exp, xs, w1, b1, w2, b2)


def _combine_kernel(cp_ref, cn_ref, pos_ref, p_ref, x1_ref, mod_ref, fw_ref, y_hbm, o_ref, yl_scr, w_scr, sems):
    j = pl.program_id(0)
    n = pl.num_programs(0)
    slot = j % 2
    tt = x1_ref.shape[0]

    def tile_copies(tile, slot_, op):
        def make(local, glob, nchunks):
            return pltpu.make_async_copy(y_hbm.at[_chunk_rows(glob, nchunks)],
                                         yl_scr.at[slot_, _chunk_rows(local, nchunks)], sems.at[slot_])
        _for_tile_copies(cp_ref, cn_ref, tile, make, op)

    def start_tile(tile, slot_):
        tile_copies(tile, slot_, lambda cp: cp.start())

    @pl.when(j == 0)
    def _():
        yl_scr[...] = jnp.zeros_like(yl_scr)
        start_tile(0, 0)

    @pl.when(j + 1 < n)
    def _():
        start_tile(j + 1, 1 - slot)

    tile_copies(j, slot, lambda cp: cp.wait())

    pos = pos_ref[...]
    p = p_ref[...]
    for cb in range(MOE_RLOC // MOE_RB):
        io = lax.broadcasted_iota(jnp.int32, (tt, MOE_RB), 1) + cb * MOE_RB
        w = jnp.zeros((tt, MOE_RB), F32)
        for k in range(TOP_K):
            w = jnp.where(io == pos[:, k:k + 1], p[:, k:k + 1], w)
        w_scr[:, cb * MOE_RB:(cb + 1) * MOE_RB] = w.astype(BF16)
    moe = _dot(w_scr[...], yl_scr[slot])
    x2 = x1_ref[...] + _mod_slice(mod_ref, 0, 5) * moe
    o_ref[...] = _rms(x2, fw_ref[...])


def _combine_call(copies, ncopies, pos, p, x1, mod, fw, y):
    T = x1.shape[0]
    tt = MOE_TT
    row = lambda j, *_: (j, 0)
    fixed = lambda j, *_: (0, 0)
    grid_spec = pltpu.PrefetchScalarGridSpec(
        num_scalar_prefetch=2,
        grid=(T // tt,),
        in_specs=[
            pl.BlockSpec((tt, TOP_K), row),
            pl.BlockSpec((tt, TOP_K), row),
            pl.BlockSpec((tt, D_MODEL), row),
            pl.BlockSpec(mod.shape, fixed),
            pl.BlockSpec((1, D_MODEL), fixed),
            pl.BlockSpec(memory_space=pl.ANY),
        ],
        out_specs=pl.BlockSpec((tt, D_MODEL), row),
        scratch_shapes=[
            pltpu.VMEM((2, MOE_RLOC, D_MODEL), BF16),
            pltpu.VMEM((tt, MOE_RLOC), BF16),
            pltpu.SemaphoreType.DMA((2,)),
        ],
    )
    return pl.pallas_call(
        _combine_kernel,
        grid_spec=grid_spec,
        out_shape=jax.ShapeDtypeStruct((T, D_MODEL), F32),
        compiler_params=_params(("arbitrary",)),
        name="moe_combine",
    )(copies, ncopies, pos, p, x1, mod, fw, y)


def _rope_tables(T):
    rows = T // GRID_W
    n_freq = DK // 4
    inv = ROPE_BASE ** (-jnp.arange(n_freq, dtype=F32) / n_freq)
    row_ang = jnp.arange(rows, dtype=F32)[:, None] * inv
    col_ang = jnp.arange(GRID_W, dtype=F32)[:, None] * inv
    rot_row = jnp.pad(jnp.stack([jnp.cos(row_ang), jnp.sin(row_ang)]), ((0, 0), (0, 0), (0, n_freq)))
    rot_col = jnp.pad(jnp.stack([jnp.cos(col_ang), jnp.sin(col_ang)]), ((0, 0), (0, 0), (n_freq, 0)))
    return rot_row, rot_col


def kernel(x, c, ctx, c_ctx, w_mod, b_mod, norm1_w, norm2_w, w_in, sgu_ln_w, sgu_ln_b, sgu_w, sgu_b,
           ret_decay_fwd, ret_decay_bwd, w_proj_a, w_proj_b, w_out, router_w, router_b,
           moe_w1, moe_b1, moe_w2, moe_b2, final_norm_w):
    B, T, D = x.shape
    assert B == 1 and D == D_MODEL and w_mod.shape[0] == 1 and T % 1024 == 0
    x2d = x.reshape(T, D)
    cc = jnp.stack([c.reshape(D), c_ctx.reshape(D)], axis=1)
    mod = _mod_call(cc, w_mod[0], b_mod)
    dec = jnp.stack([ret_decay_fwd[0], ret_decay_bwd[0]], axis=0)
    w_in_bf = w_in[0].astype(BF16)
    s0 = _ctx_call(ctx.reshape(ctx.shape[1], D), norm1_w, mod, w_in_bf, dec)
    rot_row, rot_col = _rope_tables(T)
    proj = _inproj_call(x2d, norm1_w, mod, w_in_bf, rot_row, rot_col, sgu_ln_w, sgu_ln_b)
    za = _sgu_call(proj, sgu_w[0].astype(BF16), sgu_b[0].T, w_proj_a[0].astype(BF16))
    yf, yb = _ret_call(proj, s0, dec)
    x1, h2, logits_t = _merge_call(yf, yb, za, proj, x2d, mod, norm2_w, w_proj_b[0].astype(BF16),
                                   w_out[0].astype(BF16), router_w[0].T, router_b.reshape(N_EXPERTS, 1))
    pos_t, p_t, nch = _route_call(logits_t)
    copies, ncopies, zstart, zcount, tile_exp, tile_blk, n_used, next_exp = _routing_tables(nch[:, :, 0])
    n_slots = tile_exp.shape[0] * MOE_TM
    xs = _dispatch_call((copies, ncopies, zstart, zcount, n_used), h2, pos_t, n_slots)
    y = _experts_call(tile_exp, tile_blk, n_used, next_exp, xs, moe_w1[0], moe_b1[0][:, None, :], moe_w2[0],
                      moe_b2[0][:, None, :])
    out = _combine_call(copies, ncopies, pos_t.T, p_t.T, x1, mod, final_norm_w.reshape(1, D), y)
    return out.reshape(B, T, D)
```

```python
import functools

import jax
import jax.numpy as jnp
from jax import lax
from jax.experimental import pallas as pl
from jax.experimental.pallas import tpu as pltpu

D_MODEL = 1024
GRID_W = 64
SGU_CHUNK = 128
SGU_GROUPS = 8
HEADS = 4
DK = D_MODEL // HEADS
DV = 2 * DK
ROPE_BASE = 10000.0
N_EXPERTS = 32
TOP_K = 4
D_FF = D_MODEL
SWIGLU_LIMIT = 7.0
SWIGLU_ALPHA = 1.702
EPS = 1e-6
IN_WIDTH = 12 * D_MODEL
COL_U, COL_V, COL_Q, COL_K, COL_VR, COL_GF, COL_GB, COL_GA, COL_GBR = 0, 1, 2, 3, 4, 6, 8, 10, 11

RET_CHUNK = 256
VMEM_LIMIT = 56 * 1024 * 1024

MOE_TT = 512
MOE_CH = 16
MOE_TM = 512
MOE_CPT = MOE_TM // MOE_CH
MOE_MAXC = TOP_K * MOE_TT // MOE_CH + N_EXPERTS
MOE_RLOC = MOE_MAXC * MOE_CH
MOE_RB = 256
MOE_RUN = 4
MOE_MAXQ = MOE_MAXC // MOE_RUN
MOE_MAXS = (MOE_RUN - 1) * N_EXPERTS
MOE_CPROW = 2 * (MOE_MAXQ + MOE_MAXS)

F32 = jnp.float32
BF16 = jnp.bfloat16


def _params(sem):
    return pltpu.CompilerParams(dimension_semantics=sem, vmem_limit_bytes=VMEM_LIMIT)


def _dot(a, b):
    return jnp.dot(a, b, preferred_element_type=F32)


def _dot_nt(a, b):
    return lax.dot_general(a, b, (((1,), (1,)), ((), ())), preferred_element_type=F32)


def _dot_tn(a, b):
    return lax.dot_general(a, b, (((0,), (0,)), ((), ())), preferred_element_type=F32)


def _rms(x, w):
    return x * lax.rsqrt(jnp.mean(x * x, axis=-1, keepdims=True) + EPS) * w


def _gelu(x):
    return 0.5 * x * (1.0 + lax.erf(x * (2.0 ** -0.5)))


def _sigmoid(x):
    return 0.5 * jnp.tanh(0.5 * x) + 0.5


def _mod_slice(mod_ref, row, k):
    return mod_ref[row:row + 1, k * D_MODEL:(k + 1) * D_MODEL]


def _mod_kernel(cc_ref, w_ref, b_ref, o_ref):
    s = cc_ref[...]
    s = s * jax.nn.sigmoid(s)
    w = w_ref[...]
    r0 = jnp.sum(s[:, 0:1] * w, axis=0, keepdims=True)
    r1 = jnp.sum(s[:, 1:2] * w, axis=0, keepdims=True)
    o_ref[...] = jnp.concatenate([r0, r1], axis=0) + b_ref[...]


def _mod_call(cc, w_mod, b_mod):
    bn = 1536
    n = w_mod.shape[1]
    return pl.pallas_call(
        _mod_kernel,
        grid=(n // bn,),
        in_specs=[
            pl.BlockSpec((D_MODEL, 2), lambda j: (0, 0)),
            pl.BlockSpec((D_MODEL, bn), lambda j: (0, j)),
            pl.BlockSpec((1, bn), lambda j: (0, j)),
        ],
        out_specs=pl.BlockSpec((2, bn), lambda j: (0, j)),
        out_shape=jax.ShapeDtypeStruct((2, n), F32),
        compiler_params=_params(("arbitrary",)),
        name="mod_vectors",
    )(cc, w_mod, b_mod)


def _ctx_kernel(ctx_ref, nw_ref, mod_ref, w_ref, dec_ref, s0_ref):
    L = ctx_ref.shape[0]
    sh = _mod_slice(mod_ref, 1, 0)
    sc = _mod_slice(mod_ref, 1, 1)
    hc = (_rms(ctx_ref[...], nw_ref[...]) * (1.0 + sc) + sh).astype(BF16)
    kv = _dot(hc, w_ref[...])
    lg = jnp.log1p(-jnp.exp2(dec_ref[...]))
    pos = lax.broadcasted_iota(jnp.int32, (L, 1), 0).astype(F32)
    for h in range(HEADS):
        k = kv[:, h * DK:(h + 1) * DK] * (DK ** -0.5)
        v = kv[:, HEADS * DK + h * DV:HEADS * DK + (h + 1) * DV].astype(BF16)
        wf = jnp.exp((L - 1.0 - pos) * lg[0:1, h:h + 1])
        wb = jnp.exp(pos * lg[1:2, h:h + 1])
        s0_ref[0, h] = _dot_tn((k * wf).astype(BF16), v)
        s0_ref[1, h] = _dot_tn((k * wb).astype(BF16), v)


def _ctx_call(ctx2d, nw, mod, w_in_bf, dec):
    L = ctx2d.shape[0]
    wcols = HEADS * DK + HEADS * DV
    return pl.pallas_call(
        _ctx_kernel,
        grid=(1,),
        in_specs=[
            pl.BlockSpec((L, D_MODEL), lambda i: (0, 0)),
            pl.BlockSpec((1, D_MODEL), lambda i: (0, 0)),
            pl.BlockSpec(mod.shape, lambda i: (0, 0)),
            pl.BlockSpec((D_MODEL, wcols), lambda i: (0, COL_K * D_MODEL // wcols)),
            pl.BlockSpec(dec.shape, lambda i: (0, 0)),
        ],
        out_specs=pl.BlockSpec((2, HEADS, DK, DV), lambda i: (0, 0, 0, 0)),
        out_shape=jax.ShapeDtypeStruct((2, HEADS, DK, DV), F32),
        compiler_params=_params(("arbitrary",)),
        name="ctx_states",
    )(ctx2d, nw, mod, w_in_bf, dec)


def _inproj_kernel(x_ref, xn_ref, nw_ref, mod_ref, w_ref, rot_row_ref, rot_col_ref, lnw_ref, lnb_ref, o_ref,
                   h_scr, g_scr, cos_ref, sin_ref):
    i = pl.program_id(0)
    j = pl.program_id(1)
    cur = i % 2

    def normed(xr):
        sh = _mod_slice(mod_ref, 0, 0)
        sc = _mod_slice(mod_ref, 0, 1)
        return (_rms(xr[...], nw_ref[...]) * (1.0 + sc) + sh).astype(BF16)

    @pl.when(j == COL_Q)
    def _():
        for r in range(h_scr.shape[1] // GRID_W):
            tok = slice(r * GRID_W, (r + 1) * GRID_W)
            cos_ref[tok, :] = rot_row_ref[0, r:r + 1, :] + rot_col_ref[0]
            sin_ref[tok, :] = rot_row_ref[1, r:r + 1, :] + rot_col_ref[1]

    @pl.when((i == 0) & (j == 0))
    def _():
        h_scr[0] = normed(x_ref)

    tm = h_scr.shape[1]

    def pieces(emit, rsplit=1):
        for s in range(D_MODEL // DK):
            cols = slice(s * DK, (s + 1) * DK)
            for r in range(rsplit):
                rows = slice(r * tm // rsplit, (r + 1) * tm // rsplit)
                emit(rows, cols, _dot(h_scr[cur, rows, :], w_ref[:, cols]))

    def store(fn):
        def emit(rows, cols, acc):
            o_ref[rows, cols] = fn(acc).astype(BF16)
        return emit

    @pl.when(j == COL_U)
    def _():
        pieces(store(_gelu))

    @pl.when(j == COL_V)
    def _():
        def emit(rows, cols, acc):
            g_scr[rows, cols] = _gelu(acc)
        pieces(emit)
        g = g_scr[...]
        mu = jnp.mean(g, axis=-1, keepdims=True)
        d = g - mu
        var = jnp.mean(d * d, axis=-1, keepdims=True)
        o_ref[...] = (d * lax.rsqrt(var + EPS) * lnw_ref[...] + lnb_ref[...]).astype(BF16)

    def rope(scale):
        half = DK // 2

        def emit(rows, cols, acc):
            cos = cos_ref[rows, :]
            sin = sin_ref[rows, :]
            x1 = acc[:, :half]
            x2 = acc[:, half:]
            o_ref[rows, cols.start:cols.start + half] = ((x1 * cos - x2 * sin) * scale).astype(BF16)
            o_ref[rows, cols.start + half:cols.stop] = ((x1 * sin + x2 * cos) * scale).astype(BF16)
        pieces(emit)

    @pl.when(j == COL_Q)
    def _():
        rope(1.0)

    @pl.when(j == COL_K)
    def _():
        rope(DK ** -0.5)

    @pl.when((j >= COL_VR) & (j < COL_GF))
    def _():
        pieces(store(lambda a: a))

    @pl.when((j >= COL_GF) & (j < COL_GA))
    def _():
        pieces(store(lambda a: a * _sigmoid(a)), rsplit=2)

    @pl.when(j == COL_GA)
    def _():
        pieces(store(_sigmoid), rsplit=2)

    @pl.when(j == COL_GBR)
    def _():
        pieces(store(_sigmoid), rsplit=2)
        h_scr[1 - cur] = normed(xn_ref)


def _inproj_call(x2d, nw, mod, w_in_bf, rot_row, rot_col, lnw, lnb):
    T = x2d.shape[0]
    tm = 1024
    half = DK // 2
    return pl.pallas_call(
        _inproj_kernel,
        grid=(T // tm, IN_WIDTH // D_MODEL),
        in_specs=[
            pl.BlockSpec((tm, D_MODEL), lambda i, j: (i, 0)),
            pl.BlockSpec((tm, D_MODEL), lambda i, j: (jnp.minimum(i + 1, T // tm - 1), 0)),
            pl.BlockSpec((1, D_MODEL), lambda i, j: (0, 0)),
            pl.BlockSpec(mod.shape, lambda i, j: (0, 0)),
            pl.BlockSpec((D_MODEL, D_MODEL), lambda i, j: (0, j)),
            pl.BlockSpec((2, tm // GRID_W, half), lambda i, j: (0, i, 0)),
            pl.BlockSpec((2, GRID_W, half), lambda i, j: (0, 0, 0)),
            pl.BlockSpec((1, D_MODEL), lambda i, j: (0, 0)),
            pl.BlockSpec((1, D_MODEL), lambda i, j: (0, 0)),
        ],
        out_specs=pl.BlockSpec((tm, D_MODEL), lambda i, j: (i, j)),
        out_shape=jax.ShapeDtypeStruct((T, IN_WIDTH), BF16),
        scratch_shapes=[
            pltpu.VMEM((2, tm, D_MODEL), BF16),
            pltpu.VMEM((tm, D_MODEL), F32),
            pltpu.VMEM((tm, half), F32),
            pltpu.VMEM((tm, half), F32),
        ],
        compiler_params=_params(("arbitrary", "arbitrary")),
        name="in_proj",
    )(x2d, x2d, nw, mod, w_in_bf, rot_row, rot_col, lnw, lnb)


def _sgu_kernel(u_ref, v_ref, ga_ref, ws_ref, bt_ref, wpa_ref, o_ref, ya_scr):
    tm = u_ref.shape[0]
    gd = D_MODEL // SGU_GROUPS
    for n in range(tm // SGU_CHUNK):
        rows = slice(n * SGU_CHUNK, (n + 1) * SGU_CHUNK)
        for g in range(SGU_GROUPS):
            cols = slice(g * gd, (g + 1) * gd)
            mixed = _dot(ws_ref[g], v_ref[rows, cols]) + bt_ref[:, g:g + 1]
            ya_scr[rows, cols] = (u_ref[rows, cols].astype(F32) * mixed).astype(BF16)
    pa = _dot(ya_scr[...], wpa_ref[...])
    o_ref[...] = (ga_ref[...].astype(F32) * pa).astype(BF16)


def _sgu_call(proj, ws_bf, bt, wpa_bf):
    T = proj.shape[0]
    tm = 512
    return pl.pallas_call(
        _sgu_kernel,
        grid=(T // tm,),
        in_specs=[
            pl.BlockSpec((tm, D_MODEL), lambda i: (i, COL_U)),
            pl.BlockSpec((tm, D_MODEL), lambda i: (i, COL_V)),
            pl.BlockSpec((tm, D_MODEL), lambda i: (i, COL_GA)),
            pl.BlockSpec(ws_bf.shape, lambda i: (0, 0, 0)),
            pl.BlockSpec(bt.shape, lambda i: (0, 0)),
            pl.BlockSpec((D_MODEL, D_MODEL), lambda i: (0, 0)),
        ],
        out_specs=pl.BlockSpec((tm, D_MODEL), lambda i: (i, 0)),
        out_shape=jax.ShapeDtypeStruct((T, D_MODEL), BF16),
        scratch_shapes=[pltpu.VMEM((tm, D_MODEL), BF16)],
        compiler_params=_params(("arbitrary",)),
        name="sgu_proj_a",
    )(proj, proj, proj, ws_bf, bt, wpa_bf)


def _ret_kernel(qf_ref, kf_ref, vf_ref, gf_ref, qb_ref, kb_ref, vb_ref, gb_ref, s0_ref, dec_ref,
                yf_ref, yb_ref, s_scr, intra_scr, qd_scr, kd_scr, cd_scr):
    C = RET_CHUNK
    i = pl.program_id(0)

    @pl.when(i == 0)
    def _():
        lg = jnp.log1p(-jnp.exp2(dec_ref[...]))
        r = lax.broadcasted_iota(jnp.int32, (C, C), 0).astype(F32)
        c = lax.broadcasted_iota(jnp.int32, (C, C), 1).astype(F32)
        pos = lax.broadcasted_iota(jnp.int32, (C, 1), 0).astype(F32)
        for h in range(HEADS):
            lf = lg[0:1, h:h + 1]
            lb = lg[1:2, h:h + 1]
            intra_scr[0, h] = jnp.where(r >= c, jnp.exp(jnp.maximum(r - c, 0.0) * lf), 0.0)
            intra_scr[1, h] = jnp.where(c >= r, jnp.exp(jnp.maximum(c - r, 0.0) * lb), 0.0)
            wide = lambda col: jnp.broadcast_to(col, (C, DK)).astype(BF16)
            qd_scr[0, h] = wide(jnp.exp((pos + 1.0) * lf))
            qd_scr[1, h] = wide(jnp.exp((C - pos) * lb))
            kd_scr[0, h] = wide(jnp.exp((C - 1.0 - pos) * lf))
            kd_scr[1, h] = wide(jnp.exp(pos * lb))
            cd_scr[0, h] = jnp.exp(C * lf)
            cd_scr[1, h] = jnp.exp(C * lb)
        s_scr[...] = s0_ref[...]

    dirs = ((qf_ref, kf_ref, vf_ref, gf_ref, yf_ref), (qb_ref, kb_ref, vb_ref, gb_ref, yb_ref))
    for d, (q_ref, k_ref, v_ref, g_ref, y_ref) in enumerate(dirs):
        for h in range(HEADS):
            q = q_ref[:, h * DK:(h + 1) * DK]
            k = k_ref[:, h * DK:(h + 1) * DK]
            v = v_ref[:, h * DV:(h + 1) * DV]
            s = s_scr[d, h]
            a = (_dot_nt(q, k) * intra_scr[d, h]).astype(BF16)
            lhs = jnp.concatenate([a, q * qd_scr[d, h]], axis=1)
            rhs = jnp.concatenate([v, s.astype(BF16)], axis=0)
            o = _dot(lhs, rhs)
            hn = (o * lax.rsqrt(jnp.mean(o * o, axis=-1, keepdims=True) + EPS)).astype(BF16)
            y_ref[:, h * DV:(h + 1) * DV] = g_ref[:, h * DV:(h + 1) * DV] * hn
            s_scr[d, h] = s * cd_scr[d, h] + _dot_tn(k * kd_scr[d, h], v)


def _ret_call(proj, s0, dec):
    T = proj.shape[0]
    C = RET_CHUNK
    n = T // C
    qk_w = HEADS * DK
    v_w = HEADS * DV
    fwd = lambda col: (lambda i: (i, col))
    bwd = lambda col: (lambda i: (n - 1 - i, col))
    return pl.pallas_call(
        _ret_kernel,
        grid=(n,),
        in_specs=[
            pl.BlockSpec((C, qk_w), fwd(COL_Q)),
            pl.BlockSpec((C, qk_w), fwd(COL_K)),
            pl.BlockSpec((C, v_w), fwd(COL_VR // 2)),
            pl.BlockSpec((C, v_w), fwd(COL_GF // 2)),
            pl.BlockSpec((C, qk_w), bwd(COL_Q)),
            pl.BlockSpec((C, qk_w), bwd(COL_K)),
            pl.BlockSpec((C, v_w), bwd(COL_VR // 2)),
            pl.BlockSpec((C, v_w), bwd(COL_GB // 2)),
            pl.BlockSpec(s0.shape, lambda i: (0, 0, 0, 0)),
            pl.BlockSpec(dec.shape, lambda i: (0, 0)),
        ],
        out_specs=[
            pl.BlockSpec((C, v_w), lambda i: (i, 0)),
            pl.BlockSpec((C, v_w), lambda i: (n - 1 - i, 0)),
        ],
        out_shape=[jax.ShapeDtypeStruct((T, v_w), BF16), jax.ShapeDtypeStruct((T, v_w), BF16)],
        scratch_shapes=[
            pltpu.VMEM((2, HEADS, DK, DV), F32),
            pltpu.VMEM((2, HEADS, C, C), F32),
            pltpu.VMEM((2, HEADS, C, DK), BF16),
            pltpu.VMEM((2, HEADS, C, DK), BF16),
            pltpu.VMEM((2, HEADS, 1, 1), F32),
        ],
        compiler_params=_params(("arbitrary",)),
        name="retention",
    )(proj, proj, proj, proj, proj, proj, proj, proj, s0, dec)


def _merge_kernel(yf_ref, yb_ref, za_ref, gbr_ref, x_ref, mod_ref, nw_ref, wpb_ref, wo_ref, rw_ref, rb_ref,
                  x1_ref, h2_ref, lg_ref):
    yb = yf_ref[...] + yb_ref[...]
    pb = _dot(yb, wpb_ref[...])
    y = za_ref[...].astype(F32) + gbr_ref[...].astype(F32) * pb
    yo = _dot(y.astype(BF16), wo_ref[...])
    x1 = x_ref[...] + _mod_slice(mod_ref, 0, 2) * yo
    x1_ref[...] = x1
    h2 = _rms(x1, nw_ref[...]) * (1.0 + _mod_slice(mod_ref, 0, 4)) + _mod_slice(mod_ref, 0, 3)
    h2_hi = h2.astype(BF16)
    h2_ref[...] = h2_hi
    h2_lo = (h2 - h2_hi.astype(F32)).astype(BF16)
    rw = rw_ref[...]
    rw_hi = rw.astype(BF16)
    rw_lo = (rw - rw_hi.astype(F32)).astype(BF16)
    both = _dot_nt(jnp.concatenate([rw_hi, rw_lo], axis=0), h2_hi)
    lg_ref[...] = both[:N_EXPERTS] + both[N_EXPERTS:] + _dot_nt(rw_hi, h2_lo) + rb_ref[...]


def _merge_call(yf, yb, za, proj, x2d, mod, nw2, wpb_bf, wo_bf, rw, rb):
    T = x2d.shape[0]
    tm = 512
    row = lambda i: (i, 0)
    fixed = lambda i: (0, 0)
    return pl.pallas_call(
        _merge_kernel,
        grid=(T // tm,),
        in_specs=[
            pl.BlockSpec((tm, HEADS * DV), row),
            pl.BlockSpec((tm, HEADS * DV), row),
            pl.BlockSpec((tm, D_MODEL), row),
            pl.BlockSpec((tm, D_MODEL), lambda i: (i, COL_GBR)),
            pl.BlockSpec((tm, D_MODEL), row),
            pl.BlockSpec(mod.shape, fixed),
            pl.BlockSpec((1, D_MODEL), fixed),
            pl.BlockSpec(wpb_bf.shape, fixed),
            pl.BlockSpec(wo_bf.shape, fixed),
            pl.BlockSpec(rw.shape, fixed),
            pl.BlockSpec(rb.shape, fixed),
        ],
        out_specs=[
            pl.BlockSpec((tm, D_MODEL), row),
            pl.BlockSpec((tm, D_MODEL), row),
            pl.BlockSpec((N_EXPERTS, tm), lambda i: (0, i)),
        ],
        out_shape=[
            jax.ShapeDtypeStruct((T, D_MODEL), F32),
            jax.ShapeDtypeStruct((T, D_MODEL), BF16),
            jax.ShapeDtypeStruct((N_EXPERTS, T), F32),
        ],
        compiler_params=_params(("arbitrary",)),
        name="merge_router",
    )(yf, yb, za, proj, x2d, mod, nw2, wpb_bf, wo_bf, rw, rb)


def _route_kernel(lg_ref, pos_ref, tok_ref, nch_ref, su_scr):
    E, tt = lg_ref.shape

    @pl.when(pl.program_id(0) == 0)
    def _():
        r = lax.broadcasted_iota(jnp.int32, (tt, tt), 0)
        c = lax.broadcasted_iota(jnp.int32, (tt, tt), 1)
        su_scr[...] = jnp.where(r < c, 1.0, 0.0).astype(BF16)

    sub = lax.broadcasted_iota(jnp.int32, (E, tt), 0)
    work = lg_ref[...]
    vals, hots = [], []
    for _ in range(TOP_K):
        m = jnp.max(work, axis=0, keepdims=True)
        first = jnp.min(jnp.where(work == m, sub, E), axis=0, keepdims=True)
        hot = sub == first
        vals.append(m)
        hots.append(hot)
        work = jnp.where(hot, -jnp.inf, work)
    exps = [jnp.exp(v - vals[0]) for v in vals]
    inv = 1.0 / functools.reduce(lambda a, b: a + b, exps)

    member = functools.reduce(lambda a, b: a + b, [jnp.where(h, 1.0, 0.0) for h in hots])
    cnt = jnp.sum(member, axis=1, keepdims=True)
    nch = jnp.floor((cnt + (MOE_CH - 1.0)) * (1.0 / MOE_CH))
    nch_b = jnp.broadcast_to(nch, (E, 128))
    rank = _dot(member.astype(BF16), su_scr[...])
    er = lax.broadcasted_iota(jnp.int32, (E, E), 0)
    ec = lax.broadcasted_iota(jnp.int32, (E, E), 1)
    first_chunk = _dot(jnp.where(ec < er, 1.0, 0.0).astype(BF16), nch_b.astype(BF16))[:, 0:1]
    base = first_chunk * MOE_CH + rank
    pos = [jnp.sum(jnp.where(hots[k], base, 0.0), axis=0, keepdims=True) for k in range(TOP_K)]
    for k in range(TOP_K):
        pos_ref[k:k + 1, :] = pos[k].astype(jnp.int32)
    rows = jnp.concatenate(pos + [e * inv for e in exps] + [jnp.zeros((128 - 2 * TOP_K, tt), F32)], axis=0)
    tok_ref[...] = rows.T
    nch_ref[0] = nch_b.astype(jnp.int32)


def _route_call(logits_t):
    E, T = logits_t.shape
    tt = MOE_TT
    return pl.pallas_call(
        _route_kernel,
        grid=(T // tt,),
        in_specs=[pl.BlockSpec((E, tt), lambda j: (0, j))],
        out_specs=[
            pl.BlockSpec((TOP_K, tt), lambda j: (0, j)),
            pl.BlockSpec((tt, 128), lambda j: (j, 0)),
            pl.BlockSpec((1, E, 128), lambda j: (j, 0, 0)),
        ],
        out_shape=[
            jax.ShapeDtypeStruct((TOP_K, T), jnp.int32),
            jax.ShapeDtypeStruct((T, 128), F32),
            jax.ShapeDtypeStruct((T // tt, E, 128), jnp.int32),
        ],
        scratch_shapes=[pltpu.VMEM((tt, tt), BF16)],
        compiler_params=_params(("arbitrary",)),
        name="moe_route",
    )(logits_t)


def _slot_tiles_max(n_tiles):
    return -(-(n_tiles * MOE_MAXC + N_EXPERTS * (MOE_CPT - 1)) // MOE_CPT)


def _routing_tables(nch):
    n_tiles, E = nch.shape

    def cumsum_last(a):
        m = a.shape[-1]
        keep = jnp.arange(m)[:, None] <= jnp.arange(m)[None, :]
        return jnp.sum(jnp.where(keep, a[..., :, None], 0), axis=-2)

    tot = jnp.sum(nch, axis=0)
    seg = (tot + MOE_CPT - 1) // MOE_CPT * MOE_CPT
    seg_end = cumsum_last(seg)
    seg_start = seg_end - seg
    gstart = seg_start[None, :] + cumsum_last(nch.T).T - nch
    l_end = cumsum_last(nch)
    lstart = l_end - nch

    def copy_list(count, first_local, first_global, step, max_len):
        end = cumsum_last(count)
        start = end - count
        i = jnp.arange(max_len, dtype=jnp.int32)[None, :, None]
        owner = (start[:, None, :] <= i) & (i < end[:, None, :])
        off = (i - start[:, None, :]) * step
        src = jnp.sum(jnp.where(owner, first_local[:, None, :] + off, 0), axis=-1)
        dst = jnp.sum(jnp.where(owner, first_global[:, None, :] + off, 0), axis=-1)
        return src, dst, end[:, -1]

    runs = nch // MOE_RUN
    rsrc, rdst, nruns = copy_list(runs, lstart, gstart, MOE_RUN, MOE_MAXQ)
    ssrc, sdst, nsingles = copy_list(nch - runs * MOE_RUN, lstart + runs * MOE_RUN, gstart + runs * MOE_RUN, 1,
                                     MOE_MAXS)
    copies = jnp.concatenate([rsrc, rdst, ssrc, sdst], axis=1).astype(jnp.int32).reshape(-1)
    ncopies = jnp.stack([nruns, nsingles], axis=1).astype(jnp.int32).reshape(-1)
    nt_max = _slot_tiles_max(n_tiles)
    n_used = (seg_end[-1] // MOE_CPT).astype(jnp.int32)
    tile_blk = jnp.minimum(jnp.arange(nt_max, dtype=jnp.int32), n_used - 1)
    tile_exp = jnp.sum(seg_end[None, :] <= (tile_blk * MOE_CPT)[:, None], axis=1)
    tile_exp = jnp.minimum(tile_exp, E - 1).astype(jnp.int32)
    experts = jnp.arange(E, dtype=jnp.int32)
    own = tile_exp[:, None] == experts[None, :]
    end_tile = jnp.sum(jnp.where(own, (seg_end // MOE_CPT)[None, :], 0), axis=1)
    follows = ((seg_start // MOE_CPT)[None, :] == end_tile[:, None]) & (seg > 0)[None, :]
    next_exp = jnp.where(jnp.any(follows, axis=1), jnp.sum(jnp.where(follows, experts[None, :], 0), axis=1), -1)
    next_exp = next_exp.astype(jnp.int32)
    zstart = (seg_start + tot).astype(jnp.int32)
    zcount = (seg - tot).astype(jnp.int32)
    return copies, ncopies, zstart, zcount, tile_exp, tile_blk, n_used.reshape(1), next_exp


def _chunk_rows(c, nchunks=1):
    return pl.ds(pl.multiple_of(c * MOE_CH, MOE_CH), nchunks * MOE_CH)


def _for_tile_copies(cp_ref, cn_ref, tile, make, op):
    base = tile * MOE_CPROW

    def run(i, carry):
        op(make(cp_ref[base + i], cp_ref[base + MOE_MAXQ + i], MOE_RUN))
        return carry
    lax.fori_loop(0, cn_ref[2 * tile], run, 0)

    def single(i, carry):
        op(make(cp_ref[base + 2 * MOE_MAXQ + i], cp_ref[base + 2 * MOE_MAXQ + MOE_MAXS + i], 1))
        return carry
    lax.fori_loop(0, cn_ref[2 * tile + 1], single, 0)


def _dispatch_kernel(cp_ref, cn_ref, zstart_ref, zcount_ref, nu_ref, h2_ref, pos_ref, xs_hbm, xl_scr, zero_scr,
                     sems, zsem):
    j = pl.program_id(0)
    n = pl.num_programs(0)
    slot = j % 2
    tt = h2_ref.shape[0]
    nt_max = xs_hbm.shape[0] // MOE_TM

    def tile_copies(tile, slot_, op):
        def make(local, glob, nchunks):
            return pltpu.make_async_copy(xl_scr.at[slot_, _chunk_rows(local, nchunks)],
                                         xs_hbm.at[_chunk_rows(glob, nchunks)], sems.at[slot_])
        _for_tile_copies(cp_ref, cn_ref, tile, make, op)

    def zero_chunk_copy(d):
        return pltpu.make_async_copy(zero_scr.at[pl.ds(0, MOE_CH)], xs_hbm.at[_chunk_rows(d)], zsem.at[0])

    def zero_tile_copy(t):
        rows = pl.ds(pl.multiple_of(t * MOE_TM, MOE_TM), MOE_TM)
        return pltpu.make_async_copy(zero_scr, xs_hbm.at[rows], zsem.at[0])

    def zero_fill(op):
        def expert(e, carry):
            def chunk(r, carry_):
                op(zero_chunk_copy(zstart_ref[e] + r))
                return carry_
            return lax.fori_loop(0, zcount_ref[e], chunk, carry)
        lax.fori_loop(0, N_EXPERTS, expert, 0)

        def tile(t, carry):
            op(zero_tile_copy(t))
            return carry
        lax.fori_loop(nu_ref[0], nt_max, tile, 0)

    def wait_tile(tile, slot_):
        tile_copies(tile, slot_, lambda cp: cp.wait())

    @pl.when(j == 0)
    def _():
        zero_scr[...] = jnp.zeros_like(zero_scr)
        zero_fill(lambda cp: cp.start())

    @pl.when(j >= 2)
    def _():
        wait_tile(j - 2, slot)

    pos = pos_ref[...]
    h2 = h2_ref[...]
    for rb in range(MOE_RLOC // MOE_RB):
        io = lax.broadcasted_iota(jnp.int32, (MOE_RB, tt), 0) + rb * MOE_RB
        onehot = jnp.zeros((MOE_RB, tt), F32)
        for k in range(TOP_K):
            onehot = jnp.where(io == pos[k:k + 1, :], 1.0, onehot)
        xl_scr[slot, rb * MOE_RB:(rb + 1) * MOE_RB, :] = _dot(onehot.astype(BF16), h2).astype(BF16)

    tile_copies(j, slot, lambda cp: cp.start())

    @pl.when(j == n - 1)
    def _():
        @pl.when(j >= 1)
        def _():
            wait_tile(j - 1, 1 - slot)
        wait_tile(j, slot)
        zero_fill(lambda cp: cp.wait())


def _dispatch_call(tables, h2, pos_t, n_slots):
    T = h2.shape[0]
    tt = MOE_TT
    copies, ncopies, zstart, zcount, n_used = tables
    grid_spec = pltpu.PrefetchScalarGridSpec(
        num_scalar_prefetch=5,
        grid=(T // tt,),
        in_specs=[
            pl.BlockSpec((tt, D_MODEL), lambda j, *_: (j, 0)),
            pl.BlockSpec((TOP_K, tt), lambda j, *_: (0, j)),
        ],
        out_specs=pl.BlockSpec(memory_space=pl.ANY),
        scratch_shapes=[
            pltpu.VMEM((2, MOE_RLOC, D_MODEL), BF16),
            pltpu.VMEM((MOE_TM, D_MODEL), BF16),
            pltpu.SemaphoreType.DMA((2,)),
            pltpu.SemaphoreType.DMA((1,)),
        ],
    )
    return pl.pallas_call(
        _dispatch_kernel,
        grid_spec=grid_spec,
        out_shape=jax.ShapeDtypeStruct((n_slots, D_MODEL), BF16),
        compiler_params=_params(("arbitrary",)),
        name="moe_dispatch",
    )(copies, ncopies, zstart, zcount, n_used, h2, pos_t)


def _experts_kernel(te_ref, tb_ref, nu_ref, nx_ref, xs_ref, w1_hbm, b1_ref, w2_hbm, b2_ref, y_ref,
                    w1_f32, w2_f32, w1_scr, w2_scr, sems):
    i = pl.program_id(0)
    valid = i < nu_ref[0]
    new_expert = (i == 0) | (te_ref[i] != te_ref[jnp.maximum(i - 1, 0)])

    def fetch(e):
        return (pltpu.make_async_copy(w1_hbm.at[e], w1_f32, sems.at[0]),
                pltpu.make_async_copy(w2_hbm.at[e], w2_f32, sems.at[1]))

    @pl.when(i == 0)
    def _():
        for cp in fetch(te_ref[0]):
            cp.start()

    @pl.when(valid & new_expert)
    def _():
        for cp in fetch(te_ref[i]):
            cp.wait()
        w1_scr[...] = w1_f32[...].astype(BF16)
        w2_scr[...] = w2_f32[...].astype(BF16)

        @pl.when(nx_ref[i] >= 0)
        def _():
            for cp in fetch(nx_ref[i]):
                cp.start()

    @pl.when(valid)
    def _():
        hh = _dot(xs_ref[...], w1_scr[...]) + b1_ref[...]
        gate = jnp.minimum(hh[:, :D_FF], SWIGLU_LIMIT)
        up = jnp.clip(hh[:, D_FF:], -SWIGLU_LIMIT, SWIGLU_LIMIT)
        act = (up + 1.0) * gate * _sigmoid(SWIGLU_ALPHA * gate)
        y_ref[...] = (_dot(act.astype(BF16), w2_scr[...]) + b2_ref[...]).astype(BF16)

    @pl.when(jnp.logical_not(valid))
    def _():
        y_ref[...] = jnp.zeros_like(y_ref)


def _experts_call(tile_exp, tile_blk, n_used, next_exp, xs, w1, b1, w2, b2):
    nt_max = tile_exp.shape[0]
    slot_tile = lambda i, te, tb, nu, nx: (tb[i], 0)
    per_e = lambda i, te, tb, nu, nx: (te[i], 0, 0)
    grid_spec = pltpu.PrefetchScalarGridSpec(
        num_scalar_prefetch=4,
        grid=(nt_max,),
        in_specs=[
            pl.BlockSpec((MOE_TM, D_MODEL), slot_tile),
            pl.BlockSpec(memory_space=pl.ANY),
            pl.BlockSpec((None, 1, 2 * D_FF), per_e),
            pl.BlockSpec(memory_space=pl.ANY),
            pl.BlockSpec((None, 1, D_MODEL), per_e),
        ],
        out_specs=pl.BlockSpec((MOE_TM, D_MODEL), lambda i, te, tb, nu, nx: (i, 0)),
        scratch_shapes=[
            pltpu.VMEM((D_MODEL, 2 * D_FF), F32),
            pltpu.VMEM((D_FF, D_MODEL), F32),
            pltpu.VMEM((D_MODEL, 2 * D_FF), BF16),
            pltpu.VMEM((D_FF, D_MODEL), BF16),
            pltpu.SemaphoreType.DMA((2,)),
        ],
    )
    return pl.pallas_call(
        _experts_kernel,
        grid_spec=grid_spec,
        out_shape=jax.ShapeDtypeStruct(xs.shape, BF16),
        compiler_params=_params(("arbitrary",)),
        name="moe_experts",
    )(tile_exp, tile_blk, n_used, next_exp, xs, w1, b1, w2, b2)


def _combine_kernel(cp_ref, cn_ref, tok_ref, x1_ref, mod_ref, fw_ref, y_hbm, o_ref, yl_scr, w_scr, sems):
    j = pl.program_id(0)
    n = pl.num_programs(0)
    slot = j % 2
    tt = x1_ref.shape[0]

    def tile_copies(tile, slot_, op):
        def make(local, glob, nchunks):
            return pltpu.make_async_copy(y_hbm.at[_chunk_rows(glob, nchunks)],
                                         yl_scr.at[slot_, _chunk_rows(local, nchunks)], sems.at[slot_])
        _for_tile_copies(cp_ref, cn_ref, tile, make, op)

    def start_tile(tile, slot_):
        tile_copies(tile, slot_, lambda cp: cp.start())

    @pl.when(j == 0)
    def _():
        yl_scr[...] = jnp.zeros_like(yl_scr)
        start_tile(0, 0)

    @pl.when(j + 1 < n)
    def _():
        start_tile(j + 1, 1 - slot)

    tile_copies(j, slot, lambda cp: cp.wait())

    pos = tok_ref[:, 0:TOP_K].astype(jnp.int32)
    p = tok_ref[:, TOP_K:2 * TOP_K]
    for cb in range(MOE_RLOC // MOE_RB):
        io = lax.broadcasted_iota(jnp.int32, (tt, MOE_RB), 1) + cb * MOE_RB
        w = jnp.zeros((tt, MOE_RB), F32)
        for k in range(TOP_K):
            w = jnp.where(io == pos[:, k:k + 1], p[:, k:k + 1], w)
        w_scr[:, cb * MOE_RB:(cb + 1) * MOE_RB] = w.astype(BF16)
    moe = _dot(w_scr[...], yl_scr[slot])
    x2 = x1_ref[...] + _mod_slice(mod_ref, 0, 5) * moe
    o_ref[...] = _rms(x2, fw_ref[...])


def _combine_call(copies, ncopies, tok, x1, mod, fw, y):
    T = x1.shape[0]
    tt = MOE_TT
    row = lambda j, *_: (j, 0)
    fixed = lambda j, *_: (0, 0)
    grid_spec = pltpu.PrefetchScalarGridSpec(
        num_scalar_prefetch=2,
        grid=(T // tt,),
        in_specs=[
            pl.BlockSpec((tt, 128), row),
            pl.BlockSpec((tt, D_MODEL), row),
            pl.BlockSpec(mod.shape, fixed),
            pl.BlockSpec((1, D_MODEL), fixed),
            pl.BlockSpec(memory_space=pl.ANY),
        ],
        out_specs=pl.BlockSpec((tt, D_MODEL), row),
        scratch_shapes=[
            pltpu.VMEM((2, MOE_RLOC, D_MODEL), BF16),
            pltpu.VMEM((tt, MOE_RLOC), BF16),
            pltpu.SemaphoreType.DMA((2,)),
        ],
    )
    return pl.pallas_call(
        _combine_kernel,
        grid_spec=grid_spec,
        out_shape=jax.ShapeDtypeStruct((T, D_MODEL), F32),
        compiler_params=_params(("arbitrary",)),
        name="moe_combine",
    )(copies, ncopies, tok, x1, mod, fw, y)


def _rope_tables(T):
    rows = T // GRID_W
    n_freq = DK // 4
    inv = ROPE_BASE ** (-jnp.arange(n_freq, dtype=F32) / n_freq)
    row_ang = jnp.arange(rows, dtype=F32)[:, None] * inv
    col_ang = jnp.arange(GRID_W, dtype=F32)[:, None] * inv
    rot_row = jnp.pad(jnp.stack([jnp.cos(row_ang), jnp.sin(row_ang)]), ((0, 0), (0, 0), (0, n_freq)))
    rot_col = jnp.pad(jnp.stack([jnp.cos(col_ang), jnp.sin(col_ang)]), ((0, 0), (0, 0), (n_freq, 0)))
    return rot_row, rot_col


def kernel(x, c, ctx, c_ctx, w_mod, b_mod, norm1_w, norm2_w, w_in, sgu_ln_w, sgu_ln_b, sgu_w, sgu_b,
           ret_decay_fwd, ret_decay_bwd, w_proj_a, w_proj_b, w_out, router_w, router_b,
           moe_w1, moe_b1, moe_w2, moe_b2, final_norm_w):
    B, T, D = x.shape
    assert B == 1 and D == D_MODEL and w_mod.shape[0] == 1 and T % 1024 == 0
    x2d = x.reshape(T, D)
    cc = jnp.stack([c.reshape(D), c_ctx.reshape(D)], axis=1)
    mod = _mod_call(cc, w_mod[0], b_mod)
    dec = jnp.stack([ret_decay_fwd[0], ret_decay_bwd[0]], axis=0)
    w_in_bf = w_in[0].astype(BF16)
    s0 = _ctx_call(ctx.reshape(ctx.shape[1], D), norm1_w, mod, w_in_bf, dec)
    rot_row, rot_col = _rope_tables(T)
    proj = _inproj_call(x2d, norm1_w, mod, w_in_bf, rot_row, rot_col, sgu_ln_w, sgu_ln_b)
    za = _sgu_call(proj, sgu_w[0].astype(BF16), sgu_b[0].T, w_proj_a[0].astype(BF16))
    yf, yb = _ret_call(proj, s0, dec)
    x1, h2, logits_t = _merge_call(yf, yb, za, proj, x2d, mod, norm2_w, w_proj_b[0].astype(BF16),
                                   w_out[0].astype(BF16), router_w[0].T, router_b.reshape(N_EXPERTS, 1))
    pos_t, tok, nch = _route_call(logits_t)
    copies, ncopies, zstart, zcount, tile_exp, tile_blk, n_used, next_exp = _routing_tables(nch[:, :, 0])
    n_slots = tile_exp.shape[0] * MOE_TM
    xs = _dispatch_call((copies, ncopies, zstart, zcount, n_used), h2, pos_t, n_slots)
    y = _experts_call(tile_exp, tile_blk, n_used, next_exp, xs, moe_w1[0], moe_b1[0][:, None, :], moe_w2[0],
                      moe_b2[0][:, None, :])
    out = _combine_call(copies, ncopies, tok, x1, mod, final_norm_w.reshape(1, D), y)
    return out.reshape(B, T, D)
```

```python
import functools

import jax
import jax.numpy as jnp
from jax import lax
from jax.experimental import pallas as pl
from jax.experimental.pallas import tpu as pltpu

D_MODEL = 1024
GRID_W = 64
SGU_CHUNK = 128
SGU_GROUPS = 8
HEADS = 4
DK = D_MODEL // HEADS
DV = 2 * DK
ROPE_BASE = 10000.0
N_EXPERTS = 32
TOP_K = 4
D_FF = D_MODEL
SWIGLU_LIMIT = 7.0
SWIGLU_ALPHA = 1.702
EPS = 1e-6
IN_WIDTH = 12 * D_MODEL
COL_U, COL_V, COL_Q, COL_K, COL_VR, COL_GF, COL_GB, COL_GA, COL_GBR = 0, 1, 2, 3, 4, 6, 8, 10, 11

RET_CHUNK = 256
VMEM_LIMIT = 56 * 1024 * 1024

MOE_TT = 512
MOE_CH = 16
MOE_TM = 512
MOE_CPT = MOE_TM // MOE_CH
MOE_MAXC = TOP_K * MOE_TT // MOE_CH + N_EXPERTS
MOE_RLOC = MOE_MAXC * MOE_CH
MOE_RB = 256
MOE_RUN = 4
MOE_MAXQ = MOE_MAXC // MOE_RUN
MOE_MAXS = (MOE_RUN - 1) * N_EXPERTS
MOE_CPROW = 2 * (MOE_MAXQ + MOE_MAXS)

F32 = jnp.float32
BF16 = jnp.bfloat16


def _params(sem):
    return pltpu.CompilerParams(dimension_semantics=sem, vmem_limit_bytes=VMEM_LIMIT)


def _dot(a, b):
    return jnp.dot(a, b, preferred_element_type=F32)


def _dot_nt(a, b):
    return lax.dot_general(a, b, (((1,), (1,)), ((), ())), preferred_element_type=F32)


def _dot_tn(a, b):
    return lax.dot_general(a, b, (((0,), (0,)), ((), ())), preferred_element_type=F32)


def _rms(x, w):
    return x * lax.rsqrt(jnp.mean(x * x, axis=-1, keepdims=True) + EPS) * w


def _gelu(x):
    return 0.5 * x * (1.0 + lax.erf(x * (2.0 ** -0.5)))


def _sigmoid(x):
    return 0.5 * jnp.tanh(0.5 * x) + 0.5


def _mod_slice(mod_ref, row, k):
    return mod_ref[row:row + 1, k * D_MODEL:(k + 1) * D_MODEL]


def _mod_kernel(cc_ref, w_ref, b_ref, o_ref):
    s = cc_ref[...]
    s = s * jax.nn.sigmoid(s)
    w = w_ref[...]
    r0 = jnp.sum(s[:, 0:1] * w, axis=0, keepdims=True)
    r1 = jnp.sum(s[:, 1:2] * w, axis=0, keepdims=True)
    o_ref[...] = jnp.concatenate([r0, r1], axis=0) + b_ref[...]


def _mod_call(cc, w_mod, b_mod):
    bn = 1536
    n = w_mod.shape[1]
    return pl.pallas_call(
        _mod_kernel,
        grid=(n // bn,),
        in_specs=[
            pl.BlockSpec((D_MODEL, 2), lambda j: (0, 0)),
            pl.BlockSpec((D_MODEL, bn), lambda j: (0, j)),
            pl.BlockSpec((1, bn), lambda j: (0, j)),
        ],
        out_specs=pl.BlockSpec((2, bn), lambda j: (0, j)),
        out_shape=jax.ShapeDtypeStruct((2, n), F32),
        compiler_params=_params(("arbitrary",)),
        name="mod_vectors",
    )(cc, w_mod, b_mod)


def _ctx_kernel(ctx_ref, nw_ref, mod_ref, w_ref, dec_ref, s0_ref):
    L = ctx_ref.shape[0]
    sh = _mod_slice(mod_ref, 1, 0)
    sc = _mod_slice(mod_ref, 1, 1)
    hc = (_rms(ctx_ref[...], nw_ref[...]) * (1.0 + sc) + sh).astype(BF16)
    kv = _dot(hc, w_ref[...])
    lg = jnp.log1p(-jnp.exp2(dec_ref[...]))
    pos = lax.broadcasted_iota(jnp.int32, (L, 1), 0).astype(F32)
    for h in range(HEADS):
        k = kv[:, h * DK:(h + 1) * DK] * (DK ** -0.5)
        v = kv[:, HEADS * DK + h * DV:HEADS * DK + (h + 1) * DV].astype(BF16)
        wf = jnp.exp((L - 1.0 - pos) * lg[0:1, h:h + 1])
        wb = jnp.exp(pos * lg[1:2, h:h + 1])
        s0_ref[0, h] = _dot_tn((k * wf).astype(BF16), v)
        s0_ref[1, h] = _dot_tn((k * wb).astype(BF16), v)


def _ctx_call(ctx2d, nw, mod, w_in_bf, dec):
    L = ctx2d.shape[0]
    wcols = HEADS * DK + HEADS * DV
    return pl.pallas_call(
        _ctx_kernel,
        grid=(1,),
        in_specs=[
            pl.BlockSpec((L, D_MODEL), lambda i: (0, 0)),
            pl.BlockSpec((1, D_MODEL), lambda i: (0, 0)),
            pl.BlockSpec(mod.shape, lambda i: (0, 0)),
            pl.BlockSpec((D_MODEL, wcols), lambda i: (0, COL_K * D_MODEL // wcols)),
            pl.BlockSpec(dec.shape, lambda i: (0, 0)),
        ],
        out_specs=pl.BlockSpec((2, HEADS, DK, DV), lambda i: (0, 0, 0, 0)),
        out_shape=jax.ShapeDtypeStruct((2, HEADS, DK, DV), F32),
        compiler_params=_params(("arbitrary",)),
        name="ctx_states",
    )(ctx2d, nw, mod, w_in_bf, dec)


def _inproj_kernel(x_ref, nw_ref, mod_ref, w_ref, rot_row_ref, rot_col_ref, lnw_ref, lnb_ref, o_ref,
                   h_scr, g_scr, cos_ref, sin_ref):
    j = pl.program_id(1)
    tm = h_scr.shape[0]
    first, second = slice(0, D_MODEL), slice(D_MODEL, 2 * D_MODEL)

    @pl.when(j == COL_Q // 2)
    def _():
        for r in range(tm // GRID_W):
            tok = slice(r * GRID_W, (r + 1) * GRID_W)
            cos_ref[tok, :] = rot_row_ref[0, r:r + 1, :] + rot_col_ref[0]
            sin_ref[tok, :] = rot_row_ref[1, r:r + 1, :] + rot_col_ref[1]

    @pl.when(j == 0)
    def _():
        sh = _mod_slice(mod_ref, 0, 0)
        sc = _mod_slice(mod_ref, 0, 1)
        h_scr[...] = (_rms(x_ref[...], nw_ref[...]) * (1.0 + sc) + sh).astype(BF16)

    def pieces(emit, span, rsplit=1):
        for c0 in range(span.start, span.stop, DK):
            cols = slice(c0, c0 + DK)
            for r in range(rsplit):
                rows = slice(r * tm // rsplit, (r + 1) * tm // rsplit)
                emit(rows, cols, _dot(h_scr[rows, :], w_ref[:, cols]))

    def store(fn):
        def emit(rows, cols, acc):
            o_ref[rows, cols] = fn(acc).astype(BF16)
        return emit

    def rope(scale):
        half = DK // 2

        def emit(rows, cols, acc):
            cos = cos_ref[rows, :]
            sin = sin_ref[rows, :]
            x1 = acc[:, :half]
            x2 = acc[:, half:]
            o_ref[rows, cols.start:cols.start + half] = ((x1 * cos - x2 * sin) * scale).astype(BF16)
            o_ref[rows, cols.start + half:cols.stop] = ((x1 * sin + x2 * cos) * scale).astype(BF16)
        return emit

    @pl.when(j == COL_U // 2)
    def _():
        pieces(store(_gelu), first)

        def emit(rows, cols, acc):
            g_scr[rows, cols.start - D_MODEL:cols.stop - D_MODEL] = _gelu(acc)
        pieces(emit, second)
        g = g_scr[...]
        mu = jnp.mean(g, axis=-1, keepdims=True)
        d = g - mu
        var = jnp.mean(d * d, axis=-1, keepdims=True)
        o_ref[:, second] = (d * lax.rsqrt(var + EPS) * lnw_ref[...] + lnb_ref[...]).astype(BF16)

    @pl.when(j == COL_Q // 2)
    def _():
        pieces(rope(1.0), first)
        pieces(rope(DK ** -0.5), second)

    @pl.when(j == COL_VR // 2)
    def _():
        pieces(store(lambda a: a), slice(0, 2 * D_MODEL))

    @pl.when((j == COL_GF // 2) | (j == COL_GB // 2))
    def _():
        pieces(store(lambda a: a * _sigmoid(a)), slice(0, 2 * D_MODEL), rsplit=2)

    @pl.when(j == COL_GA // 2)
    def _():
        pieces(store(_sigmoid), slice(0, 2 * D_MODEL), rsplit=2)


def _inproj_call(x2d, nw, mod, w_in_bf, rot_row, rot_col, lnw, lnb):
    T = x2d.shape[0]
    tm = 1024
    half = DK // 2
    return pl.pallas_call(
        _inproj_kernel,
        grid=(T // tm, IN_WIDTH // (2 * D_MODEL)),
        in_specs=[
            pl.BlockSpec((tm, D_MODEL), lambda i, j: (i, 0)),
            pl.BlockSpec((1, D_MODEL), lambda i, j: (0, 0)),
            pl.BlockSpec(mod.shape, lambda i, j: (0, 0)),
            pl.BlockSpec((D_MODEL, 2 * D_MODEL), lambda i, j: (0, j)),
            pl.BlockSpec((2, tm // GRID_W, half), lambda i, j: (0, i, 0)),
            pl.BlockSpec((2, GRID_W, half), lambda i, j: (0, 0, 0)),
            pl.BlockSpec((1, D_MODEL), lambda i, j: (0, 0)),
            pl.BlockSpec((1, D_MODEL), lambda i, j: (0, 0)),
        ],
        out_specs=pl.BlockSpec((tm, 2 * D_MODEL), lambda i, j: (i, j)),
        out_shape=jax.ShapeDtypeStruct((T, IN_WIDTH), BF16),
        scratch_shapes=[
            pltpu.VMEM((tm, D_MODEL), BF16),
            pltpu.VMEM((tm, D_MODEL), F32),
            pltpu.VMEM((tm, half), F32),
            pltpu.VMEM((tm, half), F32),
        ],
        compiler_params=_params(("arbitrary", "arbitrary")),
        name="in_proj",
    )(x2d, nw, mod, w_in_bf, rot_row, rot_col, lnw, lnb)


def _sgu_kernel(u_ref, v_ref, ga_ref, ws_ref, bt_ref, wpa_ref, o_ref, ya_scr):
    tm = u_ref.shape[0]
    gd = D_MODEL // SGU_GROUPS
    for n in range(tm // SGU_CHUNK):
        rows = slice(n * SGU_CHUNK, (n + 1) * SGU_CHUNK)
        for g in range(SGU_GROUPS):
            cols = slice(g * gd, (g + 1) * gd)
            mixed = _dot(ws_ref[g], v_ref[rows, cols]) + bt_ref[:, g:g + 1]
            ya_scr[rows, cols] = (u_ref[rows, cols].astype(F32) * mixed).astype(BF16)
    pa = _dot(ya_scr[...], wpa_ref[...])
    o_ref[...] = (ga_ref[...].astype(F32) * pa).astype(BF16)


def _sgu_call(proj, ws_bf, bt, wpa_bf):
    T = proj.shape[0]
    tm = 512
    return pl.pallas_call(
        _sgu_kernel,
        grid=(T // tm,),
        in_specs=[
            pl.BlockSpec((tm, D_MODEL), lambda i: (i, COL_U)),
            pl.BlockSpec((tm, D_MODEL), lambda i: (i, COL_V)),
            pl.BlockSpec((tm, D_MODEL), lambda i: (i, COL_GA)),
            pl.BlockSpec(ws_bf.shape, lambda i: (0, 0, 0)),
            pl.BlockSpec(bt.shape, lambda i: (0, 0)),
            pl.BlockSpec((D_MODEL, D_MODEL), lambda i: (0, 0)),
        ],
        out_specs=pl.BlockSpec((tm, D_MODEL), lambda i: (i, 0)),
        out_shape=jax.ShapeDtypeStruct((T, D_MODEL), BF16),
        scratch_shapes=[pltpu.VMEM((tm, D_MODEL), BF16)],
        compiler_params=_params(("arbitrary",)),
        name="sgu_proj_a",
    )(proj, proj, proj, ws_bf, bt, wpa_bf)


def _ret_kernel(qf_ref, kf_ref, vf_ref, gf_ref, qb_ref, kb_ref, vb_ref, gb_ref, s0_ref, dec_ref,
                yf_ref, yb_ref, s_scr, intra_scr, qd_scr, kd_scr, cd_scr):
    C = RET_CHUNK
    i = pl.program_id(0)

    @pl.when(i == 0)
    def _():
        lg = jnp.log1p(-jnp.exp2(dec_ref[...]))
        r = lax.broadcasted_iota(jnp.int32, (C, C), 0).astype(F32)
        c = lax.broadcasted_iota(jnp.int32, (C, C), 1).astype(F32)
        pos = lax.broadcasted_iota(jnp.int32, (C, 1), 0).astype(F32)
        for h in range(HEADS):
            lf = lg[0:1, h:h + 1]
            lb = lg[1:2, h:h + 1]
            intra_scr[0, h] = jnp.where(r >= c, jnp.exp(jnp.maximum(r - c, 0.0) * lf), 0.0)
            intra_scr[1, h] = jnp.where(c >= r, jnp.exp(jnp.maximum(c - r, 0.0) * lb), 0.0)
            qd_scr[0, h] = jnp.exp((pos + 1.0) * lf)
            qd_scr[1, h] = jnp.exp((C - pos) * lb)
            kd_scr[0, h] = jnp.exp((C - 1.0 - pos) * lf)
            kd_scr[1, h] = jnp.exp(pos * lb)
            cd_scr[0, h] = jnp.exp(C * lf)
            cd_scr[1, h] = jnp.exp(C * lb)
        s_scr[...] = s0_ref[...]

    dirs = ((qf_ref, kf_ref, vf_ref, gf_ref, yf_ref), (qb_ref, kb_ref, vb_ref, gb_ref, yb_ref))
    for d, (q_ref, k_ref, v_ref, g_ref, y_ref) in enumerate(dirs):
        for h in range(HEADS):
            q = q_ref[:, h * DK:(h + 1) * DK]
            k = k_ref[:, h * DK:(h + 1) * DK]
            v = v_ref[:, h * DV:(h + 1) * DV]
            s = s_scr[d, h]
            a = (_dot_nt(q, k) * intra_scr[d, h]).astype(BF16)
            o = _dot(a, v) + _dot(q, s.astype(BF16)) * qd_scr[d, h]
            hn = o * lax.rsqrt(jnp.mean(o * o, axis=-1, keepdims=True) + EPS)
            y_ref[:, h * DV:(h + 1) * DV] = (g_ref[:, h * DV:(h + 1) * DV].astype(F32) * hn).astype(BF16)
            kd = (k.astype(F32) * kd_scr[d, h]).astype(BF16)
            s_scr[d, h] = s * cd_scr[d, h] + _dot_tn(kd, v)


def _ret_call(proj, s0, dec):
    T = proj.shape[0]
    C = RET_CHUNK
    n = T // C
    qk_w = HEADS * DK
    v_w = HEADS * DV
    fwd = lambda col: (lambda i: (i, col))
    bwd = lambda col: (lambda i: (n - 1 - i, col))
    return pl.pallas_call(
        _ret_kernel,
        grid=(n,),
        in_specs=[
            pl.BlockSpec((C, qk_w), fwd(COL_Q)),
            pl.BlockSpec((C, qk_w), fwd(COL_K)),
            pl.BlockSpec((C, v_w), fwd(COL_VR // 2)),
            pl.BlockSpec((C, v_w), fwd(COL_GF // 2)),
            pl.BlockSpec((C, qk_w), bwd(COL_Q)),
            pl.BlockSpec((C, qk_w), bwd(COL_K)),
            pl.BlockSpec((C, v_w), bwd(COL_VR // 2)),
            pl.BlockSpec((C, v_w), bwd(COL_GB // 2)),
            pl.BlockSpec(s0.shape, lambda i: (0, 0, 0, 0)),
            pl.BlockSpec(dec.shape, lambda i: (0, 0)),
        ],
        out_specs=[
            pl.BlockSpec((C, v_w), lambda i: (i, 0)),
            pl.BlockSpec((C, v_w), lambda i: (n - 1 - i, 0)),
        ],
        out_shape=[jax.ShapeDtypeStruct((T, v_w), BF16), jax.ShapeDtypeStruct((T, v_w), BF16)],
        scratch_shapes=[
            pltpu.VMEM((2, HEADS, DK, DV), F32),
            pltpu.VMEM((2, HEADS, C, C), F32),
            pltpu.VMEM((2, HEADS, C, 1), F32),
            pltpu.VMEM((2, HEADS, C, 1), F32),
            pltpu.VMEM((2, HEADS, 1, 1), F32),
        ],
        compiler_params=_params(("arbitrary",)),
        name="retention",
    )(proj, proj, proj, proj, proj, proj, proj, proj, s0, dec)


def _merge_kernel(yf_ref, yb_ref, za_ref, gbr_ref, x_ref, mod_ref, nw_ref, wpb_ref, wo_ref, rw_ref, rb_ref,
                  x1_ref, h2_ref, lg_ref):
    yb = yf_ref[...] + yb_ref[...]
    pb = _dot(yb, wpb_ref[...])
    y = za_ref[...].astype(F32) + gbr_ref[...].astype(F32) * pb
    yo = _dot(y.astype(BF16), wo_ref[...])
    x1 = x_ref[...] + _mod_slice(mod_ref, 0, 2) * yo
    x1_ref[...] = x1
    h2 = _rms(x1, nw_ref[...]) * (1.0 + _mod_slice(mod_ref, 0, 4)) + _mod_slice(mod_ref, 0, 3)
    h2_hi = h2.astype(BF16)
    h2_ref[...] = h2_hi
    h2_lo = (h2 - h2_hi.astype(F32)).astype(BF16)
    rw = rw_ref[...]
    rw_hi = rw.astype(BF16)
    rw_lo = (rw - rw_hi.astype(F32)).astype(BF16)
    both = _dot_nt(jnp.concatenate([rw_hi, rw_lo], axis=0), h2_hi)
    lg_ref[...] = both[:N_EXPERTS] + both[N_EXPERTS:] + _dot_nt(rw_hi, h2_lo) + rb_ref[...]


def _merge_call(yf, yb, za, proj, x2d, mod, nw2, wpb_bf, wo_bf, rw, rb):
    T = x2d.shape[0]
    tm = 512
    row = lambda i: (i, 0)
    fixed = lambda i: (0, 0)
    return pl.pallas_call(
        _merge_kernel,
        grid=(T // tm,),
        in_specs=[
            pl.BlockSpec((tm, HEADS * DV), row),
            pl.BlockSpec((tm, HEADS * DV), row),
            pl.BlockSpec((tm, D_MODEL), row),
            pl.BlockSpec((tm, D_MODEL), lambda i: (i, COL_GBR)),
            pl.BlockSpec((tm, D_MODEL), row),
            pl.BlockSpec(mod.shape, fixed),
            pl.BlockSpec((1, D_MODEL), fixed),
            pl.BlockSpec(wpb_bf.shape, fixed),
            pl.BlockSpec(wo_bf.shape, fixed),
            pl.BlockSpec(rw.shape, fixed),
            pl.BlockSpec(rb.shape, fixed),
        ],
        out_specs=[
            pl.BlockSpec((tm, D_MODEL), row),
            pl.BlockSpec((tm, D_MODEL), row),
            pl.BlockSpec((N_EXPERTS, tm), lambda i: (0, i)),
        ],
        out_shape=[
            jax.ShapeDtypeStruct((T, D_MODEL), F32),
            jax.ShapeDtypeStruct((T, D_MODEL), BF16),
            jax.ShapeDtypeStruct((N_EXPERTS, T), F32),
        ],
        compiler_params=_params(("arbitrary",)),
        name="merge_router",
    )(yf, yb, za, proj, x2d, mod, nw2, wpb_bf, wo_bf, rw, rb)


def _route_kernel(lg_ref, pos_ref, tok_ref, nch_ref, su_scr):
    E, tt = lg_ref.shape

    @pl.when(pl.program_id(0) == 0)
    def _():
        r = lax.broadcasted_iota(jnp.int32, (tt, tt), 0)
        c = lax.broadcasted_iota(jnp.int32, (tt, tt), 1)
        su_scr[...] = jnp.where(r < c, 1.0, 0.0).astype(BF16)

    sub = lax.broadcasted_iota(jnp.int32, (E, tt), 0)
    work = lg_ref[...]
    vals, hots = [], []
    for _ in range(TOP_K):
        m = jnp.max(work, axis=0, keepdims=True)
        first = jnp.min(jnp.where(work == m, sub, E), axis=0, keepdims=True)
        hot = sub == first
        vals.append(m)
        hots.append(hot)
        work = jnp.where(hot, -jnp.inf, work)
    exps = [jnp.exp(v - vals[0]) for v in vals]
    inv = 1.0 / functools.reduce(lambda a, b: a + b, exps)

    member = functools.reduce(lambda a, b: a + b, [jnp.where(h, 1.0, 0.0) for h in hots])
    cnt = jnp.sum(member, axis=1, keepdims=True)
    nch = jnp.floor((cnt + (MOE_CH - 1.0)) * (1.0 / MOE_CH))
    nch_b = jnp.broadcast_to(nch, (E, 128))
    rank = _dot(member.astype(BF16), su_scr[...])
    er = lax.broadcasted_iota(jnp.int32, (E, E), 0)
    ec = lax.broadcasted_iota(jnp.int32, (E, E), 1)
    first_chunk = _dot(jnp.where(ec < er, 1.0, 0.0).astype(BF16), nch_b.astype(BF16))[:, 0:1]
    base = first_chunk * MOE_CH + rank
    pos = [jnp.sum(jnp.where(hots[k], base, 0.0), axis=0, keepdims=True) for k in range(TOP_K)]
    for k in range(TOP_K):
        pos_ref[k:k + 1, :] = pos[k].astype(jnp.int32)
    rows = jnp.concatenate(pos + [e * inv for e in exps] + [jnp.zeros((128 - 2 * TOP_K, tt), F32)], axis=0)
    tok_ref[...] = rows.T
    nch_ref[0] = nch_b.astype(jnp.int32)


def _route_call(logits_t):
    E, T = logits_t.shape
    tt = MOE_TT
    return pl.pallas_call(
        _route_kernel,
        grid=(T // tt,),
        in_specs=[pl.BlockSpec((E, tt), lambda j: (0, j))],
        out_specs=[
            pl.BlockSpec((TOP_K, tt), lambda j: (0, j)),
            pl.BlockSpec((tt, 128), lambda j: (j, 0)),
            pl.BlockSpec((1, E, 128), lambda j: (j, 0, 0)),
        ],
        out_shape=[
            jax.ShapeDtypeStruct((TOP_K, T), jnp.int32),
            jax.ShapeDtypeStruct((T, 128), F32),
            jax.ShapeDtypeStruct((T // tt, E, 128), jnp.int32),
        ],
        scratch_shapes=[pltpu.VMEM((tt, tt), BF16)],
        compiler_params=_params(("arbitrary",)),
        name="moe_route",
    )(logits_t)


def _slot_tiles_max(n_tiles):
    return -(-(n_tiles * MOE_MAXC + N_EXPERTS * (MOE_CPT - 1)) // MOE_CPT)


def _routing_tables(nch):
    n_tiles, E = nch.shape

    def cumsum_last(a):
        m = a.shape[-1]
        keep = jnp.arange(m)[:, None] <= jnp.arange(m)[None, :]
        return jnp.sum(jnp.where(keep, a[..., :, None], 0), axis=-2)

    tot = jnp.sum(nch, axis=0)
    seg = (tot + MOE_CPT - 1) // MOE_CPT * MOE_CPT
    seg_end = cumsum_last(seg)
    seg_start = seg_end - seg
    gstart = seg_start[None, :] + cumsum_last(nch.T).T - nch
    l_end = cumsum_last(nch)
    lstart = l_end - nch

    def copy_list(count, first_local, first_global, step, max_len):
        end = cumsum_last(count)
        start = end - count
        i = jnp.arange(max_len, dtype=jnp.int32)[None, :, None]
        owner = (start[:, None, :] <= i) & (i < end[:, None, :])
        off = (i - start[:, None, :]) * step
        src = jnp.sum(jnp.where(owner, first_local[:, None, :] + off, 0), axis=-1)
        dst = jnp.sum(jnp.where(owner, first_global[:, None, :] + off, 0), axis=-1)
        return src, dst, end[:, -1]

    runs = nch // MOE_RUN
    rsrc, rdst, nruns = copy_list(runs, lstart, gstart, MOE_RUN, MOE_MAXQ)
    ssrc, sdst, nsingles = copy_list(nch - runs * MOE_RUN, lstart + runs * MOE_RUN, gstart + runs * MOE_RUN, 1,
                                     MOE_MAXS)
    copies = jnp.concatenate([rsrc, rdst, ssrc, sdst], axis=1).astype(jnp.int32).reshape(-1)
    ncopies = jnp.stack([nruns, nsingles], axis=1).astype(jnp.int32).reshape(-1)
    nt_max = _slot_tiles_max(n_tiles)
    n_used = (seg_end[-1] // MOE_CPT).astype(jnp.int32)
    tile_blk = jnp.minimum(jnp.arange(nt_max, dtype=jnp.int32), n_used - 1)
    tile_exp = jnp.sum(seg_end[None, :] <= (tile_blk * MOE_CPT)[:, None], axis=1)
    tile_exp = jnp.minimum(tile_exp, E - 1).astype(jnp.int32)
    experts = jnp.arange(E, dtype=jnp.int32)
    own = tile_exp[:, None] == experts[None, :]
    end_tile = jnp.sum(jnp.where(own, (seg_end // MOE_CPT)[None, :], 0), axis=1)
    follows = ((seg_start // MOE_CPT)[None, :] == end_tile[:, None]) & (seg > 0)[None, :]
    next_exp = jnp.where(jnp.any(follows, axis=1), jnp.sum(jnp.where(follows, experts[None, :], 0), axis=1), -1)
    next_exp = next_exp.astype(jnp.int32)
    zstart = (seg_start + tot).astype(jnp.int32)
    zcount = (seg - tot).astype(jnp.int32)
    return copies, ncopies, zstart, zcount, tile_exp, tile_blk, n_used.reshape(1), next_exp


def _chunk_rows(c, nchunks=1):
    return pl.ds(pl.multiple_of(c * MOE_CH, MOE_CH), nchunks * MOE_CH)


def _for_tile_copies(cp_ref, cn_ref, tile, make, op):
    base = tile * MOE_CPROW

    def run(i, carry):
        op(make(cp_ref[base + i], cp_ref[base + MOE_MAXQ + i], MOE_RUN))
        return carry
    lax.fori_loop(0, cn_ref[2 * tile], run, 0)

    def single(i, carry):
        op(make(cp_ref[base + 2 * MOE_MAXQ + i], cp_ref[base + 2 * MOE_MAXQ + MOE_MAXS + i], 1))
        return carry
    lax.fori_loop(0, cn_ref[2 * tile + 1], single, 0)


def _dispatch_kernel(cp_ref, cn_ref, zstart_ref, zcount_ref, nu_ref, h2_ref, pos_ref, xs_hbm, xl_scr, zero_scr,
                     sems, zsem):
    j = pl.program_id(0)
    n = pl.num_programs(0)
    slot = j % 2
    tt = h2_ref.shape[0]
    nt_max = xs_hbm.shape[0] // MOE_TM

    def tile_copies(tile, slot_, op):
        def make(local, glob, nchunks):
            return pltpu.make_async_copy(xl_scr.at[slot_, _chunk_rows(local, nchunks)],
                                         xs_hbm.at[_chunk_rows(glob, nchunks)], sems.at[slot_])
        _for_tile_copies(cp_ref, cn_ref, tile, make, op)

    def zero_chunk_copy(d):
        return pltpu.make_async_copy(zero_scr.at[pl.ds(0, MOE_CH)], xs_hbm.at[_chunk_rows(d)], zsem.at[0])

    def zero_tile_copy(t):
        rows = pl.ds(pl.multiple_of(t * MOE_TM, MOE_TM), MOE_TM)
        return pltpu.make_async_copy(zero_scr, xs_hbm.at[rows], zsem.at[0])

    def zero_fill(op):
        def expert(e, carry):
            def chunk(r, carry_):
                op(zero_chunk_copy(zstart_ref[e] + r))
                return carry_
            return lax.fori_loop(0, zcount_ref[e], chunk, carry)
        lax.fori_loop(0, N_EXPERTS, expert, 0)

        def tile(t, carry):
            op(zero_tile_copy(t))
            return carry
        lax.fori_loop(nu_ref[0], nt_max, tile, 0)

    def wait_tile(tile, slot_):
        tile_copies(tile, slot_, lambda cp: cp.wait())

    @pl.when(j == 0)
    def _():
        zero_scr[...] = jnp.zeros_like(zero_scr)
        zero_fill(lambda cp: cp.start())

    @pl.when(j >= 2)
    def _():
        wait_tile(j - 2, slot)

    pos = pos_ref[...]
    h2 = h2_ref[...]
    for rb in range(MOE_RLOC // MOE_RB):
        io = lax.broadcasted_iota(jnp.int32, (MOE_RB, tt), 0) + rb * MOE_RB
        onehot = jnp.zeros((MOE_RB, tt), F32)
        for k in range(TOP_K):
            onehot = jnp.where(io == pos[k:k + 1, :], 1.0, onehot)
        xl_scr[slot, rb * MOE_RB:(rb + 1) * MOE_RB, :] = _dot(onehot.astype(BF16), h2).astype(BF16)

    tile_copies(j, slot, lambda cp: cp.start())

    @pl.when(j == n - 1)
    def _():
        @pl.when(j >= 1)
        def _():
            wait_tile(j - 1, 1 - slot)
        wait_tile(j, slot)
        zero_fill(lambda cp: cp.wait())


def _dispatch_call(tables, h2, pos_t, n_slots):
    T = h2.shape[0]
    tt = MOE_TT
    copies, ncopies, zstart, zcount, n_used = tables
    grid_spec = pltpu.PrefetchScalarGridSpec(
        num_scalar_prefetch=5,
        grid=(T // tt,),
        in_specs=[
            pl.BlockSpec((tt, D_MODEL), lambda j, *_: (j, 0)),
            pl.BlockSpec((TOP_K, tt), lambda j, *_: (0, j)),
        ],
        out_specs=pl.BlockSpec(memory_space=pl.ANY),
        scratch_shapes=[
            pltpu.VMEM((2, MOE_RLOC, D_MODEL), BF16),
            pltpu.VMEM((MOE_TM, D_MODEL), BF16),
            pltpu.SemaphoreType.DMA((2,)),
            pltpu.SemaphoreType.DMA((1,)),
        ],
    )
    return pl.pallas_call(
        _dispatch_kernel,
        grid_spec=grid_spec,
        out_shape=jax.ShapeDtypeStruct((n_slots, D_MODEL), BF16),
        compiler_params=_params(("arbitrary",)),
        name="moe_dispatch",
    )(copies, ncopies, zstart, zcount, n_used, h2, pos_t)


def _experts_kernel(te_ref, tb_ref, nu_ref, nx_ref, xs_ref, w1_hbm, b1_ref, w2_hbm, b2_ref, y_ref,
                    w1_f32, w2_f32, w1_scr, w2_scr, sems):
    i = pl.program_id(0)
    valid = i < nu_ref[0]
    new_expert = (i == 0) | (te_ref[i] != te_ref[jnp.maximum(i - 1, 0)])

    def fetch(e):
        return (pltpu.make_async_copy(w1_hbm.at[e], w1_f32, sems.at[0]),
                pltpu.make_async_copy(w2_hbm.at[e], w2_f32, sems.at[1]))

    @pl.when(i == 0)
    def _():
        for cp in fetch(te_ref[0]):
            cp.start()

    @pl.when(valid & new_expert)
    def _():
        for cp in fetch(te_ref[i]):
            cp.wait()
        w1_scr[...] = w1_f32[...].astype(BF16)
        w2_scr[...] = w2_f32[...].astype(BF16)

        @pl.when(nx_ref[i] >= 0)
        def _():
            for cp in fetch(nx_ref[i]):
                cp.start()

    @pl.when(valid)
    def _():
        hh = _dot(xs_ref[...], w1_scr[...]) + b1_ref[...]
        gate = jnp.minimum(hh[:, :D_FF], SWIGLU_LIMIT)
        up = jnp.clip(hh[:, D_FF:], -SWIGLU_LIMIT, SWIGLU_LIMIT)
        act = (up + 1.0) * gate * _sigmoid(SWIGLU_ALPHA * gate)
        y_ref[...] = (_dot(act.astype(BF16), w2_scr[...]) + b2_ref[...]).astype(BF16)

    @pl.when(jnp.logical_not(valid))
    def _():
        y_ref[...] = jnp.zeros_like(y_ref)


def _experts_call(tile_exp, tile_blk, n_used, next_exp, xs, w1, b1, w2, b2):
    nt_max = tile_exp.shape[0]
    slot_tile = lambda i, te, tb, nu, nx: (tb[i], 0)
    per_e = lambda i, te, tb, nu, nx: (te[i], 0, 0)
    grid_spec = pltpu.PrefetchScalarGridSpec(
        num_scalar_prefetch=4,
        grid=(nt_max,),
        in_specs=[
            pl.BlockSpec((MOE_TM, D_MODEL), slot_tile),
            pl.BlockSpec(memory_space=pl.ANY),
            pl.BlockSpec((None, 1, 2 * D_FF), per_e),
            pl.BlockSpec(memory_space=pl.ANY),
            pl.BlockSpec((None, 1, D_MODEL), per_e),
        ],
        out_specs=pl.BlockSpec((MOE_TM, D_MODEL), lambda i, te, tb, nu, nx: (i, 0)),
        scratch_shapes=[
            pltpu.VMEM((D_MODEL, 2 * D_FF), F32),
            pltpu.VMEM((D_FF, D_MODEL), F32),
            pltpu.VMEM((D_MODEL, 2 * D_FF), BF16),
            pltpu.VMEM((D_FF, D_MODEL), BF16),
            pltpu.SemaphoreType.DMA((2,)),
        ],
    )
    return pl.pallas_call(
        _experts_kernel,
        grid_spec=grid_spec,
        out_shape=jax.ShapeDtypeStruct(xs.shape, BF16),
        compiler_params=_params(("arbitrary",)),
        name="moe_experts",
    )(tile_exp, tile_blk, n_used, next_exp, xs, w1, b1, w2, b2)


def _combine_kernel(cp_ref, cn_ref, tok_ref, x1_ref, mod_ref, fw_ref, y_hbm, o_ref, yl_scr, w_scr, sems):
    j = pl.program_id(0)
    n = pl.num_programs(0)
    slot = j % 2
    tt = x1_ref.shape[0]

    def tile_copies(tile, slot_, op):
        def make(local, glob, nchunks):
            return pltpu.make_async_copy(y_hbm.at[_chunk_rows(glob, nchunks)],
                                         yl_scr.at[slot_, _chunk_rows(local, nchunks)], sems.at[slot_])
        _for_tile_copies(cp_ref, cn_ref, tile, make, op)

    def start_tile(tile, slot_):
        tile_copies(tile, slot_, lambda cp: cp.start())

    @pl.when(j == 0)
    def _():
        yl_scr[...] = jnp.zeros_like(yl_scr)
        start_tile(0, 0)

    @pl.when(j + 1 < n)
    def _():
        start_tile(j + 1, 1 - slot)

    tile_copies(j, slot, lambda cp: cp.wait())

    pos = tok_ref[:, 0:TOP_K].astype(jnp.int32)
    p = tok_ref[:, TOP_K:2 * TOP_K]
    for cb in range(MOE_RLOC // MOE_RB):
        io = lax.broadcasted_iota(jnp.int32, (tt, MOE_RB), 1) + cb * MOE_RB
        w = jnp.zeros((tt, MOE_RB), F32)
        for k in range(TOP_K):
            w = jnp.where(io == pos[:, k:k + 1], p[:, k:k + 1], w)
        w_scr[:, cb * MOE_RB:(cb + 1) * MOE_RB] = w.astype(BF16)
    moe = _dot(w_scr[...], yl_scr[slot])
    x2 = x1_ref[...] + _mod_slice(mod_ref, 0, 5) * moe
    o_ref[...] = _rms(x2, fw_ref[...])


def _combine_call(copies, ncopies, tok, x1, mod, fw, y):
    T = x1.shape[0]
    tt = MOE_TT
    row = lambda j, *_: (j, 0)
    fixed = lambda j, *_: (0, 0)
    grid_spec = pltpu.PrefetchScalarGridSpec(
        num_scalar_prefetch=2,
        grid=(T // tt,),
        in_specs=[
            pl.BlockSpec((tt, 128), row),
            pl.BlockSpec((tt, D_MODEL), row),
            pl.BlockSpec(mod.shape, fixed),
            pl.BlockSpec((1, D_MODEL), fixed),
            pl.BlockSpec(memory_space=pl.ANY),
        ],
        out_specs=pl.BlockSpec((tt, D_MODEL), row),
        scratch_shapes=[
            pltpu.VMEM((2, MOE_RLOC, D_MODEL), BF16),
            pltpu.VMEM((tt, MOE_RLOC), BF16),
            pltpu.SemaphoreType.DMA((2,)),
        ],
    )
    return pl.pallas_call(
        _combine_kernel,
        grid_spec=grid_spec,
        out_shape=jax.ShapeDtypeStruct((T, D_MODEL), F32),
        compiler_params=_params(("arbitrary",)),
        name="moe_combine",
    )(copies, ncopies, tok, x1, mod, fw, y)


def _rope_tables(T):
    rows = T // GRID_W
    n_freq = DK // 4
    inv = ROPE_BASE ** (-jnp.arange(n_freq, dtype=F32) / n_freq)
    row_ang = jnp.arange(rows, dtype=F32)[:, None] * inv
    col_ang = jnp.arange(GRID_W, dtype=F32)[:, None] * inv
    rot_row = jnp.pad(jnp.stack([jnp.cos(row_ang), jnp.sin(row_ang)]), ((0, 0), (0, 0), (0, n_freq)))
    rot_col = jnp.pad(jnp.stack([jnp.cos(col_ang), jnp.sin(col_ang)]), ((0, 0), (0, 0), (n_freq, 0)))
    return rot_row, rot_col


def kernel(x, c, ctx, c_ctx, w_mod, b_mod, norm1_w, norm2_w, w_in, sgu_ln_w, sgu_ln_b, sgu_w, sgu_b,
           ret_decay_fwd, ret_decay_bwd, w_proj_a, w_proj_b, w_out, router_w, router_b,
           moe_w1, moe_b1, moe_w2, moe_b2, final_norm_w):
    B, T, D = x.shape
    assert B == 1 and D == D_MODEL and w_mod.shape[0] == 1 and T % 1024 == 0
    x2d = x.reshape(T, D)
    cc = jnp.stack([c.reshape(D), c_ctx.reshape(D)], axis=1)
    mod = _mod_call(cc, w_mod[0], b_mod)
    dec = jnp.stack([ret_decay_fwd[0], ret_decay_bwd[0]], axis=0)
    w_in_bf = w_in[0].astype(BF16)
    s0 = _ctx_call(ctx.reshape(ctx.shape[1], D), norm1_w, mod, w_in_bf, dec)
    rot_row, rot_col = _rope_tables(T)
    proj = _inproj_call(x2d, norm1_w, mod, w_in_bf, rot_row, rot_col, sgu_ln_w, sgu_ln_b)
    za = _sgu_call(proj, sgu_w[0].astype(BF16), sgu_b[0].T, w_proj_a[0].astype(BF16))
    yf, yb = _ret_call(proj, s0, dec)
    x1, h2, logits_t = _merge_call(yf, yb, za, proj, x2d, mod, norm2_w, w_proj_b[0].astype(BF16),
                                   w_out[0].astype(BF16), router_w[0].T, router_b.reshape(N_EXPERTS, 1))
    pos_t, tok, nch = _route_call(logits_t)
    copies, ncopies, zstart, zcount, tile_exp, tile_blk, n_used, next_exp = _routing_tables(nch[:, :, 0])
    n_slots = tile_exp.shape[0] * MOE_TM
    xs = _dispatch_call((copies, ncopies, zstart, zcount, n_used), h2, pos_t, n_slots)
    y = _experts_call(tile_exp, tile_blk, n_used, next_exp, xs, moe_w1[0], moe_b1[0][:, None, :], moe_w2[0],
                      moe_b2[0][:, None, :])
    out = _combine_call(copies, ncopies, tok, x1, mod, final_norm_w.reshape(1, D), y)
    return out.reshape(B, T, D)
```

```python
import functools

import jax
import jax.numpy as jnp
from jax import lax
from jax.experimental import pallas as pl
from jax.experimental.pallas import tpu as pltpu

D_MODEL = 1024
GRID_W = 64
SGU_CHUNK = 128
SGU_GROUPS = 8
HEADS = 4
DK = D_MODEL // HEADS
DV = 2 * DK
ROPE_BASE = 10000.0
N_EXPERTS = 32
TOP_K = 4
D_FF = D_MODEL
SWIGLU_LIMIT = 7.0
SWIGLU_ALPHA = 1.702
EPS = 1e-6
IN_WIDTH = 12 * D_MODEL
COL_U, COL_V, COL_Q, COL_K, COL_VR, COL_GF, COL_GB, COL_GA, COL_GBR = 0, 1, 2, 3, 4, 6, 8, 10, 11

INPROJ_NB = 4
RET_CHUNK = 256
VMEM_LIMIT = 56 * 1024 * 1024

MOE_TT = 512
MOE_CH = 16
MOE_TM = 512
MOE_CPT = MOE_TM // MOE_CH
MOE_MAXC = TOP_K * MOE_TT // MOE_CH + N_EXPERTS
MOE_RLOC = MOE_MAXC * MOE_CH
MOE_RB = 256
MOE_RUN = 4
MOE_MAXQ = MOE_MAXC // MOE_RUN
MOE_MAXS = (MOE_RUN - 1) * N_EXPERTS
MOE_CPROW = 2 * (MOE_MAXQ + MOE_MAXS)

F32 = jnp.float32
BF16 = jnp.bfloat16


def _params(sem):
    return pltpu.CompilerParams(dimension_semantics=sem, vmem_limit_bytes=VMEM_LIMIT)


def _dot(a, b):
    return jnp.dot(a, b, preferred_element_type=F32)


def _dot_nt(a, b):
    return lax.dot_general(a, b, (((1,), (1,)), ((), ())), preferred_element_type=F32)


def _dot_tn(a, b):
    return lax.dot_general(a, b, (((0,), (0,)), ((), ())), preferred_element_type=F32)


def _rms(x, w):
    return x * lax.rsqrt(jnp.mean(x * x, axis=-1, keepdims=True) + EPS) * w


def _gelu(x):
    return 0.5 * x * (1.0 + lax.erf(x * (2.0 ** -0.5)))


def _sigmoid(x):
    return 0.5 * jnp.tanh(0.5 * x) + 0.5


def _mod_slice(mod_ref, row, k):
    return mod_ref[row:row + 1, k * D_MODEL:(k + 1) * D_MODEL]


def _mod_kernel(cc_ref, w_ref, b_ref, o_ref):
    s = cc_ref[...]
    s = s * jax.nn.sigmoid(s)
    w = w_ref[...]
    r0 = jnp.sum(s[:, 0:1] * w, axis=0, keepdims=True)
    r1 = jnp.sum(s[:, 1:2] * w, axis=0, keepdims=True)
    o_ref[...] = jnp.concatenate([r0, r1], axis=0) + b_ref[...]


def _mod_call(cc, w_mod, b_mod):
    bn = 1536
    n = w_mod.shape[1]
    return pl.pallas_call(
        _mod_kernel,
        grid=(n // bn,),
        in_specs=[
            pl.BlockSpec((D_MODEL, 2), lambda j: (0, 0)),
            pl.BlockSpec((D_MODEL, bn), lambda j: (0, j)),
            pl.BlockSpec((1, bn), lambda j: (0, j)),
        ],
        out_specs=pl.BlockSpec((2, bn), lambda j: (0, j)),
        out_shape=jax.ShapeDtypeStruct((2, n), F32),
        compiler_params=_params(("arbitrary",)),
        name="mod_vectors",
    )(cc, w_mod, b_mod)


def _ctx_kernel(ctx_ref, nw_ref, mod_ref, w_ref, dec_ref, s0_ref):
    L = ctx_ref.shape[0]
    sh = _mod_slice(mod_ref, 1, 0)
    sc = _mod_slice(mod_ref, 1, 1)
    hc = (_rms(ctx_ref[...], nw_ref[...]) * (1.0 + sc) + sh).astype(BF16)
    kv = _dot(hc, w_ref[...])
    lg = jnp.log1p(-jnp.exp2(dec_ref[...]))
    pos = lax.broadcasted_iota(jnp.int32, (L, 1), 0).astype(F32)
    for h in range(HEADS):
        k = kv[:, h * DK:(h + 1) * DK] * (DK ** -0.5)
        v = kv[:, HEADS * DK + h * DV:HEADS * DK + (h + 1) * DV].astype(BF16)
        wf = jnp.exp((L - 1.0 - pos) * lg[0:1, h:h + 1])
        wb = jnp.exp(pos * lg[1:2, h:h + 1])
        s0_ref[0, h] = _dot_tn((k * wf).astype(BF16), v)
        s0_ref[1, h] = _dot_tn((k * wb).astype(BF16), v)


def _ctx_call(ctx2d, nw, mod, w_in_bf, dec):
    L = ctx2d.shape[0]
    wcols = HEADS * DK + HEADS * DV
    return pl.pallas_call(
        _ctx_kernel,
        grid=(1,),
        in_specs=[
            pl.BlockSpec((L, D_MODEL), lambda i: (0, 0)),
            pl.BlockSpec((1, D_MODEL), lambda i: (0, 0)),
            pl.BlockSpec(mod.shape, lambda i: (0, 0)),
            pl.BlockSpec((D_MODEL, wcols), lambda i: (0, COL_K * D_MODEL // wcols)),
            pl.BlockSpec(dec.shape, lambda i: (0, 0)),
        ],
        out_specs=pl.BlockSpec((2, HEADS, DK, DV), lambda i: (0, 0, 0, 0)),
        out_shape=jax.ShapeDtypeStruct((2, HEADS, DK, DV), F32),
        compiler_params=_params(("arbitrary",)),
        name="ctx_states",
    )(ctx2d, nw, mod, w_in_bf, dec)


def _inproj_kernel(x_ref, nw_ref, mod_ref, w_ref, rot_row_ref, rot_col_ref, lnw_ref, lnb_ref, o_ref,
                   h_scr, g_scr, cos_ref, sin_ref):
    j = pl.program_id(1)
    tm = h_scr.shape[0]

    @pl.when(j == 0)
    def _():
        sh = _mod_slice(mod_ref, 0, 0)
        sc = _mod_slice(mod_ref, 0, 1)
        h_scr[...] = (_rms(x_ref[...], nw_ref[...]) * (1.0 + sc) + sh).astype(BF16)

    def pieces(emit, span, rsplit=1):
        for c0 in range(span.start, span.stop, DK):
            cols = slice(c0, c0 + DK)
            for r in range(rsplit):
                rows = slice(r * tm // rsplit, (r + 1) * tm // rsplit)
                emit(rows, cols, _dot(h_scr[rows, :], w_ref[:, cols]))

    def store(fn):
        def emit(rows, cols, acc):
            o_ref[rows, cols] = fn(acc).astype(BF16)
        return emit

    def rope(scale):
        half = DK // 2

        def emit(rows, cols, acc):
            cos = cos_ref[rows, :]
            sin = sin_ref[rows, :]
            x1 = acc[:, :half]
            x2 = acc[:, half:]
            o_ref[rows, cols.start:cols.start + half] = ((x1 * cos - x2 * sin) * scale).astype(BF16)
            o_ref[rows, cols.start + half:cols.stop] = ((x1 * sin + x2 * cos) * scale).astype(BF16)
        return emit

    def gelu_layer_norm(span):
        def emit(rows, cols, acc):
            g_scr[rows, cols.start - span.start:cols.stop - span.start] = _gelu(acc)
        pieces(emit, span)
        g = g_scr[...]
        mu = jnp.mean(g, axis=-1, keepdims=True)
        d = g - mu
        var = jnp.mean(d * d, axis=-1, keepdims=True)
        o_ref[:, span] = (d * lax.rsqrt(var + EPS) * lnw_ref[...] + lnb_ref[...]).astype(BF16)

    def rotary_tables():
        for r in range(tm // GRID_W):
            tok = slice(r * GRID_W, (r + 1) * GRID_W)
            cos_ref[tok, :] = rot_row_ref[0, r:r + 1, :] + rot_col_ref[0]
            sin_ref[tok, :] = rot_row_ref[1, r:r + 1, :] + rot_col_ref[1]

    def column_block(block, span):
        if block == COL_U:
            pieces(store(_gelu), span)
        elif block == COL_V:
            gelu_layer_norm(span)
        elif block == COL_Q:
            rotary_tables()
            pieces(rope(1.0), span)
        elif block == COL_K:
            pieces(rope(DK ** -0.5), span)
        elif block < COL_GF:
            pieces(store(lambda a: a), span)
        elif block < COL_GA:
            pieces(store(lambda a: a * _sigmoid(a)), span, rsplit=2)
        else:
            pieces(store(_sigmoid), span, rsplit=2)

    for step in range(IN_WIDTH // D_MODEL // INPROJ_NB):
        @pl.when(j == step)
        def _():
            for b in range(INPROJ_NB):
                column_block(step * INPROJ_NB + b, slice(b * D_MODEL, (b + 1) * D_MODEL))


def _inproj_call(x2d, nw, mod, w_in_bf, rot_row, rot_col, lnw, lnb):
    T = x2d.shape[0]
    tm = 1024
    half = DK // 2
    return pl.pallas_call(
        _inproj_kernel,
        grid=(T // tm, IN_WIDTH // (INPROJ_NB * D_MODEL)),
        in_specs=[
            pl.BlockSpec((tm, D_MODEL), lambda i, j: (i, 0)),
            pl.BlockSpec((1, D_MODEL), lambda i, j: (0, 0)),
            pl.BlockSpec(mod.shape, lambda i, j: (0, 0)),
            pl.BlockSpec((D_MODEL, INPROJ_NB * D_MODEL), lambda i, j: (0, j)),
            pl.BlockSpec((2, tm // GRID_W, half), lambda i, j: (0, i, 0)),
            pl.BlockSpec((2, GRID_W, half), lambda i, j: (0, 0, 0)),
            pl.BlockSpec((1, D_MODEL), lambda i, j: (0, 0)),
            pl.BlockSpec((1, D_MODEL), lambda i, j: (0, 0)),
        ],
        out_specs=pl.BlockSpec((tm, INPROJ_NB * D_MODEL), lambda i, j: (i, j)),
        out_shape=jax.ShapeDtypeStruct((T, IN_WIDTH), BF16),
        scratch_shapes=[
            pltpu.VMEM((tm, D_MODEL), BF16),
            pltpu.VMEM((tm, D_MODEL), F32),
            pltpu.VMEM((tm, half), F32),
            pltpu.VMEM((tm, half), F32),
        ],
        compiler_params=_params(("arbitrary", "arbitrary")),
        name="in_proj",
    )(x2d, nw, mod, w_in_bf, rot_row, rot_col, lnw, lnb)


def _sgu_kernel(u_ref, v_ref, ga_ref, ws_ref, bt_ref, wpa_ref, o_ref, ya_scr):
    tm = u_ref.shape[0]
    gd = D_MODEL // SGU_GROUPS
    for n in range(tm // SGU_CHUNK):
        rows = slice(n * SGU_CHUNK, (n + 1) * SGU_CHUNK)
        for g in range(SGU_GROUPS):
            cols = slice(g * gd, (g + 1) * gd)
            mixed = _dot(ws_ref[g], v_ref[rows, cols]) + bt_ref[:, g:g + 1]
            ya_scr[rows, cols] = (u_ref[rows, cols].astype(F32) * mixed).astype(BF16)
    pa = _dot(ya_scr[...], wpa_ref[...])
    o_ref[...] = (ga_ref[...].astype(F32) * pa).astype(BF16)


def _sgu_call(proj, ws_bf, bt, wpa_bf):
    T = proj.shape[0]
    tm = 1024
    return pl.pallas_call(
        _sgu_kernel,
        grid=(T // tm,),
        in_specs=[
            pl.BlockSpec((tm, D_MODEL), lambda i: (i, COL_U)),
            pl.BlockSpec((tm, D_MODEL), lambda i: (i, COL_V)),
            pl.BlockSpec((tm, D_MODEL), lambda i: (i, COL_GA)),
            pl.BlockSpec(ws_bf.shape, lambda i: (0, 0, 0)),
            pl.BlockSpec(bt.shape, lambda i: (0, 0)),
            pl.BlockSpec((D_MODEL, D_MODEL), lambda i: (0, 0)),
        ],
        out_specs=pl.BlockSpec((tm, D_MODEL), lambda i: (i, 0)),
        out_shape=jax.ShapeDtypeStruct((T, D_MODEL), BF16),
        scratch_shapes=[pltpu.VMEM((tm, D_MODEL), BF16)],
        compiler_params=_params(("arbitrary",)),
        name="sgu_proj_a",
    )(proj, proj, proj, ws_bf, bt, wpa_bf)


def _ret_kernel(qf_ref, kf_ref, vf_ref, gf_ref, qb_ref, kb_ref, vb_ref, gb_ref, s0_ref, dec_ref,
                yf_ref, yb_ref, s_scr, intra_scr, qd_scr, kd_scr, cd_scr):
    C = RET_CHUNK
    i = pl.program_id(0)

    @pl.when(i == 0)
    def _():
        lg = jnp.log1p(-jnp.exp2(dec_ref[...]))
        r = lax.broadcasted_iota(jnp.int32, (C, C), 0).astype(F32)
        c = lax.broadcasted_iota(jnp.int32, (C, C), 1).astype(F32)
        pos = lax.broadcasted_iota(jnp.int32, (C, 1), 0).astype(F32)
        for h in range(HEADS):
            lf = lg[0:1, h:h + 1]
            lb = lg[1:2, h:h + 1]
            intra_scr[0, h] = jnp.where(r >= c, jnp.exp(jnp.maximum(r - c, 0.0) * lf), 0.0)
            intra_scr[1, h] = jnp.where(c >= r, jnp.exp(jnp.maximum(c - r, 0.0) * lb), 0.0)
            qd_scr[0, h] = jnp.exp((pos + 1.0) * lf)
            qd_scr[1, h] = jnp.exp((C - pos) * lb)
            kd_scr[0, h] = jnp.exp((C - 1.0 - pos) * lf)
            kd_scr[1, h] = jnp.exp(pos * lb)
            cd_scr[0, h] = jnp.exp(C * lf)
            cd_scr[1, h] = jnp.exp(C * lb)
        s_scr[...] = s0_ref[...]

    dirs = ((qf_ref, kf_ref, vf_ref, gf_ref, yf_ref), (qb_ref, kb_ref, vb_ref, gb_ref, yb_ref))
    for d, (q_ref, k_ref, v_ref, g_ref, y_ref) in enumerate(dirs):
        for h in range(HEADS):
            q = q_ref[:, h * DK:(h + 1) * DK]
            k = k_ref[:, h * DK:(h + 1) * DK]
            v = v_ref[:, h * DV:(h + 1) * DV]
            s = s_scr[d, h]
            a = (_dot_nt(q, k) * intra_scr[d, h]).astype(BF16)
            o = _dot(a, v) + _dot(q, s.astype(BF16)) * qd_scr[d, h]
            hn = o * lax.rsqrt(jnp.mean(o * o, axis=-1, keepdims=True) + EPS)
            y_ref[:, h * DV:(h + 1) * DV] = (g_ref[:, h * DV:(h + 1) * DV].astype(F32) * hn).astype(BF16)
            kd = (k.astype(F32) * kd_scr[d, h]).astype(BF16)
            s_scr[d, h] = s * cd_scr[d, h] + _dot_tn(kd, v)


def _ret_call(proj, s0, dec):
    T = proj.shape[0]
    C = RET_CHUNK
    n = T // C
    qk_w = HEADS * DK
    v_w = HEADS * DV
    fwd = lambda col: (lambda i: (i, col))
    bwd = lambda col: (lambda i: (n - 1 - i, col))
    return pl.pallas_call(
        _ret_kernel,
        grid=(n,),
        in_specs=[
            pl.BlockSpec((C, qk_w), fwd(COL_Q)),
            pl.BlockSpec((C, qk_w), fwd(COL_K)),
            pl.BlockSpec((C, v_w), fwd(COL_VR // 2)),
            pl.BlockSpec((C, v_w), fwd(COL_GF // 2)),
            pl.BlockSpec((C, qk_w), bwd(COL_Q)),
            pl.BlockSpec((C, qk_w), bwd(COL_K)),
            pl.BlockSpec((C, v_w), bwd(COL_VR // 2)),
            pl.BlockSpec((C, v_w), bwd(COL_GB // 2)),
            pl.BlockSpec(s0.shape, lambda i: (0, 0, 0, 0)),
            pl.BlockSpec(dec.shape, lambda i: (0, 0)),
        ],
        out_specs=[
            pl.BlockSpec((C, v_w), lambda i: (i, 0)),
            pl.BlockSpec((C, v_w), lambda i: (n - 1 - i, 0)),
        ],
        out_shape=[jax.ShapeDtypeStruct((T, v_w), BF16), jax.ShapeDtypeStruct((T, v_w), BF16)],
        scratch_shapes=[
            pltpu.VMEM((2, HEADS, DK, DV), F32),
            pltpu.VMEM((2, HEADS, C, C), F32),
            pltpu.VMEM((2, HEADS, C, 1), F32),
            pltpu.VMEM((2, HEADS, C, 1), F32),
            pltpu.VMEM((2, HEADS, 1, 1), F32),
        ],
        compiler_params=_params(("arbitrary",)),
        name="retention",
    )(proj, proj, proj, proj, proj, proj, proj, proj, s0, dec)


def _merge_kernel(yf_ref, yb_ref, za_ref, gbr_ref, x_ref, mod_ref, nw_ref, wpb_ref, wo_ref, rw_ref, rb_ref,
                  x1_ref, h2_ref, lg_ref):
    yb = yf_ref[...] + yb_ref[...]
    pb = _dot(yb, wpb_ref[...])
    y = za_ref[...].astype(F32) + gbr_ref[...].astype(F32) * pb
    yo = _dot(y.astype(BF16), wo_ref[...])
    x1 = x_ref[...] + _mod_slice(mod_ref, 0, 2) * yo
    x1_ref[...] = x1
    h2 = _rms(x1, nw_ref[...]) * (1.0 + _mod_slice(mod_ref, 0, 4)) + _mod_slice(mod_ref, 0, 3)
    h2_hi = h2.astype(BF16)
    h2_ref[...] = h2_hi
    h2_lo = (h2 - h2_hi.astype(F32)).astype(BF16)
    rw = rw_ref[...]
    rw_hi = rw.astype(BF16)
    rw_lo = (rw - rw_hi.astype(F32)).astype(BF16)
    both = _dot_nt(jnp.concatenate([rw_hi, rw_lo], axis=0), h2_hi)
    lg_ref[...] = both[:N_EXPERTS] + both[N_EXPERTS:] + _dot_nt(rw_hi, h2_lo) + rb_ref[...]


def _merge_call(yf, yb, za, proj, x2d, mod, nw2, wpb_bf, wo_bf, rw, rb):
    T = x2d.shape[0]
    tm = 512
    row = lambda i: (i, 0)
    fixed = lambda i: (0, 0)
    return pl.pallas_call(
        _merge_kernel,
        grid=(T // tm,),
        in_specs=[
            pl.BlockSpec((tm, HEADS * DV), row),
            pl.BlockSpec((tm, HEADS * DV), row),
            pl.BlockSpec((tm, D_MODEL), row),
            pl.BlockSpec((tm, D_MODEL), lambda i: (i, COL_GBR)),
            pl.BlockSpec((tm, D_MODEL), row),
            pl.BlockSpec(mod.shape, fixed),
            pl.BlockSpec((1, D_MODEL), fixed),
            pl.BlockSpec(wpb_bf.shape, fixed),
            pl.BlockSpec(wo_bf.shape, fixed),
            pl.BlockSpec(rw.shape, fixed),
            pl.BlockSpec(rb.shape, fixed),
        ],
        out_specs=[
            pl.BlockSpec((tm, D_MODEL), row),
            pl.BlockSpec((tm, D_MODEL), row),
            pl.BlockSpec((N_EXPERTS, tm), lambda i: (0, i)),
        ],
        out_shape=[
            jax.ShapeDtypeStruct((T, D_MODEL), F32),
            jax.ShapeDtypeStruct((T, D_MODEL), BF16),
            jax.ShapeDtypeStruct((N_EXPERTS, T), F32),
        ],
        compiler_params=_params(("arbitrary",)),
        name="merge_router",
    )(yf, yb, za, proj, x2d, mod, nw2, wpb_bf, wo_bf, rw, rb)


def _route_kernel(lg_ref, pos_ref, tok_ref, nch_ref, su_scr):
    E, tt = lg_ref.shape

    @pl.when(pl.program_id(0) == 0)
    def _():
        r = lax.broadcasted_iota(jnp.int32, (tt, tt), 0)
        c = lax.broadcasted_iota(jnp.int32, (tt, tt), 1)
        su_scr[...] = jnp.where(r < c, 1.0, 0.0).astype(BF16)

    sub = lax.broadcasted_iota(jnp.int32, (E, tt), 0)
    work = lg_ref[...]
    vals, hots = [], []
    for _ in range(TOP_K):
        m = jnp.max(work, axis=0, keepdims=True)
        first = jnp.min(jnp.where(work == m, sub, E), axis=0, keepdims=True)
        hot = sub == first
        vals.append(m)
        hots.append(hot)
        work = jnp.where(hot, -jnp.inf, work)
    exps = [jnp.exp(v - vals[0]) for v in vals]
    inv = 1.0 / functools.reduce(lambda a, b: a + b, exps)

    member = functools.reduce(lambda a, b: a + b, [jnp.where(h, 1.0, 0.0) for h in hots])
    cnt = jnp.sum(member, axis=1, keepdims=True)
    nch = jnp.floor((cnt + (MOE_CH - 1.0)) * (1.0 / MOE_CH))
    nch_b = jnp.broadcast_to(nch, (E, 128))
    rank = _dot(member.astype(BF16), su_scr[...])
    er = lax.broadcasted_iota(jnp.int32, (E, E), 0)
    ec = lax.broadcasted_iota(jnp.int32, (E, E), 1)
    first_chunk = _dot(jnp.where(ec < er, 1.0, 0.0).astype(BF16), nch_b.astype(BF16))[:, 0:1]
    base = first_chunk * MOE_CH + rank
    pos = [jnp.sum(jnp.where(hots[k], base, 0.0), axis=0, keepdims=True) for k in range(TOP_K)]
    for k in range(TOP_K):
        pos_ref[k:k + 1, :] = pos[k].astype(jnp.int32)
    rows = jnp.concatenate(pos + [e * inv for e in exps] + [jnp.zeros((128 - 2 * TOP_K, tt), F32)], axis=0)
    tok_ref[...] = rows.T
    nch_ref[0] = nch_b.astype(jnp.int32)


def _route_call(logits_t):
    E, T = logits_t.shape
    tt = MOE_TT
    return pl.pallas_call(
        _route_kernel,
        grid=(T // tt,),
        in_specs=[pl.BlockSpec((E, tt), lambda j: (0, j))],
        out_specs=[
            pl.BlockSpec((TOP_K, tt), lambda j: (0, j)),
            pl.BlockSpec((tt, 128), lambda j: (j, 0)),
            pl.BlockSpec((1, E, 128), lambda j: (j, 0, 0)),
        ],
        out_shape=[
            jax.ShapeDtypeStruct((TOP_K, T), jnp.int32),
            jax.ShapeDtypeStruct((T, 128), F32),
            jax.ShapeDtypeStruct((T // tt, E, 128), jnp.int32),
        ],
        scratch_shapes=[pltpu.VMEM((tt, tt), BF16)],
        compiler_params=_params(("arbitrary",)),
        name="moe_route",
    )(logits_t)


def _slot_tiles_max(n_tiles):
    return -(-(n_tiles * MOE_MAXC + N_EXPERTS * (MOE_CPT - 1)) // MOE_CPT)


def _routing_tables(nch):
    n_tiles, E = nch.shape

    def cumsum_last(a):
        m = a.shape[-1]
        keep = jnp.arange(m)[:, None] <= jnp.arange(m)[None, :]
        return jnp.sum(jnp.where(keep, a[..., :, None], 0), axis=-2)

    tot = jnp.sum(nch, axis=0)
    seg = (tot + MOE_CPT - 1) // MOE_CPT * MOE_CPT
    seg_end = cumsum_last(seg)
    seg_start = seg_end - seg
    gstart = seg_start[None, :] + cumsum_last(nch.T).T - nch
    l_end = cumsum_last(nch)
    lstart = l_end - nch

    def copy_list(count, first_local, first_global, step, max_len):
        end = cumsum_last(count)
        start = end - count
        i = jnp.arange(max_len, dtype=jnp.int32)[None, :, None]
        owner = (start[:, None, :] <= i) & (i < end[:, None, :])
        off = (i - start[:, None, :]) * step
        src = jnp.sum(jnp.where(owner, first_local[:, None, :] + off, 0), axis=-1)
        dst = jnp.sum(jnp.where(owner, first_global[:, None, :] + off, 0), axis=-1)
        return src, dst, end[:, -1]

    runs = nch // MOE_RUN
    rsrc, rdst, nruns = copy_list(runs, lstart, gstart, MOE_RUN, MOE_MAXQ)
    ssrc, sdst, nsingles = copy_list(nch - runs * MOE_RUN, lstart + runs * MOE_RUN, gstart + runs * MOE_RUN, 1,
                                     MOE_MAXS)
    copies = jnp.concatenate([rsrc, rdst, ssrc, sdst], axis=1).astype(jnp.int32).reshape(-1)
    ncopies = jnp.stack([nruns, nsingles], axis=1).astype(jnp.int32).reshape(-1)
    nt_max = _slot_tiles_max(n_tiles)
    n_used = (seg_end[-1] // MOE_CPT).astype(jnp.int32)
    tile_blk = jnp.minimum(jnp.arange(nt_max, dtype=jnp.int32), n_used - 1)
    tile_exp = jnp.sum(seg_end[None, :] <= (tile_blk * MOE_CPT)[:, None], axis=1)
    tile_exp = jnp.minimum(tile_exp, E - 1).astype(jnp.int32)
    experts = jnp.arange(E, dtype=jnp.int32)
    own = tile_exp[:, None] == experts[None, :]
    end_tile = jnp.sum(jnp.where(own, (seg_end // MOE_CPT)[None, :], 0), axis=1)
    follows = ((seg_start // MOE_CPT)[None, :] == end_tile[:, None]) & (seg > 0)[None, :]
    next_exp = jnp.where(jnp.any(follows, axis=1), jnp.sum(jnp.where(follows, experts[None, :], 0), axis=1), -1)
    next_exp = next_exp.astype(jnp.int32)
    zstart = (seg_start + tot).astype(jnp.int32)
    zcount = (seg - tot).astype(jnp.int32)
    return copies, ncopies, zstart, zcount, tile_exp, tile_blk, n_used.reshape(1), next_exp


def _chunk_rows(c, nchunks=1):
    return pl.ds(pl.multiple_of(c * MOE_CH, MOE_CH), nchunks * MOE_CH)


def _for_tile_copies(cp_ref, cn_ref, tile, make, op):
    base = tile * MOE_CPROW

    def run(i, carry):
        op(make(cp_ref[base + i], cp_ref[base + MOE_MAXQ + i], MOE_RUN))
        return carry
    lax.fori_loop(0, cn_ref[2 * tile], run, 0)

    def single(i, carry):
        op(make(cp_ref[base + 2 * MOE_MAXQ + i], cp_ref[base + 2 * MOE_MAXQ + MOE_MAXS + i], 1))
        return carry
    lax.fori_loop(0, cn_ref[2 * tile + 1], single, 0)


def _dispatch_kernel(cp_ref, cn_ref, zstart_ref, zcount_ref, nu_ref, h2_ref, pos_ref, xs_hbm, xl_scr, zero_scr,
                     sems, zsem):
    j = pl.program_id(0)
    n = pl.num_programs(0)
    slot = j % 2
    tt = h2_ref.shape[0]
    nt_max = xs_hbm.shape[0] // MOE_TM

    def tile_copies(tile, slot_, op):
        def make(local, glob, nchunks):
            return pltpu.make_async_copy(xl_scr.at[slot_, _chunk_rows(local, nchunks)],
                                         xs_hbm.at[_chunk_rows(glob, nchunks)], sems.at[slot_])
        _for_tile_copies(cp_ref, cn_ref, tile, make, op)

    def zero_chunk_copy(d):
        return pltpu.make_async_copy(zero_scr.at[pl.ds(0, MOE_CH)], xs_hbm.at[_chunk_rows(d)], zsem.at[0])

    def zero_tile_copy(t):
        rows = pl.ds(pl.multiple_of(t * MOE_TM, MOE_TM), MOE_TM)
        return pltpu.make_async_copy(zero_scr, xs_hbm.at[rows], zsem.at[0])

    def zero_fill(op):
        def expert(e, carry):
            def chunk(r, carry_):
                op(zero_chunk_copy(zstart_ref[e] + r))
                return carry_
            return lax.fori_loop(0, zcount_ref[e], chunk, carry)
        lax.fori_loop(0, N_EXPERTS, expert, 0)

        def tile(t, carry):
            op(zero_tile_copy(t))
            return carry
        lax.fori_loop(nu_ref[0], nt_max, tile, 0)

    def wait_tile(tile, slot_):
        tile_copies(tile, slot_, lambda cp: cp.wait())

    @pl.when(j == 0)
    def _():
        zero_scr[...] = jnp.zeros_like(zero_scr)
        zero_fill(lambda cp: cp.start())

    @pl.when(j >= 2)
    def _():
        wait_tile(j - 2, slot)

    pos = pos_ref[...]
    h2 = h2_ref[...]
    for rb in range(MOE_RLOC // MOE_RB):
        io = lax.broadcasted_iota(jnp.int32, (MOE_RB, tt), 0) + rb * MOE_RB
        onehot = jnp.zeros((MOE_RB, tt), F32)
        for k in range(TOP_K):
            onehot = jnp.where(io == pos[k:k + 1, :], 1.0, onehot)
        xl_scr[slot, rb * MOE_RB:(rb + 1) * MOE_RB, :] = _dot(onehot.astype(BF16), h2).astype(BF16)

    tile_copies(j, slot, lambda cp: cp.start())

    @pl.when(j == n - 1)
    def _():
        @pl.when(j >= 1)
        def _():
            wait_tile(j - 1, 1 - slot)
        wait_tile(j, slot)
        zero_fill(lambda cp: cp.wait())


def _dispatch_call(tables, h2, pos_t, n_slots):
    T = h2.shape[0]
    tt = MOE_TT
    copies, ncopies, zstart, zcount, n_used = tables
    grid_spec = pltpu.PrefetchScalarGridSpec(
        num_scalar_prefetch=5,
        grid=(T // tt,),
        in_specs=[
            pl.BlockSpec((tt, D_MODEL), lambda j, *_: (j, 0)),
            pl.BlockSpec((TOP_K, tt), lambda j, *_: (0, j)),
        ],
        out_specs=pl.BlockSpec(memory_space=pl.ANY),
        scratch_shapes=[
            pltpu.VMEM((2, MOE_RLOC, D_MODEL), BF16),
            pltpu.VMEM((MOE_TM, D_MODEL), BF16),
            pltpu.SemaphoreType.DMA((2,)),
            pltpu.SemaphoreType.DMA((1,)),
        ],
    )
    return pl.pallas_call(
        _dispatch_kernel,
        grid_spec=grid_spec,
        out_shape=jax.ShapeDtypeStruct((n_slots, D_MODEL), BF16),
        compiler_params=_params(("arbitrary",)),
        name="moe_dispatch",
    )(copies, ncopies, zstart, zcount, n_used, h2, pos_t)


def _experts_kernel(te_ref, tb_ref, nu_ref, nx_ref, xs_ref, w1_hbm, b1_ref, w2_hbm, b2_ref, y_ref,
                    w1_f32, w2_f32, w1_scr, w2_scr, sems):
    i = pl.program_id(0)
    valid = i < nu_ref[0]
    new_expert = (i == 0) | (te_ref[i] != te_ref[jnp.maximum(i - 1, 0)])

    def fetch(e):
        return (pltpu.make_async_copy(w1_hbm.at[e], w1_f32, sems.at[0]),
                pltpu.make_async_copy(w2_hbm.at[e], w2_f32, sems.at[1]))

    @pl.when(i == 0)
    def _():
        for cp in fetch(te_ref[0]):
            cp.start()

    @pl.when(valid & new_expert)
    def _():
        for cp in fetch(te_ref[i]):
            cp.wait()
        w1_scr[...] = w1_f32[...].astype(BF16)
        w2_scr[...] = w2_f32[...].astype(BF16)

        @pl.when(nx_ref[i] >= 0)
        def _():
            for cp in fetch(nx_ref[i]):
                cp.start()

    @pl.when(valid)
    def _():
        hh = _dot(xs_ref[...], w1_scr[...]) + b1_ref[...]
        gate = jnp.minimum(hh[:, :D_FF], SWIGLU_LIMIT)
        up = jnp.clip(hh[:, D_FF:], -SWIGLU_LIMIT, SWIGLU_LIMIT)
        act = (up + 1.0) * gate * _sigmoid(SWIGLU_ALPHA * gate)
        y_ref[...] = (_dot(act.astype(BF16), w2_scr[...]) + b2_ref[...]).astype(BF16)


def _experts_call(tile_exp, tile_blk, n_used, next_exp, xs, w1, b1, w2, b2):
    nt_max = tile_exp.shape[0]
    slot_tile = lambda i, te, tb, nu, nx: (tb[i], 0)
    per_e = lambda i, te, tb, nu, nx: (te[i], 0, 0)
    grid_spec = pltpu.PrefetchScalarGridSpec(
        num_scalar_prefetch=4,
        grid=(nt_max,),
        in_specs=[
            pl.BlockSpec((MOE_TM, D_MODEL), slot_tile),
            pl.BlockSpec(memory_space=pl.ANY),
            pl.BlockSpec((None, 1, 2 * D_FF), per_e),
            pl.BlockSpec(memory_space=pl.ANY),
            pl.BlockSpec((None, 1, D_MODEL), per_e),
        ],
        out_specs=pl.BlockSpec((MOE_TM, D_MODEL), slot_tile),
        scratch_shapes=[
            pltpu.VMEM((D_MODEL, 2 * D_FF), F32),
            pltpu.VMEM((D_FF, D_MODEL), F32),
            pltpu.VMEM((D_MODEL, 2 * D_FF), BF16),
            pltpu.VMEM((D_FF, D_MODEL), BF16),
            pltpu.SemaphoreType.DMA((2,)),
        ],
    )
    return pl.pallas_call(
        _experts_kernel,
        grid_spec=grid_spec,
        out_shape=jax.ShapeDtypeStruct(xs.shape, BF16),
        input_output_aliases={4: 0},
        compiler_params=_params(("arbitrary",)),
        name="moe_experts",
    )(tile_exp, tile_blk, n_used, next_exp, xs, w1, b1, w2, b2)


def _combine_kernel(cp_ref, cn_ref, tok_ref, x1_ref, mod_ref, fw_ref, y_hbm, o_ref, yl_scr, w_scr, sems):
    j = pl.program_id(0)
    n = pl.num_programs(0)
    slot = j % 2
    tt = x1_ref.shape[0]

    def tile_copies(tile, slot_, op):
        def make(local, glob, nchunks):
            return pltpu.make_async_copy(y_hbm.at[_chunk_rows(glob, nchunks)],
                                         yl_scr.at[slot_, _chunk_rows(local, nchunks)], sems.at[slot_])
        _for_tile_copies(cp_ref, cn_ref, tile, make, op)

    def start_tile(tile, slot_):
        tile_copies(tile, slot_, lambda cp: cp.start())

    @pl.when(j == 0)
    def _():
        yl_scr[...] = jnp.zeros_like(yl_scr)
        start_tile(0, 0)

    @pl.when(j + 1 < n)
    def _():
        start_tile(j + 1, 1 - slot)

    tile_copies(j, slot, lambda cp: cp.wait())

    pos = tok_ref[:, 0:TOP_K].astype(jnp.int32)
    p = tok_ref[:, TOP_K:2 * TOP_K]
    for cb in range(MOE_RLOC // MOE_RB):
        io = lax.broadcasted_iota(jnp.int32, (tt, MOE_RB), 1) + cb * MOE_RB
        w = jnp.zeros((tt, MOE_RB), F32)
        for k in range(TOP_K):
            w = jnp.where(io == pos[:, k:k + 1], p[:, k:k + 1], w)
        w_scr[:, cb * MOE_RB:(cb + 1) * MOE_RB] = w.astype(BF16)
    moe = _dot(w_scr[...], yl_scr[slot])
    x2 = x1_ref[...] + _mod_slice(mod_ref, 0, 5) * moe
    o_ref[...] = _rms(x2, fw_ref[...])


def _combine_call(copies, ncopies, tok, x1, mod, fw, y):
    T = x1.shape[0]
    tt = MOE_TT
    row = lambda j, *_: (j, 0)
    fixed = lambda j, *_: (0, 0)
    grid_spec = pltpu.PrefetchScalarGridSpec(
        num_scalar_prefetch=2,
        grid=(T // tt,),
        in_specs=[
            pl.BlockSpec((tt, 128), row),
            pl.BlockSpec((tt, D_MODEL), row),
            pl.BlockSpec(mod.shape, fixed),
            pl.BlockSpec((1, D_MODEL), fixed),
            pl.BlockSpec(memory_space=pl.ANY),
        ],
        out_specs=pl.BlockSpec((tt, D_MODEL), row),
        scratch_shapes=[
            pltpu.VMEM((2, MOE_RLOC, D_MODEL), BF16),
            pltpu.VMEM((tt, MOE_RLOC), BF16),
            pltpu.SemaphoreType.DMA((2,)),
        ],
    )
    return pl.pallas_call(
        _combine_kernel,
        grid_spec=grid_spec,
        out_shape=jax.ShapeDtypeStruct((T, D_MODEL), F32),
        compiler_params=_params(("arbitrary",)),
        name="moe_combine",
    )(copies, ncopies, tok, x1, mod, fw, y)


def _rope_tables(T):
    rows = T // GRID_W
    n_freq = DK // 4
    inv = ROPE_BASE ** (-jnp.arange(n_freq, dtype=F32) / n_freq)
    row_ang = jnp.arange(rows, dtype=F32)[:, None] * inv
    col_ang = jnp.arange(GRID_W, dtype=F32)[:, None] * inv
    rot_row = jnp.pad(jnp.stack([jnp.cos(row_ang), jnp.sin(row_ang)]), ((0, 0), (0, 0), (0, n_freq)))
    rot_col = jnp.pad(jnp.stack([jnp.cos(col_ang), jnp.sin(col_ang)]), ((0, 0), (0, 0), (n_freq, 0)))
    return rot_row, rot_col


def kernel(x, c, ctx, c_ctx, w_mod, b_mod, norm1_w, norm2_w, w_in, sgu_ln_w, sgu_ln_b, sgu_w, sgu_b,
           ret_decay_fwd, ret_decay_bwd, w_proj_a, w_proj_b, w_out, router_w, router_b,
           moe_w1, moe_b1, moe_w2, moe_b2, final_norm_w):
    B, T, D = x.shape
    assert B == 1 and D == D_MODEL and w_mod.shape[0] == 1 and T % 1024 == 0
    x2d = x.reshape(T, D)
    cc = jnp.stack([c.reshape(D), c_ctx.reshape(D)], axis=1)
    mod = _mod_call(cc, w_mod[0], b_mod)
    dec = jnp.stack([ret_decay_fwd[0], ret_decay_bwd[0]], axis=0)
    w_in_bf = w_in[0].astype(BF16)
    s0 = _ctx_call(ctx.reshape(ctx.shape[1], D), norm1_w, mod, w_in_bf, dec)
    rot_row, rot_col = _rope_tables(T)
    proj = _inproj_call(x2d, norm1_w, mod, w_in_bf, rot_row, rot_col, sgu_ln_w, sgu_ln_b)
    za = _sgu_call(proj, sgu_w[0].astype(BF16), sgu_b[0].T, w_proj_a[0].astype(BF16))
    yf, yb = _ret_call(proj, s0, dec)
    x1, h2, logits_t = _merge_call(yf, yb, za, proj, x2d, mod, norm2_w, w_proj_b[0].astype(BF16),
                                   w_out[0].astype(BF16), router_w[0].T, router_b.reshape(N_EXPERTS, 1))
    pos_t, tok, nch = _route_call(logits_t)
    copies, ncopies, zstart, zcount, tile_exp, tile_blk, n_used, next_exp = _routing_tables(nch[:, :, 0])
    n_slots = tile_exp.shape[0] * MOE_TM
    xs = _dispatch_call((copies, ncopies, zstart, zcount, n_used), h2, pos_t, n_slots)
    y = _experts_call(tile_exp, tile_blk, n_used, next_exp, xs, moe_w1[0], moe_b1[0][:, None, :], moe_w2[0],
                      moe_b2[0][:, None, :])
    out = _combine_call(copies, ncopies, tok, x1, mod, final_norm_w.reshape(1, D), y)
    return out.reshape(B, T, D)
```

```python
import functools

import jax
import jax.numpy as jnp
from jax import lax
from jax.experimental import pallas as pl
from jax.experimental.pallas import tpu as pltpu

D_MODEL = 1024
GRID_W = 64
SGU_CHUNK = 128
SGU_GROUPS = 8
HEADS = 4
DK = D_MODEL // HEADS
DV = 2 * DK
ROPE_BASE = 10000.0
N_EXPERTS = 32
TOP_K = 4
D_FF = D_MODEL
SWIGLU_LIMIT = 7.0
SWIGLU_ALPHA = 1.702
EPS = 1e-6
IN_WIDTH = 12 * D_MODEL
COL_U, COL_V, COL_Q, COL_K, COL_VR, COL_GF, COL_GB, COL_GA, COL_GBR = 0, 1, 2, 3, 4, 6, 8, 10, 11

INPROJ_NB = 4
RET_CHUNK = 256
RET_SUB = 1
VMEM_LIMIT = 56 * 1024 * 1024

MOE_TT = 512
MOE_CH = 16
MOE_TM = 512
MOE_CPT = MOE_TM // MOE_CH
MOE_MAXC = TOP_K * MOE_TT // MOE_CH + N_EXPERTS
MOE_RLOC = MOE_MAXC * MOE_CH
MOE_RB = 256
MOE_RUN = 4
MOE_MAXQ = MOE_MAXC // MOE_RUN
MOE_MAXS = (MOE_RUN - 1) * N_EXPERTS
MOE_CPROW = 2 * (MOE_MAXQ + MOE_MAXS)

F32 = jnp.float32
BF16 = jnp.bfloat16


def _params(sem):
    return pltpu.CompilerParams(dimension_semantics=sem, vmem_limit_bytes=VMEM_LIMIT)


def _dot(a, b):
    return jnp.dot(a, b, preferred_element_type=F32)


def _dot_nt(a, b):
    return lax.dot_general(a, b, (((1,), (1,)), ((), ())), preferred_element_type=F32)


def _dot_tn(a, b):
    return lax.dot_general(a, b, (((0,), (0,)), ((), ())), preferred_element_type=F32)


def _rms(x, w):
    return x * lax.rsqrt(jnp.mean(x * x, axis=-1, keepdims=True) + EPS) * w


def _gelu(x):
    return 0.5 * x * (1.0 + lax.erf(x * (2.0 ** -0.5)))


def _sigmoid(x):
    return 0.5 * jnp.tanh(0.5 * x) + 0.5


def _mod_slice(mod_ref, row, k):
    return mod_ref[row:row + 1, k * D_MODEL:(k + 1) * D_MODEL]


def _mod_kernel(cc_ref, w_ref, b_ref, o_ref):
    s = cc_ref[...]
    s = s * jax.nn.sigmoid(s)
    w = w_ref[...]
    r0 = jnp.sum(s[:, 0:1] * w, axis=0, keepdims=True)
    r1 = jnp.sum(s[:, 1:2] * w, axis=0, keepdims=True)
    o_ref[...] = jnp.concatenate([r0, r1], axis=0) + b_ref[...]


def _mod_call(cc, w_mod, b_mod):
    bn = 1536
    n = w_mod.shape[1]
    return pl.pallas_call(
        _mod_kernel,
        grid=(n // bn,),
        in_specs=[
            pl.BlockSpec((D_MODEL, 2), lambda j: (0, 0)),
            pl.BlockSpec((D_MODEL, bn), lambda j: (0, j)),
            pl.BlockSpec((1, bn), lambda j: (0, j)),
        ],
        out_specs=pl.BlockSpec((2, bn), lambda j: (0, j)),
        out_shape=jax.ShapeDtypeStruct((2, n), F32),
        compiler_params=_params(("arbitrary",)),
        name="mod_vectors",
    )(cc, w_mod, b_mod)


def _ctx_kernel(ctx_ref, nw_ref, mod_ref, w_ref, dec_ref, s0_ref):
    L = ctx_ref.shape[0]
    sh = _mod_slice(mod_ref, 1, 0)
    sc = _mod_slice(mod_ref, 1, 1)
    hc = (_rms(ctx_ref[...], nw_ref[...]) * (1.0 + sc) + sh).astype(BF16)
    kv = _dot(hc, w_ref[...])
    lg = jnp.log1p(-jnp.exp2(dec_ref[...]))
    pos = lax.broadcasted_iota(jnp.int32, (L, 1), 0).astype(F32)
    for h in range(HEADS):
        k = kv[:, h * DK:(h + 1) * DK] * (DK ** -0.5)
        v = kv[:, HEADS * DK + h * DV:HEADS * DK + (h + 1) * DV].astype(BF16)
        wf = jnp.exp((L - 1.0 - pos) * lg[0:1, h:h + 1])
        wb = jnp.exp(pos * lg[1:2, h:h + 1])
        s0_ref[0, h] = _dot_tn((k * wf).astype(BF16), v)
        s0_ref[1, h] = _dot_tn((k * wb).astype(BF16), v)


def _ctx_call(ctx2d, nw, mod, w_in_bf, dec):
    L = ctx2d.shape[0]
    wcols = HEADS * DK + HEADS * DV
    return pl.pallas_call(
        _ctx_kernel,
        grid=(1,),
        in_specs=[
            pl.BlockSpec((L, D_MODEL), lambda i: (0, 0)),
            pl.BlockSpec((1, D_MODEL), lambda i: (0, 0)),
            pl.BlockSpec(mod.shape, lambda i: (0, 0)),
            pl.BlockSpec((D_MODEL, wcols), lambda i: (0, COL_K * D_MODEL // wcols)),
            pl.BlockSpec(dec.shape, lambda i: (0, 0)),
        ],
        out_specs=pl.BlockSpec((2, HEADS, DK, DV), lambda i: (0, 0, 0, 0)),
        out_shape=jax.ShapeDtypeStruct((2, HEADS, DK, DV), F32),
        compiler_params=_params(("arbitrary",)),
        name="ctx_states",
    )(ctx2d, nw, mod, w_in_bf, dec)


def _inproj_kernel(x_ref, nw_ref, mod_ref, w_ref, rot_row_ref, rot_col_ref, lnw_ref, lnb_ref, o_ref,
                   h_scr, g_scr, cos_ref, sin_ref):
    j = pl.program_id(1)
    tm = h_scr.shape[0]

    @pl.when(j == 0)
    def _():
        sh = _mod_slice(mod_ref, 0, 0)
        sc = _mod_slice(mod_ref, 0, 1)
        h_scr[...] = (_rms(x_ref[...], nw_ref[...]) * (1.0 + sc) + sh).astype(BF16)

    def pieces(emit, span, rsplit=1):
        for c0 in range(span.start, span.stop, DK):
            cols = slice(c0, c0 + DK)
            for r in range(rsplit):
                rows = slice(r * tm // rsplit, (r + 1) * tm // rsplit)
                emit(rows, cols, _dot(h_scr[rows, :], w_ref[:, cols]))

    def store(fn):
        def emit(rows, cols, acc):
            o_ref[rows, cols] = fn(acc).astype(BF16)
        return emit

    def rope(scale):
        half = DK // 2

        def emit(rows, cols, acc):
            cos = cos_ref[rows, :]
            sin = sin_ref[rows, :]
            x1 = acc[:, :half]
            x2 = acc[:, half:]
            o_ref[rows, cols.start:cols.start + half] = ((x1 * cos - x2 * sin) * scale).astype(BF16)
            o_ref[rows, cols.start + half:cols.stop] = ((x1 * sin + x2 * cos) * scale).astype(BF16)
        return emit

    def gelu_layer_norm(span):
        def emit(rows, cols, acc):
            g_scr[rows, cols.start - span.start:cols.stop - span.start] = _gelu(acc)
        pieces(emit, span)
        g = g_scr[...]
        mu = jnp.mean(g, axis=-1, keepdims=True)
        d = g - mu
        var = jnp.mean(d * d, axis=-1, keepdims=True)
        o_ref[:, span] = (d * lax.rsqrt(var + EPS) * lnw_ref[...] + lnb_ref[...]).astype(BF16)

    def rotary_tables():
        for r in range(tm // GRID_W):
            tok = slice(r * GRID_W, (r + 1) * GRID_W)
            cos_ref[tok, :] = rot_row_ref[0, r:r + 1, :] + rot_col_ref[0]
            sin_ref[tok, :] = rot_row_ref[1, r:r + 1, :] + rot_col_ref[1]

    def column_block(block, span):
        if block == COL_U:
            pieces(store(_gelu), span)
        elif block == COL_V:
            gelu_layer_norm(span)
        elif block == COL_Q:
            rotary_tables()
            pieces(rope(1.0), span)
        elif block == COL_K:
            pieces(rope(DK ** -0.5), span)
        elif block < COL_GF:
            pieces(store(lambda a: a), span)
        elif block < COL_GA:
            pieces(store(lambda a: a * _sigmoid(a)), span, rsplit=2)
        else:
            pieces(store(_sigmoid), span, rsplit=2)

    for step in range(IN_WIDTH // D_MODEL // INPROJ_NB):
        @pl.when(j == step)
        def _():
            for b in range(INPROJ_NB):
                column_block(step * INPROJ_NB + b, slice(b * D_MODEL, (b + 1) * D_MODEL))


def _inproj_call(x2d, nw, mod, w_in_bf, rot_row, rot_col, lnw, lnb):
    T = x2d.shape[0]
    tm = 1024
    half = DK // 2
    return pl.pallas_call(
        _inproj_kernel,
        grid=(T // tm, IN_WIDTH // (INPROJ_NB * D_MODEL)),
        in_specs=[
            pl.BlockSpec((tm, D_MODEL), lambda i, j: (i, 0)),
            pl.BlockSpec((1, D_MODEL), lambda i, j: (0, 0)),
            pl.BlockSpec(mod.shape, lambda i, j: (0, 0)),
            pl.BlockSpec((D_MODEL, INPROJ_NB * D_MODEL), lambda i, j: (0, j)),
            pl.BlockSpec((2, tm // GRID_W, half), lambda i, j: (0, i, 0)),
            pl.BlockSpec((2, GRID_W, half), lambda i, j: (0, 0, 0)),
            pl.BlockSpec((1, D_MODEL), lambda i, j: (0, 0)),
            pl.BlockSpec((1, D_MODEL), lambda i, j: (0, 0)),
        ],
        out_specs=pl.BlockSpec((tm, INPROJ_NB * D_MODEL), lambda i, j: (i, j)),
        out_shape=jax.ShapeDtypeStruct((T, IN_WIDTH), BF16),
        scratch_shapes=[
            pltpu.VMEM((tm, D_MODEL), BF16),
            pltpu.VMEM((tm, D_MODEL), F32),
            pltpu.VMEM((tm, half), F32),
            pltpu.VMEM((tm, half), F32),
        ],
        compiler_params=_params(("arbitrary", "arbitrary")),
        name="in_proj",
    )(x2d, nw, mod, w_in_bf, rot_row, rot_col, lnw, lnb)


def _sgu_kernel(u_ref, v_ref, ga_ref, ws_ref, bt_ref, wpa_ref, o_ref, ya_scr):
    tm = u_ref.shape[0]
    gd = D_MODEL // SGU_GROUPS
    for n in range(tm // SGU_CHUNK):
        rows = slice(n * SGU_CHUNK, (n + 1) * SGU_CHUNK)
        for g in range(SGU_GROUPS):
            cols = slice(g * gd, (g + 1) * gd)
            mixed = _dot(ws_ref[g], v_ref[rows, cols]) + bt_ref[:, g:g + 1]
            ya_scr[rows, cols] = (u_ref[rows, cols].astype(F32) * mixed).astype(BF16)
    pa = _dot(ya_scr[...], wpa_ref[...])
    o_ref[...] = (ga_ref[...].astype(F32) * pa).astype(BF16)


def _sgu_call(proj, ws_bf, bt, wpa_bf):
    T = proj.shape[0]
    tm = 1024
    return pl.pallas_call(
        _sgu_kernel,
        grid=(T // tm,),
        in_specs=[
            pl.BlockSpec((tm, D_MODEL), lambda i: (i, COL_U)),
            pl.BlockSpec((tm, D_MODEL), lambda i: (i, COL_V)),
            pl.BlockSpec((tm, D_MODEL), lambda i: (i, COL_GA)),
            pl.BlockSpec(ws_bf.shape, lambda i: (0, 0, 0)),
            pl.BlockSpec(bt.shape, lambda i: (0, 0)),
            pl.BlockSpec((D_MODEL, D_MODEL), lambda i: (0, 0)),
        ],
        out_specs=pl.BlockSpec((tm, D_MODEL), lambda i: (i, 0)),
        out_shape=jax.ShapeDtypeStruct((T, D_MODEL), BF16),
        scratch_shapes=[pltpu.VMEM((tm, D_MODEL), BF16)],
        compiler_params=_params(("arbitrary",)),
        name="sgu_proj_a",
    )(proj, proj, proj, ws_bf, bt, wpa_bf)


def _ret_kernel(qf_ref, kf_ref, vf_ref, gf_ref, qb_ref, kb_ref, vb_ref, gb_ref, s0_ref, dec_ref,
                yf_ref, yb_ref, s_scr, intra_scr, qd_scr, kd_scr, cd_scr):
    C = RET_CHUNK
    i = pl.program_id(0)

    @pl.when(i == 0)
    def _():
        lg = jnp.log1p(-jnp.exp2(dec_ref[...]))
        r = lax.broadcasted_iota(jnp.int32, (C, C), 0).astype(F32)
        c = lax.broadcasted_iota(jnp.int32, (C, C), 1).astype(F32)
        pos = lax.broadcasted_iota(jnp.int32, (C, 1), 0).astype(F32)
        for h in range(HEADS):
            lf = lg[0:1, h:h + 1]
            lb = lg[1:2, h:h + 1]
            intra_scr[0, h] = jnp.where(r >= c, jnp.exp(jnp.maximum(r - c, 0.0) * lf), 0.0)
            intra_scr[1, h] = jnp.where(c >= r, jnp.exp(jnp.maximum(c - r, 0.0) * lb), 0.0)
            qd_scr[0, h] = jnp.exp((pos + 1.0) * lf)
            qd_scr[1, h] = jnp.exp((C - pos) * lb)
            kd_scr[0, h] = jnp.exp((C - 1.0 - pos) * lf)
            kd_scr[1, h] = jnp.exp(pos * lb)
            cd_scr[0, h] = jnp.exp(C * lf)
            cd_scr[1, h] = jnp.exp(C * lb)
        s_scr[...] = s0_ref[...]

    dirs = ((qf_ref, kf_ref, vf_ref, gf_ref, yf_ref), (qb_ref, kb_ref, vb_ref, gb_ref, yb_ref))
    nsub = qf_ref.shape[0] // C
    for sub in range(nsub):
        for d, (q_ref, k_ref, v_ref, g_ref, y_ref) in enumerate(dirs):
            c0 = (sub if d == 0 else nsub - 1 - sub) * C
            rows = slice(c0, c0 + C)
            for h in range(HEADS):
                q = q_ref[rows, h * DK:(h + 1) * DK]
                k = k_ref[rows, h * DK:(h + 1) * DK]
                v = v_ref[rows, h * DV:(h + 1) * DV]
                s = s_scr[d, h]
                a = (_dot_nt(q, k) * intra_scr[d, h]).astype(BF16)
                o = _dot(a, v) + _dot(q, s.astype(BF16)) * qd_scr[d, h]
                hn = o * lax.rsqrt(jnp.mean(o * o, axis=-1, keepdims=True) + EPS)
                y_ref[rows, h * DV:(h + 1) * DV] = (g_ref[rows, h * DV:(h + 1) * DV].astype(F32) * hn).astype(BF16)
                kd = (k.astype(F32) * kd_scr[d, h]).astype(BF16)
                s_scr[d, h] = s * cd_scr[d, h] + _dot_tn(kd, v)


def _ret_call(proj, s0, dec):
    T = proj.shape[0]
    C = RET_CHUNK
    rows = RET_SUB * C
    n = T // rows
    qk_w = HEADS * DK
    v_w = HEADS * DV
    fwd = lambda col: (lambda i: (i, col))
    bwd = lambda col: (lambda i: (n - 1 - i, col))
    return pl.pallas_call(
        _ret_kernel,
        grid=(n,),
        in_specs=[
            pl.BlockSpec((rows, qk_w), fwd(COL_Q)),
            pl.BlockSpec((rows, qk_w), fwd(COL_K)),
            pl.BlockSpec((rows, v_w), fwd(COL_VR // 2)),
            pl.BlockSpec((rows, v_w), fwd(COL_GF // 2)),
            pl.BlockSpec((rows, qk_w), bwd(COL_Q)),
            pl.BlockSpec((rows, qk_w), bwd(COL_K)),
            pl.BlockSpec((rows, v_w), bwd(COL_VR // 2)),
            pl.BlockSpec((rows, v_w), bwd(COL_GB // 2)),
            pl.BlockSpec(s0.shape, lambda i: (0, 0, 0, 0)),
            pl.BlockSpec(dec.shape, lambda i: (0, 0)),
        ],
        out_specs=[
            pl.BlockSpec((rows, v_w), lambda i: (i, 0)),
            pl.BlockSpec((rows, v_w), lambda i: (n - 1 - i, 0)),
        ],
        out_shape=[jax.ShapeDtypeStruct((T, v_w), BF16), jax.ShapeDtypeStruct((T, v_w), BF16)],
        scratch_shapes=[
            pltpu.VMEM((2, HEADS, DK, DV), F32),
            pltpu.VMEM((2, HEADS, C, C), F32),
            pltpu.VMEM((2, HEADS, C, 1), F32),
            pltpu.VMEM((2, HEADS, C, 1), F32),
            pltpu.VMEM((2, HEADS, 1, 1), F32),
        ],
        compiler_params=_params(("arbitrary",)),
        name="retention",
    )(proj, proj, proj, proj, proj, proj, proj, proj, s0, dec)


def _merge_kernel(yf_ref, yb_ref, za_ref, gbr_ref, x_ref, mod_ref, nw_ref, wpb_ref, wo_ref, rw_ref, rb_ref,
                  x1_ref, h2_ref, lg_ref):
    yb = yf_ref[...] + yb_ref[...]
    pb = _dot(yb, wpb_ref[...])
    y = za_ref[...].astype(F32) + gbr_ref[...].astype(F32) * pb
    yo = _dot(y.astype(BF16), wo_ref[...])
    x1 = x_ref[...] + _mod_slice(mod_ref, 0, 2) * yo
    x1_ref[...] = x1
    h2 = _rms(x1, nw_ref[...]) * (1.0 + _mod_slice(mod_ref, 0, 4)) + _mod_slice(mod_ref, 0, 3)
    h2_hi = h2.astype(BF16)
    h2_ref[...] = h2_hi
    h2_lo = (h2 - h2_hi.astype(F32)).astype(BF16)
    rw = rw_ref[...]
    rw_hi = rw.astype(BF16)
    rw_lo = (rw - rw_hi.astype(F32)).astype(BF16)
    both = _dot_nt(jnp.concatenate([rw_hi, rw_lo], axis=0), h2_hi)
    lg_ref[...] = both[:N_EXPERTS] + both[N_EXPERTS:] + _dot_nt(rw_hi, h2_lo) + rb_ref[...]


def _merge_call(yf, yb, za, proj, x2d, mod, nw2, wpb_bf, wo_bf, rw, rb):
    T = x2d.shape[0]
    tm = 512
    row = lambda i: (i, 0)
    fixed = lambda i: (0, 0)
    return pl.pallas_call(
        _merge_kernel,
        grid=(T // tm,),
        in_specs=[
            pl.BlockSpec((tm, HEADS * DV), row),
            pl.BlockSpec((tm, HEADS * DV), row),
            pl.BlockSpec((tm, D_MODEL), row),
            pl.BlockSpec((tm, D_MODEL), lambda i: (i, COL_GBR)),
            pl.BlockSpec((tm, D_MODEL), row),
            pl.BlockSpec(mod.shape, fixed),
            pl.BlockSpec((1, D_MODEL), fixed),
            pl.BlockSpec(wpb_bf.shape, fixed),
            pl.BlockSpec(wo_bf.shape, fixed),
            pl.BlockSpec(rw.shape, fixed),
            pl.BlockSpec(rb.shape, fixed),
        ],
        out_specs=[
            pl.BlockSpec((tm, D_MODEL), row),
            pl.BlockSpec((tm, D_MODEL), row),
            pl.BlockSpec((N_EXPERTS, tm), lambda i: (0, i)),
        ],
        out_shape=[
            jax.ShapeDtypeStruct((T, D_MODEL), F32),
            jax.ShapeDtypeStruct((T, D_MODEL), BF16),
            jax.ShapeDtypeStruct((N_EXPERTS, T), F32),
        ],
        compiler_params=_params(("arbitrary",)),
        name="merge_router",
    )(yf, yb, za, proj, x2d, mod, nw2, wpb_bf, wo_bf, rw, rb)


def _route_kernel(lg_ref, pos_ref, tok_ref, nch_ref, su_scr):
    E, tt = lg_ref.shape

    @pl.when(pl.program_id(0) == 0)
    def _():
        r = lax.broadcasted_iota(jnp.int32, (tt, tt), 0)
        c = lax.broadcasted_iota(jnp.int32, (tt, tt), 1)
        su_scr[...] = jnp.where(r < c, 1.0, 0.0).astype(BF16)

    sub = lax.broadcasted_iota(jnp.int32, (E, tt), 0)
    work = lg_ref[...]
    vals, hots = [], []
    for _ in range(TOP_K):
        m = jnp.max(work, axis=0, keepdims=True)
        first = jnp.min(jnp.where(work == m, sub, E), axis=0, keepdims=True)
        hot = sub == first
        vals.append(m)
        hots.append(hot)
        work = jnp.where(hot, -jnp.inf, work)
    exps = [jnp.exp(v - vals[0]) for v in vals]
    inv = 1.0 / functools.reduce(lambda a, b: a + b, exps)

    member = functools.reduce(lambda a, b: a + b, [jnp.where(h, 1.0, 0.0) for h in hots])
    cnt = jnp.sum(member, axis=1, keepdims=True)
    nch = jnp.floor((cnt + (MOE_CH - 1.0)) * (1.0 / MOE_CH))
    nch_b = jnp.broadcast_to(nch, (E, 128))
    rank = _dot(member.astype(BF16), su_scr[...])
    er = lax.broadcasted_iota(jnp.int32, (E, E), 0)
    ec = lax.broadcasted_iota(jnp.int32, (E, E), 1)
    first_chunk = _dot(jnp.where(ec < er, 1.0, 0.0).astype(BF16), nch_b.astype(BF16))[:, 0:1]
    base = first_chunk * MOE_CH + rank
    pos = [jnp.sum(jnp.where(hots[k], base, 0.0), axis=0, keepdims=True) for k in range(TOP_K)]
    for k in range(TOP_K):
        pos_ref[k:k + 1, :] = pos[k].astype(jnp.int32)
    rows = jnp.concatenate(pos + [e * inv for e in exps] + [jnp.zeros((128 - 2 * TOP_K, tt), F32)], axis=0)
    tok_ref[...] = rows.T
    nch_ref[0] = nch_b.astype(jnp.int32)


def _route_call(logits_t):
    E, T = logits_t.shape
    tt = MOE_TT
    return pl.pallas_call(
        _route_kernel,
        grid=(T // tt,),
        in_specs=[pl.BlockSpec((E, tt), lambda j: (0, j))],
        out_specs=[
            pl.BlockSpec((TOP_K, tt), lambda j: (0, j)),
            pl.BlockSpec((tt, 128), lambda j: (j, 0)),
            pl.BlockSpec((1, E, 128), lambda j: (j, 0, 0)),
        ],
        out_shape=[
            jax.ShapeDtypeStruct((TOP_K, T), jnp.int32),
            jax.ShapeDtypeStruct((T, 128), F32),
            jax.ShapeDtypeStruct((T // tt, E, 128), jnp.int32),
        ],
        scratch_shapes=[pltpu.VMEM((tt, tt), BF16)],
        compiler_params=_params(("arbitrary",)),
        name="moe_route",
    )(logits_t)


def _slot_tiles_max(n_tiles):
    return -(-(n_tiles * MOE_MAXC + N_EXPERTS * (MOE_CPT - 1)) // MOE_CPT)


def _routing_tables(nch):
    n_tiles, E = nch.shape

    def cumsum_last(a):
        m = a.shape[-1]
        keep = jnp.arange(m)[:, None] <= jnp.arange(m)[None, :]
        return jnp.sum(jnp.where(keep, a[..., :, None], 0), axis=-2)

    tot = jnp.sum(nch, axis=0)
    seg = (tot + MOE_CPT - 1) // MOE_CPT * MOE_CPT
    seg_end = cumsum_last(seg)
    seg_start = seg_end - seg
    gstart = seg_start[None, :] + cumsum_last(nch.T).T - nch
    l_end = cumsum_last(nch)
    lstart = l_end - nch

    def copy_list(count, first_local, first_global, step, max_len):
        end = cumsum_last(count)
        start = end - count
        i = jnp.arange(max_len, dtype=jnp.int32)[None, :, None]
        owner = (start[:, None, :] <= i) & (i < end[:, None, :])
        off = (i - start[:, None, :]) * step
        src = jnp.sum(jnp.where(owner, first_local[:, None, :] + off, 0), axis=-1)
        dst = jnp.sum(jnp.where(owner, first_global[:, None, :] + off, 0), axis=-1)
        return src, dst, end[:, -1]

    runs = nch // MOE_RUN
    rsrc, rdst, nruns = copy_list(runs, lstart, gstart, MOE_RUN, MOE_MAXQ)
    ssrc, sdst, nsingles = copy_list(nch - runs * MOE_RUN, lstart + runs * MOE_RUN, gstart + runs * MOE_RUN, 1,
                                     MOE_MAXS)
    copies = jnp.concatenate([rsrc, rdst, ssrc, sdst], axis=1).astype(jnp.int32).reshape(-1)
    ncopies = jnp.stack([nruns, nsingles], axis=1).astype(jnp.int32).reshape(-1)
    nt_max = _slot_tiles_max(n_tiles)
    n_used = (seg_end[-1] // MOE_CPT).astype(jnp.int32)
    tile_blk = jnp.minimum(jnp.arange(nt_max, dtype=jnp.int32), n_used - 1)
    tile_exp = jnp.sum(seg_end[None, :] <= (tile_blk * MOE_CPT)[:, None], axis=1)
    tile_exp = jnp.minimum(tile_exp, E - 1).astype(jnp.int32)
    experts = jnp.arange(E, dtype=jnp.int32)
    own = tile_exp[:, None] == experts[None, :]
    end_tile = jnp.sum(jnp.where(own, (seg_end // MOE_CPT)[None, :], 0), axis=1)
    follows = ((seg_start // MOE_CPT)[None, :] == end_tile[:, None]) & (seg > 0)[None, :]
    next_exp = jnp.where(jnp.any(follows, axis=1), jnp.sum(jnp.where(follows, experts[None, :], 0), axis=1), -1)
    next_exp = next_exp.astype(jnp.int32)
    tiles = jnp.arange(nt_max, dtype=jnp.int32)
    hands_over = ((tiles + 1 == end_tile) & (next_exp >= 0) & (tiles < n_used)).astype(jnp.int32)
    after_next = jnp.sum(jnp.where(tiles[None, :] == end_tile[:, None], next_exp[None, :], 0), axis=1)
    after_next = jnp.where(next_exp >= 0, after_next, -1).astype(jnp.int32)
    switches = jnp.concatenate([jnp.zeros((1,), jnp.int32), (tile_exp[1:] != tile_exp[:-1]).astype(jnp.int32)])
    parity = (cumsum_last(switches) % 2).astype(jnp.int32)
    zstart = (seg_start + tot).astype(jnp.int32)
    zcount = (seg - tot).astype(jnp.int32)
    expert_tables = (tile_exp, tile_blk, n_used.reshape(1), next_exp, after_next, hands_over, parity)
    return copies, ncopies, zstart, zcount, expert_tables


def _chunk_rows(c, nchunks=1):
    return pl.ds(pl.multiple_of(c * MOE_CH, MOE_CH), nchunks * MOE_CH)


def _for_tile_copies(cp_ref, cn_ref, tile, make, op):
    base = tile * MOE_CPROW

    def run(i, carry):
        op(make(cp_ref[base + i], cp_ref[base + MOE_MAXQ + i], MOE_RUN))
        return carry
    lax.fori_loop(0, cn_ref[2 * tile], run, 0)

    def single(i, carry):
        op(make(cp_ref[base + 2 * MOE_MAXQ + i], cp_ref[base + 2 * MOE_MAXQ + MOE_MAXS + i], 1))
        return carry
    lax.fori_loop(0, cn_ref[2 * tile + 1], single, 0)


def _dispatch_kernel(cp_ref, cn_ref, zstart_ref, zcount_ref, nu_ref, h2_ref, pos_ref, xs_hbm, xl_scr, zero_scr,
                     sems, zsem):
    j = pl.program_id(0)
    n = pl.num_programs(0)
    slot = j % 2
    tt = h2_ref.shape[0]
    nt_max = xs_hbm.shape[0] // MOE_TM

    def tile_copies(tile, slot_, op):
        def make(local, glob, nchunks):
            return pltpu.make_async_copy(xl_scr.at[slot_, _chunk_rows(local, nchunks)],
                                         xs_hbm.at[_chunk_rows(glob, nchunks)], sems.at[slot_])
        _for_tile_copies(cp_ref, cn_ref, tile, make, op)

    def zero_chunk_copy(d):
        return pltpu.make_async_copy(zero_scr.at[pl.ds(0, MOE_CH)], xs_hbm.at[_chunk_rows(d)], zsem.at[0])

    def zero_tile_copy(t):
        rows = pl.ds(pl.multiple_of(t * MOE_TM, MOE_TM), MOE_TM)
        return pltpu.make_async_copy(zero_scr, xs_hbm.at[rows], zsem.at[0])

    def zero_fill(op):
        def expert(e, carry):
            def chunk(r, carry_):
                op(zero_chunk_copy(zstart_ref[e] + r))
                return carry_
            return lax.fori_loop(0, zcount_ref[e], chunk, carry)
        lax.fori_loop(0, N_EXPERTS, expert, 0)

        def tile(t, carry):
            op(zero_tile_copy(t))
            return carry
        lax.fori_loop(nu_ref[0], nt_max, tile, 0)

    def wait_tile(tile, slot_):
        tile_copies(tile, slot_, lambda cp: cp.wait())

    @pl.when(j == 0)
    def _():
        zero_scr[...] = jnp.zeros_like(zero_scr)
        zero_fill(lambda cp: cp.start())

    @pl.when(j >= 2)
    def _():
        wait_tile(j - 2, slot)

    pos = pos_ref[...]
    h2 = h2_ref[...]
    for rb in range(MOE_RLOC // MOE_RB):
        io = lax.broadcasted_iota(jnp.int32, (MOE_RB, tt), 0) + rb * MOE_RB
        onehot = jnp.zeros((MOE_RB, tt), F32)
        for k in range(TOP_K):
            onehot = jnp.where(io == pos[k:k + 1, :], 1.0, onehot)
        xl_scr[slot, rb * MOE_RB:(rb + 1) * MOE_RB, :] = _dot(onehot.astype(BF16), h2).astype(BF16)

    tile_copies(j, slot, lambda cp: cp.start())

    @pl.when(j == n - 1)
    def _():
        @pl.when(j >= 1)
        def _():
            wait_tile(j - 1, 1 - slot)
        wait_tile(j, slot)
        zero_fill(lambda cp: cp.wait())


def _dispatch_call(tables, h2, pos_t, n_slots):
    T = h2.shape[0]
    tt = MOE_TT
    copies, ncopies, zstart, zcount, n_used = tables
    grid_spec = pltpu.PrefetchScalarGridSpec(
        num_scalar_prefetch=5,
        grid=(T // tt,),
        in_specs=[
            pl.BlockSpec((tt, D_MODEL), lambda j, *_: (j, 0)),
            pl.BlockSpec((TOP_K, tt), lambda j, *_: (0, j)),
        ],
        out_specs=pl.BlockSpec(memory_space=pl.ANY),
        scratch_shapes=[
            pltpu.VMEM((2, MOE_RLOC, D_MODEL), BF16),
            pltpu.VMEM((MOE_TM, D_MODEL), BF16),
            pltpu.SemaphoreType.DMA((2,)),
            pltpu.SemaphoreType.DMA((1,)),
        ],
    )
    return pl.pallas_call(
        _dispatch_kernel,
        grid_spec=grid_spec,
        out_shape=jax.ShapeDtypeStruct((n_slots, D_MODEL), BF16),
        compiler_params=_params(("arbitrary",)),
        name="moe_dispatch",
    )(copies, ncopies, zstart, zcount, n_used, h2, pos_t)


def _experts_kernel(te_ref, tb_ref, nu_ref, nx_ref, nx2_ref, last_ref, par_ref, xs_ref, w1_hbm, b1_ref, w2_hbm,
                    b2_ref, y_ref, w1_f32, w2_f32, w1_scr, w2_scr, sems):
    i = pl.program_id(0)
    valid = i < nu_ref[0]
    buf = par_ref[i]
    hands_over = last_ref[i] == 1

    def fetch(e):
        return (pltpu.make_async_copy(w1_hbm.at[e], w1_f32, sems.at[0]),
                pltpu.make_async_copy(w2_hbm.at[e], w2_f32, sems.at[1]))

    def cast_staged(slot):
        w1_scr[slot] = w1_f32[...].astype(BF16)
        w2_scr[slot] = w2_f32[...].astype(BF16)

    def start_fetch(e):
        @pl.when(e >= 0)
        def _():
            for cp in fetch(e):
                cp.start()

    def compute_tile(cast_to=None):
        x = xs_ref[...]
        pw = D_FF // 2
        acts = []
        for p in range(D_FF // pw):
            gcols = slice(p * pw, (p + 1) * pw)
            ucols = slice(D_FF + p * pw, D_FF + (p + 1) * pw)
            gate = jnp.minimum(_dot(x, w1_scr[buf, :, gcols]) + b1_ref[:, gcols], SWIGLU_LIMIT)
            if cast_to is not None:
                w1_scr[cast_to, :, gcols] = w1_f32[:, gcols].astype(BF16)
            up = jnp.clip(_dot(x, w1_scr[buf, :, ucols]) + b1_ref[:, ucols], -SWIGLU_LIMIT, SWIGLU_LIMIT)
            if cast_to is not None:
                w1_scr[cast_to, :, ucols] = w1_f32[:, ucols].astype(BF16)
            acts.append(((up + 1.0) * gate * _sigmoid(SWIGLU_ALPHA * gate)).astype(BF16))
        act = jnp.concatenate(acts, axis=1)
        for p in range(D_MODEL // pw):
            cols = slice(p * pw, (p + 1) * pw)
            y_ref[:, cols] = (_dot(act, w2_scr[buf, :, cols]) + b2_ref[:, cols]).astype(BF16)
            if cast_to is not None:
                w2_scr[cast_to, :, cols] = w2_f32[:, cols].astype(BF16)

    @pl.when(i == 0)
    def _():
        for cp in fetch(te_ref[0]):
            cp.start()
        for cp in fetch(te_ref[0]):
            cp.wait()
        cast_staged(buf)
        start_fetch(nx_ref[0])

    @pl.when(valid & jnp.logical_not(hands_over))
    def _():
        compute_tile()

    @pl.when(valid & hands_over)
    def _():
        for cp in fetch(nx_ref[i]):
            cp.wait()
        compute_tile(cast_to=1 - buf)
        start_fetch(nx2_ref[i])


def _experts_call(tables, xs, w1, b1, w2, b2):
    tile_exp = tables[0]
    nt_max = tile_exp.shape[0]
    slot_tile = lambda i, te, tb, *_: (tb[i], 0)
    per_e = lambda i, te, *_: (te[i], 0, 0)
    grid_spec = pltpu.PrefetchScalarGridSpec(
        num_scalar_prefetch=len(tables),
        grid=(nt_max,),
        in_specs=[
            pl.BlockSpec((MOE_TM, D_MODEL), slot_tile),
            pl.BlockSpec(memory_space=pl.ANY),
            pl.BlockSpec((None, 1, 2 * D_FF), per_e),
            pl.BlockSpec(memory_space=pl.ANY),
            pl.BlockSpec((None, 1, D_MODEL), per_e),
        ],
        out_specs=pl.BlockSpec((MOE_TM, D_MODEL), slot_tile),
        scratch_shapes=[
            pltpu.VMEM((D_MODEL, 2 * D_FF), F32),
            pltpu.VMEM((D_FF, D_MODEL), F32),
            pltpu.VMEM((2, D_MODEL, 2 * D_FF), BF16),
            pltpu.VMEM((2, D_FF, D_MODEL), BF16),
            pltpu.SemaphoreType.DMA((2,)),
        ],
    )
    return pl.pallas_call(
        _experts_kernel,
        grid_spec=grid_spec,
        out_shape=jax.ShapeDtypeStruct(xs.shape, BF16),
        input_output_aliases={len(tables): 0},
        compiler_params=_params(("arbitrary",)),
        name="moe_experts",
    )(*tables, xs, w1, b1, w2, b2)


def _combine_kernel(cp_ref, cn_ref, tok_ref, x1_ref, mod_ref, fw_ref, y_hbm, o_ref, yl_scr, w_scr, sems):
    j = pl.program_id(0)
    n = pl.num_programs(0)
    slot = j % 2
    tt = x1_ref.shape[0]

    def tile_copies(tile, slot_, op):
        def make(local, glob, nchunks):
            return pltpu.make_async_copy(y_hbm.at[_chunk_rows(glob, nchunks)],
                                         yl_scr.at[slot_, _chunk_rows(local, nchunks)], sems.at[slot_])
        _for_tile_copies(cp_ref, cn_ref, tile, make, op)

    def start_tile(tile, slot_):
        tile_copies(tile, slot_, lambda cp: cp.start())

    @pl.when(j == 0)
    def _():
        yl_scr[...] = jnp.zeros_like(yl_scr)
        start_tile(0, 0)

    @pl.when(j + 1 < n)
    def _():
        start_tile(j + 1, 1 - slot)

    tile_copies(j, slot, lambda cp: cp.wait())

    pos = tok_ref[:, 0:TOP_K].astype(jnp.int32)
    p = tok_ref[:, TOP_K:2 * TOP_K]
    for cb in range(MOE_RLOC // MOE_RB):
        io = lax.broadcasted_iota(jnp.int32, (tt, MOE_RB), 1) + cb * MOE_RB
        w = jnp.zeros((tt, MOE_RB), F32)
        for k in range(TOP_K):
            w = jnp.where(io == pos[:, k:k + 1], p[:, k:k + 1], w)
        w_scr[:, cb * MOE_RB:(cb + 1) * MOE_RB] = w.astype(BF16)
    moe = _dot(w_scr[...], yl_scr[slot])
    x2 = x1_ref[...] + _mod_slice(mod_ref, 0, 5) * moe
    o_ref[...] = _rms(x2, fw_ref[...])


def _combine_call(copies, ncopies, tok, x1, mod, fw, y):
    T = x1.shape[0]
    tt = MOE_TT
    row = lambda j, *_: (j, 0)
    fixed = lambda j, *_: (0, 0)
    grid_spec = pltpu.PrefetchScalarGridSpec(
        num_scalar_prefetch=2,
        grid=(T // tt,),
        in_specs=[
            pl.BlockSpec((tt, 128), row),
            pl.BlockSpec((tt, D_MODEL), row),
            pl.BlockSpec(mod.shape, fixed),
            pl.BlockSpec((1, D_MODEL), fixed),
            pl.BlockSpec(memory_space=pl.ANY),
        ],
        out_specs=pl.BlockSpec((tt, D_MODEL), row),
        scratch_shapes=[
            pltpu.VMEM((2, MOE_RLOC, D_MODEL), BF16),
            pltpu.VMEM((tt, MOE_RLOC), BF16),
            pltpu.SemaphoreType.DMA((2,)),
        ],
    )
    return pl.pallas_call(
        _combine_kernel,
        grid_spec=grid_spec,
        out_shape=jax.ShapeDtypeStruct((T, D_MODEL), F32),
        compiler_params=_params(("arbitrary",)),
        name="moe_combine",
    )(copies, ncopies, tok, x1, mod, fw, y)


def _rope_tables(T):
    rows = T // GRID_W
    n_freq = DK // 4
    inv = ROPE_BASE ** (-jnp.arange(n_freq, dtype=F32) / n_freq)
    row_ang = jnp.arange(rows, dtype=F32)[:, None] * inv
    col_ang = jnp.arange(GRID_W, dtype=F32)[:, None] * inv
    rot_row = jnp.pad(jnp.stack([jnp.cos(row_ang), jnp.sin(row_ang)]), ((0, 0), (0, 0), (0, n_freq)))
    rot_col = jnp.pad(jnp.stack([jnp.cos(col_ang), jnp.sin(col_ang)]), ((0, 0), (0, 0), (n_freq, 0)))
    return rot_row, rot_col


def kernel(x, c, ctx, c_ctx, w_mod, b_mod, norm1_w, norm2_w, w_in, sgu_ln_w, sgu_ln_b, sgu_w, sgu_b,
           ret_decay_fwd, ret_decay_bwd, w_proj_a, w_proj_b, w_out, router_w, router_b,
           moe_w1, moe_b1, moe_w2, moe_b2, final_norm_w):
    B, T, D = x.shape
    assert B == 1 and D == D_MODEL and w_mod.shape[0] == 1 and T % 1024 == 0
    x2d = x.reshape(T, D)
    cc = jnp.stack([c.reshape(D), c_ctx.reshape(D)], axis=1)
    mod = _mod_call(cc, w_mod[0], b_mod)
    dec = jnp.stack([ret_decay_fwd[0], ret_decay_bwd[0]], axis=0)
    w_in_bf = w_in[0].astype(BF16)
    s0 = _ctx_call(ctx.reshape(ctx.shape[1], D), norm1_w, mod, w_in_bf, dec)
    rot_row, rot_col = _rope_tables(T)
    proj = _inproj_call(x2d, norm1_w, mod, w_in_bf, rot_row, rot_col, sgu_ln_w, sgu_ln_b)
    za = _sgu_call(proj, sgu_w[0].astype(BF16), sgu_b[0].T, w_proj_a[0].astype(BF16))
    yf, yb = _ret_call(proj, s0, dec)
    x1, h2, logits_t = _merge_call(yf, yb, za, proj, x2d, mod, norm2_w, w_proj_b[0].astype(BF16),
                                   w_out[0].astype(BF16), router_w[0].T, router_b.reshape(N_EXPERTS, 1))
    pos_t, tok, nch = _route_call(logits_t)
    copies, ncopies, zstart, zcount, expert_tables = _routing_tables(nch[:, :, 0])
    n_slots = expert_tables[0].shape[0] * MOE_TM
    xs = _dispatch_call((copies, ncopies, zstart, zcount, expert_tables[2]), h2, pos_t, n_slots)
    y = _experts_call(expert_tables, xs, moe_w1[0], moe_b1[0][:, None, :], moe_w2[0], moe_b2[0][:, None, :])
    out = _combine_call(copies, ncopies, tok, x1, mod, final_norm_w.reshape(1, D), y)
    return out.reshape(B, T, D)
```

```python
import functools

import jax
import jax.numpy as jnp
from jax import lax
from jax.experimental import pallas as pl
from jax.experimental.pallas import tpu as pltpu

D_MODEL = 1024
GRID_W = 64
SGU_CHUNK = 128
SGU_GROUPS = 8
HEADS = 4
DK = D_MODEL // HEADS
DV = 2 * DK
ROPE_BASE = 10000.0
N_EXPERTS = 32
TOP_K = 4
D_FF = D_MODEL
SWIGLU_LIMIT = 7.0
SWIGLU_ALPHA = 1.702
EPS = 1e-6
IN_WIDTH = 12 * D_MODEL
COL_U, COL_V, COL_Q, COL_K, COL_VR, COL_GF, COL_GB, COL_GA, COL_GBR = 0, 1, 2, 3, 4, 6, 8, 10, 11

INPROJ_NB = 4
RET_CHUNK = 256
RET_SUB = 1
VMEM_LIMIT = 56 * 1024 * 1024

MOE_TT = 512
MOE_CH = 16
MOE_TM = 512
MOE_CPT = MOE_TM // MOE_CH
MOE_MAXC = TOP_K * MOE_TT // MOE_CH + N_EXPERTS
MOE_RLOC = MOE_MAXC * MOE_CH
MOE_RB = 256
MOE_RUN = 4
MOE_MAXQ = MOE_MAXC // MOE_RUN
MOE_MAXS = (MOE_RUN - 1) * N_EXPERTS
MOE_CPROW = 2 * (MOE_MAXQ + MOE_MAXS)

F32 = jnp.float32
BF16 = jnp.bfloat16


def _params(sem):
    return pltpu.CompilerParams(dimension_semantics=sem, vmem_limit_bytes=VMEM_LIMIT)


def _dot(a, b):
    return jnp.dot(a, b, preferred_element_type=F32)


def _dot_nt(a, b):
    return lax.dot_general(a, b, (((1,), (1,)), ((), ())), preferred_element_type=F32)


def _dot_tn(a, b):
    return lax.dot_general(a, b, (((0,), (0,)), ((), ())), preferred_element_type=F32)


def _rms(x, w):
    return x * lax.rsqrt(jnp.mean(x * x, axis=-1, keepdims=True) + EPS) * w


def _gelu(x):
    return 0.5 * x * (1.0 + lax.erf(x * (2.0 ** -0.5)))


def _sigmoid(x):
    return 0.5 * jnp.tanh(0.5 * x) + 0.5


def _mod_slice(mod_ref, row, k):
    return mod_ref[row:row + 1, k * D_MODEL:(k + 1) * D_MODEL]


def _mod_kernel(cc_ref, w_ref, b_ref, o_ref):
    s = cc_ref[...]
    s = s * jax.nn.sigmoid(s)
    w = w_ref[...]
    r0 = jnp.sum(s[:, 0:1] * w, axis=0, keepdims=True)
    r1 = jnp.sum(s[:, 1:2] * w, axis=0, keepdims=True)
    o_ref[...] = jnp.concatenate([r0, r1], axis=0) + b_ref[...]


def _mod_call(cc, w_mod, b_mod):
    bn = 1536
    n = w_mod.shape[1]
    return pl.pallas_call(
        _mod_kernel,
        grid=(n // bn,),
        in_specs=[
            pl.BlockSpec((D_MODEL, 2), lambda j: (0, 0)),
            pl.BlockSpec((D_MODEL, bn), lambda j: (0, j)),
            pl.BlockSpec((1, bn), lambda j: (0, j)),
        ],
        out_specs=pl.BlockSpec((2, bn), lambda j: (0, j)),
        out_shape=jax.ShapeDtypeStruct((2, n), F32),
        compiler_params=_params(("arbitrary",)),
        name="mod_vectors",
    )(cc, w_mod, b_mod)


def _ctx_kernel(ctx_ref, nw_ref, mod_ref, w_ref, dec_ref, s0_ref):
    L = ctx_ref.shape[0]
    sh = _mod_slice(mod_ref, 1, 0)
    sc = _mod_slice(mod_ref, 1, 1)
    hc = (_rms(ctx_ref[...], nw_ref[...]) * (1.0 + sc) + sh).astype(BF16)
    kv = _dot(hc, w_ref[...])
    lg = jnp.log1p(-jnp.exp2(dec_ref[...]))
    pos = lax.broadcasted_iota(jnp.int32, (L, 1), 0).astype(F32)
    for h in range(HEADS):
        k = kv[:, h * DK:(h + 1) * DK] * (DK ** -0.5)
        v = kv[:, HEADS * DK + h * DV:HEADS * DK + (h + 1) * DV].astype(BF16)
        wf = jnp.exp((L - 1.0 - pos) * lg[0:1, h:h + 1])
        wb = jnp.exp(pos * lg[1:2, h:h + 1])
        s0_ref[0, h] = _dot_tn((k * wf).astype(BF16), v)
        s0_ref[1, h] = _dot_tn((k * wb).astype(BF16), v)


def _ctx_call(ctx2d, nw, mod, w_in_bf, dec):
    L = ctx2d.shape[0]
    wcols = HEADS * DK + HEADS * DV
    return pl.pallas_call(
        _ctx_kernel,
        grid=(1,),
        in_specs=[
            pl.BlockSpec((L, D_MODEL), lambda i: (0, 0)),
            pl.BlockSpec((1, D_MODEL), lambda i: (0, 0)),
            pl.BlockSpec(mod.shape, lambda i: (0, 0)),
            pl.BlockSpec((D_MODEL, wcols), lambda i: (0, COL_K * D_MODEL // wcols)),
            pl.BlockSpec(dec.shape, lambda i: (0, 0)),
        ],
        out_specs=pl.BlockSpec((2, HEADS, DK, DV), lambda i: (0, 0, 0, 0)),
        out_shape=jax.ShapeDtypeStruct((2, HEADS, DK, DV), F32),
        compiler_params=_params(("arbitrary",)),
        name="ctx_states",
    )(ctx2d, nw, mod, w_in_bf, dec)


def _inproj_kernel(x_ref, nw_ref, mod_ref, w_ref, rot_row_ref, rot_col_ref, lnw_ref, lnb_ref, o_ref,
                   h_scr, g_scr, cos_ref, sin_ref):
    j = pl.program_id(1)
    tm = h_scr.shape[0]

    @pl.when(j == 0)
    def _():
        sh = _mod_slice(mod_ref, 0, 0)
        sc = _mod_slice(mod_ref, 0, 1)
        h_scr[...] = (_rms(x_ref[...], nw_ref[...]) * (1.0 + sc) + sh).astype(BF16)

    def pieces(emit, span, rsplit=1):
        for c0 in range(span.start, span.stop, DK):
            cols = slice(c0, c0 + DK)
            for r in range(rsplit):
                rows = slice(r * tm // rsplit, (r + 1) * tm // rsplit)
                emit(rows, cols, _dot(h_scr[rows, :], w_ref[:, cols]))

    def store(fn):
        def emit(rows, cols, acc):
            o_ref[rows, cols] = fn(acc).astype(BF16)
        return emit

    def rope(scale):
        half = DK // 2

        def emit(rows, cols, acc):
            cos = cos_ref[rows, :]
            sin = sin_ref[rows, :]
            x1 = acc[:, :half]
            x2 = acc[:, half:]
            o_ref[rows, cols.start:cols.start + half] = ((x1 * cos - x2 * sin) * scale).astype(BF16)
            o_ref[rows, cols.start + half:cols.stop] = ((x1 * sin + x2 * cos) * scale).astype(BF16)
        return emit

    def gelu_layer_norm(span):
        def emit(rows, cols, acc):
            g_scr[rows, cols.start - span.start:cols.stop - span.start] = _gelu(acc)
        pieces(emit, span)
        g = g_scr[...]
        mu = jnp.mean(g, axis=-1, keepdims=True)
        d = g - mu
        var = jnp.mean(d * d, axis=-1, keepdims=True)
        o_ref[:, span] = (d * lax.rsqrt(var + EPS) * lnw_ref[...] + lnb_ref[...]).astype(BF16)

    def rotary_tables():
        for r in range(tm // GRID_W):
            tok = slice(r * GRID_W, (r + 1) * GRID_W)
            cos_ref[tok, :] = rot_row_ref[0, r:r + 1, :] + rot_col_ref[0]
            sin_ref[tok, :] = rot_row_ref[1, r:r + 1, :] + rot_col_ref[1]

    def column_block(block, span):
        if block == COL_U:
            pieces(store(_gelu), span)
        elif block == COL_V:
            gelu_layer_norm(span)
        elif block == COL_Q:
            rotary_tables()
            pieces(rope(1.0), span)
        elif block == COL_K:
            pieces(rope(DK ** -0.5), span)
        elif block < COL_GF:
            pieces(store(lambda a: a), span)
        elif block < COL_GA:
            pieces(store(lambda a: a * _sigmoid(a)), span, rsplit=2)
        else:
            pieces(store(_sigmoid), span, rsplit=2)

    for step in range(IN_WIDTH // D_MODEL // INPROJ_NB):
        @pl.when(j == step)
        def _():
            for b in range(INPROJ_NB):
                column_block(step * INPROJ_NB + b, slice(b * D_MODEL, (b + 1) * D_MODEL))


def _inproj_call(x2d, nw, mod, w_in_bf, rot_row, rot_col, lnw, lnb):
    T = x2d.shape[0]
    tm = 1024
    half = DK // 2
    return pl.pallas_call(
        _inproj_kernel,
        grid=(T // tm, IN_WIDTH // (INPROJ_NB * D_MODEL)),
        in_specs=[
            pl.BlockSpec((tm, D_MODEL), lambda i, j: (i, 0)),
            pl.BlockSpec((1, D_MODEL), lambda i, j: (0, 0)),
            pl.BlockSpec(mod.shape, lambda i, j: (0, 0)),
            pl.BlockSpec((D_MODEL, INPROJ_NB * D_MODEL), lambda i, j: (0, j)),
            pl.BlockSpec((2, tm // GRID_W, half), lambda i, j: (0, i, 0)),
            pl.BlockSpec((2, GRID_W, half), lambda i, j: (0, 0, 0)),
            pl.BlockSpec((1, D_MODEL), lambda i, j: (0, 0)),
            pl.BlockSpec((1, D_MODEL), lambda i, j: (0, 0)),
        ],
        out_specs=pl.BlockSpec((tm, INPROJ_NB * D_MODEL), lambda i, j: (i, j)),
        out_shape=jax.ShapeDtypeStruct((T, IN_WIDTH), BF16),
        scratch_shapes=[
            pltpu.VMEM((tm, D_MODEL), BF16),
            pltpu.VMEM((tm, D_MODEL), F32),
            pltpu.VMEM((tm, half), F32),
            pltpu.VMEM((tm, half), F32),
        ],
        compiler_params=_params(("arbitrary", "arbitrary")),
        name="in_proj",
    )(x2d, nw, mod, w_in_bf, rot_row, rot_col, lnw, lnb)


def _sgu_kernel(u_ref, v_ref, ga_ref, ws_ref, bt_ref, wpa_ref, o_ref, ya_scr):
    tm = u_ref.shape[0]
    gd = D_MODEL // SGU_GROUPS
    for n in range(tm // SGU_CHUNK):
        rows = slice(n * SGU_CHUNK, (n + 1) * SGU_CHUNK)
        for g in range(SGU_GROUPS):
            cols = slice(g * gd, (g + 1) * gd)
            mixed = _dot(ws_ref[g], v_ref[rows, cols]) + bt_ref[:, g:g + 1]
            ya_scr[rows, cols] = (u_ref[rows, cols].astype(F32) * mixed).astype(BF16)
    pa = _dot(ya_scr[...], wpa_ref[...])
    o_ref[...] = (ga_ref[...].astype(F32) * pa).astype(BF16)


def _sgu_call(proj, ws_bf, bt, wpa_bf):
    T = proj.shape[0]
    tm = 1024
    return pl.pallas_call(
        _sgu_kernel,
        grid=(T // tm,),
        in_specs=[
            pl.BlockSpec((tm, D_MODEL), lambda i: (i, COL_U)),
            pl.BlockSpec((tm, D_MODEL), lambda i: (i, COL_V)),
            pl.BlockSpec((tm, D_MODEL), lambda i: (i, COL_GA)),
            pl.BlockSpec(ws_bf.shape, lambda i: (0, 0, 0)),
            pl.BlockSpec(bt.shape, lambda i: (0, 0)),
            pl.BlockSpec((D_MODEL, D_MODEL), lambda i: (0, 0)),
        ],
        out_specs=pl.BlockSpec((tm, D_MODEL), lambda i: (i, 0)),
        out_shape=jax.ShapeDtypeStruct((T, D_MODEL), BF16),
        scratch_shapes=[pltpu.VMEM((tm, D_MODEL), BF16)],
        compiler_params=_params(("arbitrary",)),
        name="sgu_proj_a",
    )(proj, proj, proj, ws_bf, bt, wpa_bf)


def _ret_kernel(qf_ref, kf_ref, vf_ref, gf_ref, qb_ref, kb_ref, vb_ref, gb_ref, s0_ref, dec_ref,
                yf_ref, yb_ref, s_scr, intra_scr, qd_scr, kd_scr, cd_scr):
    C = RET_CHUNK
    i = pl.program_id(0)

    @pl.when(i == 0)
    def _():
        lg = jnp.log1p(-jnp.exp2(dec_ref[...]))
        r = lax.broadcasted_iota(jnp.int32, (C, C), 0).astype(F32)
        c = lax.broadcasted_iota(jnp.int32, (C, C), 1).astype(F32)
        pos = lax.broadcasted_iota(jnp.int32, (C, 1), 0).astype(F32)
        for h in range(HEADS):
            lf = lg[0:1, h:h + 1]
            lb = lg[1:2, h:h + 1]
            intra_scr[0, h] = jnp.where(r >= c, jnp.exp(jnp.maximum(r - c, 0.0) * lf), 0.0)
            intra_scr[1, h] = jnp.where(c >= r, jnp.exp(jnp.maximum(c - r, 0.0) * lb), 0.0)
            qd_scr[0, h] = jnp.exp((pos + 1.0) * lf)
            qd_scr[1, h] = jnp.exp((C - pos) * lb)
            kd_scr[0, h] = jnp.exp((C - 1.0 - pos) * lf)
            kd_scr[1, h] = jnp.exp(pos * lb)
            cd_scr[0, h] = jnp.exp(C * lf)
            cd_scr[1, h] = jnp.exp(C * lb)
        s_scr[...] = s0_ref[...]

    dirs = ((qf_ref, kf_ref, vf_ref, gf_ref, yf_ref), (qb_ref, kb_ref, vb_ref, gb_ref, yb_ref))
    nsub = qf_ref.shape[0] // C
    for sub in range(nsub):
        for d, (q_ref, k_ref, v_ref, g_ref, y_ref) in enumerate(dirs):
            c0 = (sub if d == 0 else nsub - 1 - sub) * C
            rows = slice(c0, c0 + C)
            for h in range(HEADS):
                q = q_ref[rows, h * DK:(h + 1) * DK]
                k = k_ref[rows, h * DK:(h + 1) * DK]
                v = v_ref[rows, h * DV:(h + 1) * DV]
                s = s_scr[d, h]
                a = (_dot_nt(q, k) * intra_scr[d, h]).astype(BF16)
                o = _dot(a, v) + _dot(q, s.astype(BF16)) * qd_scr[d, h]
                hn = o * lax.rsqrt(jnp.mean(o * o, axis=-1, keepdims=True) + EPS)
                y_ref[rows, h * DV:(h + 1) * DV] = (g_ref[rows, h * DV:(h + 1) * DV].astype(F32) * hn).astype(BF16)
                kd = (k.astype(F32) * kd_scr[d, h]).astype(BF16)
                s_scr[d, h] = s * cd_scr[d, h] + _dot_tn(kd, v)


def _ret_call(proj, s0, dec):
    T = proj.shape[0]
    C = RET_CHUNK
    rows = RET_SUB * C
    n = T // rows
    qk_w = HEADS * DK
    v_w = HEADS * DV
    fwd = lambda col: (lambda i: (i, col))
    bwd = lambda col: (lambda i: (n - 1 - i, col))
    return pl.pallas_call(
        _ret_kernel,
        grid=(n,),
        in_specs=[
            pl.BlockSpec((rows, qk_w), fwd(COL_Q)),
            pl.BlockSpec((rows, qk_w), fwd(COL_K)),
            pl.BlockSpec((rows, v_w), fwd(COL_VR // 2)),
            pl.BlockSpec((rows, v_w), fwd(COL_GF // 2)),
            pl.BlockSpec((rows, qk_w), bwd(COL_Q)),
            pl.BlockSpec((rows, qk_w), bwd(COL_K)),
            pl.BlockSpec((rows, v_w), bwd(COL_VR // 2)),
            pl.BlockSpec((rows, v_w), bwd(COL_GB // 2)),
            pl.BlockSpec(s0.shape, lambda i: (0, 0, 0, 0)),
            pl.BlockSpec(dec.shape, lambda i: (0, 0)),
        ],
        out_specs=[
            pl.BlockSpec((rows, v_w), lambda i: (i, 0)),
            pl.BlockSpec((rows, v_w), lambda i: (n - 1 - i, 0)),
        ],
        out_shape=[jax.ShapeDtypeStruct((T, v_w), BF16), jax.ShapeDtypeStruct((T, v_w), BF16)],
        scratch_shapes=[
            pltpu.VMEM((2, HEADS, DK, DV), F32),
            pltpu.VMEM((2, HEADS, C, C), F32),
            pltpu.VMEM((2, HEADS, C, 1), F32),
            pltpu.VMEM((2, HEADS, C, 1), F32),
            pltpu.VMEM((2, HEADS, 1, 1), F32),
        ],
        compiler_params=_params(("arbitrary",)),
        name="retention",
    )(proj, proj, proj, proj, proj, proj, proj, proj, s0, dec)


def _merge_kernel(yf_ref, yb_ref, za_ref, gbr_ref, x_ref, mod_ref, nw_ref, wpb_ref, wo_ref, rw_ref, rb_ref,
                  x1_ref, h2_ref, lg_ref):
    yb = yf_ref[...] + yb_ref[...]
    pb = _dot(yb, wpb_ref[...])
    y = za_ref[...].astype(F32) + gbr_ref[...].astype(F32) * pb
    yo = _dot(y.astype(BF16), wo_ref[...])
    x1 = x_ref[...] + _mod_slice(mod_ref, 0, 2) * yo
    x1_ref[...] = x1
    h2 = _rms(x1, nw_ref[...]) * (1.0 + _mod_slice(mod_ref, 0, 4)) + _mod_slice(mod_ref, 0, 3)
    h2_hi = h2.astype(BF16)
    h2_ref[...] = h2_hi
    h2_lo = (h2 - h2_hi.astype(F32)).astype(BF16)
    rw = rw_ref[...]
    rw_hi = rw.astype(BF16)
    rw_lo = (rw - rw_hi.astype(F32)).astype(BF16)
    both = _dot_nt(jnp.concatenate([rw_hi, rw_lo], axis=0), h2_hi)
    lg_ref[...] = both[:N_EXPERTS] + both[N_EXPERTS:] + _dot_nt(rw_hi, h2_lo) + rb_ref[...]


def _merge_call(yf, yb, za, proj, x2d, mod, nw2, wpb_bf, wo_bf, rw, rb):
    T = x2d.shape[0]
    tm = 512
    row = lambda i: (i, 0)
    fixed = lambda i: (0, 0)
    return pl.pallas_call(
        _merge_kernel,
        grid=(T // tm,),
        in_specs=[
            pl.BlockSpec((tm, HEADS * DV), row),
            pl.BlockSpec((tm, HEADS * DV), row),
            pl.BlockSpec((tm, D_MODEL), row),
            pl.BlockSpec((tm, D_MODEL), lambda i: (i, COL_GBR)),
            pl.BlockSpec((tm, D_MODEL), row),
            pl.BlockSpec(mod.shape, fixed),
            pl.BlockSpec((1, D_MODEL), fixed),
            pl.BlockSpec(wpb_bf.shape, fixed),
            pl.BlockSpec(wo_bf.shape, fixed),
            pl.BlockSpec(rw.shape, fixed),
            pl.BlockSpec(rb.shape, fixed),
        ],
        out_specs=[
            pl.BlockSpec((tm, D_MODEL), row),
            pl.BlockSpec((tm, D_MODEL), row),
            pl.BlockSpec((N_EXPERTS, tm), lambda i: (0, i)),
        ],
        out_shape=[
            jax.ShapeDtypeStruct((T, D_MODEL), F32),
            jax.ShapeDtypeStruct((T, D_MODEL), BF16),
            jax.ShapeDtypeStruct((N_EXPERTS, T), F32),
        ],
        compiler_params=_params(("arbitrary",)),
        name="merge_router",
    )(yf, yb, za, proj, x2d, mod, nw2, wpb_bf, wo_bf, rw, rb)


def _route_kernel(lg_ref, pos_ref, tok_ref, nch_ref, su_scr):
    E, tt = lg_ref.shape

    @pl.when(pl.program_id(0) == 0)
    def _():
        r = lax.broadcasted_iota(jnp.int32, (tt, tt), 0)
        c = lax.broadcasted_iota(jnp.int32, (tt, tt), 1)
        su_scr[...] = jnp.where(r < c, 1.0, 0.0).astype(BF16)

    sub = lax.broadcasted_iota(jnp.int32, (E, tt), 0)
    work = lg_ref[...]
    vals, hots = [], []
    for _ in range(TOP_K):
        m = jnp.max(work, axis=0, keepdims=True)
        first = jnp.min(jnp.where(work == m, sub, E), axis=0, keepdims=True)
        hot = sub == first
        vals.append(m)
        hots.append(hot)
        work = jnp.where(hot, -jnp.inf, work)
    exps = [jnp.exp(v - vals[0]) for v in vals]
    inv = 1.0 / functools.reduce(lambda a, b: a + b, exps)

    member = functools.reduce(lambda a, b: a + b, [jnp.where(h, 1.0, 0.0) for h in hots])
    cnt = jnp.sum(member, axis=1, keepdims=True)
    nch = jnp.floor((cnt + (MOE_CH - 1.0)) * (1.0 / MOE_CH))
    nch_b = jnp.broadcast_to(nch, (E, 128))
    rank = _dot(member.astype(BF16), su_scr[...])
    er = lax.broadcasted_iota(jnp.int32, (E, E), 0)
    ec = lax.broadcasted_iota(jnp.int32, (E, E), 1)
    first_chunk = _dot(jnp.where(ec < er, 1.0, 0.0).astype(BF16), nch_b.astype(BF16))[:, 0:1]
    base = first_chunk * MOE_CH + rank
    pos = [jnp.sum(jnp.where(hots[k], base, 0.0), axis=0, keepdims=True) for k in range(TOP_K)]
    for k in range(TOP_K):
        pos_ref[k:k + 1, :] = pos[k].astype(jnp.int32)
    rows = jnp.concatenate(pos + [e * inv for e in exps] + [jnp.zeros((128 - 2 * TOP_K, tt), F32)], axis=0)
    tok_ref[...] = rows.T
    nch_ref[0] = nch_b.astype(jnp.int32)


def _route_call(logits_t):
    E, T = logits_t.shape
    tt = MOE_TT
    return pl.pallas_call(
        _route_kernel,
        grid=(T // tt,),
        in_specs=[pl.BlockSpec((E, tt), lambda j: (0, j))],
        out_specs=[
            pl.BlockSpec((TOP_K, tt), lambda j: (0, j)),
            pl.BlockSpec((tt, 128), lambda j: (j, 0)),
            pl.BlockSpec((1, E, 128), lambda j: (j, 0, 0)),
        ],
        out_shape=[
            jax.ShapeDtypeStruct((TOP_K, T), jnp.int32),
            jax.ShapeDtypeStruct((T, 128), F32),
            jax.ShapeDtypeStruct((T // tt, E, 128), jnp.int32),
        ],
        scratch_shapes=[pltpu.VMEM((tt, tt), BF16)],
        compiler_params=_params(("arbitrary",)),
        name="moe_route",
    )(logits_t)


def _slot_tiles_max(n_tiles):
    return -(-(n_tiles * MOE_MAXC + N_EXPERTS * (MOE_CPT - 1)) // MOE_CPT)


def _routing_tables(nch):
    n_tiles, E = nch.shape

    def cumsum_last(a):
        m = a.shape[-1]
        keep = jnp.arange(m)[:, None] <= jnp.arange(m)[None, :]
        return jnp.sum(jnp.where(keep, a[..., :, None], 0), axis=-2)

    tot = jnp.sum(nch, axis=0)
    seg = (tot + MOE_CPT - 1) // MOE_CPT * MOE_CPT
    seg_end = cumsum_last(seg)
    seg_start = seg_end - seg
    gstart = seg_start[None, :] + cumsum_last(nch.T).T - nch
    l_end = cumsum_last(nch)
    lstart = l_end - nch

    def copy_list(count, first_local, first_global, step, max_len):
        end = cumsum_last(count)
        start = end - count
        i = jnp.arange(max_len, dtype=jnp.int32)[None, :, None]
        owner = (start[:, None, :] <= i) & (i < end[:, None, :])
        off = (i - start[:, None, :]) * step
        src = jnp.sum(jnp.where(owner, first_local[:, None, :] + off, 0), axis=-1)
        dst = jnp.sum(jnp.where(owner, first_global[:, None, :] + off, 0), axis=-1)
        return src, dst, end[:, -1]

    runs = nch // MOE_RUN
    rsrc, rdst, nruns = copy_list(runs, lstart, gstart, MOE_RUN, MOE_MAXQ)
    ssrc, sdst, nsingles = copy_list(nch - runs * MOE_RUN, lstart + runs * MOE_RUN, gstart + runs * MOE_RUN, 1,
                                     MOE_MAXS)
    copies = jnp.concatenate([rsrc, rdst, ssrc, sdst], axis=1).astype(jnp.int32).reshape(-1)
    ncopies = jnp.stack([nruns, nsingles], axis=1).astype(jnp.int32).reshape(-1)
    nt_max = _slot_tiles_max(n_tiles)
    n_used = (seg_end[-1] // MOE_CPT).astype(jnp.int32)
    tile_blk = jnp.minimum(jnp.arange(nt_max, dtype=jnp.int32), n_used - 1)
    tile_exp = jnp.sum(seg_end[None, :] <= (tile_blk * MOE_CPT)[:, None], axis=1)
    tile_exp = jnp.minimum(tile_exp, E - 1).astype(jnp.int32)
    experts = jnp.arange(E, dtype=jnp.int32)
    own = tile_exp[:, None] == experts[None, :]
    end_tile = jnp.sum(jnp.where(own, (seg_end // MOE_CPT)[None, :], 0), axis=1)
    follows = ((seg_start // MOE_CPT)[None, :] == end_tile[:, None]) & (seg > 0)[None, :]
    next_exp = jnp.where(jnp.any(follows, axis=1), jnp.sum(jnp.where(follows, experts[None, :], 0), axis=1), -1)
    next_exp = next_exp.astype(jnp.int32)
    zstart = (seg_start + tot).astype(jnp.int32)
    zcount = (seg - tot).astype(jnp.int32)
    first_tile = jnp.sum(jnp.where(own, (seg_start // MOE_CPT)[None, :], 0), axis=1)
    own_chunks = jnp.sum(jnp.where(own, tot[None, :], 0), axis=1)
    filled = own_chunks - (jnp.arange(nt_max, dtype=jnp.int32) - first_tile) * MOE_CPT
    half_full = (filled <= MOE_CPT // 2).astype(jnp.int32)
    expert_tables = (tile_exp, tile_blk, n_used.reshape(1), next_exp, half_full)
    return copies, ncopies, zstart, zcount, expert_tables


def _chunk_rows(c, nchunks=1):
    return pl.ds(pl.multiple_of(c * MOE_CH, MOE_CH), nchunks * MOE_CH)


def _for_tile_copies(cp_ref, cn_ref, tile, make, op):
    base = tile * MOE_CPROW

    def run(i, carry):
        op(make(cp_ref[base + i], cp_ref[base + MOE_MAXQ + i], MOE_RUN))
        return carry
    lax.fori_loop(0, cn_ref[2 * tile], run, 0)

    def single(i, carry):
        op(make(cp_ref[base + 2 * MOE_MAXQ + i], cp_ref[base + 2 * MOE_MAXQ + MOE_MAXS + i], 1))
        return carry
    lax.fori_loop(0, cn_ref[2 * tile + 1], single, 0)


def _dispatch_kernel(cp_ref, cn_ref, zstart_ref, zcount_ref, nu_ref, h2_ref, pos_ref, xs_hbm, xl_scr, zero_scr,
                     sems, zsem):
    j = pl.program_id(0)
    n = pl.num_programs(0)
    slot = j % 2
    tt = h2_ref.shape[0]
    nt_max = xs_hbm.shape[0] // MOE_TM

    def tile_copies(tile, slot_, op):
        def make(local, glob, nchunks):
            return pltpu.make_async_copy(xl_scr.at[slot_, _chunk_rows(local, nchunks)],
                                         xs_hbm.at[_chunk_rows(glob, nchunks)], sems.at[slot_])
        _for_tile_copies(cp_ref, cn_ref, tile, make, op)

    def zero_chunk_copy(d):
        return pltpu.make_async_copy(zero_scr.at[pl.ds(0, MOE_CH)], xs_hbm.at[_chunk_rows(d)], zsem.at[0])

    def zero_tile_copy(t):
        rows = pl.ds(pl.multiple_of(t * MOE_TM, MOE_TM), MOE_TM)
        return pltpu.make_async_copy(zero_scr, xs_hbm.at[rows], zsem.at[0])

    def zero_fill(op):
        def expert(e, carry):
            def chunk(r, carry_):
                op(zero_chunk_copy(zstart_ref[e] + r))
                return carry_
            return lax.fori_loop(0, zcount_ref[e], chunk, carry)
        lax.fori_loop(0, N_EXPERTS, expert, 0)

        def tile(t, carry):
            op(zero_tile_copy(t))
            return carry
        lax.fori_loop(nu_ref[0], nt_max, tile, 0)

    def wait_tile(tile, slot_):
        tile_copies(tile, slot_, lambda cp: cp.wait())

    @pl.when(j == 0)
    def _():
        zero_scr[...] = jnp.zeros_like(zero_scr)
        zero_fill(lambda cp: cp.start())

    @pl.when(j >= 2)
    def _():
        wait_tile(j - 2, slot)

    pos = pos_ref[...]
    h2 = h2_ref[...]
    for rb in range(MOE_RLOC // MOE_RB):
        io = lax.broadcasted_iota(jnp.int32, (MOE_RB, tt), 0) + rb * MOE_RB
        onehot = jnp.zeros((MOE_RB, tt), F32)
        for k in range(TOP_K):
            onehot = jnp.where(io == pos[k:k + 1, :], 1.0, onehot)
        xl_scr[slot, rb * MOE_RB:(rb + 1) * MOE_RB, :] = _dot(onehot.astype(BF16), h2).astype(BF16)

    tile_copies(j, slot, lambda cp: cp.start())

    @pl.when(j == n - 1)
    def _():
        @pl.when(j >= 1)
        def _():
            wait_tile(j - 1, 1 - slot)
        wait_tile(j, slot)
        zero_fill(lambda cp: cp.wait())


def _dispatch_call(tables, h2, pos_t, n_slots):
    T = h2.shape[0]
    tt = MOE_TT
    copies, ncopies, zstart, zcount, n_used = tables
    grid_spec = pltpu.PrefetchScalarGridSpec(
        num_scalar_prefetch=5,
        grid=(T // tt,),
        in_specs=[
            pl.BlockSpec((tt, D_MODEL), lambda j, *_: (j, 0)),
            pl.BlockSpec((TOP_K, tt), lambda j, *_: (0, j)),
        ],
        out_specs=pl.BlockSpec(memory_space=pl.ANY),
        scratch_shapes=[
            pltpu.VMEM((2, MOE_RLOC, D_MODEL), BF16),
            pltpu.VMEM((MOE_TM, D_MODEL), BF16),
            pltpu.SemaphoreType.DMA((2,)),
            pltpu.SemaphoreType.DMA((1,)),
        ],
    )
    return pl.pallas_call(
        _dispatch_kernel,
        grid_spec=grid_spec,
        out_shape=jax.ShapeDtypeStruct((n_slots, D_MODEL), BF16),
        compiler_params=_params(("arbitrary",)),
        name="moe_dispatch",
    )(copies, ncopies, zstart, zcount, n_used, h2, pos_t)


def _experts_kernel(te_ref, tb_ref, nu_ref, nx_ref, hf_ref, xs_ref, w1_hbm, b1_ref, w2_hbm, b2_ref, y_ref,
                    w1_f32, w2_f32, w1_scr, w2_scr, sems):
    i = pl.program_id(0)
    valid = i < nu_ref[0]
    half_full = hf_ref[i] == 1
    new_expert = (i == 0) | (te_ref[i] != te_ref[jnp.maximum(i - 1, 0)])

    def fetch(e):
        return (pltpu.make_async_copy(w1_hbm.at[e], w1_f32, sems.at[0]),
                pltpu.make_async_copy(w2_hbm.at[e], w2_f32, sems.at[1]))

    @pl.when(i == 0)
    def _():
        for cp in fetch(te_ref[0]):
            cp.start()

    @pl.when(valid & new_expert)
    def _():
        for cp in fetch(te_ref[i]):
            cp.wait()
        w1_scr[...] = w1_f32[...].astype(BF16)
        w2_scr[...] = w2_f32[...].astype(BF16)

        @pl.when(nx_ref[i] >= 0)
        def _():
            for cp in fetch(nx_ref[i]):
                cp.start()

    def mlp(rows):
        hh = _dot(xs_ref[rows, :], w1_scr[...]) + b1_ref[...]
        gate = jnp.minimum(hh[:, :D_FF], SWIGLU_LIMIT)
        up = jnp.clip(hh[:, D_FF:], -SWIGLU_LIMIT, SWIGLU_LIMIT)
        act = (up + 1.0) * gate * _sigmoid(SWIGLU_ALPHA * gate)
        y_ref[rows, :] = (_dot(act.astype(BF16), w2_scr[...]) + b2_ref[...]).astype(BF16)

    @pl.when(valid & jnp.logical_not(half_full))
    def _():
        mlp(slice(0, MOE_TM))

    @pl.when(valid & half_full)
    def _():
        mlp(slice(0, MOE_TM // 2))
        y_ref[MOE_TM // 2:, :] = jnp.zeros((MOE_TM // 2, D_MODEL), BF16)


def _experts_call(tables, xs, w1, b1, w2, b2):
    tile_exp = tables[0]
    nt_max = tile_exp.shape[0]
    slot_tile = lambda i, te, tb, *_: (tb[i], 0)
    per_e = lambda i, te, *_: (te[i], 0, 0)
    grid_spec = pltpu.PrefetchScalarGridSpec(
        num_scalar_prefetch=len(tables),
        grid=(nt_max,),
        in_specs=[
            pl.BlockSpec((MOE_TM, D_MODEL), slot_tile),
            pl.BlockSpec(memory_space=pl.ANY),
            pl.BlockSpec((None, 1, 2 * D_FF), per_e),
            pl.BlockSpec(memory_space=pl.ANY),
            pl.BlockSpec((None, 1, D_MODEL), per_e),
        ],
        out_specs=pl.BlockSpec((MOE_TM, D_MODEL), slot_tile),
        scratch_shapes=[
            pltpu.VMEM((D_MODEL, 2 * D_FF), F32),
            pltpu.VMEM((D_FF, D_MODEL), F32),
            pltpu.VMEM((D_MODEL, 2 * D_FF), BF16),
            pltpu.VMEM((D_FF, D_MODEL), BF16),
            pltpu.SemaphoreType.DMA((2,)),
        ],
    )
    return pl.pallas_call(
        _experts_kernel,
        grid_spec=grid_spec,
        out_shape=jax.ShapeDtypeStruct(xs.shape, BF16),
        input_output_aliases={len(tables): 0},
        compiler_params=_params(("arbitrary",)),
        name="moe_experts",
    )(*tables, xs, w1, b1, w2, b2)


def _combine_kernel(cp_ref, cn_ref, tok_ref, x1_ref, mod_ref, fw_ref, y_hbm, o_ref, yl_scr, w_scr, sems):
    j = pl.program_id(0)
    n = pl.num_programs(0)
    slot = j % 2
    tt = x1_ref.shape[0]

    def tile_copies(tile, slot_, op):
        def make(local, glob, nchunks):
            return pltpu.make_async_copy(y_hbm.at[_chunk_rows(glob, nchunks)],
                                         yl_scr.at[slot_, _chunk_rows(local, nchunks)], sems.at[slot_])
        _for_tile_copies(cp_ref, cn_ref, tile, make, op)

    def start_tile(tile, slot_):
        tile_copies(tile, slot_, lambda cp: cp.start())

    @pl.when(j == 0)
    def _():
        yl_scr[...] = jnp.zeros_like(yl_scr)
        start_tile(0, 0)

    @pl.when(j + 1 < n)
    def _():
        start_tile(j + 1, 1 - slot)

    tile_copies(j, slot, lambda cp: cp.wait())

    pos = tok_ref[:, 0:TOP_K].astype(jnp.int32)
    p = tok_ref[:, TOP_K:2 * TOP_K]
    for cb in range(MOE_RLOC // MOE_RB):
        io = lax.broadcasted_iota(jnp.int32, (tt, MOE_RB), 1) + cb * MOE_RB
        w = jnp.zeros((tt, MOE_RB), F32)
        for k in range(TOP_K):
            w = jnp.where(io == pos[:, k:k + 1], p[:, k:k + 1], w)
        w_scr[:, cb * MOE_RB:(cb + 1) * MOE_RB] = w.astype(BF16)
    moe = _dot(w_scr[...], yl_scr[slot])
    x2 = x1_ref[...] + _mod_slice(mod_ref, 0, 5) * moe
    o_ref[...] = _rms(x2, fw_ref[...])


def _combine_call(copies, ncopies, tok, x1, mod, fw, y):
    T = x1.shape[0]
    tt = MOE_TT
    row = lambda j, *_: (j, 0)
    fixed = lambda j, *_: (0, 0)
    grid_spec = pltpu.PrefetchScalarGridSpec(
        num_scalar_prefetch=2,
        grid=(T // tt,),
        in_specs=[
            pl.BlockSpec((tt, 128), row),
            pl.BlockSpec((tt, D_MODEL), row),
            pl.BlockSpec(mod.shape, fixed),
            pl.BlockSpec((1, D_MODEL), fixed),
            pl.BlockSpec(memory_space=pl.ANY),
        ],
        out_specs=pl.BlockSpec((tt, D_MODEL), row),
        scratch_shapes=[
            pltpu.VMEM((2, MOE_RLOC, D_MODEL), BF16),
            pltpu.VMEM((tt, MOE_RLOC), BF16),
            pltpu.SemaphoreType.DMA((2,)),
        ],
    )
    return pl.pallas_call(
        _combine_kernel,
        grid_spec=grid_spec,
        out_shape=jax.ShapeDtypeStruct((T, D_MODEL), F32),
        compiler_params=_params(("arbitrary",)),
        name="moe_combine",
    )(copies, ncopies, tok, x1, mod, fw, y)


def _rope_tables(T):
    rows = T // GRID_W
    n_freq = DK // 4
    inv = ROPE_BASE ** (-jnp.arange(n_freq, dtype=F32) / n_freq)
    row_ang = jnp.arange(rows, dtype=F32)[:, None] * inv
    col_ang = jnp.arange(GRID_W, dtype=F32)[:, None] * inv
    rot_row = jnp.pad(jnp.stack([jnp.cos(row_ang), jnp.sin(row_ang)]), ((0, 0), (0, 0), (0, n_freq)))
    rot_col = jnp.pad(jnp.stack([jnp.cos(col_ang), jnp.sin(col_ang)]), ((0, 0), (0, 0), (n_freq, 0)))
    return rot_row, rot_col


def kernel(x, c, ctx, c_ctx, w_mod, b_mod, norm1_w, norm2_w, w_in, sgu_ln_w, sgu_ln_b, sgu_w, sgu_b,
           ret_decay_fwd, ret_decay_bwd, w_proj_a, w_proj_b, w_out, router_w, router_b,
           moe_w1, moe_b1, moe_w2, moe_b2, final_norm_w):
    B, T, D = x.shape
    assert B == 1 and D == D_MODEL and w_mod.shape[0] == 1 and T % 1024 == 0
    x2d = x.reshape(T, D)
    cc = jnp.stack([c.reshape(D), c_ctx.reshape(D)], axis=1)
    mod = _mod_call(cc, w_mod[0], b_mod)
    dec = jnp.stack([ret_decay_fwd[0], ret_decay_bwd[0]], axis=0)
    w_in_bf = w_in[0].astype(BF16)
    s0 = _ctx_call(ctx.reshape(ctx.shape[1], D), norm1_w, mod, w_in_bf, dec)
    rot_row, rot_col = _rope_tables(T)
    proj = _inproj_call(x2d, norm1_w, mod, w_in_bf, rot_row, rot_col, sgu_ln_w, sgu_ln_b)
    za = _sgu_call(proj, sgu_w[0].astype(BF16), sgu_b[0].T, w_proj_a[0].astype(BF16))
    yf, yb = _ret_call(proj, s0, dec)
    x1, h2, logits_t = _merge_call(yf, yb, za, proj, x2d, mod, norm2_w, w_proj_b[0].astype(BF16),
                                   w_out[0].astype(BF16), router_w[0].T, router_b.reshape(N_EXPERTS, 1))
    pos_t, tok, nch = _route_call(logits_t)
    copies, ncopies, zstart, zcount, expert_tables = _routing_tables(nch[:, :, 0])
    n_slots = expert_tables[0].shape[0] * MOE_TM
    xs = _dispatch_call((copies, ncopies, zstart, zcount, expert_tables[2]), h2, pos_t, n_slots)
    y = _experts_call(expert_tables, xs, moe_w1[0], moe_b1[0][:, None, :], moe_w2[0], moe_b2[0][:, None, :])
    out = _combine_call(copies, ncopies, tok, x1, mod, final_norm_w.reshape(1, D), y)
    return out.reshape(B, T, D)
```

```python
import functools

import jax
import jax.numpy as jnp
from jax import lax
from jax.experimental import pallas as pl
from jax.experimental.pallas import tpu as pltpu

D_MODEL = 1024
GRID_W = 64
SGU_CHUNK = 128
SGU_GROUPS = 8
HEADS = 4
DK = D_MODEL // HEADS
DV = 2 * DK
ROPE_BASE = 10000.0
N_EXPERTS = 32
TOP_K = 4
D_FF = D_MODEL
SWIGLU_LIMIT = 7.0
SWIGLU_ALPHA = 1.702
EPS = 1e-6
IN_WIDTH = 12 * D_MODEL
COL_U, COL_V, COL_Q, COL_K, COL_VR, COL_GF, COL_GB, COL_GA, COL_GBR = 0, 1, 2, 3, 4, 6, 8, 10, 11

INPROJ_NB = 4
RET_CHUNK = 256
RET_SUB = 1
VMEM_LIMIT = 56 * 1024 * 1024

MOE_TT = 512
MOE_CH = 16
MOE_TM = 512
MOE_CPT = MOE_TM // MOE_CH
MOE_PARTS = 4
MOE_MAXC = TOP_K * MOE_TT // MOE_CH + N_EXPERTS
MOE_RLOC = MOE_MAXC * MOE_CH
MOE_RB = 256
MOE_RUN = 4
MOE_MAXQ = MOE_MAXC // MOE_RUN
MOE_MAXS = (MOE_RUN - 1) * N_EXPERTS
MOE_CPROW = 2 * (MOE_MAXQ + MOE_MAXS)

F32 = jnp.float32
BF16 = jnp.bfloat16


def _params(sem):
    return pltpu.CompilerParams(dimension_semantics=sem, vmem_limit_bytes=VMEM_LIMIT)


def _dot(a, b):
    return jnp.dot(a, b, preferred_element_type=F32)


def _dot_nt(a, b):
    return lax.dot_general(a, b, (((1,), (1,)), ((), ())), preferred_element_type=F32)


def _dot_tn(a, b):
    return lax.dot_general(a, b, (((0,), (0,)), ((), ())), preferred_element_type=F32)


def _rms(x, w):
    return x * lax.rsqrt(jnp.mean(x * x, axis=-1, keepdims=True) + EPS) * w


def _gelu(x):
    return 0.5 * x * (1.0 + lax.erf(x * (2.0 ** -0.5)))


def _sigmoid(x):
    return 0.5 * jnp.tanh(0.5 * x) + 0.5


def _mod_slice(mod_ref, row, k):
    return mod_ref[row:row + 1, k * D_MODEL:(k + 1) * D_MODEL]


def _mod_kernel(cc_ref, w_ref, b_ref, o_ref):
    s = cc_ref[...]
    s = s * jax.nn.sigmoid(s)
    w = w_ref[...]
    r0 = jnp.sum(s[:, 0:1] * w, axis=0, keepdims=True)
    r1 = jnp.sum(s[:, 1:2] * w, axis=0, keepdims=True)
    o_ref[...] = jnp.concatenate([r0, r1], axis=0) + b_ref[...]


def _mod_call(cc, w_mod, b_mod):
    bn = 1536
    n = w_mod.shape[1]
    return pl.pallas_call(
        _mod_kernel,
        grid=(n // bn,),
        in_specs=[
            pl.BlockSpec((D_MODEL, 2), lambda j: (0, 0)),
            pl.BlockSpec((D_MODEL, bn), lambda j: (0, j)),
            pl.BlockSpec((1, bn), lambda j: (0, j)),
        ],
        out_specs=pl.BlockSpec((2, bn), lambda j: (0, j)),
        out_shape=jax.ShapeDtypeStruct((2, n), F32),
        compiler_params=_params(("arbitrary",)),
        name="mod_vectors",
    )(cc, w_mod, b_mod)


def _ctx_kernel(ctx_ref, nw_ref, mod_ref, w_ref, dec_ref, s0_ref):
    L = ctx_ref.shape[0]
    sh = _mod_slice(mod_ref, 1, 0)
    sc = _mod_slice(mod_ref, 1, 1)
    hc = (_rms(ctx_ref[...], nw_ref[...]) * (1.0 + sc) + sh).astype(BF16)
    kv = _dot(hc, w_ref[...])
    lg = jnp.log1p(-jnp.exp2(dec_ref[...]))
    pos = lax.broadcasted_iota(jnp.int32, (L, 1), 0).astype(F32)
    for h in range(HEADS):
        k = kv[:, h * DK:(h + 1) * DK] * (DK ** -0.5)
        v = kv[:, HEADS * DK + h * DV:HEADS * DK + (h + 1) * DV].astype(BF16)
        wf = jnp.exp((L - 1.0 - pos) * lg[0:1, h:h + 1])
        wb = jnp.exp(pos * lg[1:2, h:h + 1])
        s0_ref[0, h] = _dot_tn((k * wf).astype(BF16), v)
        s0_ref[1, h] = _dot_tn((k * wb).astype(BF16), v)


def _ctx_call(ctx2d, nw, mod, w_in_bf, dec):
    L = ctx2d.shape[0]
    wcols = HEADS * DK + HEADS * DV
    return pl.pallas_call(
        _ctx_kernel,
        grid=(1,),
        in_specs=[
            pl.BlockSpec((L, D_MODEL), lambda i: (0, 0)),
            pl.BlockSpec((1, D_MODEL), lambda i: (0, 0)),
            pl.BlockSpec(mod.shape, lambda i: (0, 0)),
            pl.BlockSpec((D_MODEL, wcols), lambda i: (0, COL_K * D_MODEL // wcols)),
            pl.BlockSpec(dec.shape, lambda i: (0, 0)),
        ],
        out_specs=pl.BlockSpec((2, HEADS, DK, DV), lambda i: (0, 0, 0, 0)),
        out_shape=jax.ShapeDtypeStruct((2, HEADS, DK, DV), F32),
        compiler_params=_params(("arbitrary",)),
        name="ctx_states",
    )(ctx2d, nw, mod, w_in_bf, dec)


def _inproj_kernel(x_ref, nw_ref, mod_ref, w_ref, rot_row_ref, rot_col_ref, lnw_ref, lnb_ref, o_ref,
                   h_scr, g_scr, cos_ref, sin_ref):
    j = pl.program_id(1)
    tm = h_scr.shape[0]

    @pl.when(j == 0)
    def _():
        sh = _mod_slice(mod_ref, 0, 0)
        sc = _mod_slice(mod_ref, 0, 1)
        h_scr[...] = (_rms(x_ref[...], nw_ref[...]) * (1.0 + sc) + sh).astype(BF16)

    def pieces(emit, span, rsplit=1):
        for c0 in range(span.start, span.stop, DK):
            cols = slice(c0, c0 + DK)
            for r in range(rsplit):
                rows = slice(r * tm // rsplit, (r + 1) * tm // rsplit)
                emit(rows, cols, _dot(h_scr[rows, :], w_ref[:, cols]))

    def store(fn):
        def emit(rows, cols, acc):
            o_ref[rows, cols] = fn(acc).astype(BF16)
        return emit

    def rope(scale):
        half = DK // 2

        def emit(rows, cols, acc):
            cos = cos_ref[rows, :]
            sin = sin_ref[rows, :]
            x1 = acc[:, :half]
            x2 = acc[:, half:]
            o_ref[rows, cols.start:cols.start + half] = ((x1 * cos - x2 * sin) * scale).astype(BF16)
            o_ref[rows, cols.start + half:cols.stop] = ((x1 * sin + x2 * cos) * scale).astype(BF16)
        return emit

    def gelu_layer_norm(span):
        def emit(rows, cols, acc):
            g_scr[rows, cols.start - span.start:cols.stop - span.start] = _gelu(acc)
        pieces(emit, span)
        g = g_scr[...]
        mu = jnp.mean(g, axis=-1, keepdims=True)
        d = g - mu
        var = jnp.mean(d * d, axis=-1, keepdims=True)
        o_ref[:, span] = (d * lax.rsqrt(var + EPS) * lnw_ref[...] + lnb_ref[...]).astype(BF16)

    def rotary_tables():
        for r in range(tm // GRID_W):
            tok = slice(r * GRID_W, (r + 1) * GRID_W)
            cos_ref[tok, :] = rot_row_ref[0, r:r + 1, :] + rot_col_ref[0]
            sin_ref[tok, :] = rot_row_ref[1, r:r + 1, :] + rot_col_ref[1]

    def column_block(block, span):
        if block == COL_U:
            pieces(store(_gelu), span)
        elif block == COL_V:
            gelu_layer_norm(span)
        elif block == COL_Q:
            rotary_tables()
            pieces(rope(1.0), span)
        elif block == COL_K:
            pieces(rope(DK ** -0.5), span)
        elif block < COL_GF:
            pieces(store(lambda a: a), span)
        elif block < COL_GA:
            pieces(store(lambda a: a * _sigmoid(a)), span, rsplit=2)
        else:
            pieces(store(_sigmoid), span, rsplit=2)

    for step in range(IN_WIDTH // D_MODEL // INPROJ_NB):
        @pl.when(j == step)
        def _():
            for b in range(INPROJ_NB):
                column_block(step * INPROJ_NB + b, slice(b * D_MODEL, (b + 1) * D_MODEL))


def _inproj_call(x2d, nw, mod, w_in_bf, rot_row, rot_col, lnw, lnb):
    T = x2d.shape[0]
    tm = 1024
    half = DK // 2
    return pl.pallas_call(
        _inproj_kernel,
        grid=(T // tm, IN_WIDTH // (INPROJ_NB * D_MODEL)),
        in_specs=[
            pl.BlockSpec((tm, D_MODEL), lambda i, j: (i, 0)),
            pl.BlockSpec((1, D_MODEL), lambda i, j: (0, 0)),
            pl.BlockSpec(mod.shape, lambda i, j: (0, 0)),
            pl.BlockSpec((D_MODEL, INPROJ_NB * D_MODEL), lambda i, j: (0, j)),
            pl.BlockSpec((2, tm // GRID_W, half), lambda i, j: (0, i, 0)),
            pl.BlockSpec((2, GRID_W, half), lambda i, j: (0, 0, 0)),
            pl.BlockSpec((1, D_MODEL), lambda i, j: (0, 0)),
            pl.BlockSpec((1, D_MODEL), lambda i, j: (0, 0)),
        ],
        out_specs=pl.BlockSpec((tm, INPROJ_NB * D_MODEL), lambda i, j: (i, j)),
        out_shape=jax.ShapeDtypeStruct((T, IN_WIDTH), BF16),
        scratch_shapes=[
            pltpu.VMEM((tm, D_MODEL), BF16),
            pltpu.VMEM((tm, D_MODEL), F32),
            pltpu.VMEM((tm, half), F32),
            pltpu.VMEM((tm, half), F32),
        ],
        compiler_params=_params(("arbitrary", "arbitrary")),
        name="in_proj",
    )(x2d, nw, mod, w_in_bf, rot_row, rot_col, lnw, lnb)


def _sgu_kernel(u_ref, v_ref, ga_ref, ws_ref, bt_ref, wpa_ref, o_ref, ya_scr):
    tm = u_ref.shape[0]
    gd = D_MODEL // SGU_GROUPS
    for n in range(tm // SGU_CHUNK):
        rows = slice(n * SGU_CHUNK, (n + 1) * SGU_CHUNK)
        for g in range(SGU_GROUPS):
            cols = slice(g * gd, (g + 1) * gd)
            mixed = _dot(ws_ref[g], v_ref[rows, cols]) + bt_ref[:, g:g + 1]
            ya_scr[rows, cols] = (u_ref[rows, cols].astype(F32) * mixed).astype(BF16)
    pa = _dot(ya_scr[...], wpa_ref[...])
    o_ref[...] = (ga_ref[...].astype(F32) * pa).astype(BF16)


def _sgu_call(proj, ws_bf, bt, wpa_bf):
    T = proj.shape[0]
    tm = 1024
    return pl.pallas_call(
        _sgu_kernel,
        grid=(T // tm,),
        in_specs=[
            pl.BlockSpec((tm, D_MODEL), lambda i: (i, COL_U)),
            pl.BlockSpec((tm, D_MODEL), lambda i: (i, COL_V)),
            pl.BlockSpec((tm, D_MODEL), lambda i: (i, COL_GA)),
            pl.BlockSpec(ws_bf.shape, lambda i: (0, 0, 0)),
            pl.BlockSpec(bt.shape, lambda i: (0, 0)),
            pl.BlockSpec((D_MODEL, D_MODEL), lambda i: (0, 0)),
        ],
        out_specs=pl.BlockSpec((tm, D_MODEL), lambda i: (i, 0)),
        out_shape=jax.ShapeDtypeStruct((T, D_MODEL), BF16),
        scratch_shapes=[pltpu.VMEM((tm, D_MODEL), BF16)],
        compiler_params=_params(("arbitrary",)),
        name="sgu_proj_a",
    )(proj, proj, proj, ws_bf, bt, wpa_bf)


def _ret_kernel(qf_ref, kf_ref, vf_ref, gf_ref, qb_ref, kb_ref, vb_ref, gb_ref, s0_ref, dec_ref,
                yf_ref, yb_ref, s_scr, intra_scr, qd_scr, kd_scr, cd_scr):
    C = RET_CHUNK
    i = pl.program_id(0)

    @pl.when(i == 0)
    def _():
        lg = jnp.log1p(-jnp.exp2(dec_ref[...]))
        r = lax.broadcasted_iota(jnp.int32, (C, C), 0).astype(F32)
        c = lax.broadcasted_iota(jnp.int32, (C, C), 1).astype(F32)
        pos = lax.broadcasted_iota(jnp.int32, (C, 1), 0).astype(F32)
        for h in range(HEADS):
            lf = lg[0:1, h:h + 1]
            lb = lg[1:2, h:h + 1]
            intra_scr[0, h] = jnp.where(r >= c, jnp.exp(jnp.maximum(r - c, 0.0) * lf), 0.0)
            intra_scr[1, h] = jnp.where(c >= r, jnp.exp(jnp.maximum(c - r, 0.0) * lb), 0.0)
            qd_scr[0, h] = jnp.exp((pos + 1.0) * lf)
            qd_scr[1, h] = jnp.exp((C - pos) * lb)
            kd_scr[0, h] = jnp.exp((C - 1.0 - pos) * lf)
            kd_scr[1, h] = jnp.exp(pos * lb)
            cd_scr[0, h] = jnp.exp(C * lf)
            cd_scr[1, h] = jnp.exp(C * lb)
        s_scr[...] = s0_ref[...]

    dirs = ((qf_ref, kf_ref, vf_ref, gf_ref, yf_ref), (qb_ref, kb_ref, vb_ref, gb_ref, yb_ref))
    nsub = qf_ref.shape[0] // C
    for sub in range(nsub):
        for d, (q_ref, k_ref, v_ref, g_ref, y_ref) in enumerate(dirs):
            c0 = (sub if d == 0 else nsub - 1 - sub) * C
            rows = slice(c0, c0 + C)
            for h in range(HEADS):
                q = q_ref[rows, h * DK:(h + 1) * DK]
                k = k_ref[rows, h * DK:(h + 1) * DK]
                v = v_ref[rows, h * DV:(h + 1) * DV]
                s = s_scr[d, h]
                a = (_dot_nt(q, k) * intra_scr[d, h]).astype(BF16)
                o = _dot(a, v) + _dot(q, s.astype(BF16)) * qd_scr[d, h]
                hn = o * lax.rsqrt(jnp.mean(o * o, axis=-1, keepdims=True) + EPS)
                y_ref[rows, h * DV:(h + 1) * DV] = (g_ref[rows, h * DV:(h + 1) * DV].astype(F32) * hn).astype(BF16)
                kd = (k.astype(F32) * kd_scr[d, h]).astype(BF16)
                s_scr[d, h] = s * cd_scr[d, h] + _dot_tn(kd, v)


def _ret_call(proj, s0, dec):
    T = proj.shape[0]
    C = RET_CHUNK
    rows = RET_SUB * C
    n = T // rows
    qk_w = HEADS * DK
    v_w = HEADS * DV
    fwd = lambda col: (lambda i: (i, col))
    bwd = lambda col: (lambda i: (n - 1 - i, col))
    return pl.pallas_call(
        _ret_kernel,
        grid=(n,),
        in_specs=[
            pl.BlockSpec((rows, qk_w), fwd(COL_Q)),
            pl.BlockSpec((rows, qk_w), fwd(COL_K)),
            pl.BlockSpec((rows, v_w), fwd(COL_VR // 2)),
            pl.BlockSpec((rows, v_w), fwd(COL_GF // 2)),
            pl.BlockSpec((rows, qk_w), bwd(COL_Q)),
            pl.BlockSpec((rows, qk_w), bwd(COL_K)),
            pl.BlockSpec((rows, v_w), bwd(COL_VR // 2)),
            pl.BlockSpec((rows, v_w), bwd(COL_GB // 2)),
            pl.BlockSpec(s0.shape, lambda i: (0, 0, 0, 0)),
            pl.BlockSpec(dec.shape, lambda i: (0, 0)),
        ],
        out_specs=[
            pl.BlockSpec((rows, v_w), lambda i: (i, 0)),
            pl.BlockSpec((rows, v_w), lambda i: (n - 1 - i, 0)),
        ],
        out_shape=[jax.ShapeDtypeStruct((T, v_w), BF16), jax.ShapeDtypeStruct((T, v_w), BF16)],
        scratch_shapes=[
            pltpu.VMEM((2, HEADS, DK, DV), F32),
            pltpu.VMEM((2, HEADS, C, C), F32),
            pltpu.VMEM((2, HEADS, C, 1), F32),
            pltpu.VMEM((2, HEADS, C, 1), F32),
            pltpu.VMEM((2, HEADS, 1, 1), F32),
        ],
        compiler_params=_params(("arbitrary",)),
        name="retention",
    )(proj, proj, proj, proj, proj, proj, proj, proj, s0, dec)


def _merge_kernel(yf_ref, yb_ref, za_ref, gbr_ref, x_ref, mod_ref, nw_ref, wpb_ref, wo_ref, rw_ref, rb_ref,
                  x1_ref, h2_ref, lg_ref):
    yb = yf_ref[...] + yb_ref[...]
    pb = _dot(yb, wpb_ref[...])
    y = za_ref[...].astype(F32) + gbr_ref[...].astype(F32) * pb
    yo = _dot(y.astype(BF16), wo_ref[...])
    x1 = x_ref[...] + _mod_slice(mod_ref, 0, 2) * yo
    x1_ref[...] = x1
    h2 = _rms(x1, nw_ref[...]) * (1.0 + _mod_slice(mod_ref, 0, 4)) + _mod_slice(mod_ref, 0, 3)
    h2_hi = h2.astype(BF16)
    h2_ref[...] = h2_hi
    h2_lo = (h2 - h2_hi.astype(F32)).astype(BF16)
    rw = rw_ref[...]
    rw_hi = rw.astype(BF16)
    rw_lo = (rw - rw_hi.astype(F32)).astype(BF16)
    both = _dot_nt(jnp.concatenate([rw_hi, rw_lo], axis=0), h2_hi)
    lg_ref[...] = both[:N_EXPERTS] + both[N_EXPERTS:] + _dot_nt(rw_hi, h2_lo) + rb_ref[...]


def _merge_call(yf, yb, za, proj, x2d, mod, nw2, wpb_bf, wo_bf, rw, rb):
    T = x2d.shape[0]
    tm = 512
    row = lambda i: (i, 0)
    fixed = lambda i: (0, 0)
    return pl.pallas_call(
        _merge_kernel,
        grid=(T // tm,),
        in_specs=[
            pl.BlockSpec((tm, HEADS * DV), row),
            pl.BlockSpec((tm, HEADS * DV), row),
            pl.BlockSpec((tm, D_MODEL), row),
            pl.BlockSpec((tm, D_MODEL), lambda i: (i, COL_GBR)),
            pl.BlockSpec((tm, D_MODEL), row),
            pl.BlockSpec(mod.shape, fixed),
            pl.BlockSpec((1, D_MODEL), fixed),
            pl.BlockSpec(wpb_bf.shape, fixed),
            pl.BlockSpec(wo_bf.shape, fixed),
            pl.BlockSpec(rw.shape, fixed),
            pl.BlockSpec(rb.shape, fixed),
        ],
        out_specs=[
            pl.BlockSpec((tm, D_MODEL), row),
            pl.BlockSpec((tm, D_MODEL), row),
            pl.BlockSpec((N_EXPERTS, tm), lambda i: (0, i)),
        ],
        out_shape=[
            jax.ShapeDtypeStruct((T, D_MODEL), F32),
            jax.ShapeDtypeStruct((T, D_MODEL), BF16),
            jax.ShapeDtypeStruct((N_EXPERTS, T), F32),
        ],
        compiler_params=_params(("arbitrary",)),
        name="merge_router",
    )(yf, yb, za, proj, x2d, mod, nw2, wpb_bf, wo_bf, rw, rb)


def _route_kernel(lg_ref, pos_ref, tok_ref, nch_ref, su_scr):
    E, tt = lg_ref.shape

    @pl.when(pl.program_id(0) == 0)
    def _():
        r = lax.broadcasted_iota(jnp.int32, (tt, tt), 0)
        c = lax.broadcasted_iota(jnp.int32, (tt, tt), 1)
        su_scr[...] = jnp.where(r < c, 1.0, 0.0).astype(BF16)

    sub = lax.broadcasted_iota(jnp.int32, (E, tt), 0)
    work = lg_ref[...]
    vals, hots = [], []
    for _ in range(TOP_K):
        m = jnp.max(work, axis=0, keepdims=True)
        first = jnp.min(jnp.where(work == m, sub, E), axis=0, keepdims=True)
        hot = sub == first
        vals.append(m)
        hots.append(hot)
        work = jnp.where(hot, -jnp.inf, work)
    exps = [jnp.exp(v - vals[0]) for v in vals]
    inv = 1.0 / functools.reduce(lambda a, b: a + b, exps)

    member = functools.reduce(lambda a, b: a + b, [jnp.where(h, 1.0, 0.0) for h in hots])
    cnt = jnp.sum(member, axis=1, keepdims=True)
    nch = jnp.floor((cnt + (MOE_CH - 1.0)) * (1.0 / MOE_CH))
    nch_b = jnp.broadcast_to(nch, (E, 128))
    rank = _dot(member.astype(BF16), su_scr[...])
    er = lax.broadcasted_iota(jnp.int32, (E, E), 0)
    ec = lax.broadcasted_iota(jnp.int32, (E, E), 1)
    first_chunk = _dot(jnp.where(ec < er, 1.0, 0.0).astype(BF16), nch_b.astype(BF16))[:, 0:1]
    base = first_chunk * MOE_CH + rank
    pos = [jnp.sum(jnp.where(hots[k], base, 0.0), axis=0, keepdims=True) for k in range(TOP_K)]
    for k in range(TOP_K):
        pos_ref[k:k + 1, :] = pos[k].astype(jnp.int32)
    rows = jnp.concatenate(pos + [e * inv for e in exps] + [jnp.zeros((128 - 2 * TOP_K, tt), F32)], axis=0)
    tok_ref[...] = rows.T
    nch_ref[0] = nch_b.astype(jnp.int32)


def _route_call(logits_t):
    E, T = logits_t.shape
    tt = MOE_TT
    return pl.pallas_call(
        _route_kernel,
        grid=(T // tt,),
        in_specs=[pl.BlockSpec((E, tt), lambda j: (0, j))],
        out_specs=[
            pl.BlockSpec((TOP_K, tt), lambda j: (0, j)),
            pl.BlockSpec((tt, 128), lambda j: (j, 0)),
            pl.BlockSpec((1, E, 128), lambda j: (j, 0, 0)),
        ],
        out_shape=[
            jax.ShapeDtypeStruct((TOP_K, T), jnp.int32),
            jax.ShapeDtypeStruct((T, 128), F32),
            jax.ShapeDtypeStruct((T // tt, E, 128), jnp.int32),
        ],
        scratch_shapes=[pltpu.VMEM((tt, tt), BF16)],
        compiler_params=_params(("arbitrary",)),
        name="moe_route",
    )(logits_t)


def _slot_tiles_max(n_tiles):
    return -(-(n_tiles * MOE_MAXC + N_EXPERTS * (MOE_CPT - 1)) // MOE_CPT)


def _routing_tables(nch):
    n_tiles, E = nch.shape

    def cumsum_last(a):
        m = a.shape[-1]
        keep = jnp.arange(m)[:, None] <= jnp.arange(m)[None, :]
        return jnp.sum(jnp.where(keep, a[..., :, None], 0), axis=-2)

    tot = jnp.sum(nch, axis=0)
    seg = (tot + MOE_CPT - 1) // MOE_CPT * MOE_CPT
    seg_end = cumsum_last(seg)
    seg_start = seg_end - seg
    gstart = seg_start[None, :] + cumsum_last(nch.T).T - nch
    l_end = cumsum_last(nch)
    lstart = l_end - nch

    def copy_list(count, first_local, first_global, step, max_len):
        end = cumsum_last(count)
        start = end - count
        i = jnp.arange(max_len, dtype=jnp.int32)[None, :, None]
        owner = (start[:, None, :] <= i) & (i < end[:, None, :])
        off = (i - start[:, None, :]) * step
        src = jnp.sum(jnp.where(owner, first_local[:, None, :] + off, 0), axis=-1)
        dst = jnp.sum(jnp.where(owner, first_global[:, None, :] + off, 0), axis=-1)
        return src, dst, end[:, -1]

    runs = nch // MOE_RUN
    rsrc, rdst, nruns = copy_list(runs, lstart, gstart, MOE_RUN, MOE_MAXQ)
    ssrc, sdst, nsingles = copy_list(nch - runs * MOE_RUN, lstart + runs * MOE_RUN, gstart + runs * MOE_RUN, 1,
                                     MOE_MAXS)
    copies = jnp.concatenate([rsrc, rdst, ssrc, sdst], axis=1).astype(jnp.int32).reshape(-1)
    ncopies = jnp.stack([nruns, nsingles], axis=1).astype(jnp.int32).reshape(-1)
    nt_max = _slot_tiles_max(n_tiles)
    n_used = (seg_end[-1] // MOE_CPT).astype(jnp.int32)
    tile_blk = jnp.minimum(jnp.arange(nt_max, dtype=jnp.int32), n_used - 1)
    tile_exp = jnp.sum(seg_end[None, :] <= (tile_blk * MOE_CPT)[:, None], axis=1)
    tile_exp = jnp.minimum(tile_exp, E - 1).astype(jnp.int32)
    experts = jnp.arange(E, dtype=jnp.int32)
    own = tile_exp[:, None] == experts[None, :]
    end_tile = jnp.sum(jnp.where(own, (seg_end // MOE_CPT)[None, :], 0), axis=1)
    follows = ((seg_start // MOE_CPT)[None, :] == end_tile[:, None]) & (seg > 0)[None, :]
    next_exp = jnp.where(jnp.any(follows, axis=1), jnp.sum(jnp.where(follows, experts[None, :], 0), axis=1), -1)
    next_exp = next_exp.astype(jnp.int32)
    zstart = (seg_start + tot).astype(jnp.int32)
    zcount = (seg - tot).astype(jnp.int32)
    first_tile = jnp.sum(jnp.where(own, (seg_start // MOE_CPT)[None, :], 0), axis=1)
    own_chunks = jnp.sum(jnp.where(own, tot[None, :], 0), axis=1)
    filled = own_chunks - (jnp.arange(nt_max, dtype=jnp.int32) - first_tile) * MOE_CPT
    per_part = MOE_CPT // MOE_PARTS
    parts = jnp.clip((filled + per_part - 1) // per_part, 1, MOE_PARTS).astype(jnp.int32)
    expert_tables = (tile_exp, tile_blk, n_used.reshape(1), next_exp, parts)
    return copies, ncopies, zstart, zcount, expert_tables


def _chunk_rows(c, nchunks=1):
    return pl.ds(pl.multiple_of(c * MOE_CH, MOE_CH), nchunks * MOE_CH)


def _for_tile_copies(cp_ref, cn_ref, tile, make, op):
    base = tile * MOE_CPROW

    def run(i, carry):
        op(make(cp_ref[base + i], cp_ref[base + MOE_MAXQ + i], MOE_RUN))
        return carry
    lax.fori_loop(0, cn_ref[2 * tile], run, 0)

    def single(i, carry):
        op(make(cp_ref[base + 2 * MOE_MAXQ + i], cp_ref[base + 2 * MOE_MAXQ + MOE_MAXS + i], 1))
        return carry
    lax.fori_loop(0, cn_ref[2 * tile + 1], single, 0)


def _dispatch_kernel(cp_ref, cn_ref, zstart_ref, zcount_ref, nu_ref, h2_ref, pos_ref, xs_hbm, xl_scr, zero_scr,
                     sems, zsem):
    j = pl.program_id(0)
    n = pl.num_programs(0)
    slot = j % 2
    tt = h2_ref.shape[0]
    nt_max = xs_hbm.shape[0] // MOE_TM

    def tile_copies(tile, slot_, op):
        def make(local, glob, nchunks):
            return pltpu.make_async_copy(xl_scr.at[slot_, _chunk_rows(local, nchunks)],
                                         xs_hbm.at[_chunk_rows(glob, nchunks)], sems.at[slot_])
        _for_tile_copies(cp_ref, cn_ref, tile, make, op)

    def zero_chunk_copy(d):
        return pltpu.make_async_copy(zero_scr.at[pl.ds(0, MOE_CH)], xs_hbm.at[_chunk_rows(d)], zsem.at[0])

    def zero_tile_copy(t):
        rows = pl.ds(pl.multiple_of(t * MOE_TM, MOE_TM), MOE_TM)
        return pltpu.make_async_copy(zero_scr, xs_hbm.at[rows], zsem.at[0])

    def zero_fill(op):
        def expert(e, carry):
            def chunk(r, carry_):
                op(zero_chunk_copy(zstart_ref[e] + r))
                return carry_
            return lax.fori_loop(0, zcount_ref[e], chunk, carry)
        lax.fori_loop(0, N_EXPERTS, expert, 0)

        def tile(t, carry):
            op(zero_tile_copy(t))
            return carry
        lax.fori_loop(nu_ref[0], nt_max, tile, 0)

    def wait_tile(tile, slot_):
        tile_copies(tile, slot_, lambda cp: cp.wait())

    @pl.when(j == 0)
    def _():
        zero_scr[...] = jnp.zeros_like(zero_scr)
        zero_fill(lambda cp: cp.start())

    @pl.when(j >= 2)
    def _():
        wait_tile(j - 2, slot)

    pos = pos_ref[...]
    h2 = h2_ref[...]
    for rb in range(MOE_RLOC // MOE_RB):
        io = lax.broadcasted_iota(jnp.int32, (MOE_RB, tt), 0) + rb * MOE_RB
        onehot = jnp.zeros((MOE_RB, tt), F32)
        for k in range(TOP_K):
            onehot = jnp.where(io == pos[k:k + 1, :], 1.0, onehot)
        xl_scr[slot, rb * MOE_RB:(rb + 1) * MOE_RB, :] = _dot(onehot.astype(BF16), h2).astype(BF16)

    tile_copies(j, slot, lambda cp: cp.start())

    @pl.when(j == n - 1)
    def _():
        @pl.when(j >= 1)
        def _():
            wait_tile(j - 1, 1 - slot)
        wait_tile(j, slot)
        zero_fill(lambda cp: cp.wait())


def _dispatch_call(tables, h2, pos_t, n_slots):
    T = h2.shape[0]
    tt = MOE_TT
    copies, ncopies, zstart, zcount, n_used = tables
    grid_spec = pltpu.PrefetchScalarGridSpec(
        num_scalar_prefetch=5,
        grid=(T // tt,),
        in_specs=[
            pl.BlockSpec((tt, D_MODEL), lambda j, *_: (j, 0)),
            pl.BlockSpec((TOP_K, tt), lambda j, *_: (0, j)),
        ],
        out_specs=pl.BlockSpec(memory_space=pl.ANY),
        scratch_shapes=[
            pltpu.VMEM((2, MOE_RLOC, D_MODEL), BF16),
            pltpu.VMEM((MOE_TM, D_MODEL), BF16),
            pltpu.SemaphoreType.DMA((2,)),
            pltpu.SemaphoreType.DMA((1,)),
        ],
    )
    return pl.pallas_call(
        _dispatch_kernel,
        grid_spec=grid_spec,
        out_shape=jax.ShapeDtypeStruct((n_slots, D_MODEL), BF16),
        compiler_params=_params(("arbitrary",)),
        name="moe_dispatch",
    )(copies, ncopies, zstart, zcount, n_used, h2, pos_t)


def _experts_kernel(te_ref, tb_ref, nu_ref, nx_ref, parts_ref, xs_ref, w1_hbm, b1_ref, w2_hbm, b2_ref, y_ref,
                    w1_f32, w2_f32, w1_scr, w2_scr, sems):
    i = pl.program_id(0)
    valid = i < nu_ref[0]
    new_expert = (i == 0) | (te_ref[i] != te_ref[jnp.maximum(i - 1, 0)])

    def fetch(e):
        return (pltpu.make_async_copy(w1_hbm.at[e], w1_f32, sems.at[0]),
                pltpu.make_async_copy(w2_hbm.at[e], w2_f32, sems.at[1]))

    @pl.when(i == 0)
    def _():
        for cp in fetch(te_ref[0]):
            cp.start()

    @pl.when(valid & new_expert)
    def _():
        for cp in fetch(te_ref[i]):
            cp.wait()
        w1_scr[...] = w1_f32[...].astype(BF16)
        w2_scr[...] = w2_f32[...].astype(BF16)

        @pl.when(nx_ref[i] >= 0)
        def _():
            for cp in fetch(nx_ref[i]):
                cp.start()

    def mlp(rows):
        hh = _dot(xs_ref[rows, :], w1_scr[...]) + b1_ref[...]
        gate = jnp.minimum(hh[:, :D_FF], SWIGLU_LIMIT)
        up = jnp.clip(hh[:, D_FF:], -SWIGLU_LIMIT, SWIGLU_LIMIT)
        act = (up + 1.0) * gate * _sigmoid(SWIGLU_ALPHA * gate)
        y_ref[rows, :] = (_dot(act.astype(BF16), w2_scr[...]) + b2_ref[...]).astype(BF16)

    for nparts in range(1, MOE_PARTS + 1):
        @pl.when(valid & (parts_ref[i] == nparts))
        def _():
            rows = nparts * (MOE_TM // MOE_PARTS)
            mlp(slice(0, rows))
            if rows < MOE_TM:
                y_ref[rows:, :] = jnp.zeros((MOE_TM - rows, D_MODEL), BF16)


def _experts_call(tables, xs, w1, b1, w2, b2):
    tile_exp = tables[0]
    nt_max = tile_exp.shape[0]
    slot_tile = lambda i, te, tb, *_: (tb[i], 0)
    per_e = lambda i, te, *_: (te[i], 0, 0)
    grid_spec = pltpu.PrefetchScalarGridSpec(
        num_scalar_prefetch=len(tables),
        grid=(nt_max,),
        in_specs=[
            pl.BlockSpec((MOE_TM, D_MODEL), slot_tile),
            pl.BlockSpec(memory_space=pl.ANY),
            pl.BlockSpec((None, 1, 2 * D_FF), per_e),
            pl.BlockSpec(memory_space=pl.ANY),
            pl.BlockSpec((None, 1, D_MODEL), per_e),
        ],
        out_specs=pl.BlockSpec((MOE_TM, D_MODEL), slot_tile),
        scratch_shapes=[
            pltpu.VMEM((D_MODEL, 2 * D_FF), F32),
            pltpu.VMEM((D_FF, D_MODEL), F32),
            pltpu.VMEM((D_MODEL, 2 * D_FF), BF16),
            pltpu.VMEM((D_FF, D_MODEL), BF16),
            pltpu.SemaphoreType.DMA((2,)),
        ],
    )
    return pl.pallas_call(
        _experts_kernel,
        grid_spec=grid_spec,
        out_shape=jax.ShapeDtypeStruct(xs.shape, BF16),
        input_output_aliases={len(tables): 0},
        compiler_params=_params(("arbitrary",)),
        name="moe_experts",
    )(*tables, xs, w1, b1, w2, b2)


def _combine_kernel(cp_ref, cn_ref, tok_ref, x1_ref, mod_ref, fw_ref, y_hbm, o_ref, yl_scr, w_scr, sems):
    j = pl.program_id(0)
    n = pl.num_programs(0)
    slot = j % 2
    tt = x1_ref.shape[0]

    def tile_copies(tile, slot_, op):
        def make(local, glob, nchunks):
            return pltpu.make_async_copy(y_hbm.at[_chunk_rows(glob, nchunks)],
                                         yl_scr.at[slot_, _chunk_rows(local, nchunks)], sems.at[slot_])
        _for_tile_copies(cp_ref, cn_ref, tile, make, op)

    def start_tile(tile, slot_):
        tile_copies(tile, slot_, lambda cp: cp.start())

    @pl.when(j == 0)
    def _():
        yl_scr[...] = jnp.zeros_like(yl_scr)
        start_tile(0, 0)

    @pl.when(j + 1 < n)
    def _():
        start_tile(j + 1, 1 - slot)

    tile_copies(j, slot, lambda cp: cp.wait())

    pos = tok_ref[:, 0:TOP_K].astype(jnp.int32)
    p = tok_ref[:, TOP_K:2 * TOP_K]
    for cb in range(MOE_RLOC // MOE_RB):
        io = lax.broadcasted_iota(jnp.int32, (tt, MOE_RB), 1) + cb * MOE_RB
        w = jnp.zeros((tt, MOE_RB), F32)
        for k in range(TOP_K):
            w = jnp.where(io == pos[:, k:k + 1], p[:, k:k + 1], w)
        w_scr[:, cb * MOE_RB:(cb + 1) * MOE_RB] = w.astype(BF16)
    moe = _dot(w_scr[...], yl_scr[slot])
    x2 = x1_ref[...] + _mod_slice(mod_ref, 0, 5) * moe
    o_ref[...] = _rms(x2, fw_ref[...])


def _combine_call(copies, ncopies, tok, x1, mod, fw, y):
    T = x1.shape[0]
    tt = MOE_TT
    row = lambda j, *_: (j, 0)
    fixed = lambda j, *_: (0, 0)
    grid_spec = pltpu.PrefetchScalarGridSpec(
        num_scalar_prefetch=2,
        grid=(T // tt,),
        in_specs=[
            pl.BlockSpec((tt, 128), row),
            pl.BlockSpec((tt, D_MODEL), row),
            pl.BlockSpec(mod.shape, fixed),
            pl.BlockSpec((1, D_MODEL), fixed),
            pl.BlockSpec(memory_space=pl.ANY),
        ],
        out_specs=pl.BlockSpec((tt, D_MODEL), row),
        scratch_shapes=[
            pltpu.VMEM((2, MOE_RLOC, D_MODEL), BF16),
            pltpu.VMEM((tt, MOE_RLOC), BF16),
            pltpu.SemaphoreType.DMA((2,)),
        ],
    )
    return pl.pallas_call(
        _combine_kernel,
        grid_spec=grid_spec,
        out_shape=jax.ShapeDtypeStruct((T, D_MODEL), F32),
        compiler_params=_params(("arbitrary",)),
        name="moe_combine",
    )(copies, ncopies, tok, x1, mod, fw, y)


def _rope_tables(T):
    rows = T // GRID_W
    n_freq = DK // 4
    inv = ROPE_BASE ** (-jnp.arange(n_freq, dtype=F32) / n_freq)
    row_ang = jnp.arange(rows, dtype=F32)[:, None] * inv
    col_ang = jnp.arange(GRID_W, dtype=F32)[:, None] * inv
    rot_row = jnp.pad(jnp.stack([jnp.cos(row_ang), jnp.sin(row_ang)]), ((0, 0), (0, 0), (0, n_freq)))
    rot_col = jnp.pad(jnp.stack([jnp.cos(col_ang), jnp.sin(col_ang)]), ((0, 0), (0, 0), (n_freq, 0)))
    return rot_row, rot_col


def kernel(x, c, ctx, c_ctx, w_mod, b_mod, norm1_w, norm2_w, w_in, sgu_ln_w, sgu_ln_b, sgu_w, sgu_b,
           ret_decay_fwd, ret_decay_bwd, w_proj_a, w_proj_b, w_out, router_w, router_b,
           moe_w1, moe_b1, moe_w2, moe_b2, final_norm_w):
    B, T, D = x.shape
    assert B == 1 and D == D_MODEL and w_mod.shape[0] == 1 and T % 1024 == 0
    x2d = x.reshape(T, D)
    cc = jnp.stack([c.reshape(D), c_ctx.reshape(D)], axis=1)
    mod = _mod_call(cc, w_mod[0], b_mod)
    dec = jnp.stack([ret_decay_fwd[0], ret_decay_bwd[0]], axis=0)
    w_in_bf = w_in[0].astype(BF16)
    s0 = _ctx_call(ctx.reshape(ctx.shape[1], D), norm1_w, mod, w_in_bf, dec)
    rot_row, rot_col = _rope_tables(T)
    proj = _inproj_call(x2d, norm1_w, mod, w_in_bf, rot_row, rot_col, sgu_ln_w, sgu_ln_b)
    za = _sgu_call(proj, sgu_w[0].astype(BF16), sgu_b[0].T, w_proj_a[0].astype(BF16))
    yf, yb = _ret_call(proj, s0, dec)
    x1, h2, logits_t = _merge_call(yf, yb, za, proj, x2d, mod, norm2_w, w_proj_b[0].astype(BF16),
                                   w_out[0].astype(BF16), router_w[0].T, router_b.reshape(N_EXPERTS, 1))
    pos_t, tok, nch = _route_call(logits_t)
    copies, ncopies, zstart, zcount, expert_tables = _routing_tables(nch[:, :, 0])
    n_slots = expert_tables[0].shape[0] * MOE_TM
    xs = _dispatch_call((copies, ncopies, zstart, zcount, expert_tables[2]), h2, pos_t, n_slots)
    y = _experts_call(expert_tables, xs, moe_w1[0], moe_b1[0][:, None, :], moe_w2[0], moe_b2[0][:, None, :])
    out = _combine_call(copies, ncopies, tok, x1, mod, final_norm_w.reshape(1, D), y)
    return out.reshape(B, T, D)
```

```python
import functools

import jax
import jax.numpy as jnp
from jax import lax
from jax.experimental import pallas as pl
from jax.experimental.pallas import tpu as pltpu

D_MODEL = 1024
GRID_W = 64
SGU_CHUNK = 128
SGU_GROUPS = 8
HEADS = 4
DK = D_MODEL // HEADS
DV = 2 * DK
ROPE_BASE = 10000.0
N_EXPERTS = 32
TOP_K = 4
D_FF = D_MODEL
SWIGLU_LIMIT = 7.0
SWIGLU_ALPHA = 1.702
EPS = 1e-6
IN_WIDTH = 12 * D_MODEL
COL_U, COL_V, COL_Q, COL_K, COL_VR, COL_GF, COL_GB, COL_GA, COL_GBR = 0, 1, 2, 3, 4, 6, 8, 10, 11

INPROJ_NB = 4
RET_CHUNK = 256
RET_SUB = 1
VMEM_LIMIT = 56 * 1024 * 1024

MOE_TT = 512
MOE_CH = 16
MOE_TM = 1024
MOE_CPT = MOE_TM // MOE_CH
MOE_PARTS = 8
MOE_MAXC = TOP_K * MOE_TT // MOE_CH + N_EXPERTS
MOE_RLOC = MOE_MAXC * MOE_CH
MOE_RB = 256
MOE_RUN = 4
MOE_MAXQ = MOE_MAXC // MOE_RUN
MOE_MAXS = (MOE_RUN - 1) * N_EXPERTS
MOE_CPROW = 2 * (MOE_MAXQ + MOE_MAXS)

F32 = jnp.float32
BF16 = jnp.bfloat16


def _params(sem):
    return pltpu.CompilerParams(dimension_semantics=sem, vmem_limit_bytes=VMEM_LIMIT)


def _dot(a, b):
    return jnp.dot(a, b, preferred_element_type=F32)


def _dot_nt(a, b):
    return lax.dot_general(a, b, (((1,), (1,)), ((), ())), preferred_element_type=F32)


def _dot_tn(a, b):
    return lax.dot_general(a, b, (((0,), (0,)), ((), ())), preferred_element_type=F32)


def _rms(x, w):
    return x * lax.rsqrt(jnp.mean(x * x, axis=-1, keepdims=True) + EPS) * w


def _gelu(x):
    return 0.5 * x * (1.0 + lax.erf(x * (2.0 ** -0.5)))


def _sigmoid(x):
    return 0.5 * jnp.tanh(0.5 * x) + 0.5


def _mod_slice(mod_ref, row, k):
    return mod_ref[row:row + 1, k * D_MODEL:(k + 1) * D_MODEL]


def _mod_kernel(cc_ref, w_ref, b_ref, o_ref):
    s = cc_ref[...]
    s = s * jax.nn.sigmoid(s)
    w = w_ref[...]
    r0 = jnp.sum(s[:, 0:1] * w, axis=0, keepdims=True)
    r1 = jnp.sum(s[:, 1:2] * w, axis=0, keepdims=True)
    o_ref[...] = jnp.concatenate([r0, r1], axis=0) + b_ref[...]


def _mod_call(cc, w_mod, b_mod):
    bn = 1536
    n = w_mod.shape[1]
    return pl.pallas_call(
        _mod_kernel,
        grid=(n // bn,),
        in_specs=[
            pl.BlockSpec((D_MODEL, 2), lambda j: (0, 0)),
            pl.BlockSpec((D_MODEL, bn), lambda j: (0, j)),
            pl.BlockSpec((1, bn), lambda j: (0, j)),
        ],
        out_specs=pl.BlockSpec((2, bn), lambda j: (0, j)),
        out_shape=jax.ShapeDtypeStruct((2, n), F32),
        compiler_params=_params(("arbitrary",)),
        name="mod_vectors",
    )(cc, w_mod, b_mod)


def _ctx_kernel(ctx_ref, nw_ref, mod_ref, w_ref, dec_ref, s0_ref):
    L = ctx_ref.shape[0]
    sh = _mod_slice(mod_ref, 1, 0)
    sc = _mod_slice(mod_ref, 1, 1)
    hc = (_rms(ctx_ref[...], nw_ref[...]) * (1.0 + sc) + sh).astype(BF16)
    kv = _dot(hc, w_ref[...])
    lg = jnp.log1p(-jnp.exp2(dec_ref[...]))
    pos = lax.broadcasted_iota(jnp.int32, (L, 1), 0).astype(F32)
    for h in range(HEADS):
        k = kv[:, h * DK:(h + 1) * DK] * (DK ** -0.5)
        v = kv[:, HEADS * DK + h * DV:HEADS * DK + (h + 1) * DV].astype(BF16)
        wf = jnp.exp((L - 1.0 - pos) * lg[0:1, h:h + 1])
        wb = jnp.exp(pos * lg[1:2, h:h + 1])
        s0_ref[0, h] = _dot_tn((k * wf).astype(BF16), v)
        s0_ref[1, h] = _dot_tn((k * wb).astype(BF16), v)


def _ctx_call(ctx2d, nw, mod, w_in_bf, dec):
    L = ctx2d.shape[0]
    wcols = HEADS * DK + HEADS * DV
    return pl.pallas_call(
        _ctx_kernel,
        grid=(1,),
        in_specs=[
            pl.BlockSpec((L, D_MODEL), lambda i: (0, 0)),
            pl.BlockSpec((1, D_MODEL), lambda i: (0, 0)),
            pl.BlockSpec(mod.shape, lambda i: (0, 0)),
            pl.BlockSpec((D_MODEL, wcols), lambda i: (0, COL_K * D_MODEL // wcols)),
            pl.BlockSpec(dec.shape, lambda i: (0, 0)),
        ],
        out_specs=pl.BlockSpec((2, HEADS, DK, DV), lambda i: (0, 0, 0, 0)),
        out_shape=jax.ShapeDtypeStruct((2, HEADS, DK, DV), F32),
        compiler_params=_params(("arbitrary",)),
        name="ctx_states",
    )(ctx2d, nw, mod, w_in_bf, dec)


def _inproj_kernel(x_ref, nw_ref, mod_ref, w_ref, rot_row_ref, rot_col_ref, lnw_ref, lnb_ref, o_ref,
                   h_scr, g_scr, cos_ref, sin_ref):
    j = pl.program_id(1)
    tm = h_scr.shape[0]

    @pl.when(j == 0)
    def _():
        sh = _mod_slice(mod_ref, 0, 0)
        sc = _mod_slice(mod_ref, 0, 1)
        h_scr[...] = (_rms(x_ref[...], nw_ref[...]) * (1.0 + sc) + sh).astype(BF16)

    def pieces(emit, span, rsplit=1):
        for c0 in range(span.start, span.stop, DK):
            cols = slice(c0, c0 + DK)
            for r in range(rsplit):
                rows = slice(r * tm // rsplit, (r + 1) * tm // rsplit)
                emit(rows, cols, _dot(h_scr[rows, :], w_ref[:, cols]))

    def store(fn):
        def emit(rows, cols, acc):
            o_ref[rows, cols] = fn(acc).astype(BF16)
        return emit

    def rope(scale):
        half = DK // 2

        def emit(rows, cols, acc):
            cos = cos_ref[rows, :]
            sin = sin_ref[rows, :]
            x1 = acc[:, :half]
            x2 = acc[:, half:]
            o_ref[rows, cols.start:cols.start + half] = ((x1 * cos - x2 * sin) * scale).astype(BF16)
            o_ref[rows, cols.start + half:cols.stop] = ((x1 * sin + x2 * cos) * scale).astype(BF16)
        return emit

    def gelu_layer_norm(span):
        def emit(rows, cols, acc):
            g_scr[rows, cols.start - span.start:cols.stop - span.start] = _gelu(acc)
        pieces(emit, span)
        g = g_scr[...]
        mu = jnp.mean(g, axis=-1, keepdims=True)
        d = g - mu
        var = jnp.mean(d * d, axis=-1, keepdims=True)
        o_ref[:, span] = (d * lax.rsqrt(var + EPS) * lnw_ref[...] + lnb_ref[...]).astype(BF16)

    def rotary_tables():
        for r in range(tm // GRID_W):
            tok = slice(r * GRID_W, (r + 1) * GRID_W)
            cos_ref[tok, :] = rot_row_ref[0, r:r + 1, :] + rot_col_ref[0]
            sin_ref[tok, :] = rot_row_ref[1, r:r + 1, :] + rot_col_ref[1]

    def column_block(block, span):
        if block == COL_U:
            pieces(store(_gelu), span)
        elif block == COL_V:
            gelu_layer_norm(span)
        elif block == COL_Q:
            rotary_tables()
            pieces(rope(1.0), span)
        elif block == COL_K:
            pieces(rope(DK ** -0.5), span)
        elif block < COL_GF:
            pieces(store(lambda a: a), span)
        elif block < COL_GA:
            pieces(store(lambda a: a * _sigmoid(a)), span, rsplit=2)
        else:
            pieces(store(_sigmoid), span, rsplit=2)

    for step in range(IN_WIDTH // D_MODEL // INPROJ_NB):
        @pl.when(j == step)
        def _():
            for b in range(INPROJ_NB):
                column_block(step * INPROJ_NB + b, slice(b * D_MODEL, (b + 1) * D_MODEL))


def _inproj_call(x2d, nw, mod, w_in_bf, rot_row, rot_col, lnw, lnb):
    T = x2d.shape[0]
    tm = 1024
    half = DK // 2
    return pl.pallas_call(
        _inproj_kernel,
        grid=(T // tm, IN_WIDTH // (INPROJ_NB * D_MODEL)),
        in_specs=[
            pl.BlockSpec((tm, D_MODEL), lambda i, j: (i, 0)),
            pl.BlockSpec((1, D_MODEL), lambda i, j: (0, 0)),
            pl.BlockSpec(mod.shape, lambda i, j: (0, 0)),
            pl.BlockSpec((D_MODEL, INPROJ_NB * D_MODEL), lambda i, j: (0, j)),
            pl.BlockSpec((2, tm // GRID_W, half), lambda i, j: (0, i, 0)),
            pl.BlockSpec((2, GRID_W, half), lambda i, j: (0, 0, 0)),
            pl.BlockSpec((1, D_MODEL), lambda i, j: (0, 0)),
            pl.BlockSpec((1, D_MODEL), lambda i, j: (0, 0)),
        ],
        out_specs=pl.BlockSpec((tm, INPROJ_NB * D_MODEL), lambda i, j: (i, j)),
        out_shape=jax.ShapeDtypeStruct((T, IN_WIDTH), BF16),
        scratch_shapes=[
            pltpu.VMEM((tm, D_MODEL), BF16),
            pltpu.VMEM((tm, D_MODEL), F32),
            pltpu.VMEM((tm, half), F32),
            pltpu.VMEM((tm, half), F32),
        ],
        compiler_params=_params(("arbitrary", "arbitrary")),
        name="in_proj",
    )(x2d, nw, mod, w_in_bf, rot_row, rot_col, lnw, lnb)


def _sgu_kernel(u_ref, v_ref, ga_ref, ws_ref, bt_ref, wpa_ref, o_ref, ya_scr):
    tm = u_ref.shape[0]
    gd = D_MODEL // SGU_GROUPS
    for n in range(tm // SGU_CHUNK):
        rows = slice(n * SGU_CHUNK, (n + 1) * SGU_CHUNK)
        for g in range(SGU_GROUPS):
            cols = slice(g * gd, (g + 1) * gd)
            mixed = _dot(ws_ref[g], v_ref[rows, cols]) + bt_ref[:, g:g + 1]
            ya_scr[rows, cols] = (u_ref[rows, cols].astype(F32) * mixed).astype(BF16)
    pa = _dot(ya_scr[...], wpa_ref[...])
    o_ref[...] = (ga_ref[...].astype(F32) * pa).astype(BF16)


def _sgu_call(proj, ws_bf, bt, wpa_bf):
    T = proj.shape[0]
    tm = 1024
    return pl.pallas_call(
        _sgu_kernel,
        grid=(T // tm,),
        in_specs=[
            pl.BlockSpec((tm, D_MODEL), lambda i: (i, COL_U)),
            pl.BlockSpec((tm, D_MODEL), lambda i: (i, COL_V)),
            pl.BlockSpec((tm, D_MODEL), lambda i: (i, COL_GA)),
            pl.BlockSpec(ws_bf.shape, lambda i: (0, 0, 0)),
            pl.BlockSpec(bt.shape, lambda i: (0, 0)),
            pl.BlockSpec((D_MODEL, D_MODEL), lambda i: (0, 0)),
        ],
        out_specs=pl.BlockSpec((tm, D_MODEL), lambda i: (i, 0)),
        out_shape=jax.ShapeDtypeStruct((T, D_MODEL), BF16),
        scratch_shapes=[pltpu.VMEM((tm, D_MODEL), BF16)],
        compiler_params=_params(("arbitrary",)),
        name="sgu_proj_a",
    )(proj, proj, proj, ws_bf, bt, wpa_bf)


def _ret_kernel(qf_ref, kf_ref, vf_ref, gf_ref, qb_ref, kb_ref, vb_ref, gb_ref, s0_ref, dec_ref,
                yf_ref, yb_ref, s_scr, intra_scr, qd_scr, kd_scr, cd_scr):
    C = RET_CHUNK
    i = pl.program_id(0)

    @pl.when(i == 0)
    def _():
        lg = jnp.log1p(-jnp.exp2(dec_ref[...]))
        r = lax.broadcasted_iota(jnp.int32, (C, C), 0).astype(F32)
        c = lax.broadcasted_iota(jnp.int32, (C, C), 1).astype(F32)
        pos = lax.broadcasted_iota(jnp.int32, (C, 1), 0).astype(F32)
        for h in range(HEADS):
            lf = lg[0:1, h:h + 1]
            lb = lg[1:2, h:h + 1]
            intra_scr[0, h] = jnp.where(r >= c, jnp.exp(jnp.maximum(r - c, 0.0) * lf), 0.0)
            intra_scr[1, h] = jnp.where(c >= r, jnp.exp(jnp.maximum(c - r, 0.0) * lb), 0.0)
            qd_scr[0, h] = jnp.exp((pos + 1.0) * lf)
            qd_scr[1, h] = jnp.exp((C - pos) * lb)
            kd_scr[0, h] = jnp.exp((C - 1.0 - pos) * lf)
            kd_scr[1, h] = jnp.exp(pos * lb)
            cd_scr[0, h] = jnp.exp(C * lf)
            cd_scr[1, h] = jnp.exp(C * lb)
        s_scr[...] = s0_ref[...]

    dirs = ((qf_ref, kf_ref, vf_ref, gf_ref, yf_ref), (qb_ref, kb_ref, vb_ref, gb_ref, yb_ref))
    nsub = qf_ref.shape[0] // C
    for sub in range(nsub):
        for d, (q_ref, k_ref, v_ref, g_ref, y_ref) in enumerate(dirs):
            c0 = (sub if d == 0 else nsub - 1 - sub) * C
            rows = slice(c0, c0 + C)
            for h in range(HEADS):
                q = q_ref[rows, h * DK:(h + 1) * DK]
                k = k_ref[rows, h * DK:(h + 1) * DK]
                v = v_ref[rows, h * DV:(h + 1) * DV]
                s = s_scr[d, h]
                a = (_dot_nt(q, k) * intra_scr[d, h]).astype(BF16)
                o = _dot(a, v) + _dot(q, s.astype(BF16)) * qd_scr[d, h]
                hn = o * lax.rsqrt(jnp.mean(o * o, axis=-1, keepdims=True) + EPS)
                y_ref[rows, h * DV:(h + 1) * DV] = (g_ref[rows, h * DV:(h + 1) * DV].astype(F32) * hn).astype(BF16)
                kd = (k.astype(F32) * kd_scr[d, h]).astype(BF16)
                s_scr[d, h] = s * cd_scr[d, h] + _dot_tn(kd, v)


def _ret_call(proj, s0, dec):
    T = proj.shape[0]
    C = RET_CHUNK
    rows = RET_SUB * C
    n = T // rows
    qk_w = HEADS * DK
    v_w = HEADS * DV
    fwd = lambda col: (lambda i: (i, col))
    bwd = lambda col: (lambda i: (n - 1 - i, col))
    return pl.pallas_call(
        _ret_kernel,
        grid=(n,),
        in_specs=[
            pl.BlockSpec((rows, qk_w), fwd(COL_Q)),
            pl.BlockSpec((rows, qk_w), fwd(COL_K)),
            pl.BlockSpec((rows, v_w), fwd(COL_VR // 2)),
            pl.BlockSpec((rows, v_w), fwd(COL_GF // 2)),
            pl.BlockSpec((rows, qk_w), bwd(COL_Q)),
            pl.BlockSpec((rows, qk_w), bwd(COL_K)),
            pl.BlockSpec((rows, v_w), bwd(COL_VR // 2)),
            pl.BlockSpec((rows, v_w), bwd(COL_GB // 2)),
            pl.BlockSpec(s0.shape, lambda i: (0, 0, 0, 0)),
            pl.BlockSpec(dec.shape, lambda i: (0, 0)),
        ],
        out_specs=[
            pl.BlockSpec((rows, v_w), lambda i: (i, 0)),
            pl.BlockSpec((rows, v_w), lambda i: (n - 1 - i, 0)),
        ],
        out_shape=[jax.ShapeDtypeStruct((T, v_w), BF16), jax.ShapeDtypeStruct((T, v_w), BF16)],
        scratch_shapes=[
            pltpu.VMEM((2, HEADS, DK, DV), F32),
            pltpu.VMEM((2, HEADS, C, C), F32),
            pltpu.VMEM((2, HEADS, C, 1), F32),
            pltpu.VMEM((2, HEADS, C, 1), F32),
            pltpu.VMEM((2, HEADS, 1, 1), F32),
        ],
        compiler_params=_params(("arbitrary",)),
        name="retention",
    )(proj, proj, proj, proj, proj, proj, proj, proj, s0, dec)


def _merge_kernel(yf_ref, yb_ref, za_ref, gbr_ref, x_ref, mod_ref, nw_ref, wpb_ref, wo_ref, rw_ref, rb_ref,
                  x1_ref, h2_ref, lg_ref):
    yb = yf_ref[...] + yb_ref[...]
    pb = _dot(yb, wpb_ref[...])
    y = za_ref[...].astype(F32) + gbr_ref[...].astype(F32) * pb
    yo = _dot(y.astype(BF16), wo_ref[...])
    x1 = x_ref[...] + _mod_slice(mod_ref, 0, 2) * yo
    x1_ref[...] = x1
    h2 = _rms(x1, nw_ref[...]) * (1.0 + _mod_slice(mod_ref, 0, 4)) + _mod_slice(mod_ref, 0, 3)
    h2_hi = h2.astype(BF16)
    h2_ref[...] = h2_hi
    h2_lo = (h2 - h2_hi.astype(F32)).astype(BF16)
    rw = rw_ref[...]
    rw_hi = rw.astype(BF16)
    rw_lo = (rw - rw_hi.astype(F32)).astype(BF16)
    both = _dot_nt(jnp.concatenate([rw_hi, rw_lo], axis=0), h2_hi)
    lg_ref[...] = both[:N_EXPERTS] + both[N_EXPERTS:] + _dot_nt(rw_hi, h2_lo) + rb_ref[...]


def _merge_call(yf, yb, za, proj, x2d, mod, nw2, wpb_bf, wo_bf, rw, rb):
    T = x2d.shape[0]
    tm = 512
    row = lambda i: (i, 0)
    fixed = lambda i: (0, 0)
    return pl.pallas_call(
        _merge_kernel,
        grid=(T // tm,),
        in_specs=[
            pl.BlockSpec((tm, HEADS * DV), row),
            pl.BlockSpec((tm, HEADS * DV), row),
            pl.BlockSpec((tm, D_MODEL), row),
            pl.BlockSpec((tm, D_MODEL), lambda i: (i, COL_GBR)),
            pl.BlockSpec((tm, D_MODEL), row),
            pl.BlockSpec(mod.shape, fixed),
            pl.BlockSpec((1, D_MODEL), fixed),
            pl.BlockSpec(wpb_bf.shape, fixed),
            pl.BlockSpec(wo_bf.shape, fixed),
            pl.BlockSpec(rw.shape, fixed),
            pl.BlockSpec(rb.shape, fixed),
        ],
        out_specs=[
            pl.BlockSpec((tm, D_MODEL), row),
            pl.BlockSpec((tm, D_MODEL), row),
            pl.BlockSpec((N_EXPERTS, tm), lambda i: (0, i)),
        ],
        out_shape=[
            jax.ShapeDtypeStruct((T, D_MODEL), F32),
            jax.ShapeDtypeStruct((T, D_MODEL), BF16),
            jax.ShapeDtypeStruct((N_EXPERTS, T), F32),
        ],
        compiler_params=_params(("arbitrary",)),
        name="merge_router",
    )(yf, yb, za, proj, x2d, mod, nw2, wpb_bf, wo_bf, rw, rb)


def _route_kernel(lg_ref, pos_ref, tok_ref, nch_ref, su_scr):
    E, tt = lg_ref.shape

    @pl.when(pl.program_id(0) == 0)
    def _():
        r = lax.broadcasted_iota(jnp.int32, (tt, tt), 0)
        c = lax.broadcasted_iota(jnp.int32, (tt, tt), 1)
        su_scr[...] = jnp.where(r < c, 1.0, 0.0).astype(BF16)

    sub = lax.broadcasted_iota(jnp.int32, (E, tt), 0)
    work = lg_ref[...]
    vals, hots = [], []
    for _ in range(TOP_K):
        m = jnp.max(work, axis=0, keepdims=True)
        first = jnp.min(jnp.where(work == m, sub, E), axis=0, keepdims=True)
        hot = sub == first
        vals.append(m)
        hots.append(hot)
        work = jnp.where(hot, -jnp.inf, work)
    exps = [jnp.exp(v - vals[0]) for v in vals]
    inv = 1.0 / functools.reduce(lambda a, b: a + b, exps)

    member = functools.reduce(lambda a, b: a + b, [jnp.where(h, 1.0, 0.0) for h in hots])
    cnt = jnp.sum(member, axis=1, keepdims=True)
    nch = jnp.floor((cnt + (MOE_CH - 1.0)) * (1.0 / MOE_CH))
    nch_b = jnp.broadcast_to(nch, (E, 128))
    rank = _dot(member.astype(BF16), su_scr[...])
    er = lax.broadcasted_iota(jnp.int32, (E, E), 0)
    ec = lax.broadcasted_iota(jnp.int32, (E, E), 1)
    first_chunk = _dot(jnp.where(ec < er, 1.0, 0.0).astype(BF16), nch_b.astype(BF16))[:, 0:1]
    base = first_chunk * MOE_CH + rank
    pos = [jnp.sum(jnp.where(hots[k], base, 0.0), axis=0, keepdims=True) for k in range(TOP_K)]
    for k in range(TOP_K):
        pos_ref[k:k + 1, :] = pos[k].astype(jnp.int32)
    rows = jnp.concatenate(pos + [e * inv for e in exps] + [jnp.zeros((128 - 2 * TOP_K, tt), F32)], axis=0)
    tok_ref[...] = rows.T
    nch_ref[0] = nch_b.astype(jnp.int32)


def _route_call(logits_t):
    E, T = logits_t.shape
    tt = MOE_TT
    return pl.pallas_call(
        _route_kernel,
        grid=(T // tt,),
        in_specs=[pl.BlockSpec((E, tt), lambda j: (0, j))],
        out_specs=[
            pl.BlockSpec((TOP_K, tt), lambda j: (0, j)),
            pl.BlockSpec((tt, 128), lambda j: (j, 0)),
            pl.BlockSpec((1, E, 128), lambda j: (j, 0, 0)),
        ],
        out_shape=[
            jax.ShapeDtypeStruct((TOP_K, T), jnp.int32),
            jax.ShapeDtypeStruct((T, 128), F32),
            jax.ShapeDtypeStruct((T // tt, E, 128), jnp.int32),
        ],
        scratch_shapes=[pltpu.VMEM((tt, tt), BF16)],
        compiler_params=_params(("arbitrary",)),
        name="moe_route",
    )(logits_t)


def _slot_tiles_max(n_tiles):
    return -(-(n_tiles * MOE_MAXC + N_EXPERTS * (MOE_CPT - 1)) // MOE_CPT)


def _routing_tables(nch):
    n_tiles, E = nch.shape

    def cumsum_last(a):
        m = a.shape[-1]
        keep = jnp.arange(m)[:, None] <= jnp.arange(m)[None, :]
        return jnp.sum(jnp.where(keep, a[..., :, None], 0), axis=-2)

    tot = jnp.sum(nch, axis=0)
    seg = (tot + MOE_CPT - 1) // MOE_CPT * MOE_CPT
    seg_end = cumsum_last(seg)
    seg_start = seg_end - seg
    gstart = seg_start[None, :] + cumsum_last(nch.T).T - nch
    l_end = cumsum_last(nch)
    lstart = l_end - nch

    def copy_list(count, first_local, first_global, step, max_len):
        end = cumsum_last(count)
        start = end - count
        i = jnp.arange(max_len, dtype=jnp.int32)[None, :, None]
        owner = (start[:, None, :] <= i) & (i < end[:, None, :])
        off = (i - start[:, None, :]) * step
        src = jnp.sum(jnp.where(owner, first_local[:, None, :] + off, 0), axis=-1)
        dst = jnp.sum(jnp.where(owner, first_global[:, None, :] + off, 0), axis=-1)
        return src, dst, end[:, -1]

    runs = nch // MOE_RUN
    rsrc, rdst, nruns = copy_list(runs, lstart, gstart, MOE_RUN, MOE_MAXQ)
    ssrc, sdst, nsingles = copy_list(nch - runs * MOE_RUN, lstart + runs * MOE_RUN, gstart + runs * MOE_RUN, 1,
                                     MOE_MAXS)
    copies = jnp.concatenate([rsrc, rdst, ssrc, sdst], axis=1).astype(jnp.int32).reshape(-1)
    ncopies = jnp.stack([nruns, nsingles], axis=1).astype(jnp.int32).reshape(-1)
    nt_max = _slot_tiles_max(n_tiles)
    n_used = (seg_end[-1] // MOE_CPT).astype(jnp.int32)
    tile_blk = jnp.minimum(jnp.arange(nt_max, dtype=jnp.int32), n_used - 1)
    tile_exp = jnp.sum(seg_end[None, :] <= (tile_blk * MOE_CPT)[:, None], axis=1)
    tile_exp = jnp.minimum(tile_exp, E - 1).astype(jnp.int32)
    experts = jnp.arange(E, dtype=jnp.int32)
    own = tile_exp[:, None] == experts[None, :]
    end_tile = jnp.sum(jnp.where(own, (seg_end // MOE_CPT)[None, :], 0), axis=1)
    follows = ((seg_start // MOE_CPT)[None, :] == end_tile[:, None]) & (seg > 0)[None, :]
    next_exp = jnp.where(jnp.any(follows, axis=1), jnp.sum(jnp.where(follows, experts[None, :], 0), axis=1), -1)
    next_exp = next_exp.astype(jnp.int32)
    zstart = (seg_start + tot).astype(jnp.int32)
    zcount = (seg - tot).astype(jnp.int32)
    first_tile = jnp.sum(jnp.where(own, (seg_start // MOE_CPT)[None, :], 0), axis=1)
    own_chunks = jnp.sum(jnp.where(own, tot[None, :], 0), axis=1)
    filled = own_chunks - (jnp.arange(nt_max, dtype=jnp.int32) - first_tile) * MOE_CPT
    per_part = MOE_CPT // MOE_PARTS
    parts = jnp.clip((filled + per_part - 1) // per_part, 1, MOE_PARTS).astype(jnp.int32)
    expert_tables = (tile_exp, tile_blk, n_used.reshape(1), next_exp, parts)
    return copies, ncopies, zstart, zcount, expert_tables


def _chunk_rows(c, nchunks=1):
    return pl.ds(pl.multiple_of(c * MOE_CH, MOE_CH), nchunks * MOE_CH)


def _for_tile_copies(cp_ref, cn_ref, tile, make, op):
    base = tile * MOE_CPROW

    def run(i, carry):
        op(make(cp_ref[base + i], cp_ref[base + MOE_MAXQ + i], MOE_RUN))
        return carry
    lax.fori_loop(0, cn_ref[2 * tile], run, 0)

    def single(i, carry):
        op(make(cp_ref[base + 2 * MOE_MAXQ + i], cp_ref[base + 2 * MOE_MAXQ + MOE_MAXS + i], 1))
        return carry
    lax.fori_loop(0, cn_ref[2 * tile + 1], single, 0)


def _dispatch_kernel(cp_ref, cn_ref, zstart_ref, zcount_ref, nu_ref, h2_ref, pos_ref, xs_hbm, xl_scr, zero_scr,
                     sems, zsem):
    j = pl.program_id(0)
    n = pl.num_programs(0)
    slot = j % 2
    tt = h2_ref.shape[0]
    nt_max = xs_hbm.shape[0] // MOE_TM

    def tile_copies(tile, slot_, op):
        def make(local, glob, nchunks):
            return pltpu.make_async_copy(xl_scr.at[slot_, _chunk_rows(local, nchunks)],
                                         xs_hbm.at[_chunk_rows(glob, nchunks)], sems.at[slot_])
        _for_tile_copies(cp_ref, cn_ref, tile, make, op)

    def zero_chunk_copy(d):
        return pltpu.make_async_copy(zero_scr.at[pl.ds(0, MOE_CH)], xs_hbm.at[_chunk_rows(d)], zsem.at[0])

    def zero_tile_copy(t):
        rows = pl.ds(pl.multiple_of(t * MOE_TM, MOE_TM), MOE_TM)
        return pltpu.make_async_copy(zero_scr, xs_hbm.at[rows], zsem.at[0])

    def zero_fill(op):
        def expert(e, carry):
            def chunk(r, carry_):
                op(zero_chunk_copy(zstart_ref[e] + r))
                return carry_
            return lax.fori_loop(0, zcount_ref[e], chunk, carry)
        lax.fori_loop(0, N_EXPERTS, expert, 0)

        def tile(t, carry):
            op(zero_tile_copy(t))
            return carry
        lax.fori_loop(nu_ref[0], nt_max, tile, 0)

    def wait_tile(tile, slot_):
        tile_copies(tile, slot_, lambda cp: cp.wait())

    @pl.when(j == 0)
    def _():
        zero_scr[...] = jnp.zeros_like(zero_scr)
        zero_fill(lambda cp: cp.start())

    @pl.when(j >= 2)
    def _():
        wait_tile(j - 2, slot)

    pos = pos_ref[...]
    h2 = h2_ref[...]
    for rb in range(MOE_RLOC // MOE_RB):
        io = lax.broadcasted_iota(jnp.int32, (MOE_RB, tt), 0) + rb * MOE_RB
        onehot = jnp.zeros((MOE_RB, tt), F32)
        for k in range(TOP_K):
            onehot = jnp.where(io == pos[k:k + 1, :], 1.0, onehot)
        xl_scr[slot, rb * MOE_RB:(rb + 1) * MOE_RB, :] = _dot(onehot.astype(BF16), h2).astype(BF16)

    tile_copies(j, slot, lambda cp: cp.start())

    @pl.when(j == n - 1)
    def _():
        @pl.when(j >= 1)
        def _():
            wait_tile(j - 1, 1 - slot)
        wait_tile(j, slot)
        zero_fill(lambda cp: cp.wait())


def _dispatch_call(tables, h2, pos_t, n_slots):
    T = h2.shape[0]
    tt = MOE_TT
    copies, ncopies, zstart, zcount, n_used = tables
    grid_spec = pltpu.PrefetchScalarGridSpec(
        num_scalar_prefetch=5,
        grid=(T // tt,),
        in_specs=[
            pl.BlockSpec((tt, D_MODEL), lambda j, *_: (j, 0)),
            pl.BlockSpec((TOP_K, tt), lambda j, *_: (0, j)),
        ],
        out_specs=pl.BlockSpec(memory_space=pl.ANY),
        scratch_shapes=[
            pltpu.VMEM((2, MOE_RLOC, D_MODEL), BF16),
            pltpu.VMEM((MOE_TM, D_MODEL), BF16),
            pltpu.SemaphoreType.DMA((2,)),
            pltpu.SemaphoreType.DMA((1,)),
        ],
    )
    return pl.pallas_call(
        _dispatch_kernel,
        grid_spec=grid_spec,
        out_shape=jax.ShapeDtypeStruct((n_slots, D_MODEL), BF16),
        compiler_params=_params(("arbitrary",)),
        name="moe_dispatch",
    )(copies, ncopies, zstart, zcount, n_used, h2, pos_t)


def _experts_kernel(te_ref, tb_ref, nu_ref, nx_ref, parts_ref, xs_ref, w1_hbm, b1_ref, w2_hbm, b2_ref, y_ref,
                    w1_f32, w2_f32, w1_scr, w2_scr, sems):
    i = pl.program_id(0)
    valid = i < nu_ref[0]
    new_expert = (i == 0) | (te_ref[i] != te_ref[jnp.maximum(i - 1, 0)])

    def fetch(e):
        return (pltpu.make_async_copy(w1_hbm.at[e], w1_f32, sems.at[0]),
                pltpu.make_async_copy(w2_hbm.at[e], w2_f32, sems.at[1]))

    @pl.when(i == 0)
    def _():
        for cp in fetch(te_ref[0]):
            cp.start()

    @pl.when(valid & new_expert)
    def _():
        for cp in fetch(te_ref[i]):
            cp.wait()
        w1_scr[...] = w1_f32[...].astype(BF16)
        w2_scr[...] = w2_f32[...].astype(BF16)

        @pl.when(nx_ref[i] >= 0)
        def _():
            for cp in fetch(nx_ref[i]):
                cp.start()

    def mlp(rows):
        hh = _dot(xs_ref[rows, :], w1_scr[...]) + b1_ref[...]
        gate = jnp.minimum(hh[:, :D_FF], SWIGLU_LIMIT)
        up = jnp.clip(hh[:, D_FF:], -SWIGLU_LIMIT, SWIGLU_LIMIT)
        act = (up + 1.0) * gate * _sigmoid(SWIGLU_ALPHA * gate)
        y_ref[rows, :] = (_dot(act.astype(BF16), w2_scr[...]) + b2_ref[...]).astype(BF16)

    for nparts in range(1, MOE_PARTS + 1):
        @pl.when(valid & (parts_ref[i] == nparts))
        def _():
            rows = nparts * (MOE_TM // MOE_PARTS)
            mlp(slice(0, rows))
            if rows < MOE_TM:
                y_ref[rows:, :] = jnp.zeros((MOE_TM - rows, D_MODEL), BF16)


def _experts_call(tables, xs, w1, b1, w2, b2):
    tile_exp = tables[0]
    nt_max = tile_exp.shape[0]
    slot_tile = lambda i, te, tb, *_: (tb[i], 0)
    per_e = lambda i, te, *_: (te[i], 0, 0)
    grid_spec = pltpu.PrefetchScalarGridSpec(
        num_scalar_prefetch=len(tables),
        grid=(nt_max,),
        in_specs=[
            pl.BlockSpec((MOE_TM, D_MODEL), slot_tile),
            pl.BlockSpec(memory_space=pl.ANY),
            pl.BlockSpec((None, 1, 2 * D_FF), per_e),
            pl.BlockSpec(memory_space=pl.ANY),
            pl.BlockSpec((None, 1, D_MODEL), per_e),
        ],
        out_specs=pl.BlockSpec((MOE_TM, D_MODEL), slot_tile),
        scratch_shapes=[
            pltpu.VMEM((D_MODEL, 2 * D_FF), F32),
            pltpu.VMEM((D_FF, D_MODEL), F32),
            pltpu.VMEM((D_MODEL, 2 * D_FF), BF16),
            pltpu.VMEM((D_FF, D_MODEL), BF16),
            pltpu.SemaphoreType.DMA((2,)),
        ],
    )
    return pl.pallas_call(
        _experts_kernel,
        grid_spec=grid_spec,
        out_shape=jax.ShapeDtypeStruct(xs.shape, BF16),
        input_output_aliases={len(tables): 0},
        compiler_params=_params(("arbitrary",)),
        name="moe_experts",
    )(*tables, xs, w1, b1, w2, b2)


def _combine_kernel(cp_ref, cn_ref, tok_ref, x1_ref, mod_ref, fw_ref, y_hbm, o_ref, yl_scr, w_scr, sems):
    j = pl.program_id(0)
    n = pl.num_programs(0)
    slot = j % 2
    tt = x1_ref.shape[0]

    def tile_copies(tile, slot_, op):
        def make(local, glob, nchunks):
            return pltpu.make_async_copy(y_hbm.at[_chunk_rows(glob, nchunks)],
                                         yl_scr.at[slot_, _chunk_rows(local, nchunks)], sems.at[slot_])
        _for_tile_copies(cp_ref, cn_ref, tile, make, op)

    def start_tile(tile, slot_):
        tile_copies(tile, slot_, lambda cp: cp.start())

    @pl.when(j == 0)
    def _():
        yl_scr[...] = jnp.zeros_like(yl_scr)
        start_tile(0, 0)

    @pl.when(j + 1 < n)
    def _():
        start_tile(j + 1, 1 - slot)

    tile_copies(j, slot, lambda cp: cp.wait())

    pos = tok_ref[:, 0:TOP_K].astype(jnp.int32)
    p = tok_ref[:, TOP_K:2 * TOP_K]
    for cb in range(MOE_RLOC // MOE_RB):
        io = lax.broadcasted_iota(jnp.int32, (tt, MOE_RB), 1) + cb * MOE_RB
        w = jnp.zeros((tt, MOE_RB), F32)
        for k in range(TOP_K):
            w = jnp.where(io == pos[:, k:k + 1], p[:, k:k + 1], w)
        w_scr[:, cb * MOE_RB:(cb + 1) * MOE_RB] = w.astype(BF16)
    moe = _dot(w_scr[...], yl_scr[slot])
    x2 = x1_ref[...] + _mod_slice(mod_ref, 0, 5) * moe
    o_ref[...] = _rms(x2, fw_ref[...])


def _combine_call(copies, ncopies, tok, x1, mod, fw, y):
    T = x1.shape[0]
    tt = MOE_TT
    row = lambda j, *_: (j, 0)
    fixed = lambda j, *_: (0, 0)
    grid_spec = pltpu.PrefetchScalarGridSpec(
        num_scalar_prefetch=2,
        grid=(T // tt,),
        in_specs=[
            pl.BlockSpec((tt, 128), row),
            pl.BlockSpec((tt, D_MODEL), row),
            pl.BlockSpec(mod.shape, fixed),
            pl.BlockSpec((1, D_MODEL), fixed),
            pl.BlockSpec(memory_space=pl.ANY),
        ],
        out_specs=pl.BlockSpec((tt, D_MODEL), row),
        scratch_shapes=[
            pltpu.VMEM((2, MOE_RLOC, D_MODEL), BF16),
            pltpu.VMEM((tt, MOE_RLOC), BF16),
            pltpu.SemaphoreType.DMA((2,)),
        ],
    )
    return pl.pallas_call(
        _combine_kernel,
        grid_spec=grid_spec,
        out_shape=jax.ShapeDtypeStruct((T, D_MODEL), F32),
        compiler_params=_params(("arbitrary",)),
        name="moe_combine",
    )(copies, ncopies, tok, x1, mod, fw, y)


def _rope_tables(T):
    rows = T // GRID_W
    n_freq = DK // 4
    inv = ROPE_BASE ** (-jnp.arange(n_freq, dtype=F32) / n_freq)
    row_ang = jnp.arange(rows, dtype=F32)[:, None] * inv
    col_ang = jnp.arange(GRID_W, dtype=F32)[:, None] * inv
    rot_row = jnp.pad(jnp.stack([jnp.cos(row_ang), jnp.sin(row_ang)]), ((0, 0), (0, 0), (0, n_freq)))
    rot_col = jnp.pad(jnp.stack([jnp.cos(col_ang), jnp.sin(col_ang)]), ((0, 0), (0, 0), (n_freq, 0)))
    return rot_row, rot_col


def kernel(x, c, ctx, c_ctx, w_mod, b_mod, norm1_w, norm2_w, w_in, sgu_ln_w, sgu_ln_b, sgu_w, sgu_b,
           ret_decay_fwd, ret_decay_bwd, w_proj_a, w_proj_b, w_out, router_w, router_b,
           moe_w1, moe_b1, moe_w2, moe_b2, final_norm_w):
    B, T, D = x.shape
    assert B == 1 and D == D_MODEL and w_mod.shape[0] == 1 and T % 1024 == 0
    x2d = x.reshape(T, D)
    cc = jnp.stack([c.reshape(D), c_ctx.reshape(D)], axis=1)
    mod = _mod_call(cc, w_mod[0], b_mod)
    dec = jnp.stack([ret_decay_fwd[0], ret_decay_bwd[0]], axis=0)
    w_in_bf = w_in[0].astype(BF16)
    s0 = _ctx_call(ctx.reshape(ctx.shape[1], D), norm1_w, mod, w_in_bf, dec)
    rot_row, rot_col = _rope_tables(T)
    proj = _inproj_call(x2d, norm1_w, mod, w_in_bf, rot_row, rot_col, sgu_ln_w, sgu_ln_b)
    za = _sgu_call(proj, sgu_w[0].astype(BF16), sgu_b[0].T, w_proj_a[0].astype(BF16))
    yf, yb = _ret_call(proj, s0, dec)
    x1, h2, logits_t = _merge_call(yf, yb, za, proj, x2d, mod, norm2_w, w_proj_b[0].astype(BF16),
                                   w_out[0].astype(BF16), router_w[0].T, router_b.reshape(N_EXPERTS, 1))
    pos_t, tok, nch = _route_call(logits_t)
    copies, ncopies, zstart, zcount, expert_tables = _routing_tables(nch[:, :, 0])
    n_slots = expert_tables[0].shape[0] * MOE_TM
    xs = _dispatch_call((copies, ncopies, zstart, zcount, expert_tables[2]), h2, pos_t, n_slots)
    y = _experts_call(expert_tables, xs, moe_w1[0], moe_b1[0][:, None, :], moe_w2[0], moe_b2[0][:, None, :])
    out = _combine_call(copies, ncopies, tok, x1, mod, final_norm_w.reshape(1, D), y)
    return out.reshape(B, T, D)
```

```python
import functools

import jax
import jax.numpy as jnp
from jax import lax
from jax.experimental import pallas as pl
from jax.experimental.pallas import tpu as pltpu

D_MODEL = 1024
GRID_W = 64
SGU_CHUNK = 128
SGU_GROUPS = 8
HEADS = 4
DK = D_MODEL // HEADS
DV = 2 * DK
ROPE_BASE = 10000.0
N_EXPERTS = 32
TOP_K = 4
D_FF = D_MODEL
SWIGLU_LIMIT = 7.0
SWIGLU_ALPHA = 1.702
EPS = 1e-6
IN_WIDTH = 12 * D_MODEL
COL_U, COL_V, COL_Q, COL_K, COL_VR, COL_GF, COL_GB, COL_GA, COL_GBR = 0, 1, 2, 3, 4, 6, 8, 10, 11

INPROJ_NB = 4
RET_CHUNK = 256
RET_SUB = 1
VMEM_LIMIT = 56 * 1024 * 1024

MOE_TT = 512
MOE_CH = 16
MOE_TM = 512
MOE_CPT = MOE_TM // MOE_CH
MOE_PARTS = 4
MOE_MAXC = TOP_K * MOE_TT // MOE_CH + N_EXPERTS
MOE_RLOC = MOE_MAXC * MOE_CH
MOE_RB = 256
MOE_RUN = 4
MOE_MAXQ = MOE_MAXC // MOE_RUN
MOE_MAXS = (MOE_RUN - 1) * N_EXPERTS
MOE_CPROW = 2 * (MOE_MAXQ + MOE_MAXS)

F32 = jnp.float32
BF16 = jnp.bfloat16


def _params(sem):
    return pltpu.CompilerParams(dimension_semantics=sem, vmem_limit_bytes=VMEM_LIMIT)


def _dot(a, b):
    return jnp.dot(a, b, preferred_element_type=F32)


def _dot_nt(a, b):
    return lax.dot_general(a, b, (((1,), (1,)), ((), ())), preferred_element_type=F32)


def _dot_tn(a, b):
    return lax.dot_general(a, b, (((0,), (0,)), ((), ())), preferred_element_type=F32)


def _rms(x, w):
    return x * lax.rsqrt(jnp.mean(x * x, axis=-1, keepdims=True) + EPS) * w


def _gelu(x):
    return 0.5 * x * (1.0 + lax.erf(x * (2.0 ** -0.5)))


def _sigmoid(x):
    return 0.5 * jnp.tanh(0.5 * x) + 0.5


def _mod_slice(mod_ref, row, k):
    return mod_ref[row:row + 1, k * D_MODEL:(k + 1) * D_MODEL]


def _mod_kernel(cc_ref, w_ref, b_ref, o_ref):
    s = cc_ref[...]
    s = s * jax.nn.sigmoid(s)
    w = w_ref[...]
    r0 = jnp.sum(s[:, 0:1] * w, axis=0, keepdims=True)
    r1 = jnp.sum(s[:, 1:2] * w, axis=0, keepdims=True)
    o_ref[...] = jnp.concatenate([r0, r1], axis=0) + b_ref[...]


def _mod_call(cc, w_mod, b_mod):
    bn = 1536
    n = w_mod.shape[1]
    return pl.pallas_call(
        _mod_kernel,
        grid=(n // bn,),
        in_specs=[
            pl.BlockSpec((D_MODEL, 2), lambda j: (0, 0)),
            pl.BlockSpec((D_MODEL, bn), lambda j: (0, j)),
            pl.BlockSpec((1, bn), lambda j: (0, j)),
        ],
        out_specs=pl.BlockSpec((2, bn), lambda j: (0, j)),
        out_shape=jax.ShapeDtypeStruct((2, n), F32),
        compiler_params=_params(("arbitrary",)),
        name="mod_vectors",
    )(cc, w_mod, b_mod)


def _ctx_kernel(ctx_ref, nw_ref, mod_ref, w_ref, dec_ref, s0_ref):
    L = ctx_ref.shape[0]
    sh = _mod_slice(mod_ref, 1, 0)
    sc = _mod_slice(mod_ref, 1, 1)
    hc = (_rms(ctx_ref[...], nw_ref[...]) * (1.0 + sc) + sh).astype(BF16)
    kv = _dot(hc, w_ref[...])
    lg = jnp.log1p(-jnp.exp2(dec_ref[...]))
    pos = lax.broadcasted_iota(jnp.int32, (L, 1), 0).astype(F32)
    for h in range(HEADS):
        k = kv[:, h * DK:(h + 1) * DK] * (DK ** -0.5)
        v = kv[:, HEADS * DK + h * DV:HEADS * DK + (h + 1) * DV].astype(BF16)
        wf = jnp.exp((L - 1.0 - pos) * lg[0:1, h:h + 1])
        wb = jnp.exp(pos * lg[1:2, h:h + 1])
        s0_ref[0, h] = _dot_tn((k * wf).astype(BF16), v)
        s0_ref[1, h] = _dot_tn((k * wb).astype(BF16), v)


def _ctx_call(ctx2d, nw, mod, w_in_bf, dec):
    L = ctx2d.shape[0]
    wcols = HEADS * DK + HEADS * DV
    return pl.pallas_call(
        _ctx_kernel,
        grid=(1,),
        in_specs=[
            pl.BlockSpec((L, D_MODEL), lambda i: (0, 0)),
            pl.BlockSpec((1, D_MODEL), lambda i: (0, 0)),
            pl.BlockSpec(mod.shape, lambda i: (0, 0)),
            pl.BlockSpec((D_MODEL, wcols), lambda i: (0, COL_K * D_MODEL // wcols)),
            pl.BlockSpec(dec.shape, lambda i: (0, 0)),
        ],
        out_specs=pl.BlockSpec((2, HEADS, DK, DV), lambda i: (0, 0, 0, 0)),
        out_shape=jax.ShapeDtypeStruct((2, HEADS, DK, DV), F32),
        compiler_params=_params(("arbitrary",)),
        name="ctx_states",
    )(ctx2d, nw, mod, w_in_bf, dec)


def _inproj_kernel(x_ref, nw_ref, mod_ref, w_ref, rot_row_ref, rot_col_ref, lnw_ref, lnb_ref, o_ref,
                   h_scr, g_scr, cos_ref, sin_ref):
    j = pl.program_id(1)
    tm = h_scr.shape[0]

    @pl.when(j == 0)
    def _():
        sh = _mod_slice(mod_ref, 0, 0)
        sc = _mod_slice(mod_ref, 0, 1)
        h_scr[...] = (_rms(x_ref[...], nw_ref[...]) * (1.0 + sc) + sh).astype(BF16)

    def pieces(emit, span, rsplit=1):
        for c0 in range(span.start, span.stop, DK):
            cols = slice(c0, c0 + DK)
            for r in range(rsplit):
                rows = slice(r * tm // rsplit, (r + 1) * tm // rsplit)
                emit(rows, cols, _dot(h_scr[rows, :], w_ref[:, cols]))

    def store(fn):
        def emit(rows, cols, acc):
            o_ref[rows, cols] = fn(acc).astype(BF16)
        return emit

    def rope(scale):
        half = DK // 2

        def emit(rows, cols, acc):
            cos = cos_ref[rows, :]
            sin = sin_ref[rows, :]
            x1 = acc[:, :half]
            x2 = acc[:, half:]
            o_ref[rows, cols.start:cols.start + half] = ((x1 * cos - x2 * sin) * scale).astype(BF16)
            o_ref[rows, cols.start + half:cols.stop] = ((x1 * sin + x2 * cos) * scale).astype(BF16)
        return emit

    def gelu_layer_norm(span):
        def emit(rows, cols, acc):
            g_scr[rows, cols.start - span.start:cols.stop - span.start] = _gelu(acc)
        pieces(emit, span)
        g = g_scr[...]
        mu = jnp.mean(g, axis=-1, keepdims=True)
        d = g - mu
        var = jnp.mean(d * d, axis=-1, keepdims=True)
        o_ref[:, span] = (d * lax.rsqrt(var + EPS) * lnw_ref[...] + lnb_ref[...]).astype(BF16)

    def rotary_tables():
        for r in range(tm // GRID_W):
            tok = slice(r * GRID_W, (r + 1) * GRID_W)
            cos_ref[tok, :] = rot_row_ref[0, r:r + 1, :] + rot_col_ref[0]
            sin_ref[tok, :] = rot_row_ref[1, r:r + 1, :] + rot_col_ref[1]

    def column_block(block, span):
        if block == COL_U:
            pieces(store(_gelu), span)
        elif block == COL_V:
            gelu_layer_norm(span)
        elif block == COL_Q:
            rotary_tables()
            pieces(rope(1.0), span)
        elif block == COL_K:
            pieces(rope(DK ** -0.5), span)
        elif block < COL_GF:
            pieces(store(lambda a: a), span)
        elif block < COL_GA:
            pieces(store(lambda a: a * _sigmoid(a)), span, rsplit=2)
        else:
            pieces(store(_sigmoid), span, rsplit=2)

    for step in range(IN_WIDTH // D_MODEL // INPROJ_NB):
        @pl.when(j == step)
        def _():
            for b in range(INPROJ_NB):
                column_block(step * INPROJ_NB + b, slice(b * D_MODEL, (b + 1) * D_MODEL))


def _inproj_call(x2d, nw, mod, w_in_bf, rot_row, rot_col, lnw, lnb):
    T = x2d.shape[0]
    tm = 1024
    half = DK // 2
    return pl.pallas_call(
        _inproj_kernel,
        grid=(T // tm, IN_WIDTH // (INPROJ_NB * D_MODEL)),
        in_specs=[
            pl.BlockSpec((tm, D_MODEL), lambda i, j: (i, 0)),
            pl.BlockSpec((1, D_MODEL), lambda i, j: (0, 0)),
            pl.BlockSpec(mod.shape, lambda i, j: (0, 0)),
            pl.BlockSpec((D_MODEL, INPROJ_NB * D_MODEL), lambda i, j: (0, j)),
            pl.BlockSpec((2, tm // GRID_W, half), lambda i, j: (0, i, 0)),
            pl.BlockSpec((2, GRID_W, half), lambda i, j: (0, 0, 0)),
            pl.BlockSpec((1, D_MODEL), lambda i, j: (0, 0)),
            pl.BlockSpec((1, D_MODEL), lambda i, j: (0, 0)),
        ],
        out_specs=pl.BlockSpec((tm, INPROJ_NB * D_MODEL), lambda i, j: (i, j)),
        out_shape=jax.ShapeDtypeStruct((T, IN_WIDTH), BF16),
        scratch_shapes=[
            pltpu.VMEM((tm, D_MODEL), BF16),
            pltpu.VMEM((tm, D_MODEL), F32),
            pltpu.VMEM((tm, half), F32),
            pltpu.VMEM((tm, half), F32),
        ],
        compiler_params=_params(("arbitrary", "arbitrary")),
        name="in_proj",
    )(x2d, nw, mod, w_in_bf, rot_row, rot_col, lnw, lnb)


def _sgu_kernel(u_ref, v_ref, ga_ref, ws_ref, bt_ref, wpa_ref, o_ref, ya_scr):
    tm = u_ref.shape[0]
    gd = D_MODEL // SGU_GROUPS
    for n in range(tm // SGU_CHUNK):
        rows = slice(n * SGU_CHUNK, (n + 1) * SGU_CHUNK)
        for g in range(SGU_GROUPS):
            cols = slice(g * gd, (g + 1) * gd)
            mixed = _dot(ws_ref[g], v_ref[rows, cols]) + bt_ref[:, g:g + 1]
            ya_scr[rows, cols] = (u_ref[rows, cols].astype(F32) * mixed).astype(BF16)
    pa = _dot(ya_scr[...], wpa_ref[...])
    o_ref[...] = (ga_ref[...].astype(F32) * pa).astype(BF16)


def _sgu_call(proj, ws_bf, bt, wpa_bf):
    T = proj.shape[0]
    tm = 1024
    return pl.pallas_call(
        _sgu_kernel,
        grid=(T // tm,),
        in_specs=[
            pl.BlockSpec((tm, D_MODEL), lambda i: (i, COL_U)),
            pl.BlockSpec((tm, D_MODEL), lambda i: (i, COL_V)),
            pl.BlockSpec((tm, D_MODEL), lambda i: (i, COL_GA)),
            pl.BlockSpec(ws_bf.shape, lambda i: (0, 0, 0)),
            pl.BlockSpec(bt.shape, lambda i: (0, 0)),
            pl.BlockSpec((D_MODEL, D_MODEL), lambda i: (0, 0)),
        ],
        out_specs=pl.BlockSpec((tm, D_MODEL), lambda i: (i, 0)),
        out_shape=jax.ShapeDtypeStruct((T, D_MODEL), BF16),
        scratch_shapes=[pltpu.VMEM((tm, D_MODEL), BF16)],
        compiler_params=_params(("arbitrary",)),
        name="sgu_proj_a",
    )(proj, proj, proj, ws_bf, bt, wpa_bf)


def _ret_kernel(qf_ref, kf_ref, vf_ref, gf_ref, qb_ref, kb_ref, vb_ref, gb_ref, s0_ref, dec_ref,
                yf_ref, yb_ref, s_scr, intra_scr, qd_scr, kd_scr, cd_scr):
    C = RET_CHUNK
    i = pl.program_id(0)

    @pl.when(i == 0)
    def _():
        lg = jnp.log1p(-jnp.exp2(dec_ref[...]))
        r = lax.broadcasted_iota(jnp.int32, (C, C), 0).astype(F32)
        c = lax.broadcasted_iota(jnp.int32, (C, C), 1).astype(F32)
        pos = lax.broadcasted_iota(jnp.int32, (C, 1), 0).astype(F32)
        for h in range(HEADS):
            lf = lg[0:1, h:h + 1]
            lb = lg[1:2, h:h + 1]
            intra_scr[0, h] = jnp.where(r >= c, jnp.exp(jnp.maximum(r - c, 0.0) * lf), 0.0)
            intra_scr[1, h] = jnp.where(c >= r, jnp.exp(jnp.maximum(c - r, 0.0) * lb), 0.0)
            qd_scr[0, h] = jnp.exp((pos + 1.0) * lf)
            qd_scr[1, h] = jnp.exp((C - pos) * lb)
            kd_scr[0, h] = jnp.exp((C - 1.0 - pos) * lf)
            kd_scr[1, h] = jnp.exp(pos * lb)
            cd_scr[0, h] = jnp.exp(C * lf)
            cd_scr[1, h] = jnp.exp(C * lb)
        s_scr[...] = s0_ref[...]

    dirs = ((qf_ref, kf_ref, vf_ref, gf_ref, yf_ref), (qb_ref, kb_ref, vb_ref, gb_ref, yb_ref))
    nsub = qf_ref.shape[0] // C
    for sub in range(nsub):
        for d, (q_ref, k_ref, v_ref, g_ref, y_ref) in enumerate(dirs):
            c0 = (sub if d == 0 else nsub - 1 - sub) * C
            rows = slice(c0, c0 + C)
            for h in range(HEADS):
                q = q_ref[rows, h * DK:(h + 1) * DK]
                k = k_ref[rows, h * DK:(h + 1) * DK]
                v = v_ref[rows, h * DV:(h + 1) * DV]
                s = s_scr[d, h]
                a = (_dot_nt(q, k) * intra_scr[d, h]).astype(BF16)
                o = _dot(a, v) + _dot(q, s.astype(BF16)) * qd_scr[d, h]
                hn = o * lax.rsqrt(jnp.mean(o * o, axis=-1, keepdims=True) + EPS)
                y_ref[rows, h * DV:(h + 1) * DV] = (g_ref[rows, h * DV:(h + 1) * DV].astype(F32) * hn).astype(BF16)
                kd = (k.astype(F32) * kd_scr[d, h]).astype(BF16)
                s_scr[d, h] = s * cd_scr[d, h] + _dot_tn(kd, v)


def _ret_call(proj, s0, dec):
    T = proj.shape[0]
    C = RET_CHUNK
    rows = RET_SUB * C
    n = T // rows
    qk_w = HEADS * DK
    v_w = HEADS * DV
    fwd = lambda col: (lambda i: (i, col))
    bwd = lambda col: (lambda i: (n - 1 - i, col))
    return pl.pallas_call(
        _ret_kernel,
        grid=(n,),
        in_specs=[
            pl.BlockSpec((rows, qk_w), fwd(COL_Q)),
            pl.BlockSpec((rows, qk_w), fwd(COL_K)),
            pl.BlockSpec((rows, v_w), fwd(COL_VR // 2)),
            pl.BlockSpec((rows, v_w), fwd(COL_GF // 2)),
            pl.BlockSpec((rows, qk_w), bwd(COL_Q)),
            pl.BlockSpec((rows, qk_w), bwd(COL_K)),
            pl.BlockSpec((rows, v_w), bwd(COL_VR // 2)),
            pl.BlockSpec((rows, v_w), bwd(COL_GB // 2)),
            pl.BlockSpec(s0.shape, lambda i: (0, 0, 0, 0)),
            pl.BlockSpec(dec.shape, lambda i: (0, 0)),
        ],
        out_specs=[
            pl.BlockSpec((rows, v_w), lambda i: (i, 0)),
            pl.BlockSpec((rows, v_w), lambda i: (n - 1 - i, 0)),
        ],
        out_shape=[jax.ShapeDtypeStruct((T, v_w), BF16), jax.ShapeDtypeStruct((T, v_w), BF16)],
        scratch_shapes=[
            pltpu.VMEM((2, HEADS, DK, DV), F32),
            pltpu.VMEM((2, HEADS, C, C), F32),
            pltpu.VMEM((2, HEADS, C, 1), F32),
            pltpu.VMEM((2, HEADS, C, 1), F32),
            pltpu.VMEM((2, HEADS, 1, 1), F32),
        ],
        compiler_params=_params(("arbitrary",)),
        name="retention",
    )(proj, proj, proj, proj, proj, proj, proj, proj, s0, dec)


def _merge_kernel(yf_ref, yb_ref, za_ref, gbr_ref, x_ref, mod_ref, nw_ref, wpb_ref, wo_ref, rw_ref, rb_ref,
                  x1_ref, h2_ref, lg_ref):
    yb = yf_ref[...] + yb_ref[...]
    pb = _dot(yb, wpb_ref[...])
    y = za_ref[...].astype(F32) + gbr_ref[...].astype(F32) * pb
    yo = _dot(y.astype(BF16), wo_ref[...])
    x1 = x_ref[...] + _mod_slice(mod_ref, 0, 2) * yo
    x1_ref[...] = x1
    h2 = _rms(x1, nw_ref[...]) * (1.0 + _mod_slice(mod_ref, 0, 4)) + _mod_slice(mod_ref, 0, 3)
    h2_hi = h2.astype(BF16)
    h2_ref[...] = h2_hi
    h2_lo = (h2 - h2_hi.astype(F32)).astype(BF16)
    rw = rw_ref[...]
    rw_hi = rw.astype(BF16)
    rw_lo = (rw - rw_hi.astype(F32)).astype(BF16)
    both = _dot_nt(jnp.concatenate([rw_hi, rw_lo], axis=0), h2_hi)
    lg_ref[...] = both[:N_EXPERTS] + both[N_EXPERTS:] + _dot_nt(rw_hi, h2_lo) + rb_ref[...]


def _merge_call(yf, yb, za, proj, x2d, mod, nw2, wpb_bf, wo_bf, rw, rb):
    T = x2d.shape[0]
    tm = 512
    row = lambda i: (i, 0)
    fixed = lambda i: (0, 0)
    return pl.pallas_call(
        _merge_kernel,
        grid=(T // tm,),
        in_specs=[
            pl.BlockSpec((tm, HEADS * DV), row),
            pl.BlockSpec((tm, HEADS * DV), row),
            pl.BlockSpec((tm, D_MODEL), row),
            pl.BlockSpec((tm, D_MODEL), lambda i: (i, COL_GBR)),
            pl.BlockSpec((tm, D_MODEL), row),
            pl.BlockSpec(mod.shape, fixed),
            pl.BlockSpec((1, D_MODEL), fixed),
            pl.BlockSpec(wpb_bf.shape, fixed),
            pl.BlockSpec(wo_bf.shape, fixed),
            pl.BlockSpec(rw.shape, fixed),
            pl.BlockSpec(rb.shape, fixed),
        ],
        out_specs=[
            pl.BlockSpec((tm, D_MODEL), row),
            pl.BlockSpec((tm, D_MODEL), row),
            pl.BlockSpec((N_EXPERTS, tm), lambda i: (0, i)),
        ],
        out_shape=[
            jax.ShapeDtypeStruct((T, D_MODEL), F32),
            jax.ShapeDtypeStruct((T, D_MODEL), BF16),
            jax.ShapeDtypeStruct((N_EXPERTS, T), F32),
        ],
        compiler_params=_params(("arbitrary",)),
        name="merge_router",
    )(yf, yb, za, proj, x2d, mod, nw2, wpb_bf, wo_bf, rw, rb)


def _route_kernel(lg_ref, pos_ref, tok_ref, nch_ref, su_scr):
    E = lg_ref.shape[0]
    tt = MOE_TT

    @pl.when(pl.program_id(0) == 0)
    def _():
        r = lax.broadcasted_iota(jnp.int32, (tt, tt), 0)
        c = lax.broadcasted_iota(jnp.int32, (tt, tt), 1)
        su_scr[...] = jnp.where(r < c, 1.0, 0.0).astype(BF16)

    for t in range(lg_ref.shape[1] // tt):
        _route_tile(t, slice(t * tt, (t + 1) * tt), lg_ref, pos_ref, tok_ref, nch_ref, su_scr)


def _route_tile(t, toks, lg_ref, pos_ref, tok_ref, nch_ref, su_scr):
    E, tt = lg_ref.shape[0], MOE_TT
    sub = lax.broadcasted_iota(jnp.int32, (E, tt), 0)
    work = lg_ref[:, toks]
    vals, hots = [], []
    for _ in range(TOP_K):
        m = jnp.max(work, axis=0, keepdims=True)
        first = jnp.min(jnp.where(work == m, sub, E), axis=0, keepdims=True)
        hot = sub == first
        vals.append(m)
        hots.append(hot)
        work = jnp.where(hot, -jnp.inf, work)
    exps = [jnp.exp(v - vals[0]) for v in vals]
    inv = 1.0 / functools.reduce(lambda a, b: a + b, exps)

    member = functools.reduce(lambda a, b: a + b, [jnp.where(h, 1.0, 0.0) for h in hots])
    cnt = jnp.sum(member, axis=1, keepdims=True)
    nch = jnp.floor((cnt + (MOE_CH - 1.0)) * (1.0 / MOE_CH))
    nch_b = jnp.broadcast_to(nch, (E, 128))
    rank = _dot(member.astype(BF16), su_scr[...])
    er = lax.broadcasted_iota(jnp.int32, (E, E), 0)
    ec = lax.broadcasted_iota(jnp.int32, (E, E), 1)
    first_chunk = _dot(jnp.where(ec < er, 1.0, 0.0).astype(BF16), nch_b.astype(BF16))[:, 0:1]
    base = first_chunk * MOE_CH + rank
    pos = [jnp.sum(jnp.where(hots[k], base, 0.0), axis=0, keepdims=True) for k in range(TOP_K)]
    for k in range(TOP_K):
        pos_ref[k:k + 1, toks] = pos[k].astype(jnp.int32)
    rows = jnp.concatenate(pos + [e * inv for e in exps] + [jnp.zeros((128 - 2 * TOP_K, tt), F32)], axis=0)
    tok_ref[toks, :] = rows.T
    nch_ref[t] = nch_b.astype(jnp.int32)


def _route_call(logits_t):
    E, T = logits_t.shape
    tt = MOE_TT
    per_step = 4
    assert (T // tt) % per_step == 0
    span = per_step * tt
    return pl.pallas_call(
        _route_kernel,
        grid=(T // span,),
        in_specs=[pl.BlockSpec((E, span), lambda j: (0, j))],
        out_specs=[
            pl.BlockSpec((TOP_K, span), lambda j: (0, j)),
            pl.BlockSpec((span, 128), lambda j: (j, 0)),
            pl.BlockSpec((per_step, E, 128), lambda j: (j, 0, 0)),
        ],
        out_shape=[
            jax.ShapeDtypeStruct((TOP_K, T), jnp.int32),
            jax.ShapeDtypeStruct((T, 128), F32),
            jax.ShapeDtypeStruct((T // tt, E, 128), jnp.int32),
        ],
        scratch_shapes=[pltpu.VMEM((tt, tt), BF16)],
        compiler_params=_params(("arbitrary",)),
        name="moe_route",
    )(logits_t)


def _slot_tiles_max(n_tiles):
    return -(-(n_tiles * MOE_MAXC + N_EXPERTS * (MOE_CPT - 1)) // MOE_CPT)


def _routing_tables(nch):
    n_tiles, E = nch.shape

    def cumsum_last(a):
        m = a.shape[-1]
        keep = jnp.arange(m)[:, None] <= jnp.arange(m)[None, :]
        return jnp.sum(jnp.where(keep, a[..., :, None], 0), axis=-2)

    tot = jnp.sum(nch, axis=0)
    seg = (tot + MOE_CPT - 1) // MOE_CPT * MOE_CPT
    seg_end = cumsum_last(seg)
    seg_start = seg_end - seg
    gstart = seg_start[None, :] + cumsum_last(nch.T).T - nch
    l_end = cumsum_last(nch)
    lstart = l_end - nch

    def copy_list(count, first_local, first_global, step, max_len):
        end = cumsum_last(count)
        start = end - count
        i = jnp.arange(max_len, dtype=jnp.int32)[None, :, None]
        owner = (start[:, None, :] <= i) & (i < end[:, None, :])
        off = (i - start[:, None, :]) * step
        src = jnp.sum(jnp.where(owner, first_local[:, None, :] + off, 0), axis=-1)
        dst = jnp.sum(jnp.where(owner, first_global[:, None, :] + off, 0), axis=-1)
        return src, dst, end[:, -1]

    runs = nch // MOE_RUN
    rsrc, rdst, nruns = copy_list(runs, lstart, gstart, MOE_RUN, MOE_MAXQ)
    ssrc, sdst, nsingles = copy_list(nch - runs * MOE_RUN, lstart + runs * MOE_RUN, gstart + runs * MOE_RUN, 1,
                                     MOE_MAXS)
    copies = jnp.concatenate([rsrc, rdst, ssrc, sdst], axis=1).astype(jnp.int32).reshape(-1)
    ncopies = jnp.stack([nruns, nsingles], axis=1).astype(jnp.int32).reshape(-1)
    nt_max = _slot_tiles_max(n_tiles)
    n_used = (seg_end[-1] // MOE_CPT).astype(jnp.int32)
    tile_blk = jnp.minimum(jnp.arange(nt_max, dtype=jnp.int32), n_used - 1)
    tile_exp = jnp.sum(seg_end[None, :] <= (tile_blk * MOE_CPT)[:, None], axis=1)
    tile_exp = jnp.minimum(tile_exp, E - 1).astype(jnp.int32)
    experts = jnp.arange(E, dtype=jnp.int32)
    own = tile_exp[:, None] == experts[None, :]
    end_tile = jnp.sum(jnp.where(own, (seg_end // MOE_CPT)[None, :], 0), axis=1)
    follows = ((seg_start // MOE_CPT)[None, :] == end_tile[:, None]) & (seg > 0)[None, :]
    next_exp = jnp.where(jnp.any(follows, axis=1), jnp.sum(jnp.where(follows, experts[None, :], 0), axis=1), -1)
    next_exp = next_exp.astype(jnp.int32)
    zstart = (seg_start + tot).astype(jnp.int32)
    zcount = (seg - tot).astype(jnp.int32)
    first_tile = jnp.sum(jnp.where(own, (seg_start // MOE_CPT)[None, :], 0), axis=1)
    own_chunks = jnp.sum(jnp.where(own, tot[None, :], 0), axis=1)
    filled = own_chunks - (jnp.arange(nt_max, dtype=jnp.int32) - first_tile) * MOE_CPT
    per_part = MOE_CPT // MOE_PARTS
    parts = jnp.clip((filled + per_part - 1) // per_part, 1, MOE_PARTS).astype(jnp.int32)
    expert_tables = (tile_exp, tile_blk, n_used.reshape(1), next_exp, parts)
    return copies, ncopies, zstart, zcount, expert_tables


def _chunk_rows(c, nchunks=1):
    return pl.ds(pl.multiple_of(c * MOE_CH, MOE_CH), nchunks * MOE_CH)


def _for_tile_copies(cp_ref, cn_ref, tile, make, op):
    base = tile * MOE_CPROW

    def run(i, carry):
        op(make(cp_ref[base + i], cp_ref[base + MOE_MAXQ + i], MOE_RUN))
        return carry
    lax.fori_loop(0, cn_ref[2 * tile], run, 0)

    def single(i, carry):
        op(make(cp_ref[base + 2 * MOE_MAXQ + i], cp_ref[base + 2 * MOE_MAXQ + MOE_MAXS + i], 1))
        return carry
    lax.fori_loop(0, cn_ref[2 * tile + 1], single, 0)


def _dispatch_kernel(cp_ref, cn_ref, zstart_ref, zcount_ref, nu_ref, h2_ref, pos_ref, xs_hbm, xl_scr, zero_scr,
                     sems, zsem):
    j = pl.program_id(0)
    n = pl.num_programs(0)
    slot = j % 2
    tt = h2_ref.shape[0]
    nt_max = xs_hbm.shape[0] // MOE_TM

    def tile_copies(tile, slot_, op):
        def make(local, glob, nchunks):
            return pltpu.make_async_copy(xl_scr.at[slot_, _chunk_rows(local, nchunks)],
                                         xs_hbm.at[_chunk_rows(glob, nchunks)], sems.at[slot_])
        _for_tile_copies(cp_ref, cn_ref, tile, make, op)

    def zero_chunk_copy(d, nchunks):
        return pltpu.make_async_copy(zero_scr.at[pl.ds(0, nchunks * MOE_CH)], xs_hbm.at[_chunk_rows(d, nchunks)],
                                     zsem.at[0])

    def zero_tile_copy(t):
        rows = pl.ds(pl.multiple_of(t * MOE_TM, MOE_TM), MOE_TM)
        return pltpu.make_async_copy(zero_scr, xs_hbm.at[rows], zsem.at[0])

    def zero_fill(op):
        def expert(e, carry):
            nruns = zcount_ref[e] // MOE_RUN

            def run(r, carry_):
                op(zero_chunk_copy(zstart_ref[e] + r * MOE_RUN, MOE_RUN))
                return carry_
            carry = lax.fori_loop(0, nruns, run, carry)

            def single(r, carry_):
                op(zero_chunk_copy(zstart_ref[e] + nruns * MOE_RUN + r, 1))
                return carry_
            return lax.fori_loop(0, zcount_ref[e] - nruns * MOE_RUN, single, carry)
        lax.fori_loop(0, N_EXPERTS, expert, 0)

        def tile(t, carry):
            op(zero_tile_copy(t))
            return carry
        lax.fori_loop(nu_ref[0], nt_max, tile, 0)

    def wait_tile(tile, slot_):
        tile_copies(tile, slot_, lambda cp: cp.wait())

    @pl.when(j == 0)
    def _():
        zero_scr[...] = jnp.zeros_like(zero_scr)
        zero_fill(lambda cp: cp.start())

    @pl.when(j >= 2)
    def _():
        wait_tile(j - 2, slot)

    pos = pos_ref[...]
    h2 = h2_ref[...]
    for rb in range(MOE_RLOC // MOE_RB):
        io = lax.broadcasted_iota(jnp.int32, (MOE_RB, tt), 0) + rb * MOE_RB
        onehot = jnp.zeros((MOE_RB, tt), F32)
        for k in range(TOP_K):
            onehot = jnp.where(io == pos[k:k + 1, :], 1.0, onehot)
        xl_scr[slot, rb * MOE_RB:(rb + 1) * MOE_RB, :] = _dot(onehot.astype(BF16), h2).astype(BF16)

    tile_copies(j, slot, lambda cp: cp.start())

    @pl.when(j == n - 1)
    def _():
        @pl.when(j >= 1)
        def _():
            wait_tile(j - 1, 1 - slot)
        wait_tile(j, slot)
        zero_fill(lambda cp: cp.wait())


def _dispatch_call(tables, h2, pos_t, n_slots):
    T = h2.shape[0]
    tt = MOE_TT
    copies, ncopies, zstart, zcount, n_used = tables
    grid_spec = pltpu.PrefetchScalarGridSpec(
        num_scalar_prefetch=5,
        grid=(T // tt,),
        in_specs=[
            pl.BlockSpec((tt, D_MODEL), lambda j, *_: (j, 0)),
            pl.BlockSpec((TOP_K, tt), lambda j, *_: (0, j)),
        ],
        out_specs=pl.BlockSpec(memory_space=pl.ANY),
        scratch_shapes=[
            pltpu.VMEM((2, MOE_RLOC, D_MODEL), BF16),
            pltpu.VMEM((MOE_TM, D_MODEL), BF16),
            pltpu.SemaphoreType.DMA((2,)),
            pltpu.SemaphoreType.DMA((1,)),
        ],
    )
    return pl.pallas_call(
        _dispatch_kernel,
        grid_spec=grid_spec,
        out_shape=jax.ShapeDtypeStruct((n_slots, D_MODEL), BF16),
        compiler_params=_params(("arbitrary",)),
        name="moe_dispatch",
    )(copies, ncopies, zstart, zcount, n_used, h2, pos_t)


def _experts_kernel(te_ref, tb_ref, nu_ref, nx_ref, parts_ref, xs_ref, w1_hbm, b1_ref, w2_hbm, b2_ref, y_ref,
                    w1_f32, w2_f32, w1_scr, w2_scr, sems):
    i = pl.program_id(0)
    valid = i < nu_ref[0]
    new_expert = (i == 0) | (te_ref[i] != te_ref[jnp.maximum(i - 1, 0)])

    def fetch(e):
        return (pltpu.make_async_copy(w1_hbm.at[e], w1_f32, sems.at[0]),
                pltpu.make_async_copy(w2_hbm.at[e], w2_f32, sems.at[1]))

    @pl.when(i == 0)
    def _():
        for cp in fetch(te_ref[0]):
            cp.start()

    @pl.when(valid & new_expert)
    def _():
        for cp in fetch(te_ref[i]):
            cp.wait()
        w1_scr[...] = w1_f32[...].astype(BF16)
        w2_scr[...] = w2_f32[...].astype(BF16)

        @pl.when(nx_ref[i] >= 0)
        def _():
            for cp in fetch(nx_ref[i]):
                cp.start()

    def mlp(rows):
        hh = _dot(xs_ref[rows, :], w1_scr[...]) + b1_ref[...]
        gate = jnp.minimum(hh[:, :D_FF], SWIGLU_LIMIT)
        up = jnp.clip(hh[:, D_FF:], -SWIGLU_LIMIT, SWIGLU_LIMIT)
        act = (up + 1.0) * gate * _sigmoid(SWIGLU_ALPHA * gate)
        y_ref[rows, :] = (_dot(act.astype(BF16), w2_scr[...]) + b2_ref[...]).astype(BF16)

    for nparts in range(1, MOE_PARTS + 1):
        @pl.when(valid & (parts_ref[i] == nparts))
        def _():
            rows = nparts * (MOE_TM // MOE_PARTS)
            mlp(slice(0, rows))
            if rows < MOE_TM:
                y_ref[rows:, :] = jnp.zeros((MOE_TM - rows, D_MODEL), BF16)


def _experts_call(tables, xs, w1, b1, w2, b2):
    tile_exp = tables[0]
    nt_max = tile_exp.shape[0]
    slot_tile = lambda i, te, tb, *_: (tb[i], 0)
    per_e = lambda i, te, *_: (te[i], 0, 0)
    grid_spec = pltpu.PrefetchScalarGridSpec(
        num_scalar_prefetch=len(tables),
        grid=(nt_max,),
        in_specs=[
            pl.BlockSpec((MOE_TM, D_MODEL), slot_tile),
            pl.BlockSpec(memory_space=pl.ANY),
            pl.BlockSpec((None, 1, 2 * D_FF), per_e),
            pl.BlockSpec(memory_space=pl.ANY),
            pl.BlockSpec((None, 1, D_MODEL), per_e),
        ],
        out_specs=pl.BlockSpec((MOE_TM, D_MODEL), slot_tile),
        scratch_shapes=[
            pltpu.VMEM((D_MODEL, 2 * D_FF), F32),
            pltpu.VMEM((D_FF, D_MODEL), F32),
            pltpu.VMEM((D_MODEL, 2 * D_FF), BF16),
            pltpu.VMEM((D_FF, D_MODEL), BF16),
            pltpu.SemaphoreType.DMA((2,)),
        ],
    )
    return pl.pallas_call(
        _experts_kernel,
        grid_spec=grid_spec,
        out_shape=jax.ShapeDtypeStruct(xs.shape, BF16),
        input_output_aliases={len(tables): 0},
        compiler_params=_params(("arbitrary",)),
        name="moe_experts",
    )(*tables, xs, w1, b1, w2, b2)


def _combine_kernel(cp_ref, cn_ref, tok_ref, x1_ref, mod_ref, fw_ref, y_hbm, o_ref, yl_scr, w_scr, sems):
    j = pl.program_id(0)
    n = pl.num_programs(0)
    slot = j % 2
    tt = x1_ref.shape[0]

    def tile_copies(tile, slot_, op):
        def make(local, glob, nchunks):
            return pltpu.make_async_copy(y_hbm.at[_chunk_rows(glob, nchunks)],
                                         yl_scr.at[slot_, _chunk_rows(local, nchunks)], sems.at[slot_])
        _for_tile_copies(cp_ref, cn_ref, tile, make, op)

    def start_tile(tile, slot_):
        tile_copies(tile, slot_, lambda cp: cp.start())

    @pl.when(j == 0)
    def _():
        yl_scr[...] = jnp.zeros_like(yl_scr)
        start_tile(0, 0)

    @pl.when(j + 1 < n)
    def _():
        start_tile(j + 1, 1 - slot)

    tile_copies(j, slot, lambda cp: cp.wait())

    pos = tok_ref[:, 0:TOP_K].astype(jnp.int32)
    p = tok_ref[:, TOP_K:2 * TOP_K]
    for cb in range(MOE_RLOC // MOE_RB):
        io = lax.broadcasted_iota(jnp.int32, (tt, MOE_RB), 1) + cb * MOE_RB
        w = jnp.zeros((tt, MOE_RB), F32)
        for k in range(TOP_K):
            w = jnp.where(io == pos[:, k:k + 1], p[:, k:k + 1], w)
        w_scr[:, cb * MOE_RB:(cb + 1) * MOE_RB] = w.astype(BF16)
    moe = _dot(w_scr[...], yl_scr[slot])
    x2 = x1_ref[...] + _mod_slice(mod_ref, 0, 5) * moe
    o_ref[...] = _rms(x2, fw_ref[...])


def _combine_call(copies, ncopies, tok, x1, mod, fw, y):
    T = x1.shape[0]
    tt = MOE_TT
    row = lambda j, *_: (j, 0)
    fixed = lambda j, *_: (0, 0)
    grid_spec = pltpu.PrefetchScalarGridSpec(
        num_scalar_prefetch=2,
        grid=(T // tt,),
        in_specs=[
            pl.BlockSpec((tt, 128), row),
            pl.BlockSpec((tt, D_MODEL), row),
            pl.BlockSpec(mod.shape, fixed),
            pl.BlockSpec((1, D_MODEL), fixed),
            pl.BlockSpec(memory_space=pl.ANY),
        ],
        out_specs=pl.BlockSpec((tt, D_MODEL), row),
        scratch_shapes=[
            pltpu.VMEM((2, MOE_RLOC, D_MODEL), BF16),
            pltpu.VMEM((tt, MOE_RLOC), BF16),
            pltpu.SemaphoreType.DMA((2,)),
        ],
    )
    return pl.pallas_call(
        _combine_kernel,
        grid_spec=grid_spec,
        out_shape=jax.ShapeDtypeStruct((T, D_MODEL), F32),
        compiler_params=_params(("arbitrary",)),
        name="moe_combine",
    )(copies, ncopies, tok, x1, mod, fw, y)


def _rope_tables(T):
    rows = T // GRID_W
    n_freq = DK // 4
    inv = ROPE_BASE ** (-jnp.arange(n_freq, dtype=F32) / n_freq)
    row_ang = jnp.arange(rows, dtype=F32)[:, None] * inv
    col_ang = jnp.arange(GRID_W, dtype=F32)[:, None] * inv
    rot_row = jnp.pad(jnp.stack([jnp.cos(row_ang), jnp.sin(row_ang)]), ((0, 0), (0, 0), (0, n_freq)))
    rot_col = jnp.pad(jnp.stack([jnp.cos(col_ang), jnp.sin(col_ang)]), ((0, 0), (0, 0), (n_freq, 0)))
    return rot_row, rot_col


def kernel(x, c, ctx, c_ctx, w_mod, b_mod, norm1_w, norm2_w, w_in, sgu_ln_w, sgu_ln_b, sgu_w, sgu_b,
           ret_decay_fwd, ret_decay_bwd, w_proj_a, w_proj_b, w_out, router_w, router_b,
           moe_w1, moe_b1, moe_w2, moe_b2, final_norm_w):
    B, T, D = x.shape
    assert B == 1 and D == D_MODEL and w_mod.shape[0] == 1 and T % 1024 == 0
    x2d = x.reshape(T, D)
    cc = jnp.stack([c.reshape(D), c_ctx.reshape(D)], axis=1)
    mod = _mod_call(cc, w_mod[0], b_mod)
    dec = jnp.stack([ret_decay_fwd[0], ret_decay_bwd[0]], axis=0)
    w_in_bf = w_in[0].astype(BF16)
    s0 = _ctx_call(ctx.reshape(ctx.shape[1], D), norm1_w, mod, w_in_bf, dec)
    rot_row, rot_col = _rope_tables(T)
    proj = _inproj_call(x2d, norm1_w, mod, w_in_bf, rot_row, rot_col, sgu_ln_w, sgu_ln_b)
    za = _sgu_call(proj, sgu_w[0].astype(BF16), sgu_b[0].T, w_proj_a[0].astype(BF16))
    yf, yb = _ret_call(proj, s0, dec)
    x1, h2, logits_t = _merge_call(yf, yb, za, proj, x2d, mod, norm2_w, w_proj_b[0].astype(BF16),
                                   w_out[0].astype(BF16), router_w[0].T, router_b.reshape(N_EXPERTS, 1))
    pos_t, tok, nch = _route_call(logits_t)
    copies, ncopies, zstart, zcount, expert_tables = _routing_tables(nch[:, :, 0])
    n_slots = expert_tables[0].shape[0] * MOE_TM
    xs = _dispatch_call((copies, ncopies, zstart, zcount, expert_tables[2]), h2, pos_t, n_slots)
    y = _experts_call(expert_tables, xs, moe_w1[0], moe_b1[0][:, None, :], moe_w2[0], moe_b2[0][:, None, :])
    out = _combine_call(copies, ncopies, tok, x1, mod, final_norm_w.reshape(1, D), y)
    return out.reshape(B, T, D)
```

```python
import functools

import jax
import jax.numpy as jnp
from jax import lax
from jax.experimental import pallas as pl
from jax.experimental.pallas import tpu as pltpu

D_MODEL = 1024
GRID_W = 64
SGU_CHUNK = 128
SGU_GROUPS = 8
HEADS = 4
DK = D_MODEL // HEADS
DV = 2 * DK
ROPE_BASE = 10000.0
N_EXPERTS = 32
TOP_K = 4
D_FF = D_MODEL
SWIGLU_LIMIT = 7.0
SWIGLU_ALPHA = 1.702
EPS = 1e-6
IN_WIDTH = 12 * D_MODEL
COL_U, COL_V, COL_Q, COL_K, COL_VR, COL_GF, COL_GB, COL_GA, COL_GBR = 0, 1, 2, 3, 4, 6, 8, 10, 11

INPROJ_NB = 4
RET_CHUNK = 256
RET_SUB = 1
VMEM_LIMIT = 56 * 1024 * 1024

MOE_TT = 512
MOE_CH = 16
MOE_TM = 512
MOE_CPT = MOE_TM // MOE_CH
MOE_PARTS = 4
MOE_MAXC = TOP_K * MOE_TT // MOE_CH + N_EXPERTS
MOE_RLOC = MOE_MAXC * MOE_CH
MOE_RB = 256
MOE_RUN = 4
MOE_MAXQ = MOE_MAXC // MOE_RUN
MOE_MAXS = (MOE_RUN - 1) * N_EXPERTS
MOE_CPROW = 2 * (MOE_MAXQ + MOE_MAXS)

F32 = jnp.float32
BF16 = jnp.bfloat16


def _params(sem):
    return pltpu.CompilerParams(dimension_semantics=sem, vmem_limit_bytes=VMEM_LIMIT)


def _dot(a, b):
    return jnp.dot(a, b, preferred_element_type=F32)


def _dot_nt(a, b):
    return lax.dot_general(a, b, (((1,), (1,)), ((), ())), preferred_element_type=F32)


def _dot_tn(a, b):
    return lax.dot_general(a, b, (((0,), (0,)), ((), ())), preferred_element_type=F32)


def _rms(x, w):
    return x * lax.rsqrt(jnp.mean(x * x, axis=-1, keepdims=True) + EPS) * w


def _gelu(x):
    return 0.5 * x * (1.0 + lax.erf(x * (2.0 ** -0.5)))


def _sigmoid(x):
    return 0.5 * jnp.tanh(0.5 * x) + 0.5


def _mod_slice(mod_ref, row, k):
    return mod_ref[row:row + 1, k * D_MODEL:(k + 1) * D_MODEL]


def _mod_kernel(cc_ref, w_ref, b_ref, o_ref):
    s = cc_ref[...]
    s = s * jax.nn.sigmoid(s)
    w = w_ref[...]
    r0 = jnp.sum(s[:, 0:1] * w, axis=0, keepdims=True)
    r1 = jnp.sum(s[:, 1:2] * w, axis=0, keepdims=True)
    o_ref[...] = jnp.concatenate([r0, r1], axis=0) + b_ref[...]


def _mod_call(cc, w_mod, b_mod):
    bn = 768
    n = w_mod.shape[1]
    return pl.pallas_call(
        _mod_kernel,
        grid=(n // bn,),
        in_specs=[
            pl.BlockSpec((D_MODEL, 2), lambda j: (0, 0)),
            pl.BlockSpec((D_MODEL, bn), lambda j: (0, j)),
            pl.BlockSpec((1, bn), lambda j: (0, j)),
        ],
        out_specs=pl.BlockSpec((2, bn), lambda j: (0, j)),
        out_shape=jax.ShapeDtypeStruct((2, n), F32),
        compiler_params=_params(("arbitrary",)),
        name="mod_vectors",
    )(cc, w_mod, b_mod)


def _ctx_kernel(ctx_ref, nw_ref, mod_ref, w_ref, dec_ref, s0_ref):
    L = ctx_ref.shape[0]
    sh = _mod_slice(mod_ref, 1, 0)
    sc = _mod_slice(mod_ref, 1, 1)
    hc = (_rms(ctx_ref[...], nw_ref[...]) * (1.0 + sc) + sh).astype(BF16)
    kv = _dot(hc, w_ref[...])
    lg = jnp.log1p(-jnp.exp2(dec_ref[...]))
    pos = lax.broadcasted_iota(jnp.int32, (L, 1), 0).astype(F32)
    for h in range(HEADS):
        k = kv[:, h * DK:(h + 1) * DK] * (DK ** -0.5)
        v = kv[:, HEADS * DK + h * DV:HEADS * DK + (h + 1) * DV].astype(BF16)
        wf = jnp.exp((L - 1.0 - pos) * lg[0:1, h:h + 1])
        wb = jnp.exp(pos * lg[1:2, h:h + 1])
        s0_ref[0, h] = _dot_tn((k * wf).astype(BF16), v)
        s0_ref[1, h] = _dot_tn((k * wb).astype(BF16), v)


def _ctx_call(ctx2d, nw, mod, w_in_bf, dec):
    L = ctx2d.shape[0]
    wcols = HEADS * DK + HEADS * DV
    return pl.pallas_call(
        _ctx_kernel,
        grid=(1,),
        in_specs=[
            pl.BlockSpec((L, D_MODEL), lambda i: (0, 0)),
            pl.BlockSpec((1, D_MODEL), lambda i: (0, 0)),
            pl.BlockSpec(mod.shape, lambda i: (0, 0)),
            pl.BlockSpec((D_MODEL, wcols), lambda i: (0, COL_K * D_MODEL // wcols)),
            pl.BlockSpec(dec.shape, lambda i: (0, 0)),
        ],
        out_specs=pl.BlockSpec((2, HEADS, DK, DV), lambda i: (0, 0, 0, 0)),
        out_shape=jax.ShapeDtypeStruct((2, HEADS, DK, DV), F32),
        compiler_params=_params(("arbitrary",)),
        name="ctx_states",
    )(ctx2d, nw, mod, w_in_bf, dec)


def _inproj_kernel(x_ref, nw_ref, mod_ref, w_ref, rot_row_ref, rot_col_ref, lnw_ref, lnb_ref, o_ref,
                   h_scr, g_scr, cos_ref, sin_ref):
    j = pl.program_id(1)
    tm = h_scr.shape[0]

    @pl.when(j == 0)
    def _():
        sh = _mod_slice(mod_ref, 0, 0)
        sc = _mod_slice(mod_ref, 0, 1)
        h_scr[...] = (_rms(x_ref[...], nw_ref[...]) * (1.0 + sc) + sh).astype(BF16)

    def pieces(emit, span, rsplit=1):
        for c0 in range(span.start, span.stop, DK):
            cols = slice(c0, c0 + DK)
            for r in range(rsplit):
                rows = slice(r * tm // rsplit, (r + 1) * tm // rsplit)
                emit(rows, cols, _dot(h_scr[rows, :], w_ref[:, cols]))

    def store(fn):
        def emit(rows, cols, acc):
            o_ref[rows, cols] = fn(acc).astype(BF16)
        return emit

    def rope(scale):
        half = DK // 2

        def emit(rows, cols, acc):
            cos = cos_ref[rows, :]
            sin = sin_ref[rows, :]
            x1 = acc[:, :half]
            x2 = acc[:, half:]
            o_ref[rows, cols.start:cols.start + half] = ((x1 * cos - x2 * sin) * scale).astype(BF16)
            o_ref[rows, cols.start + half:cols.stop] = ((x1 * sin + x2 * cos) * scale).astype(BF16)
        return emit

    def gelu_layer_norm(span):
        def emit(rows, cols, acc):
            g_scr[rows, cols.start - span.start:cols.stop - span.start] = _gelu(acc)
        pieces(emit, span)
        g = g_scr[...]
        mu = jnp.mean(g, axis=-1, keepdims=True)
        d = g - mu
        var = jnp.mean(d * d, axis=-1, keepdims=True)
        o_ref[:, span] = (d * lax.rsqrt(var + EPS) * lnw_ref[...] + lnb_ref[...]).astype(BF16)

    def rotary_tables():
        for r in range(tm // GRID_W):
            tok = slice(r * GRID_W, (r + 1) * GRID_W)
            cos_ref[tok, :] = rot_row_ref[0, r:r + 1, :] + rot_col_ref[0]
            sin_ref[tok, :] = rot_row_ref[1, r:r + 1, :] + rot_col_ref[1]

    def column_block(block, span):
        if block == COL_U:
            pieces(store(_gelu), span)
        elif block == COL_V:
            gelu_layer_norm(span)
        elif block == COL_Q:
            rotary_tables()
            pieces(rope(1.0), span)
        elif block == COL_K:
            pieces(rope(DK ** -0.5), span)
        elif block < COL_GF:
            pieces(store(lambda a: a), span)
        elif block < COL_GA:
            pieces(store(lambda a: a * _sigmoid(a)), span, rsplit=2)
        else:
            pieces(store(_sigmoid), span, rsplit=2)

    for step in range(IN_WIDTH // D_MODEL // INPROJ_NB):
        @pl.when(j == step)
        def _():
            for b in range(INPROJ_NB):
                column_block(step * INPROJ_NB + b, slice(b * D_MODEL, (b + 1) * D_MODEL))


def _inproj_call(x2d, nw, mod, w_in_bf, rot_row, rot_col, lnw, lnb):
    T = x2d.shape[0]
    tm = 1024
    half = DK // 2
    return pl.pallas_call(
        _inproj_kernel,
        grid=(T // tm, IN_WIDTH // (INPROJ_NB * D_MODEL)),
        in_specs=[
            pl.BlockSpec((tm, D_MODEL), lambda i, j: (i, 0)),
            pl.BlockSpec((1, D_MODEL), lambda i, j: (0, 0)),
            pl.BlockSpec(mod.shape, lambda i, j: (0, 0)),
            pl.BlockSpec((D_MODEL, INPROJ_NB * D_MODEL), lambda i, j: (0, j)),
            pl.BlockSpec((2, tm // GRID_W, half), lambda i, j: (0, i, 0)),
            pl.BlockSpec((2, GRID_W, half), lambda i, j: (0, 0, 0)),
            pl.BlockSpec((1, D_MODEL), lambda i, j: (0, 0)),
            pl.BlockSpec((1, D_MODEL), lambda i, j: (0, 0)),
        ],
        out_specs=pl.BlockSpec((tm, INPROJ_NB * D_MODEL), lambda i, j: (i, j)),
        out_shape=jax.ShapeDtypeStruct((T, IN_WIDTH), BF16),
        scratch_shapes=[
            pltpu.VMEM((tm, D_MODEL), BF16),
            pltpu.VMEM((tm, D_MODEL), F32),
            pltpu.VMEM((tm, half), F32),
            pltpu.VMEM((tm, half), F32),
        ],
        compiler_params=_params(("arbitrary", "arbitrary")),
        name="in_proj",
    )(x2d, nw, mod, w_in_bf, rot_row, rot_col, lnw, lnb)


def _sgu_kernel(u_ref, v_ref, ga_ref, ws_ref, bt_ref, wpa_ref, o_ref, ya_scr):
    tm = u_ref.shape[0]
    gd = D_MODEL // SGU_GROUPS
    for n in range(tm // SGU_CHUNK):
        rows = slice(n * SGU_CHUNK, (n + 1) * SGU_CHUNK)
        for g in range(SGU_GROUPS):
            cols = slice(g * gd, (g + 1) * gd)
            mixed = _dot(ws_ref[g], v_ref[rows, cols]) + bt_ref[:, g:g + 1]
            ya_scr[rows, cols] = (u_ref[rows, cols].astype(F32) * mixed).astype(BF16)
    pa = _dot(ya_scr[...], wpa_ref[...])
    o_ref[...] = (ga_ref[...].astype(F32) * pa).astype(BF16)


def _sgu_call(proj, ws_bf, bt, wpa_bf):
    T = proj.shape[0]
    tm = 2048
    return pl.pallas_call(
        _sgu_kernel,
        grid=(T // tm,),
        in_specs=[
            pl.BlockSpec((tm, D_MODEL), lambda i: (i, COL_U)),
            pl.BlockSpec((tm, D_MODEL), lambda i: (i, COL_V)),
            pl.BlockSpec((tm, D_MODEL), lambda i: (i, COL_GA)),
            pl.BlockSpec(ws_bf.shape, lambda i: (0, 0, 0)),
            pl.BlockSpec(bt.shape, lambda i: (0, 0)),
            pl.BlockSpec((D_MODEL, D_MODEL), lambda i: (0, 0)),
        ],
        out_specs=pl.BlockSpec((tm, D_MODEL), lambda i: (i, 0)),
        out_shape=jax.ShapeDtypeStruct((T, D_MODEL), BF16),
        scratch_shapes=[pltpu.VMEM((tm, D_MODEL), BF16)],
        compiler_params=_params(("arbitrary",)),
        name="sgu_proj_a",
    )(proj, proj, proj, ws_bf, bt, wpa_bf)


def _ret_kernel(qf_ref, kf_ref, vf_ref, gf_ref, qb_ref, kb_ref, vb_ref, gb_ref, s0_ref, dec_ref,
                yf_ref, yb_ref, s_scr, intra_scr, qd_scr, kd_scr, cd_scr):
    C = RET_CHUNK
    i = pl.program_id(0)

    @pl.when(i == 0)
    def _():
        lg = jnp.log1p(-jnp.exp2(dec_ref[...]))
        r = lax.broadcasted_iota(jnp.int32, (C, C), 0).astype(F32)
        c = lax.broadcasted_iota(jnp.int32, (C, C), 1).astype(F32)
        pos = lax.broadcasted_iota(jnp.int32, (C, 1), 0).astype(F32)
        for h in range(HEADS):
            lf = lg[0:1, h:h + 1]
            lb = lg[1:2, h:h + 1]
            intra_scr[0, h] = jnp.where(r >= c, jnp.exp(jnp.maximum(r - c, 0.0) * lf), 0.0)
            intra_scr[1, h] = jnp.where(c >= r, jnp.exp(jnp.maximum(c - r, 0.0) * lb), 0.0)
            qd_scr[0, h] = jnp.exp((pos + 1.0) * lf)
            qd_scr[1, h] = jnp.exp((C - pos) * lb)
            kd_scr[0, h] = jnp.exp((C - 1.0 - pos) * lf)
            kd_scr[1, h] = jnp.exp(pos * lb)
            cd_scr[0, h] = jnp.exp(C * lf)
            cd_scr[1, h] = jnp.exp(C * lb)
        s_scr[...] = s0_ref[...]

    dirs = ((qf_ref, kf_ref, vf_ref, gf_ref, yf_ref), (qb_ref, kb_ref, vb_ref, gb_ref, yb_ref))
    nsub = qf_ref.shape[0] // C
    for sub in range(nsub):
        for d, (q_ref, k_ref, v_ref, g_ref, y_ref) in enumerate(dirs):
            c0 = (sub if d == 0 else nsub - 1 - sub) * C
            rows = slice(c0, c0 + C)
            for h in range(HEADS):
                q = q_ref[rows, h * DK:(h + 1) * DK]
                k = k_ref[rows, h * DK:(h + 1) * DK]
                v = v_ref[rows, h * DV:(h + 1) * DV]
                s = s_scr[d, h]
                a = (_dot_nt(q, k) * intra_scr[d, h]).astype(BF16)
                o = _dot(a, v) + _dot(q, s.astype(BF16)) * qd_scr[d, h]
                hn = o * lax.rsqrt(jnp.mean(o * o, axis=-1, keepdims=True) + EPS)
                y_ref[rows, h * DV:(h + 1) * DV] = (g_ref[rows, h * DV:(h + 1) * DV].astype(F32) * hn).astype(BF16)
                kd = (k.astype(F32) * kd_scr[d, h]).astype(BF16)
                s_scr[d, h] = s * cd_scr[d, h] + _dot_tn(kd, v)


def _ret_call(proj, s0, dec):
    T = proj.shape[0]
    C = RET_CHUNK
    rows = RET_SUB * C
    n = T // rows
    qk_w = HEADS * DK
    v_w = HEADS * DV
    fwd = lambda col: (lambda i: (i, col))
    bwd = lambda col: (lambda i: (n - 1 - i, col))
    return pl.pallas_call(
        _ret_kernel,
        grid=(n,),
        in_specs=[
            pl.BlockSpec((rows, qk_w), fwd(COL_Q)),
            pl.BlockSpec((rows, qk_w), fwd(COL_K)),
            pl.BlockSpec((rows, v_w), fwd(COL_VR // 2)),
            pl.BlockSpec((rows, v_w), fwd(COL_GF // 2)),
            pl.BlockSpec((rows, qk_w), bwd(COL_Q)),
            pl.BlockSpec((rows, qk_w), bwd(COL_K)),
            pl.BlockSpec((rows, v_w), bwd(COL_VR // 2)),
            pl.BlockSpec((rows, v_w), bwd(COL_GB // 2)),
            pl.BlockSpec(s0.shape, lambda i: (0, 0, 0, 0)),
            pl.BlockSpec(dec.shape, lambda i: (0, 0)),
        ],
        out_specs=[
            pl.BlockSpec((rows, v_w), lambda i: (i, 0)),
            pl.BlockSpec((rows, v_w), lambda i: (n - 1 - i, 0)),
        ],
        out_shape=[jax.ShapeDtypeStruct((T, v_w), BF16), jax.ShapeDtypeStruct((T, v_w), BF16)],
        scratch_shapes=[
            pltpu.VMEM((2, HEADS, DK, DV), F32),
            pltpu.VMEM((2, HEADS, C, C), F32),
            pltpu.VMEM((2, HEADS, C, 1), F32),
            pltpu.VMEM((2, HEADS, C, 1), F32),
            pltpu.VMEM((2, HEADS, 1, 1), F32),
        ],
        compiler_params=_params(("arbitrary",)),
        name="retention",
    )(proj, proj, proj, proj, proj, proj, proj, proj, s0, dec)


def _merge_kernel(yf_ref, yb_ref, za_ref, gbr_ref, x_ref, mod_ref, nw_ref, wpb_ref, wo_ref, rw_ref, rb_ref,
                  x1_ref, h2_ref, lg_ref):
    yb = yf_ref[...] + yb_ref[...]
    pb = _dot(yb, wpb_ref[...])
    y = za_ref[...].astype(F32) + gbr_ref[...].astype(F32) * pb
    yo = _dot(y.astype(BF16), wo_ref[...])
    x1 = x_ref[...] + _mod_slice(mod_ref, 0, 2) * yo
    x1_ref[...] = x1
    h2 = _rms(x1, nw_ref[...]) * (1.0 + _mod_slice(mod_ref, 0, 4)) + _mod_slice(mod_ref, 0, 3)
    h2_hi = h2.astype(BF16)
    h2_ref[...] = h2_hi
    h2_lo = (h2 - h2_hi.astype(F32)).astype(BF16)
    rw = rw_ref[...]
    rw_hi = rw.astype(BF16)
    rw_lo = (rw - rw_hi.astype(F32)).astype(BF16)
    both = _dot_nt(jnp.concatenate([rw_hi, rw_lo], axis=0), h2_hi)
    lg_ref[...] = both[:N_EXPERTS] + both[N_EXPERTS:] + _dot_nt(rw_hi, h2_lo) + rb_ref[...]


def _merge_call(yf, yb, za, proj, x2d, mod, nw2, wpb_bf, wo_bf, rw, rb):
    T = x2d.shape[0]
    tm = 512
    row = lambda i: (i, 0)
    fixed = lambda i: (0, 0)
    return pl.pallas_call(
        _merge_kernel,
        grid=(T // tm,),
        in_specs=[
            pl.BlockSpec((tm, HEADS * DV), row),
            pl.BlockSpec((tm, HEADS * DV), row),
            pl.BlockSpec((tm, D_MODEL), row),
            pl.BlockSpec((tm, D_MODEL), lambda i: (i, COL_GBR)),
            pl.BlockSpec((tm, D_MODEL), row),
            pl.BlockSpec(mod.shape, fixed),
            pl.BlockSpec((1, D_MODEL), fixed),
            pl.BlockSpec(wpb_bf.shape, fixed),
            pl.BlockSpec(wo_bf.shape, fixed),
            pl.BlockSpec(rw.shape, fixed),
            pl.BlockSpec(rb.shape, fixed),
        ],
        out_specs=[
            pl.BlockSpec((tm, D_MODEL), row),
            pl.BlockSpec((tm, D_MODEL), row),
            pl.BlockSpec((N_EXPERTS, tm), lambda i: (0, i)),
        ],
        out_shape=[
            jax.ShapeDtypeStruct((T, D_MODEL), F32),
            jax.ShapeDtypeStruct((T, D_MODEL), BF16),
            jax.ShapeDtypeStruct((N_EXPERTS, T), F32),
        ],
        compiler_params=_params(("arbitrary",)),
        name="merge_router",
    )(yf, yb, za, proj, x2d, mod, nw2, wpb_bf, wo_bf, rw, rb)


def _route_kernel(lg_ref, pos_ref, tok_ref, nch_ref, su_scr):
    E = lg_ref.shape[0]
    tt = MOE_TT

    @pl.when(pl.program_id(0) == 0)
    def _():
        r = lax.broadcasted_iota(jnp.int32, (tt, tt), 0)
        c = lax.broadcasted_iota(jnp.int32, (tt, tt), 1)
        su_scr[...] = jnp.where(r < c, 1.0, 0.0).astype(BF16)

    for t in range(lg_ref.shape[1] // tt):
        _route_tile(t, slice(t * tt, (t + 1) * tt), lg_ref, pos_ref, tok_ref, nch_ref, su_scr)


def _route_tile(t, toks, lg_ref, pos_ref, tok_ref, nch_ref, su_scr):
    E, tt = lg_ref.shape[0], MOE_TT
    sub = lax.broadcasted_iota(jnp.int32, (E, tt), 0)
    work = lg_ref[:, toks]
    vals, hots = [], []
    for _ in range(TOP_K):
        m = jnp.max(work, axis=0, keepdims=True)
        first = jnp.min(jnp.where(work == m, sub, E), axis=0, keepdims=True)
        hot = sub == first
        vals.append(m)
        hots.append(hot)
        work = jnp.where(hot, -jnp.inf, work)
    exps = [jnp.exp(v - vals[0]) for v in vals]
    inv = 1.0 / functools.reduce(lambda a, b: a + b, exps)

    member = functools.reduce(lambda a, b: a + b, [jnp.where(h, 1.0, 0.0) for h in hots])
    cnt = jnp.sum(member, axis=1, keepdims=True)
    nch = jnp.floor((cnt + (MOE_CH - 1.0)) * (1.0 / MOE_CH))
    nch_b = jnp.broadcast_to(nch, (E, 128))
    rank = _dot(member.astype(BF16), su_scr[...])
    er = lax.broadcasted_iota(jnp.int32, (E, E), 0)
    ec = lax.broadcasted_iota(jnp.int32, (E, E), 1)
    first_chunk = _dot(jnp.where(ec < er, 1.0, 0.0).astype(BF16), nch_b.astype(BF16))[:, 0:1]
    base = first_chunk * MOE_CH + rank
    pos = [jnp.sum(jnp.where(hots[k], base, 0.0), axis=0, keepdims=True) for k in range(TOP_K)]
    for k in range(TOP_K):
        pos_ref[k:k + 1, toks] = pos[k].astype(jnp.int32)
    rows = jnp.concatenate(pos + [e * inv for e in exps] + [jnp.zeros((128 - 2 * TOP_K, tt), F32)], axis=0)
    tok_ref[toks, :] = rows.T
    nch_ref[t] = nch_b.astype(jnp.int32)


def _route_call(logits_t):
    E, T = logits_t.shape
    tt = MOE_TT
    per_step = 4
    assert (T // tt) % per_step == 0
    span = per_step * tt
    return pl.pallas_call(
        _route_kernel,
        grid=(T // span,),
        in_specs=[pl.BlockSpec((E, span), lambda j: (0, j))],
        out_specs=[
            pl.BlockSpec((TOP_K, span), lambda j: (0, j)),
            pl.BlockSpec((span, 128), lambda j: (j, 0)),
            pl.BlockSpec((per_step, E, 128), lambda j: (j, 0, 0)),
        ],
        out_shape=[
            jax.ShapeDtypeStruct((TOP_K, T), jnp.int32),
            jax.ShapeDtypeStruct((T, 128), F32),
            jax.ShapeDtypeStruct((T // tt, E, 128), jnp.int32),
        ],
        scratch_shapes=[pltpu.VMEM((tt, tt), BF16)],
        compiler_params=_params(("arbitrary",)),
        name="moe_route",
    )(logits_t)


def _slot_tiles_max(n_tiles):
    return -(-(n_tiles * MOE_MAXC + N_EXPERTS * (MOE_CPT - 1)) // MOE_CPT)


def _routing_tables(nch):
    n_tiles, E = nch.shape

    def cumsum_last(a):
        m = a.shape[-1]
        keep = jnp.arange(m)[:, None] <= jnp.arange(m)[None, :]
        return jnp.sum(jnp.where(keep, a[..., :, None], 0), axis=-2)

    tot = jnp.sum(nch, axis=0)
    seg = (tot + MOE_CPT - 1) // MOE_CPT * MOE_CPT
    seg_end = cumsum_last(seg)
    seg_start = seg_end - seg
    gstart = seg_start[None, :] + cumsum_last(nch.T).T - nch
    l_end = cumsum_last(nch)
    lstart = l_end - nch

    def copy_list(count, first_local, first_global, step, max_len):
        end = cumsum_last(count)
        start = end - count
        i = jnp.arange(max_len, dtype=jnp.int32)[None, :, None]
        owner = (start[:, None, :] <= i) & (i < end[:, None, :])
        off = (i - start[:, None, :]) * step
        src = jnp.sum(jnp.where(owner, first_local[:, None, :] + off, 0), axis=-1)
        dst = jnp.sum(jnp.where(owner, first_global[:, None, :] + off, 0), axis=-1)
        return src, dst, end[:, -1]

    runs = nch // MOE_RUN
    rsrc, rdst, nruns = copy_list(runs, lstart, gstart, MOE_RUN, MOE_MAXQ)
    ssrc, sdst, nsingles = copy_list(nch - runs * MOE_RUN, lstart + runs * MOE_RUN, gstart + runs * MOE_RUN, 1,
                                     MOE_MAXS)
    copies = jnp.concatenate([rsrc, rdst, ssrc, sdst], axis=1).astype(jnp.int32).reshape(-1)
    ncopies = jnp.stack([nruns, nsingles], axis=1).astype(jnp.int32).reshape(-1)
    nt_max = _slot_tiles_max(n_tiles)
    n_used = (seg_end[-1] // MOE_CPT).astype(jnp.int32)
    tile_blk = jnp.minimum(jnp.arange(nt_max, dtype=jnp.int32), n_used - 1)
    tile_exp = jnp.sum(seg_end[None, :] <= (tile_blk * MOE_CPT)[:, None], axis=1)
    tile_exp = jnp.minimum(tile_exp, E - 1).astype(jnp.int32)
    experts = jnp.arange(E, dtype=jnp.int32)
    own = tile_exp[:, None] == experts[None, :]
    end_tile = jnp.sum(jnp.where(own, (seg_end // MOE_CPT)[None, :], 0), axis=1)
    follows = ((seg_start // MOE_CPT)[None, :] == end_tile[:, None]) & (seg > 0)[None, :]
    next_exp = jnp.where(jnp.any(follows, axis=1), jnp.sum(jnp.where(follows, experts[None, :], 0), axis=1), -1)
    next_exp = next_exp.astype(jnp.int32)
    zstart = (seg_start + tot).astype(jnp.int32)
    zcount = (seg - tot).astype(jnp.int32)
    first_tile = jnp.sum(jnp.where(own, (seg_start // MOE_CPT)[None, :], 0), axis=1)
    own_chunks = jnp.sum(jnp.where(own, tot[None, :], 0), axis=1)
    filled = own_chunks - (jnp.arange(nt_max, dtype=jnp.int32) - first_tile) * MOE_CPT
    per_part = MOE_CPT // MOE_PARTS
    parts = jnp.clip((filled + per_part - 1) // per_part, 1, MOE_PARTS).astype(jnp.int32)
    expert_tables = (tile_exp, tile_blk, n_used.reshape(1), next_exp, parts)
    return copies, ncopies, zstart, zcount, expert_tables


def _chunk_rows(c, nchunks=1):
    return pl.ds(pl.multiple_of(c * MOE_CH, MOE_CH), nchunks * MOE_CH)


def _for_tile_copies(cp_ref, cn_ref, tile, make, op):
    base = tile * MOE_CPROW

    def run(i, carry):
        op(make(cp_ref[base + i], cp_ref[base + MOE_MAXQ + i], MOE_RUN))
        return carry
    lax.fori_loop(0, cn_ref[2 * tile], run, 0)

    def single(i, carry):
        op(make(cp_ref[base + 2 * MOE_MAXQ + i], cp_ref[base + 2 * MOE_MAXQ + MOE_MAXS + i], 1))
        return carry
    lax.fori_loop(0, cn_ref[2 * tile + 1], single, 0)


def _dispatch_kernel(cp_ref, cn_ref, zstart_ref, zcount_ref, nu_ref, h2_ref, pos_ref, xs_hbm, xl_scr, zero_scr,
                     sems, zsem):
    j = pl.program_id(0)
    n = pl.num_programs(0)
    slot = j % 2
    tt = h2_ref.shape[0]
    nt_max = xs_hbm.shape[0] // MOE_TM

    def tile_copies(tile, slot_, op):
        def make(local, glob, nchunks):
            return pltpu.make_async_copy(xl_scr.at[slot_, _chunk_rows(local, nchunks)],
                                         xs_hbm.at[_chunk_rows(glob, nchunks)], sems.at[slot_])
        _for_tile_copies(cp_ref, cn_ref, tile, make, op)

    def zero_chunk_copy(d, nchunks):
        return pltpu.make_async_copy(zero_scr.at[pl.ds(0, nchunks * MOE_CH)], xs_hbm.at[_chunk_rows(d, nchunks)],
                                     zsem.at[0])

    def zero_tile_copy(t):
        rows = pl.ds(pl.multiple_of(t * MOE_TM, MOE_TM), MOE_TM)
        return pltpu.make_async_copy(zero_scr, xs_hbm.at[rows], zsem.at[0])

    def zero_fill(op):
        def expert(e, carry):
            nruns = zcount_ref[e] // MOE_RUN

            def run(r, carry_):
                op(zero_chunk_copy(zstart_ref[e] + r * MOE_RUN, MOE_RUN))
                return carry_
            carry = lax.fori_loop(0, nruns, run, carry)

            def single(r, carry_):
                op(zero_chunk_copy(zstart_ref[e] + nruns * MOE_RUN + r, 1))
                return carry_
            return lax.fori_loop(0, zcount_ref[e] - nruns * MOE_RUN, single, carry)
        lax.fori_loop(0, N_EXPERTS, expert, 0)

        def tile(t, carry):
            op(zero_tile_copy(t))
            return carry
        lax.fori_loop(nu_ref[0], nt_max, tile, 0)

    def wait_tile(tile, slot_):
        tile_copies(tile, slot_, lambda cp: cp.wait())

    @pl.when(j == 0)
    def _():
        zero_scr[...] = jnp.zeros_like(zero_scr)
        zero_fill(lambda cp: cp.start())

    @pl.when(j >= 2)
    def _():
        wait_tile(j - 2, slot)

    pos = pos_ref[...]
    h2 = h2_ref[...]
    for rb in range(MOE_RLOC // MOE_RB):
        io = lax.broadcasted_iota(jnp.int32, (MOE_RB, tt), 0) + rb * MOE_RB
        onehot = jnp.zeros((MOE_RB, tt), F32)
        for k in range(TOP_K):
            onehot = jnp.where(io == pos[k:k + 1, :], 1.0, onehot)
        xl_scr[slot, rb * MOE_RB:(rb + 1) * MOE_RB, :] = _dot(onehot.astype(BF16), h2).astype(BF16)

    tile_copies(j, slot, lambda cp: cp.start())

    @pl.when(j == n - 1)
    def _():
        @pl.when(j >= 1)
        def _():
            wait_tile(j - 1, 1 - slot)
        wait_tile(j, slot)
        zero_fill(lambda cp: cp.wait())


def _dispatch_call(tables, h2, pos_t, n_slots):
    T = h2.shape[0]
    tt = MOE_TT
    copies, ncopies, zstart, zcount, n_used = tables
    grid_spec = pltpu.PrefetchScalarGridSpec(
        num_scalar_prefetch=5,
        grid=(T // tt,),
        in_specs=[
            pl.BlockSpec((tt, D_MODEL), lambda j, *_: (j, 0)),
            pl.BlockSpec((TOP_K, tt), lambda j, *_: (0, j)),
        ],
        out_specs=pl.BlockSpec(memory_space=pl.ANY),
        scratch_shapes=[
            pltpu.VMEM((2, MOE_RLOC, D_MODEL), BF16),
            pltpu.VMEM((MOE_TM, D_MODEL), BF16),
            pltpu.SemaphoreType.DMA((2,)),
            pltpu.SemaphoreType.DMA((1,)),
        ],
    )
    return pl.pallas_call(
        _dispatch_kernel,
        grid_spec=grid_spec,
        out_shape=jax.ShapeDtypeStruct((n_slots, D_MODEL), BF16),
        compiler_params=_params(("arbitrary",)),
        name="moe_dispatch",
    )(copies, ncopies, zstart, zcount, n_used, h2, pos_t)


def _experts_kernel(te_ref, tb_ref, nu_ref, nx_ref, parts_ref, xs_ref, w1_hbm, b1_ref, w2_hbm, b2_ref, y_ref,
                    w1_f32, w2_f32, w1_scr, w2_scr, sems):
    i = pl.program_id(0)
    valid = i < nu_ref[0]
    new_expert = (i == 0) | (te_ref[i] != te_ref[jnp.maximum(i - 1, 0)])

    def fetch(e):
        return (pltpu.make_async_copy(w1_hbm.at[e], w1_f32, sems.at[0]),
                pltpu.make_async_copy(w2_hbm.at[e], w2_f32, sems.at[1]))

    @pl.when(i == 0)
    def _():
        for cp in fetch(te_ref[0]):
            cp.start()

    @pl.when(valid & new_expert)
    def _():
        for cp in fetch(te_ref[i]):
            cp.wait()
        w1_scr[...] = w1_f32[...].astype(BF16)
        w2_scr[...] = w2_f32[...].astype(BF16)

        @pl.when(nx_ref[i] >= 0)
        def _():
            for cp in fetch(nx_ref[i]):
                cp.start()

    def mlp(rows):
        hh = _dot(xs_ref[rows, :], w1_scr[...]) + b1_ref[...]
        gate = jnp.minimum(hh[:, :D_FF], SWIGLU_LIMIT)
        up = jnp.clip(hh[:, D_FF:], -SWIGLU_LIMIT, SWIGLU_LIMIT)
        act = (up + 1.0) * gate * _sigmoid(SWIGLU_ALPHA * gate)
        y_ref[rows, :] = (_dot(act.astype(BF16), w2_scr[...]) + b2_ref[...]).astype(BF16)

    for nparts in range(1, MOE_PARTS + 1):
        @pl.when(valid & (parts_ref[i] == nparts))
        def _():
            rows = nparts * (MOE_TM // MOE_PARTS)
            mlp(slice(0, rows))
            if rows < MOE_TM:
                y_ref[rows:, :] = jnp.zeros((MOE_TM - rows, D_MODEL), BF16)


def _experts_call(tables, xs, w1, b1, w2, b2):
    tile_exp = tables[0]
    nt_max = tile_exp.shape[0]
    slot_tile = lambda i, te, tb, *_: (tb[i], 0)
    per_e = lambda i, te, *_: (te[i], 0, 0)
    grid_spec = pltpu.PrefetchScalarGridSpec(
        num_scalar_prefetch=len(tables),
        grid=(nt_max,),
        in_specs=[
            pl.BlockSpec((MOE_TM, D_MODEL), slot_tile),
            pl.BlockSpec(memory_space=pl.ANY),
            pl.BlockSpec((None, 1, 2 * D_FF), per_e),
            pl.BlockSpec(memory_space=pl.ANY),
            pl.BlockSpec((None, 1, D_MODEL), per_e),
        ],
        out_specs=pl.BlockSpec((MOE_TM, D_MODEL), slot_tile),
        scratch_shapes=[
            pltpu.VMEM((D_MODEL, 2 * D_FF), F32),
            pltpu.VMEM((D_FF, D_MODEL), F32),
            pltpu.VMEM((D_MODEL, 2 * D_FF), BF16),
            pltpu.VMEM((D_FF, D_MODEL), BF16),
            pltpu.SemaphoreType.DMA((2,)),
        ],
    )
    return pl.pallas_call(
        _experts_kernel,
        grid_spec=grid_spec,
        out_shape=jax.ShapeDtypeStruct(xs.shape, BF16),
        input_output_aliases={len(tables): 0},
        compiler_params=_params(("arbitrary",)),
        name="moe_experts",
    )(*tables, xs, w1, b1, w2, b2)


def _combine_kernel(cp_ref, cn_ref, tok_ref, x1_ref, mod_ref, fw_ref, y_hbm, o_ref, yl_scr, w_scr, sems):
    j = pl.program_id(0)
    n = pl.num_programs(0)
    slot = j % 2
    tt = x1_ref.shape[0]

    def tile_copies(tile, slot_, op):
        def make(local, glob, nchunks):
            return pltpu.make_async_copy(y_hbm.at[_chunk_rows(glob, nchunks)],
                                         yl_scr.at[slot_, _chunk_rows(local, nchunks)], sems.at[slot_])
        _for_tile_copies(cp_ref, cn_ref, tile, make, op)

    def start_tile(tile, slot_):
        tile_copies(tile, slot_, lambda cp: cp.start())

    @pl.when(j == 0)
    def _():
        yl_scr[...] = jnp.zeros_like(yl_scr)
        start_tile(0, 0)

    @pl.when(j + 1 < n)
    def _():
        start_tile(j + 1, 1 - slot)

    tile_copies(j, slot, lambda cp: cp.wait())

    pos = tok_ref[:, 0:TOP_K].astype(jnp.int32)
    p = tok_ref[:, TOP_K:2 * TOP_K]
    for cb in range(MOE_RLOC // MOE_RB):
        io = lax.broadcasted_iota(jnp.int32, (tt, MOE_RB), 1) + cb * MOE_RB
        w = jnp.zeros((tt, MOE_RB), F32)
        for k in range(TOP_K):
            w = jnp.where(io == pos[:, k:k + 1], p[:, k:k + 1], w)
        w_scr[:, cb * MOE_RB:(cb + 1) * MOE_RB] = w.astype(BF16)
    moe = _dot(w_scr[...], yl_scr[slot])
    x2 = x1_ref[...] + _mod_slice(mod_ref, 0, 5) * moe
    o_ref[...] = _rms(x2, fw_ref[...])


def _combine_call(copies, ncopies, tok, x1, mod, fw, y):
    T = x1.shape[0]
    tt = MOE_TT
    row = lambda j, *_: (j, 0)
    fixed = lambda j, *_: (0, 0)
    grid_spec = pltpu.PrefetchScalarGridSpec(
        num_scalar_prefetch=2,
        grid=(T // tt,),
        in_specs=[
            pl.BlockSpec((tt, 128), row),
            pl.BlockSpec((tt, D_MODEL), row),
            pl.BlockSpec(mod.shape, fixed),
            pl.BlockSpec((1, D_MODEL), fixed),
            pl.BlockSpec(memory_space=pl.ANY),
        ],
        out_specs=pl.BlockSpec((tt, D_MODEL), row),
        scratch_shapes=[
            pltpu.VMEM((2, MOE_RLOC, D_MODEL), BF16),
            pltpu.VMEM((tt, MOE_RLOC), BF16),
            pltpu.SemaphoreType.DMA((2,)),
        ],
    )
    return pl.pallas_call(
        _combine_kernel,
        grid_spec=grid_spec,
        out_shape=jax.ShapeDtypeStruct((T, D_MODEL), F32),
        compiler_params=_params(("arbitrary",)),
        name="moe_combine",
    )(copies, ncopies, tok, x1, mod, fw, y)


def _rope_tables(T):
    rows = T // GRID_W
    n_freq = DK // 4
    inv = ROPE_BASE ** (-jnp.arange(n_freq, dtype=F32) / n_freq)
    row_ang = jnp.arange(rows, dtype=F32)[:, None] * inv
    col_ang = jnp.arange(GRID_W, dtype=F32)[:, None] * inv
    rot_row = jnp.pad(jnp.stack([jnp.cos(row_ang), jnp.sin(row_ang)]), ((0, 0), (0, 0), (0, n_freq)))
    rot_col = jnp.pad(jnp.stack([jnp.cos(col_ang), jnp.sin(col_ang)]), ((0, 0), (0, 0), (n_freq, 0)))
    return rot_row, rot_col


def kernel(x, c, ctx, c_ctx, w_mod, b_mod, norm1_w, norm2_w, w_in, sgu_ln_w, sgu_ln_b, sgu_w, sgu_b,
           ret_decay_fwd, ret_decay_bwd, w_proj_a, w_proj_b, w_out, router_w, router_b,
           moe_w1, moe_b1, moe_w2, moe_b2, final_norm_w):
    B, T, D = x.shape
    assert B == 1 and D == D_MODEL and w_mod.shape[0] == 1 and T % 1024 == 0
    x2d = x.reshape(T, D)
    cc = jnp.stack([c.reshape(D), c_ctx.reshape(D)], axis=1)
    mod = _mod_call(cc, w_mod[0], b_mod)
    dec = jnp.stack([ret_decay_fwd[0], ret_decay_bwd[0]], axis=0)
    w_in_bf = w_in[0].astype(BF16)
    s0 = _ctx_call(ctx.reshape(ctx.shape[1], D), norm1_w, mod, w_in_bf, dec)
    rot_row, rot_col = _rope_tables(T)
    proj = _inproj_call(x2d, norm1_w, mod, w_in_bf, rot_row, rot_col, sgu_ln_w, sgu_ln_b)
    za = _sgu_call(proj, sgu_w[0].astype(BF16), sgu_b[0].T, w_proj_a[0].astype(BF16))
    yf, yb = _ret_call(proj, s0, dec)
    x1, h2, logits_t = _merge_call(yf, yb, za, proj, x2d, mod, norm2_w, w_proj_b[0].astype(BF16),
                                   w_out[0].astype(BF16), router_w[0].T, router_b.reshape(N_EXPERTS, 1))
    pos_t, tok, nch = _route_call(logits_t)
    copies, ncopies, zstart, zcount, expert_tables = _routing_tables(nch[:, :, 0])
    n_slots = expert_tables[0].shape[0] * MOE_TM
    xs = _dispatch_call((copies, ncopies, zstart, zcount, expert_tables[2]), h2, pos_t, n_slots)
    y = _experts_call(expert_tables, xs, moe_w1[0], moe_b1[0][:, None, :], moe_w2[0], moe_b2[0][:, None, :])
    out = _combine_call(copies, ncopies, tok, x1, mod, final_norm_w.reshape(1, D), y)
    return out.reshape(B, T, D)
```

```python
import functools

import jax
import jax.numpy as jnp
from jax import lax
from jax.experimental import pallas as pl
from jax.experimental.pallas import tpu as pltpu

D_MODEL = 1024
GRID_W = 64
SGU_CHUNK = 128
SGU_GROUPS = 8
HEADS = 4
DK = D_MODEL // HEADS
DV = 2 * DK
ROPE_BASE = 10000.0
N_EXPERTS = 32
TOP_K = 4
D_FF = D_MODEL
SWIGLU_LIMIT = 7.0
SWIGLU_ALPHA = 1.702
EPS = 1e-6
IN_WIDTH = 12 * D_MODEL
COL_U, COL_V, COL_Q, COL_K, COL_VR, COL_GF, COL_GB, COL_GA, COL_GBR = 0, 1, 2, 3, 4, 6, 8, 10, 11

INPROJ_NB = 4
RET_CHUNK = 256
RET_SUB = 1
VMEM_LIMIT = 56 * 1024 * 1024

MOE_TT = 512
MOE_CH = 16
MOE_TM = 512
MOE_CPT = MOE_TM // MOE_CH
MOE_PARTS = 8
MOE_MAXC = TOP_K * MOE_TT // MOE_CH + N_EXPERTS
MOE_RLOC = MOE_MAXC * MOE_CH
MOE_RB = 256
MOE_RUN = 4
MOE_MAXQ = MOE_MAXC // MOE_RUN
MOE_MAXS = (MOE_RUN - 1) * N_EXPERTS
MOE_CPROW = 2 * (MOE_MAXQ + MOE_MAXS)

F32 = jnp.float32
BF16 = jnp.bfloat16


def _params(sem):
    return pltpu.CompilerParams(dimension_semantics=sem, vmem_limit_bytes=VMEM_LIMIT)


def _dot(a, b):
    return jnp.dot(a, b, preferred_element_type=F32)


def _dot_nt(a, b):
    return lax.dot_general(a, b, (((1,), (1,)), ((), ())), preferred_element_type=F32)


def _dot_tn(a, b):
    return lax.dot_general(a, b, (((0,), (0,)), ((), ())), preferred_element_type=F32)


def _rms(x, w):
    return x * lax.rsqrt(jnp.mean(x * x, axis=-1, keepdims=True) + EPS) * w


def _gelu(x):
    return 0.5 * x * (1.0 + lax.erf(x * (2.0 ** -0.5)))


def _sigmoid(x):
    return 0.5 * jnp.tanh(0.5 * x) + 0.5


def _mod_slice(mod_ref, row, k):
    return mod_ref[row:row + 1, k * D_MODEL:(k + 1) * D_MODEL]


def _mod_kernel(cc_ref, w_ref, b_ref, o_ref):
    s = cc_ref[...]
    s = s * jax.nn.sigmoid(s)
    w = w_ref[...]
    r0 = jnp.sum(s[:, 0:1] * w, axis=0, keepdims=True)
    r1 = jnp.sum(s[:, 1:2] * w, axis=0, keepdims=True)
    o_ref[...] = jnp.concatenate([r0, r1], axis=0) + b_ref[...]


def _mod_call(cc, w_mod, b_mod):
    bn = 768
    n = w_mod.shape[1]
    return pl.pallas_call(
        _mod_kernel,
        grid=(n // bn,),
        in_specs=[
            pl.BlockSpec((D_MODEL, 2), lambda j: (0, 0)),
            pl.BlockSpec((D_MODEL, bn), lambda j: (0, j)),
            pl.BlockSpec((1, bn), lambda j: (0, j)),
        ],
        out_specs=pl.BlockSpec((2, bn), lambda j: (0, j)),
        out_shape=jax.ShapeDtypeStruct((2, n), F32),
        compiler_params=_params(("arbitrary",)),
        name="mod_vectors",
    )(cc, w_mod, b_mod)


def _ctx_kernel(ctx_ref, nw_ref, mod_ref, w_ref, dec_ref, s0_ref):
    L = ctx_ref.shape[0]
    sh = _mod_slice(mod_ref, 1, 0)
    sc = _mod_slice(mod_ref, 1, 1)
    hc = (_rms(ctx_ref[...], nw_ref[...]) * (1.0 + sc) + sh).astype(BF16)
    kv = _dot(hc, w_ref[...])
    lg = jnp.log1p(-jnp.exp2(dec_ref[...]))
    pos = lax.broadcasted_iota(jnp.int32, (L, 1), 0).astype(F32)
    for h in range(HEADS):
        k = kv[:, h * DK:(h + 1) * DK] * (DK ** -0.5)
        v = kv[:, HEADS * DK + h * DV:HEADS * DK + (h + 1) * DV].astype(BF16)
        wf = jnp.exp((L - 1.0 - pos) * lg[0:1, h:h + 1])
        wb = jnp.exp(pos * lg[1:2, h:h + 1])
        s0_ref[0, h] = _dot_tn((k * wf).astype(BF16), v)
        s0_ref[1, h] = _dot_tn((k * wb).astype(BF16), v)


def _ctx_call(ctx2d, nw, mod, w_in_bf, dec):
    L = ctx2d.shape[0]
    wcols = HEADS * DK + HEADS * DV
    return pl.pallas_call(
        _ctx_kernel,
        grid=(1,),
        in_specs=[
            pl.BlockSpec((L, D_MODEL), lambda i: (0, 0)),
            pl.BlockSpec((1, D_MODEL), lambda i: (0, 0)),
            pl.BlockSpec(mod.shape, lambda i: (0, 0)),
            pl.BlockSpec((D_MODEL, wcols), lambda i: (0, COL_K * D_MODEL // wcols)),
            pl.BlockSpec(dec.shape, lambda i: (0, 0)),
        ],
        out_specs=pl.BlockSpec((2, HEADS, DK, DV), lambda i: (0, 0, 0, 0)),
        out_shape=jax.ShapeDtypeStruct((2, HEADS, DK, DV), F32),
        compiler_params=_params(("arbitrary",)),
        name="ctx_states",
    )(ctx2d, nw, mod, w_in_bf, dec)


def _inproj_kernel(x_ref, nw_ref, mod_ref, w_ref, rot_row_ref, rot_col_ref, lnw_ref, lnb_ref, o_ref,
                   h_scr, g_scr, cos_ref, sin_ref):
    j = pl.program_id(1)
    tm = h_scr.shape[0]

    @pl.when(j == 0)
    def _():
        sh = _mod_slice(mod_ref, 0, 0)
        sc = _mod_slice(mod_ref, 0, 1)
        h_scr[...] = (_rms(x_ref[...], nw_ref[...]) * (1.0 + sc) + sh).astype(BF16)

    def pieces(emit, span, rsplit=1):
        for c0 in range(span.start, span.stop, DK):
            cols = slice(c0, c0 + DK)
            for r in range(rsplit):
                rows = slice(r * tm // rsplit, (r + 1) * tm // rsplit)
                emit(rows, cols, _dot(h_scr[rows, :], w_ref[:, cols]))

    def store(fn):
        def emit(rows, cols, acc):
            o_ref[rows, cols] = fn(acc).astype(BF16)
        return emit

    def rope(scale):
        half = DK // 2

        def emit(rows, cols, acc):
            cos = cos_ref[rows, :]
            sin = sin_ref[rows, :]
            x1 = acc[:, :half]
            x2 = acc[:, half:]
            o_ref[rows, cols.start:cols.start + half] = ((x1 * cos - x2 * sin) * scale).astype(BF16)
            o_ref[rows, cols.start + half:cols.stop] = ((x1 * sin + x2 * cos) * scale).astype(BF16)
        return emit

    def gelu_layer_norm(span):
        def emit(rows, cols, acc):
            g_scr[rows, cols.start - span.start:cols.stop - span.start] = _gelu(acc)
        pieces(emit, span)
        g = g_scr[...]
        mu = jnp.mean(g, axis=-1, keepdims=True)
        d = g - mu
        var = jnp.mean(d * d, axis=-1, keepdims=True)
        o_ref[:, span] = (d * lax.rsqrt(var + EPS) * lnw_ref[...] + lnb_ref[...]).astype(BF16)

    def rotary_tables():
        for r in range(tm // GRID_W):
            tok = slice(r * GRID_W, (r + 1) * GRID_W)
            cos_ref[tok, :] = rot_row_ref[0, r:r + 1, :] + rot_col_ref[0]
            sin_ref[tok, :] = rot_row_ref[1, r:r + 1, :] + rot_col_ref[1]

    def column_block(block, span):
        if block == COL_U:
            pieces(store(_gelu), span)
        elif block == COL_V:
            gelu_layer_norm(span)
        elif block == COL_Q:
            rotary_tables()
            pieces(rope(1.0), span)
        elif block == COL_K:
            pieces(rope(DK ** -0.5), span)
        elif block < COL_GF:
            pieces(store(lambda a: a), span)
        elif block < COL_GA:
            pieces(store(lambda a: a * _sigmoid(a)), span, rsplit=2)
        else:
            pieces(store(_sigmoid), span, rsplit=2)

    for step in range(IN_WIDTH // D_MODEL // INPROJ_NB):
        @pl.when(j == step)
        def _():
            for b in range(INPROJ_NB):
                column_block(step * INPROJ_NB + b, slice(b * D_MODEL, (b + 1) * D_MODEL))


def _inproj_call(x2d, nw, mod, w_in_bf, rot_row, rot_col, lnw, lnb):
    T = x2d.shape[0]
    tm = 1024
    half = DK // 2
    return pl.pallas_call(
        _inproj_kernel,
        grid=(T // tm, IN_WIDTH // (INPROJ_NB * D_MODEL)),
        in_specs=[
            pl.BlockSpec((tm, D_MODEL), lambda i, j: (i, 0)),
            pl.BlockSpec((1, D_MODEL), lambda i, j: (0, 0)),
            pl.BlockSpec(mod.shape, lambda i, j: (0, 0)),
            pl.BlockSpec((D_MODEL, INPROJ_NB * D_MODEL), lambda i, j: (0, j)),
            pl.BlockSpec((2, tm // GRID_W, half), lambda i, j: (0, i, 0)),
            pl.BlockSpec((2, GRID_W, half), lambda i, j: (0, 0, 0)),
            pl.BlockSpec((1, D_MODEL), lambda i, j: (0, 0)),
            pl.BlockSpec((1, D_MODEL), lambda i, j: (0, 0)),
        ],
        out_specs=pl.BlockSpec((tm, INPROJ_NB * D_MODEL), lambda i, j: (i, j)),
        out_shape=jax.ShapeDtypeStruct((T, IN_WIDTH), BF16),
        scratch_shapes=[
            pltpu.VMEM((tm, D_MODEL), BF16),
            pltpu.VMEM((tm, D_MODEL), F32),
            pltpu.VMEM((tm, half), F32),
            pltpu.VMEM((tm, half), F32),
        ],
        compiler_params=_params(("arbitrary", "arbitrary")),
        name="in_proj",
    )(x2d, nw, mod, w_in_bf, rot_row, rot_col, lnw, lnb)


def _sgu_kernel(u_ref, v_ref, ga_ref, ws_ref, bt_ref, wpa_ref, o_ref, ya_scr):
    tm = u_ref.shape[0]
    gd = D_MODEL // SGU_GROUPS
    for n in range(tm // SGU_CHUNK):
        rows = slice(n * SGU_CHUNK, (n + 1) * SGU_CHUNK)
        for g in range(SGU_GROUPS):
            cols = slice(g * gd, (g + 1) * gd)
            mixed = _dot(ws_ref[g], v_ref[rows, cols]) + bt_ref[:, g:g + 1]
            ya_scr[rows, cols] = (u_ref[rows, cols].astype(F32) * mixed).astype(BF16)
    pa = _dot(ya_scr[...], wpa_ref[...])
    o_ref[...] = (ga_ref[...].astype(F32) * pa).astype(BF16)


def _sgu_call(proj, ws_bf, bt, wpa_bf):
    T = proj.shape[0]
    tm = 2048
    return pl.pallas_call(
        _sgu_kernel,
        grid=(T // tm,),
        in_specs=[
            pl.BlockSpec((tm, D_MODEL), lambda i: (i, COL_U)),
            pl.BlockSpec((tm, D_MODEL), lambda i: (i, COL_V)),
            pl.BlockSpec((tm, D_MODEL), lambda i: (i, COL_GA)),
            pl.BlockSpec(ws_bf.shape, lambda i: (0, 0, 0)),
            pl.BlockSpec(bt.shape, lambda i: (0, 0)),
            pl.BlockSpec((D_MODEL, D_MODEL), lambda i: (0, 0)),
        ],
        out_specs=pl.BlockSpec((tm, D_MODEL), lambda i: (i, 0)),
        out_shape=jax.ShapeDtypeStruct((T, D_MODEL), BF16),
        scratch_shapes=[pltpu.VMEM((tm, D_MODEL), BF16)],
        compiler_params=_params(("arbitrary",)),
        name="sgu_proj_a",
    )(proj, proj, proj, ws_bf, bt, wpa_bf)


def _ret_kernel(qf_ref, kf_ref, vf_ref, gf_ref, qb_ref, kb_ref, vb_ref, gb_ref, s0_ref, dec_ref,
                yf_ref, yb_ref, s_scr, intra_scr, qd_scr, kd_scr, cd_scr):
    C = RET_CHUNK
    i = pl.program_id(0)

    @pl.when(i == 0)
    def _():
        lg = jnp.log1p(-jnp.exp2(dec_ref[...]))
        r = lax.broadcasted_iota(jnp.int32, (C, C), 0).astype(F32)
        c = lax.broadcasted_iota(jnp.int32, (C, C), 1).astype(F32)
        pos = lax.broadcasted_iota(jnp.int32, (C, 1), 0).astype(F32)
        for h in range(HEADS):
            lf = lg[0:1, h:h + 1]
            lb = lg[1:2, h:h + 1]
            intra_scr[0, h] = jnp.where(r >= c, jnp.exp(jnp.maximum(r - c, 0.0) * lf), 0.0)
            intra_scr[1, h] = jnp.where(c >= r, jnp.exp(jnp.maximum(c - r, 0.0) * lb), 0.0)
            qd_scr[0, h] = jnp.exp((pos + 1.0) * lf)
            qd_scr[1, h] = jnp.exp((C - pos) * lb)
            kd_scr[0, h] = jnp.exp((C - 1.0 - pos) * lf)
            kd_scr[1, h] = jnp.exp(pos * lb)
            cd_scr[0, h] = jnp.exp(C * lf)
            cd_scr[1, h] = jnp.exp(C * lb)
        s_scr[...] = s0_ref[...]

    dirs = ((qf_ref, kf_ref, vf_ref, gf_ref, yf_ref), (qb_ref, kb_ref, vb_ref, gb_ref, yb_ref))
    nsub = qf_ref.shape[0] // C
    for sub in range(nsub):
        for d, (q_ref, k_ref, v_ref, g_ref, y_ref) in enumerate(dirs):
            c0 = (sub if d == 0 else nsub - 1 - sub) * C
            rows = slice(c0, c0 + C)
            for h in range(HEADS):
                q = q_ref[rows, h * DK:(h + 1) * DK]
                k = k_ref[rows, h * DK:(h + 1) * DK]
                v = v_ref[rows, h * DV:(h + 1) * DV]
                s = s_scr[d, h]
                a = (_dot_nt(q, k) * intra_scr[d, h]).astype(BF16)
                o = _dot(a, v) + _dot(q, s.astype(BF16)) * qd_scr[d, h]
                hn = o * lax.rsqrt(jnp.mean(o * o, axis=-1, keepdims=True) + EPS)
                y_ref[rows, h * DV:(h + 1) * DV] = (g_ref[rows, h * DV:(h + 1) * DV].astype(F32) * hn).astype(BF16)
                kd = (k.astype(F32) * kd_scr[d, h]).astype(BF16)
                s_scr[d, h] = s * cd_scr[d, h] + _dot_tn(kd, v)


def _ret_call(proj, s0, dec):
    T = proj.shape[0]
    C = RET_CHUNK
    rows = RET_SUB * C
    n = T // rows
    qk_w = HEADS * DK
    v_w = HEADS * DV
    fwd = lambda col: (lambda i: (i, col))
    bwd = lambda col: (lambda i: (n - 1 - i, col))
    return pl.pallas_call(
        _ret_kernel,
        grid=(n,),
        in_specs=[
            pl.BlockSpec((rows, qk_w), fwd(COL_Q)),
            pl.BlockSpec((rows, qk_w), fwd(COL_K)),
            pl.BlockSpec((rows, v_w), fwd(COL_VR // 2)),
            pl.BlockSpec((rows, v_w), fwd(COL_GF // 2)),
            pl.BlockSpec((rows, qk_w), bwd(COL_Q)),
            pl.BlockSpec((rows, qk_w), bwd(COL_K)),
            pl.BlockSpec((rows, v_w), bwd(COL_VR // 2)),
            pl.BlockSpec((rows, v_w), bwd(COL_GB // 2)),
            pl.BlockSpec(s0.shape, lambda i: (0, 0, 0, 0)),
            pl.BlockSpec(dec.shape, lambda i: (0, 0)),
        ],
        out_specs=[
            pl.BlockSpec((rows, v_w), lambda i: (i, 0)),
            pl.BlockSpec((rows, v_w), lambda i: (n - 1 - i, 0)),
        ],
        out_shape=[jax.ShapeDtypeStruct((T, v_w), BF16), jax.ShapeDtypeStruct((T, v_w), BF16)],
        scratch_shapes=[
            pltpu.VMEM((2, HEADS, DK, DV), F32),
            pltpu.VMEM((2, HEADS, C, C), F32),
            pltpu.VMEM((2, HEADS, C, 1), F32),
            pltpu.VMEM((2, HEADS, C, 1), F32),
            pltpu.VMEM((2, HEADS, 1, 1), F32),
        ],
        compiler_params=_params(("arbitrary",)),
        name="retention",
    )(proj, proj, proj, proj, proj, proj, proj, proj, s0, dec)


def _merge_kernel(yf_ref, yb_ref, za_ref, gbr_ref, x_ref, mod_ref, nw_ref, wpb_ref, wo_ref, rw_ref, rb_ref,
                  x1_ref, h2_ref, lg_ref):
    yb = yf_ref[...] + yb_ref[...]
    pb = _dot(yb, wpb_ref[...])
    y = za_ref[...].astype(F32) + gbr_ref[...].astype(F32) * pb
    yo = _dot(y.astype(BF16), wo_ref[...])
    x1 = x_ref[...] + _mod_slice(mod_ref, 0, 2) * yo
    x1_ref[...] = x1
    h2 = _rms(x1, nw_ref[...]) * (1.0 + _mod_slice(mod_ref, 0, 4)) + _mod_slice(mod_ref, 0, 3)
    h2_hi = h2.astype(BF16)
    h2_ref[...] = h2_hi
    h2_lo = (h2 - h2_hi.astype(F32)).astype(BF16)
    rw = rw_ref[...]
    rw_hi = rw.astype(BF16)
    rw_lo = (rw - rw_hi.astype(F32)).astype(BF16)
    both = _dot_nt(jnp.concatenate([rw_hi, rw_lo], axis=0), h2_hi)
    lg_ref[...] = both[:N_EXPERTS] + both[N_EXPERTS:] + _dot_nt(rw_hi, h2_lo) + rb_ref[...]


def _merge_call(yf, yb, za, proj, x2d, mod, nw2, wpb_bf, wo_bf, rw, rb):
    T = x2d.shape[0]
    tm = 512
    row = lambda i: (i, 0)
    fixed = lambda i: (0, 0)
    return pl.pallas_call(
        _merge_kernel,
        grid=(T // tm,),
        in_specs=[
            pl.BlockSpec((tm, HEADS * DV), row),
            pl.BlockSpec((tm, HEADS * DV), row),
            pl.BlockSpec((tm, D_MODEL), row),
            pl.BlockSpec((tm, D_MODEL), lambda i: (i, COL_GBR)),
            pl.BlockSpec((tm, D_MODEL), row),
            pl.BlockSpec(mod.shape, fixed),
            pl.BlockSpec((1, D_MODEL), fixed),
            pl.BlockSpec(wpb_bf.shape, fixed),
            pl.BlockSpec(wo_bf.shape, fixed),
            pl.BlockSpec(rw.shape, fixed),
            pl.BlockSpec(rb.shape, fixed),
        ],
        out_specs=[
            pl.BlockSpec((tm, D_MODEL), row),
            pl.BlockSpec((tm, D_MODEL), row),
            pl.BlockSpec((N_EXPERTS, tm), lambda i: (0, i)),
        ],
        out_shape=[
            jax.ShapeDtypeStruct((T, D_MODEL), F32),
            jax.ShapeDtypeStruct((T, D_MODEL), BF16),
            jax.ShapeDtypeStruct((N_EXPERTS, T), F32),
        ],
        compiler_params=_params(("arbitrary",)),
        name="merge_router",
    )(yf, yb, za, proj, x2d, mod, nw2, wpb_bf, wo_bf, rw, rb)


def _route_kernel(lg_ref, pos_ref, tok_ref, nch_ref, su_scr):
    E = lg_ref.shape[0]
    tt = MOE_TT

    @pl.when(pl.program_id(0) == 0)
    def _():
        r = lax.broadcasted_iota(jnp.int32, (tt, tt), 0)
        c = lax.broadcasted_iota(jnp.int32, (tt, tt), 1)
        su_scr[...] = jnp.where(r < c, 1.0, 0.0).astype(BF16)

    for t in range(lg_ref.shape[1] // tt):
        _route_tile(t, slice(t * tt, (t + 1) * tt), lg_ref, pos_ref, tok_ref, nch_ref, su_scr)


def _route_tile(t, toks, lg_ref, pos_ref, tok_ref, nch_ref, su_scr):
    E, tt = lg_ref.shape[0], MOE_TT
    sub = lax.broadcasted_iota(jnp.int32, (E, tt), 0)
    work = lg_ref[:, toks]
    vals, hots = [], []
    for _ in range(TOP_K):
        m = jnp.max(work, axis=0, keepdims=True)
        first = jnp.min(jnp.where(work == m, sub, E), axis=0, keepdims=True)
        hot = sub == first
        vals.append(m)
        hots.append(hot)
        work = jnp.where(hot, -jnp.inf, work)
    exps = [jnp.exp(v - vals[0]) for v in vals]
    inv = 1.0 / functools.reduce(lambda a, b: a + b, exps)

    member = functools.reduce(lambda a, b: a + b, [jnp.where(h, 1.0, 0.0) for h in hots])
    cnt = jnp.sum(member, axis=1, keepdims=True)
    nch = jnp.floor((cnt + (MOE_CH - 1.0)) * (1.0 / MOE_CH))
    nch_b = jnp.broadcast_to(nch, (E, 128))
    rank = _dot(member.astype(BF16), su_scr[...])
    er = lax.broadcasted_iota(jnp.int32, (E, E), 0)
    ec = lax.broadcasted_iota(jnp.int32, (E, E), 1)
    first_chunk = _dot(jnp.where(ec < er, 1.0, 0.0).astype(BF16), nch_b.astype(BF16))[:, 0:1]
    base = first_chunk * MOE_CH + rank
    pos = [jnp.sum(jnp.where(hots[k], base, 0.0), axis=0, keepdims=True) for k in range(TOP_K)]
    for k in range(TOP_K):
        pos_ref[k:k + 1, toks] = pos[k].astype(jnp.int32)
    rows = jnp.concatenate(pos + [e * inv for e in exps] + [jnp.zeros((128 - 2 * TOP_K, tt), F32)], axis=0)
    tok_ref[toks, :] = rows.T
    nch_ref[t] = nch_b.astype(jnp.int32)


def _route_call(logits_t):
    E, T = logits_t.shape
    tt = MOE_TT
    per_step = 4
    assert (T // tt) % per_step == 0
    span = per_step * tt
    return pl.pallas_call(
        _route_kernel,
        grid=(T // span,),
        in_specs=[pl.BlockSpec((E, span), lambda j: (0, j))],
        out_specs=[
            pl.BlockSpec((TOP_K, span), lambda j: (0, j)),
            pl.BlockSpec((span, 128), lambda j: (j, 0)),
            pl.BlockSpec((per_step, E, 128), lambda j: (j, 0, 0)),
        ],
        out_shape=[
            jax.ShapeDtypeStruct((TOP_K, T), jnp.int32),
            jax.ShapeDtypeStruct((T, 128), F32),
            jax.ShapeDtypeStruct((T // tt, E, 128), jnp.int32),
        ],
        scratch_shapes=[pltpu.VMEM((tt, tt), BF16)],
        compiler_params=_params(("arbitrary",)),
        name="moe_route",
    )(logits_t)


def _slot_tiles_max(n_tiles):
    return -(-(n_tiles * MOE_MAXC + N_EXPERTS * (MOE_CPT - 1)) // MOE_CPT)


def _routing_tables(nch):
    n_tiles, E = nch.shape

    def cumsum_last(a):
        m = a.shape[-1]
        keep = jnp.arange(m)[:, None] <= jnp.arange(m)[None, :]
        return jnp.sum(jnp.where(keep, a[..., :, None], 0), axis=-2)

    tot = jnp.sum(nch, axis=0)
    seg = (tot + MOE_CPT - 1) // MOE_CPT * MOE_CPT
    seg_end = cumsum_last(seg)
    seg_start = seg_end - seg
    gstart = seg_start[None, :] + cumsum_last(nch.T).T - nch
    l_end = cumsum_last(nch)
    lstart = l_end - nch

    def copy_list(count, first_local, first_global, step, max_len):
        end = cumsum_last(count)
        start = end - count
        i = jnp.arange(max_len, dtype=jnp.int32)[None, :, None]
        owner = (start[:, None, :] <= i) & (i < end[:, None, :])
        off = (i - start[:, None, :]) * step
        src = jnp.sum(jnp.where(owner, first_local[:, None, :] + off, 0), axis=-1)
        dst = jnp.sum(jnp.where(owner, first_global[:, None, :] + off, 0), axis=-1)
        return src, dst, end[:, -1]

    runs = nch // MOE_RUN
    rsrc, rdst, nruns = copy_list(runs, lstart, gstart, MOE_RUN, MOE_MAXQ)
    ssrc, sdst, nsingles = copy_list(nch - runs * MOE_RUN, lstart + runs * MOE_RUN, gstart + runs * MOE_RUN, 1,
                                     MOE_MAXS)
    copies = jnp.concatenate([rsrc, rdst, ssrc, sdst], axis=1).astype(jnp.int32).reshape(-1)
    ncopies = jnp.stack([nruns, nsingles], axis=1).astype(jnp.int32).reshape(-1)
    nt_max = _slot_tiles_max(n_tiles)
    n_used = (seg_end[-1] // MOE_CPT).astype(jnp.int32)
    tile_blk = jnp.minimum(jnp.arange(nt_max, dtype=jnp.int32), n_used - 1)
    tile_exp = jnp.sum(seg_end[None, :] <= (tile_blk * MOE_CPT)[:, None], axis=1)
    tile_exp = jnp.minimum(tile_exp, E - 1).astype(jnp.int32)
    experts = jnp.arange(E, dtype=jnp.int32)
    own = tile_exp[:, None] == experts[None, :]
    end_tile = jnp.sum(jnp.where(own, (seg_end // MOE_CPT)[None, :], 0), axis=1)
    follows = ((seg_start // MOE_CPT)[None, :] == end_tile[:, None]) & (seg > 0)[None, :]
    next_exp = jnp.where(jnp.any(follows, axis=1), jnp.sum(jnp.where(follows, experts[None, :], 0), axis=1), -1)
    next_exp = next_exp.astype(jnp.int32)
    zstart = (seg_start + tot).astype(jnp.int32)
    zcount = (seg - tot).astype(jnp.int32)
    first_tile = jnp.sum(jnp.where(own, (seg_start // MOE_CPT)[None, :], 0), axis=1)
    own_chunks = jnp.sum(jnp.where(own, tot[None, :], 0), axis=1)
    filled = own_chunks - (jnp.arange(nt_max, dtype=jnp.int32) - first_tile) * MOE_CPT
    per_part = MOE_CPT // MOE_PARTS
    parts = jnp.clip((filled + per_part - 1) // per_part, 1, MOE_PARTS).astype(jnp.int32)
    expert_tables = (tile_exp, tile_blk, n_used.reshape(1), next_exp, parts)
    return copies, ncopies, zstart, zcount, expert_tables


def _chunk_rows(c, nchunks=1):
    return pl.ds(pl.multiple_of(c * MOE_CH, MOE_CH), nchunks * MOE_CH)


def _for_tile_copies(cp_ref, cn_ref, tile, make, op):
    base = tile * MOE_CPROW

    def run(i, carry):
        op(make(cp_ref[base + i], cp_ref[base + MOE_MAXQ + i], MOE_RUN))
        return carry
    lax.fori_loop(0, cn_ref[2 * tile], run, 0)

    def single(i, carry):
        op(make(cp_ref[base + 2 * MOE_MAXQ + i], cp_ref[base + 2 * MOE_MAXQ + MOE_MAXS + i], 1))
        return carry
    lax.fori_loop(0, cn_ref[2 * tile + 1], single, 0)


def _dispatch_kernel(cp_ref, cn_ref, zstart_ref, zcount_ref, nu_ref, h2_ref, pos_ref, xs_hbm, xl_scr, zero_scr,
                     sems, zsem):
    j = pl.program_id(0)
    n = pl.num_programs(0)
    slot = j % 2
    tt = h2_ref.shape[0]
    nt_max = xs_hbm.shape[0] // MOE_TM

    def tile_copies(tile, slot_, op):
        def make(local, glob, nchunks):
            return pltpu.make_async_copy(xl_scr.at[slot_, _chunk_rows(local, nchunks)],
                                         xs_hbm.at[_chunk_rows(glob, nchunks)], sems.at[slot_])
        _for_tile_copies(cp_ref, cn_ref, tile, make, op)

    def zero_chunk_copy(d, nchunks):
        return pltpu.make_async_copy(zero_scr.at[pl.ds(0, nchunks * MOE_CH)], xs_hbm.at[_chunk_rows(d, nchunks)],
                                     zsem.at[0])

    def zero_tile_copy(t):
        rows = pl.ds(pl.multiple_of(t * MOE_TM, MOE_TM), MOE_TM)
        return pltpu.make_async_copy(zero_scr, xs_hbm.at[rows], zsem.at[0])

    def zero_fill(op):
        def expert(e, carry):
            nruns = zcount_ref[e] // MOE_RUN

            def run(r, carry_):
                op(zero_chunk_copy(zstart_ref[e] + r * MOE_RUN, MOE_RUN))
                return carry_
            carry = lax.fori_loop(0, nruns, run, carry)

            def single(r, carry_):
                op(zero_chunk_copy(zstart_ref[e] + nruns * MOE_RUN + r, 1))
                return carry_
            return lax.fori_loop(0, zcount_ref[e] - nruns * MOE_RUN, single, carry)
        lax.fori_loop(0, N_EXPERTS, expert, 0)

        def tile(t, carry):
            op(zero_tile_copy(t))
            return carry
        lax.fori_loop(nu_ref[0], nt_max, tile, 0)

    def wait_tile(tile, slot_):
        tile_copies(tile, slot_, lambda cp: cp.wait())

    @pl.when(j == 0)
    def _():
        zero_scr[...] = jnp.zeros_like(zero_scr)
        zero_fill(lambda cp: cp.start())

    @pl.when(j >= 2)
    def _():
        wait_tile(j - 2, slot)

    pos = pos_ref[...]
    h2 = h2_ref[...]
    for rb in range(MOE_RLOC // MOE_RB):
        io = lax.broadcasted_iota(jnp.int32, (MOE_RB, tt), 0) + rb * MOE_RB
        onehot = jnp.zeros((MOE_RB, tt), F32)
        for k in range(TOP_K):
            onehot = jnp.where(io == pos[k:k + 1, :], 1.0, onehot)
        xl_scr[slot, rb * MOE_RB:(rb + 1) * MOE_RB, :] = _dot(onehot.astype(BF16), h2).astype(BF16)

    tile_copies(j, slot, lambda cp: cp.start())

    @pl.when(j == n - 1)
    def _():
        @pl.when(j >= 1)
        def _():
            wait_tile(j - 1, 1 - slot)
        wait_tile(j, slot)
        zero_fill(lambda cp: cp.wait())


def _dispatch_call(tables, h2, pos_t, n_slots):
    T = h2.shape[0]
    tt = MOE_TT
    copies, ncopies, zstart, zcount, n_used = tables
    grid_spec = pltpu.PrefetchScalarGridSpec(
        num_scalar_prefetch=5,
        grid=(T // tt,),
        in_specs=[
            pl.BlockSpec((tt, D_MODEL), lambda j, *_: (j, 0)),
            pl.BlockSpec((TOP_K, tt), lambda j, *_: (0, j)),
        ],
        out_specs=pl.BlockSpec(memory_space=pl.ANY),
        scratch_shapes=[
            pltpu.VMEM((2, MOE_RLOC, D_MODEL), BF16),
            pltpu.VMEM((MOE_TM, D_MODEL), BF16),
            pltpu.SemaphoreType.DMA((2,)),
            pltpu.SemaphoreType.DMA((1,)),
        ],
    )
    return pl.pallas_call(
        _dispatch_kernel,
        grid_spec=grid_spec,
        out_shape=jax.ShapeDtypeStruct((n_slots, D_MODEL), BF16),
        compiler_params=_params(("arbitrary",)),
        name="moe_dispatch",
    )(copies, ncopies, zstart, zcount, n_used, h2, pos_t)


def _experts_kernel(te_ref, tb_ref, nu_ref, nx_ref, parts_ref, xs_ref, w1_hbm, b1_ref, w2_hbm, b2_ref, y_ref,
                    w1_f32, w2_f32, w1_scr, w2_scr, sems):
    i = pl.program_id(0)
    valid = i < nu_ref[0]
    new_expert = (i == 0) | (te_ref[i] != te_ref[jnp.maximum(i - 1, 0)])

    def fetch(e):
        return (pltpu.make_async_copy(w1_hbm.at[e], w1_f32, sems.at[0]),
                pltpu.make_async_copy(w2_hbm.at[e], w2_f32, sems.at[1]))

    @pl.when(i == 0)
    def _():
        for cp in fetch(te_ref[0]):
            cp.start()

    @pl.when(valid & new_expert)
    def _():
        for cp in fetch(te_ref[i]):
            cp.wait()
        w1_scr[...] = w1_f32[...].astype(BF16)
        w2_scr[...] = w2_f32[...].astype(BF16)

        @pl.when(nx_ref[i] >= 0)
        def _():
            for cp in fetch(nx_ref[i]):
                cp.start()

    def mlp(rows):
        hh = _dot(xs_ref[rows, :], w1_scr[...]) + b1_ref[...]
        gate = jnp.minimum(hh[:, :D_FF], SWIGLU_LIMIT)
        up = jnp.clip(hh[:, D_FF:], -SWIGLU_LIMIT, SWIGLU_LIMIT)
        act = (up + 1.0) * gate * _sigmoid(SWIGLU_ALPHA * gate)
        y_ref[rows, :] = (_dot(act.astype(BF16), w2_scr[...]) + b2_ref[...]).astype(BF16)

    for nparts in range(1, MOE_PARTS + 1):
        @pl.when(valid & (parts_ref[i] == nparts))
        def _():
            rows = nparts * (MOE_TM // MOE_PARTS)
            mlp(slice(0, rows))
            if rows < MOE_TM:
                y_ref[rows:, :] = jnp.zeros((MOE_TM - rows, D_MODEL), BF16)


def _experts_call(tables, xs, w1, b1, w2, b2):
    tile_exp = tables[0]
    nt_max = tile_exp.shape[0]
    slot_tile = lambda i, te, tb, *_: (tb[i], 0)
    per_e = lambda i, te, *_: (te[i], 0, 0)
    grid_spec = pltpu.PrefetchScalarGridSpec(
        num_scalar_prefetch=len(tables),
        grid=(nt_max,),
        in_specs=[
            pl.BlockSpec((MOE_TM, D_MODEL), slot_tile),
            pl.BlockSpec(memory_space=pl.ANY),
            pl.BlockSpec((None, 1, 2 * D_FF), per_e),
            pl.BlockSpec(memory_space=pl.ANY),
            pl.BlockSpec((None, 1, D_MODEL), per_e),
        ],
        out_specs=pl.BlockSpec((MOE_TM, D_MODEL), slot_tile),
        scratch_shapes=[
            pltpu.VMEM((D_MODEL, 2 * D_FF), F32),
            pltpu.VMEM((D_FF, D_MODEL), F32),
            pltpu.VMEM((D_MODEL, 2 * D_FF), BF16),
            pltpu.VMEM((D_FF, D_MODEL), BF16),
            pltpu.SemaphoreType.DMA((2,)),
        ],
    )
    return pl.pallas_call(
        _experts_kernel,
        grid_spec=grid_spec,
        out_shape=jax.ShapeDtypeStruct(xs.shape, BF16),
        input_output_aliases={len(tables): 0},
        compiler_params=_params(("arbitrary",)),
        name="moe_experts",
    )(*tables, xs, w1, b1, w2, b2)


def _combine_kernel(cp_ref, cn_ref, tok_ref, x1_ref, mod_ref, fw_ref, y_hbm, o_ref, yl_scr, w_scr, sems):
    j = pl.program_id(0)
    n = pl.num_programs(0)
    slot = j % 2
    tt = x1_ref.shape[0]

    def tile_copies(tile, slot_, op):
        def make(local, glob, nchunks):
            return pltpu.make_async_copy(y_hbm.at[_chunk_rows(glob, nchunks)],
                                         yl_scr.at[slot_, _chunk_rows(local, nchunks)], sems.at[slot_])
        _for_tile_copies(cp_ref, cn_ref, tile, make, op)

    def start_tile(tile, slot_):
        tile_copies(tile, slot_, lambda cp: cp.start())

    @pl.when(j == 0)
    def _():
        yl_scr[...] = jnp.zeros_like(yl_scr)
        start_tile(0, 0)

    @pl.when(j + 1 < n)
    def _():
        start_tile(j + 1, 1 - slot)

    tile_copies(j, slot, lambda cp: cp.wait())

    pos = tok_ref[:, 0:TOP_K].astype(jnp.int32)
    p = tok_ref[:, TOP_K:2 * TOP_K]
    for cb in range(MOE_RLOC // MOE_RB):
        io = lax.broadcasted_iota(jnp.int32, (tt, MOE_RB), 1) + cb * MOE_RB
        w = jnp.zeros((tt, MOE_RB), F32)
        for k in range(TOP_K):
            w = jnp.where(io == pos[:, k:k + 1], p[:, k:k + 1], w)
        w_scr[:, cb * MOE_RB:(cb + 1) * MOE_RB] = w.astype(BF16)
    moe = _dot(w_scr[...], yl_scr[slot])
    x2 = x1_ref[...] + _mod_slice(mod_ref, 0, 5) * moe
    o_ref[...] = _rms(x2, fw_ref[...])


def _combine_call(copies, ncopies, tok, x1, mod, fw, y):
    T = x1.shape[0]
    tt = MOE_TT
    row = lambda j, *_: (j, 0)
    fixed = lambda j, *_: (0, 0)
    grid_spec = pltpu.PrefetchScalarGridSpec(
        num_scalar_prefetch=2,
        grid=(T // tt,),
        in_specs=[
            pl.BlockSpec((tt, 128), row),
            pl.BlockSpec((tt, D_MODEL), row),
            pl.BlockSpec(mod.shape, fixed),
            pl.BlockSpec((1, D_MODEL), fixed),
            pl.BlockSpec(memory_space=pl.ANY),
        ],
        out_specs=pl.BlockSpec((tt, D_MODEL), row),
        scratch_shapes=[
            pltpu.VMEM((2, MOE_RLOC, D_MODEL), BF16),
            pltpu.VMEM((tt, MOE_RLOC), BF16),
            pltpu.SemaphoreType.DMA((2,)),
        ],
    )
    return pl.pallas_call(
        _combine_kernel,
        grid_spec=grid_spec,
        out_shape=jax.ShapeDtypeStruct((T, D_MODEL), F32),
        compiler_params=_params(("arbitrary",)),
        name="moe_combine",
    )(copies, ncopies, tok, x1, mod, fw, y)


def _rope_tables(T):
    rows = T // GRID_W
    n_freq = DK // 4
    inv = ROPE_BASE ** (-jnp.arange(n_freq, dtype=F32) / n_freq)
    row_ang = jnp.arange(rows, dtype=F32)[:, None] * inv
    col_ang = jnp.arange(GRID_W, dtype=F32)[:, None] * inv
    rot_row = jnp.pad(jnp.stack([jnp.cos(row_ang), jnp.sin(row_ang)]), ((0, 0), (0, 0), (0, n_freq)))
    rot_col = jnp.pad(jnp.stack([jnp.cos(col_ang), jnp.sin(col_ang)]), ((0, 0), (0, 0), (n_freq, 0)))
    return rot_row, rot_col


def kernel(x, c, ctx, c_ctx, w_mod, b_mod, norm1_w, norm2_w, w_in, sgu_ln_w, sgu_ln_b, sgu_w, sgu_b,
           ret_decay_fwd, ret_decay_bwd, w_proj_a, w_proj_b, w_out, router_w, router_b,
           moe_w1, moe_b1, moe_w2, moe_b2, final_norm_w):
    B, T, D = x.shape
    assert B == 1 and D == D_MODEL and w_mod.shape[0] == 1 and T % 1024 == 0
    x2d = x.reshape(T, D)
    cc = jnp.stack([c.reshape(D), c_ctx.reshape(D)], axis=1)
    mod = _mod_call(cc, w_mod[0], b_mod)
    dec = jnp.stack([ret_decay_fwd[0], ret_decay_bwd[0]], axis=0)
    w_in_bf = w_in[0].astype(BF16)
    s0 = _ctx_call(ctx.reshape(ctx.shape[1], D), norm1_w, mod, w_in_bf, dec)
    rot_row, rot_col = _rope_tables(T)
    proj = _inproj_call(x2d, norm1_w, mod, w_in_bf, rot_row, rot_col, sgu_ln_w, sgu_ln_b)
    za = _sgu_call(proj, sgu_w[0].astype(BF16), sgu_b[0].T, w_proj_a[0].astype(BF16))
    yf, yb = _ret_call(proj, s0, dec)
    x1, h2, logits_t = _merge_call(yf, yb, za, proj, x2d, mod, norm2_w, w_proj_b[0].astype(BF16),
                                   w_out[0].astype(BF16), router_w[0].T, router_b.reshape(N_EXPERTS, 1))
    pos_t, tok, nch = _route_call(logits_t)
    copies, ncopies, zstart, zcount, expert_tables = _routing_tables(nch[:, :, 0])
    n_slots = expert_tables[0].shape[0] * MOE_TM
    xs = _dispatch_call((copies, ncopies, zstart, zcount, expert_tables[2]), h2, pos_t, n_slots)
    y = _experts_call(expert_tables, xs, moe_w1[0], moe_b1[0][:, None, :], moe_w2[0], moe_b2[0][:, None, :])
    out = _combine_call(copies, ncopies, tok, x1, mod, final_norm_w.reshape(1, D), y)
    return out.reshape(B, T, D)
```

```python
import functools

import jax
import jax.numpy as jnp
from jax import lax
from jax.experimental import pallas as pl
from jax.experimental.pallas import tpu as pltpu

D_MODEL = 1024
GRID_W = 64
SGU_CHUNK = 128
SGU_GROUPS = 8
HEADS = 4
DK = D_MODEL // HEADS
DV = 2 * DK
ROPE_BASE = 10000.0
N_EXPERTS = 32
TOP_K = 4
D_FF = D_MODEL
SWIGLU_LIMIT = 7.0
SWIGLU_ALPHA = 1.702
EPS = 1e-6
IN_WIDTH = 12 * D_MODEL
COL_U, COL_V, COL_Q, COL_K, COL_VR, COL_GF, COL_GB, COL_GA, COL_GBR = 0, 1, 2, 3, 4, 6, 8, 10, 11

INPROJ_NB = 4
RET_CHUNK = 256
RET_SUB = 1
VMEM_LIMIT = 56 * 1024 * 1024

MOE_TT = 512
MOE_CH = 16
MOE_TM = 512
MOE_CPT = MOE_TM // MOE_CH
MOE_PARTS = 4
MOE_MAXC = TOP_K * MOE_TT // MOE_CH + N_EXPERTS
MOE_RLOC = MOE_MAXC * MOE_CH
MOE_RB = 256
MOE_RUN = 4
MOE_MAXQ = MOE_MAXC // MOE_RUN
MOE_MAXS = (MOE_RUN - 1) * N_EXPERTS
MOE_CPROW = 2 * (MOE_MAXQ + MOE_MAXS)

F32 = jnp.float32
BF16 = jnp.bfloat16


def _params(sem):
    return pltpu.CompilerParams(dimension_semantics=sem, vmem_limit_bytes=VMEM_LIMIT)


def _dot(a, b):
    return jnp.dot(a, b, preferred_element_type=F32)


def _dot_nt(a, b):
    return lax.dot_general(a, b, (((1,), (1,)), ((), ())), preferred_element_type=F32)


def _dot_tn(a, b):
    return lax.dot_general(a, b, (((0,), (0,)), ((), ())), preferred_element_type=F32)


def _rms(x, w):
    return x * lax.rsqrt(jnp.mean(x * x, axis=-1, keepdims=True) + EPS) * w


def _gelu(x):
    return 0.5 * x * (1.0 + lax.erf(x * (2.0 ** -0.5)))


def _sigmoid(x):
    return 0.5 * jnp.tanh(0.5 * x) + 0.5


def _mod_slice(mod_ref, row, k):
    return mod_ref[row:row + 1, k * D_MODEL:(k + 1) * D_MODEL]


def _mod_kernel(cc_ref, w_ref, b_ref, o_ref):
    s = cc_ref[...]
    s = s * jax.nn.sigmoid(s)
    w = w_ref[...]
    r0 = jnp.sum(s[:, 0:1] * w, axis=0, keepdims=True)
    r1 = jnp.sum(s[:, 1:2] * w, axis=0, keepdims=True)
    o_ref[...] = jnp.concatenate([r0, r1], axis=0) + b_ref[...]


def _mod_call(cc, w_mod, b_mod):
    bn = 1536
    n = w_mod.shape[1]
    return pl.pallas_call(
        _mod_kernel,
        grid=(n // bn,),
        in_specs=[
            pl.BlockSpec((D_MODEL, 2), lambda j: (0, 0)),
            pl.BlockSpec((D_MODEL, bn), lambda j: (0, j)),
            pl.BlockSpec((1, bn), lambda j: (0, j)),
        ],
        out_specs=pl.BlockSpec((2, bn), lambda j: (0, j)),
        out_shape=jax.ShapeDtypeStruct((2, n), F32),
        compiler_params=_params(("arbitrary",)),
        name="mod_vectors",
    )(cc, w_mod, b_mod)


def _ctx_kernel(ctx_ref, nw_ref, mod_ref, w_ref, dec_ref, s0_ref):
    L = ctx_ref.shape[0]
    sh = _mod_slice(mod_ref, 1, 0)
    sc = _mod_slice(mod_ref, 1, 1)
    hc = (_rms(ctx_ref[...], nw_ref[...]) * (1.0 + sc) + sh).astype(BF16)
    kv = _dot(hc, w_ref[...])
    lg = jnp.log1p(-jnp.exp2(dec_ref[...]))
    pos = lax.broadcasted_iota(jnp.int32, (L, 1), 0).astype(F32)
    for h in range(HEADS):
        k = kv[:, h * DK:(h + 1) * DK] * (DK ** -0.5)
        v = kv[:, HEADS * DK + h * DV:HEADS * DK + (h + 1) * DV].astype(BF16)
        wf = jnp.exp((L - 1.0 - pos) * lg[0:1, h:h + 1])
        wb = jnp.exp(pos * lg[1:2, h:h + 1])
        s0_ref[0, h] = _dot_tn((k * wf).astype(BF16), v)
        s0_ref[1, h] = _dot_tn((k * wb).astype(BF16), v)


def _ctx_call(ctx2d, nw, mod, w_in_bf, dec):
    L = ctx2d.shape[0]
    wcols = HEADS * DK + HEADS * DV
    return pl.pallas_call(
        _ctx_kernel,
        grid=(1,),
        in_specs=[
            pl.BlockSpec((L, D_MODEL), lambda i: (0, 0)),
            pl.BlockSpec((1, D_MODEL), lambda i: (0, 0)),
            pl.BlockSpec(mod.shape, lambda i: (0, 0)),
            pl.BlockSpec((D_MODEL, wcols), lambda i: (0, COL_K * D_MODEL // wcols)),
            pl.BlockSpec(dec.shape, lambda i: (0, 0)),
        ],
        out_specs=pl.BlockSpec((2, HEADS, DK, DV), lambda i: (0, 0, 0, 0)),
        out_shape=jax.ShapeDtypeStruct((2, HEADS, DK, DV), F32),
        compiler_params=_params(("arbitrary",)),
        name="ctx_states",
    )(ctx2d, nw, mod, w_in_bf, dec)


def _inproj_kernel(x_ref, nw_ref, mod_ref, w_ref, rot_row_ref, rot_col_ref, lnw_ref, lnb_ref, o_ref,
                   h_scr, g_scr, cos_ref, sin_ref):
    j = pl.program_id(1)
    tm = h_scr.shape[0]

    @pl.when(j == 0)
    def _():
        sh = _mod_slice(mod_ref, 0, 0)
        sc = _mod_slice(mod_ref, 0, 1)
        h_scr[...] = (_rms(x_ref[...], nw_ref[...]) * (1.0 + sc) + sh).astype(BF16)

    def pieces(emit, span, rsplit=1):
        for c0 in range(span.start, span.stop, DK):
            cols = slice(c0, c0 + DK)
            for r in range(rsplit):
                rows = slice(r * tm // rsplit, (r + 1) * tm // rsplit)
                emit(rows, cols, _dot(h_scr[rows, :], w_ref[:, cols]))

    def store(fn):
        def emit(rows, cols, acc):
            o_ref[rows, cols] = fn(acc).astype(BF16)
        return emit

    def rope(scale):
        half = DK // 2

        def emit(rows, cols, acc):
            cos = cos_ref[rows, :]
            sin = sin_ref[rows, :]
            x1 = acc[:, :half]
            x2 = acc[:, half:]
            o_ref[rows, cols.start:cols.start + half] = ((x1 * cos - x2 * sin) * scale).astype(BF16)
            o_ref[rows, cols.start + half:cols.stop] = ((x1 * sin + x2 * cos) * scale).astype(BF16)
        return emit

    def gelu_layer_norm(span):
        def emit(rows, cols, acc):
            g_scr[rows, cols.start - span.start:cols.stop - span.start] = _gelu(acc)
        pieces(emit, span)
        g = g_scr[...]
        mu = jnp.mean(g, axis=-1, keepdims=True)
        d = g - mu
        var = jnp.mean(d * d, axis=-1, keepdims=True)
        o_ref[:, span] = (d * lax.rsqrt(var + EPS) * lnw_ref[...] + lnb_ref[...]).astype(BF16)

    def rotary_tables():
        for r in range(tm // GRID_W):
            tok = slice(r * GRID_W, (r + 1) * GRID_W)
            cos_ref[tok, :] = rot_row_ref[0, r:r + 1, :] + rot_col_ref[0]
            sin_ref[tok, :] = rot_row_ref[1, r:r + 1, :] + rot_col_ref[1]

    def column_block(block, span):
        if block == COL_U:
            pieces(store(_gelu), span)
        elif block == COL_V:
            gelu_layer_norm(span)
        elif block == COL_Q:
            rotary_tables()
            pieces(rope(1.0), span)
        elif block == COL_K:
            pieces(rope(DK ** -0.5), span)
        elif block < COL_GF:
            pieces(store(lambda a: a), span)
        elif block < COL_GA:
            pieces(store(lambda a: a * _sigmoid(a)), span, rsplit=2)
        else:
            pieces(store(_sigmoid), span, rsplit=2)

    for step in range(IN_WIDTH // D_MODEL // INPROJ_NB):
        @pl.when(j == step)
        def _():
            for b in range(INPROJ_NB):
                column_block(step * INPROJ_NB + b, slice(b * D_MODEL, (b + 1) * D_MODEL))


def _inproj_call(x2d, nw, mod, w_in_bf, rot_row, rot_col, lnw, lnb):
    T = x2d.shape[0]
    tm = 1024
    half = DK // 2
    return pl.pallas_call(
        _inproj_kernel,
        grid=(T // tm, IN_WIDTH // (INPROJ_NB * D_MODEL)),
        in_specs=[
            pl.BlockSpec((tm, D_MODEL), lambda i, j: (i, 0)),
            pl.BlockSpec((1, D_MODEL), lambda i, j: (0, 0)),
            pl.BlockSpec(mod.shape, lambda i, j: (0, 0)),
            pl.BlockSpec((D_MODEL, INPROJ_NB * D_MODEL), lambda i, j: (0, j)),
            pl.BlockSpec((2, tm // GRID_W, half), lambda i, j: (0, i, 0)),
            pl.BlockSpec((2, GRID_W, half), lambda i, j: (0, 0, 0)),
            pl.BlockSpec((1, D_MODEL), lambda i, j: (0, 0)),
            pl.BlockSpec((1, D_MODEL), lambda i, j: (0, 0)),
        ],
        out_specs=pl.BlockSpec((tm, INPROJ_NB * D_MODEL), lambda i, j: (i, j)),
        out_shape=jax.ShapeDtypeStruct((T, IN_WIDTH), BF16),
        scratch_shapes=[
            pltpu.VMEM((tm, D_MODEL), BF16),
            pltpu.VMEM((tm, D_MODEL), F32),
            pltpu.VMEM((tm, half), F32),
            pltpu.VMEM((tm, half), F32),
        ],
        compiler_params=_params(("arbitrary", "arbitrary")),
        name="in_proj",
    )(x2d, nw, mod, w_in_bf, rot_row, rot_col, lnw, lnb)


def _sgu_kernel(u_ref, v_ref, ga_ref, ws_ref, bt_ref, wpa_ref, o_ref, ya_scr):
    tm = u_ref.shape[0]
    gd = D_MODEL // SGU_GROUPS
    for n in range(tm // SGU_CHUNK):
        rows = slice(n * SGU_CHUNK, (n + 1) * SGU_CHUNK)
        for g in range(SGU_GROUPS):
            cols = slice(g * gd, (g + 1) * gd)
            mixed = _dot(ws_ref[g], v_ref[rows, cols]) + bt_ref[:, g:g + 1]
            ya_scr[rows, cols] = (u_ref[rows, cols].astype(F32) * mixed).astype(BF16)
    pa = _dot(ya_scr[...], wpa_ref[...])
    o_ref[...] = (ga_ref[...].astype(F32) * pa).astype(BF16)


def _sgu_call(proj, ws_bf, bt, wpa_bf):
    T = proj.shape[0]
    tm = 2048
    return pl.pallas_call(
        _sgu_kernel,
        grid=(T // tm,),
        in_specs=[
            pl.BlockSpec((tm, D_MODEL), lambda i: (i, COL_U)),
            pl.BlockSpec((tm, D_MODEL), lambda i: (i, COL_V)),
            pl.BlockSpec((tm, D_MODEL), lambda i: (i, COL_GA)),
            pl.BlockSpec(ws_bf.shape, lambda i: (0, 0, 0)),
            pl.BlockSpec(bt.shape, lambda i: (0, 0)),
            pl.BlockSpec((D_MODEL, D_MODEL), lambda i: (0, 0)),
        ],
        out_specs=pl.BlockSpec((tm, D_MODEL), lambda i: (i, 0)),
        out_shape=jax.ShapeDtypeStruct((T, D_MODEL), BF16),
        scratch_shapes=[pltpu.VMEM((tm, D_MODEL), BF16)],
        compiler_params=_params(("arbitrary",)),
        name="sgu_proj_a",
    )(proj, proj, proj, ws_bf, bt, wpa_bf)


def _ret_kernel(qf_ref, kf_ref, vf_ref, gf_ref, qb_ref, kb_ref, vb_ref, gb_ref, s0_ref, dec_ref,
                yf_ref, yb_ref, s_scr, intra_scr, qd_scr, kd_scr, cd_scr):
    C = RET_CHUNK
    i = pl.program_id(0)

    @pl.when(i == 0)
    def _():
        lg = jnp.log1p(-jnp.exp2(dec_ref[...]))
        r = lax.broadcasted_iota(jnp.int32, (C, C), 0).astype(F32)
        c = lax.broadcasted_iota(jnp.int32, (C, C), 1).astype(F32)
        pos = lax.broadcasted_iota(jnp.int32, (C, 1), 0).astype(F32)
        for h in range(HEADS):
            lf = lg[0:1, h:h + 1]
            lb = lg[1:2, h:h + 1]
            intra_scr[0, h] = jnp.where(r >= c, jnp.exp(jnp.maximum(r - c, 0.0) * lf), 0.0)
            intra_scr[1, h] = jnp.where(c >= r, jnp.exp(jnp.maximum(c - r, 0.0) * lb), 0.0)
            qd_scr[0, h] = jnp.exp((pos + 1.0) * lf)
            qd_scr[1, h] = jnp.exp((C - pos) * lb)
            kd_scr[0, h] = jnp.exp((C - 1.0 - pos) * lf)
            kd_scr[1, h] = jnp.exp(pos * lb)
            cd_scr[0, h] = jnp.exp(C * lf)
            cd_scr[1, h] = jnp.exp(C * lb)
        s_scr[...] = s0_ref[...]

    dirs = ((qf_ref, kf_ref, vf_ref, gf_ref, yf_ref), (qb_ref, kb_ref, vb_ref, gb_ref, yb_ref))
    nsub = qf_ref.shape[0] // C
    for sub in range(nsub):
        for d, (q_ref, k_ref, v_ref, g_ref, y_ref) in enumerate(dirs):
            c0 = (sub if d == 0 else nsub - 1 - sub) * C
            rows = slice(c0, c0 + C)
            for h in range(HEADS):
                q = q_ref[rows, h * DK:(h + 1) * DK]
                k = k_ref[rows, h * DK:(h + 1) * DK]
                v = v_ref[rows, h * DV:(h + 1) * DV]
                s = s_scr[d, h]
                a = (_dot_nt(q, k) * intra_scr[d, h]).astype(BF16)
                o = _dot(a, v) + _dot(q, s.astype(BF16)) * qd_scr[d, h]
                hn = o * lax.rsqrt(jnp.mean(o * o, axis=-1, keepdims=True) + EPS)
                y_ref[rows, h * DV:(h + 1) * DV] = (g_ref[rows, h * DV:(h + 1) * DV].astype(F32) * hn).astype(BF16)
                kd = (k.astype(F32) * kd_scr[d, h]).astype(BF16)
                s_scr[d, h] = s * cd_scr[d, h] + _dot_tn(kd, v)


def _ret_call(proj, s0, dec):
    T = proj.shape[0]
    C = RET_CHUNK
    rows = RET_SUB * C
    n = T // rows
    qk_w = HEADS * DK
    v_w = HEADS * DV
    fwd = lambda col: (lambda i: (i, col))
    bwd = lambda col: (lambda i: (n - 1 - i, col))
    return pl.pallas_call(
        _ret_kernel,
        grid=(n,),
        in_specs=[
            pl.BlockSpec((rows, qk_w), fwd(COL_Q)),
            pl.BlockSpec((rows, qk_w), fwd(COL_K)),
            pl.BlockSpec((rows, v_w), fwd(COL_VR // 2)),
            pl.BlockSpec((rows, v_w), fwd(COL_GF // 2)),
            pl.BlockSpec((rows, qk_w), bwd(COL_Q)),
            pl.BlockSpec((rows, qk_w), bwd(COL_K)),
            pl.BlockSpec((rows, v_w), bwd(COL_VR // 2)),
            pl.BlockSpec((rows, v_w), bwd(COL_GB // 2)),
            pl.BlockSpec(s0.shape, lambda i: (0, 0, 0, 0)),
            pl.BlockSpec(dec.shape, lambda i: (0, 0)),
        ],
        out_specs=[
            pl.BlockSpec((rows, v_w), lambda i: (i, 0)),
            pl.BlockSpec((rows, v_w), lambda i: (n - 1 - i, 0)),
        ],
        out_shape=[jax.ShapeDtypeStruct((T, v_w), BF16), jax.ShapeDtypeStruct((T, v_w), BF16)],
        scratch_shapes=[
            pltpu.VMEM((2, HEADS, DK, DV), F32),
            pltpu.VMEM((2, HEADS, C, C), F32),
            pltpu.VMEM((2, HEADS, C, 1), F32),
            pltpu.VMEM((2, HEADS, C, 1), F32),
            pltpu.VMEM((2, HEADS, 1, 1), F32),
        ],
        compiler_params=_params(("arbitrary",)),
        name="retention",
    )(proj, proj, proj, proj, proj, proj, proj, proj, s0, dec)


def _merge_kernel(yf_ref, yb_ref, za_ref, gbr_ref, x_ref, mod_ref, nw_ref, wpb_ref, wo_ref, rw_ref, rb_ref,
                  x1_ref, h2_ref, lg_ref):
    yb = yf_ref[...] + yb_ref[...]
    pb = _dot(yb, wpb_ref[...])
    y = za_ref[...].astype(F32) + gbr_ref[...].astype(F32) * pb
    yo = _dot(y.astype(BF16), wo_ref[...])
    x1 = x_ref[...] + _mod_slice(mod_ref, 0, 2) * yo
    x1_ref[...] = x1
    h2 = _rms(x1, nw_ref[...]) * (1.0 + _mod_slice(mod_ref, 0, 4)) + _mod_slice(mod_ref, 0, 3)
    h2_hi = h2.astype(BF16)
    h2_ref[...] = h2_hi
    h2_lo = (h2 - h2_hi.astype(F32)).astype(BF16)
    rw = rw_ref[...]
    rw_hi = rw.astype(BF16)
    rw_lo = (rw - rw_hi.astype(F32)).astype(BF16)
    both = _dot_nt(jnp.concatenate([rw_hi, rw_lo], axis=0), h2_hi)
    lg_ref[...] = both[:N_EXPERTS] + both[N_EXPERTS:] + _dot_nt(rw_hi, h2_lo) + rb_ref[...]


def _merge_call(yf, yb, za, proj, x2d, mod, nw2, wpb_bf, wo_bf, rw, rb):
    T = x2d.shape[0]
    tm = 512
    row = lambda i: (i, 0)
    fixed = lambda i: (0, 0)
    return pl.pallas_call(
        _merge_kernel,
        grid=(T // tm,),
        in_specs=[
            pl.BlockSpec((tm, HEADS * DV), row),
            pl.BlockSpec((tm, HEADS * DV), row),
            pl.BlockSpec((tm, D_MODEL), row),
            pl.BlockSpec((tm, D_MODEL), lambda i: (i, COL_GBR)),
            pl.BlockSpec((tm, D_MODEL), row),
            pl.BlockSpec(mod.shape, fixed),
            pl.BlockSpec((1, D_MODEL), fixed),
            pl.BlockSpec(wpb_bf.shape, fixed),
            pl.BlockSpec(wo_bf.shape, fixed),
            pl.BlockSpec(rw.shape, fixed),
            pl.BlockSpec(rb.shape, fixed),
        ],
        out_specs=[
            pl.BlockSpec((tm, D_MODEL), row),
            pl.BlockSpec((tm, D_MODEL), row),
            pl.BlockSpec((N_EXPERTS, tm), lambda i: (0, i)),
        ],
        out_shape=[
            jax.ShapeDtypeStruct((T, D_MODEL), F32),
            jax.ShapeDtypeStruct((T, D_MODEL), BF16),
            jax.ShapeDtypeStruct((N_EXPERTS, T), F32),
        ],
        compiler_params=_params(("arbitrary",)),
        name="merge_router",
    )(yf, yb, za, proj, x2d, mod, nw2, wpb_bf, wo_bf, rw, rb)


def _route_kernel(lg_ref, pos_ref, tok_ref, nch_ref, su_scr):
    E = lg_ref.shape[0]
    tt = MOE_TT

    @pl.when(pl.program_id(0) == 0)
    def _():
        r = lax.broadcasted_iota(jnp.int32, (tt, tt), 0)
        c = lax.broadcasted_iota(jnp.int32, (tt, tt), 1)
        su_scr[...] = jnp.where(r < c, 1.0, 0.0).astype(BF16)

    for t in range(lg_ref.shape[1] // tt):
        _route_tile(t, slice(t * tt, (t + 1) * tt), lg_ref, pos_ref, tok_ref, nch_ref, su_scr)


def _route_tile(t, toks, lg_ref, pos_ref, tok_ref, nch_ref, su_scr):
    E, tt = lg_ref.shape[0], MOE_TT
    sub = lax.broadcasted_iota(jnp.int32, (E, tt), 0)
    work = lg_ref[:, toks]
    vals, hots = [], []
    for _ in range(TOP_K):
        m = jnp.max(work, axis=0, keepdims=True)
        first = jnp.min(jnp.where(work == m, sub, E), axis=0, keepdims=True)
        hot = sub == first
        vals.append(m)
        hots.append(hot)
        work = jnp.where(hot, -jnp.inf, work)
    exps = [jnp.exp(v - vals[0]) for v in vals]
    inv = 1.0 / functools.reduce(lambda a, b: a + b, exps)

    member = functools.reduce(lambda a, b: a + b, [jnp.where(h, 1.0, 0.0) for h in hots])
    cnt = jnp.sum(member, axis=1, keepdims=True)
    nch = jnp.floor((cnt + (MOE_CH - 1.0)) * (1.0 / MOE_CH))
    nch_b = jnp.broadcast_to(nch, (E, 128))
    rank = _dot(member.astype(BF16), su_scr[...])
    er = lax.broadcasted_iota(jnp.int32, (E, E), 0)
    ec = lax.broadcasted_iota(jnp.int32, (E, E), 1)
    first_chunk = _dot(jnp.where(ec < er, 1.0, 0.0).astype(BF16), nch_b.astype(BF16))[:, 0:1]
    base = first_chunk * MOE_CH + rank
    pos = [jnp.sum(jnp.where(hots[k], base, 0.0), axis=0, keepdims=True) for k in range(TOP_K)]
    for k in range(TOP_K):
        pos_ref[k:k + 1, toks] = pos[k].astype(jnp.int32)
    rows = jnp.concatenate(pos + [e * inv for e in exps] + [jnp.zeros((128 - 2 * TOP_K, tt), F32)], axis=0)
    tok_ref[toks, :] = rows.T
    nch_ref[t] = nch_b.astype(jnp.int32)


def _route_call(logits_t):
    E, T = logits_t.shape
    tt = MOE_TT
    per_step = 4
    assert (T // tt) % per_step == 0
    span = per_step * tt
    return pl.pallas_call(
        _route_kernel,
        grid=(T // span,),
        in_specs=[pl.BlockSpec((E, span), lambda j: (0, j))],
        out_specs=[
            pl.BlockSpec((TOP_K, span), lambda j: (0, j)),
            pl.BlockSpec((span, 128), lambda j: (j, 0)),
            pl.BlockSpec((per_step, E, 128), lambda j: (j, 0, 0)),
        ],
        out_shape=[
            jax.ShapeDtypeStruct((TOP_K, T), jnp.int32),
            jax.ShapeDtypeStruct((T, 128), F32),
            jax.ShapeDtypeStruct((T // tt, E, 128), jnp.int32),
        ],
        scratch_shapes=[pltpu.VMEM((tt, tt), BF16)],
        compiler_params=_params(("arbitrary",)),
        name="moe_route",
    )(logits_t)


def _slot_tiles_max(n_tiles):
    return -(-(n_tiles * MOE_MAXC + N_EXPERTS * (MOE_CPT - 1)) // MOE_CPT)


def _routing_tables(nch):
    n_tiles, E = nch.shape

    def cumsum_last(a):
        m = a.shape[-1]
        keep = jnp.arange(m)[:, None] <= jnp.arange(m)[None, :]
        return jnp.sum(jnp.where(keep, a[..., :, None], 0), axis=-2)

    tot = jnp.sum(nch, axis=0)
    seg = (tot + MOE_CPT - 1) // MOE_CPT * MOE_CPT
    seg_end = cumsum_last(seg)
    seg_start = seg_end - seg
    gstart = seg_start[None, :] + cumsum_last(nch.T).T - nch
    l_end = cumsum_last(nch)
    lstart = l_end - nch

    def copy_list(count, first_local, first_global, step, max_len):
        end = cumsum_last(count)
        start = end - count
        i = jnp.arange(max_len, dtype=jnp.int32)[None, :, None]
        owner = (start[:, None, :] <= i) & (i < end[:, None, :])
        off = (i - start[:, None, :]) * step
        src = jnp.sum(jnp.where(owner, first_local[:, None, :] + off, 0), axis=-1)
        dst = jnp.sum(jnp.where(owner, first_global[:, None, :] + off, 0), axis=-1)
        return src, dst, end[:, -1]

    runs = nch // MOE_RUN
    rsrc, rdst, nruns = copy_list(runs, lstart, gstart, MOE_RUN, MOE_MAXQ)
    ssrc, sdst, nsingles = copy_list(nch - runs * MOE_RUN, lstart + runs * MOE_RUN, gstart + runs * MOE_RUN, 1,
                                     MOE_MAXS)
    copies = jnp.concatenate([rsrc, rdst, ssrc, sdst], axis=1).astype(jnp.int32).reshape(-1)
    ncopies = jnp.stack([nruns, nsingles], axis=1).astype(jnp.int32).reshape(-1)
    nt_max = _slot_tiles_max(n_tiles)
    n_used = (seg_end[-1] // MOE_CPT).astype(jnp.int32)
    tile_blk = jnp.minimum(jnp.arange(nt_max, dtype=jnp.int32), n_used - 1)
    tile_exp = jnp.sum(seg_end[None, :] <= (tile_blk * MOE_CPT)[:, None], axis=1)
    tile_exp = jnp.minimum(tile_exp, E - 1).astype(jnp.int32)
    experts = jnp.arange(E, dtype=jnp.int32)
    own = tile_exp[:, None] == experts[None, :]
    end_tile = jnp.sum(jnp.where(own, (seg_end // MOE_CPT)[None, :], 0), axis=1)
    follows = ((seg_start // MOE_CPT)[None, :] == end_tile[:, None]) & (seg > 0)[None, :]
    next_exp = jnp.where(jnp.any(follows, axis=1), jnp.sum(jnp.where(follows, experts[None, :], 0), axis=1), -1)
    next_exp = next_exp.astype(jnp.int32)
    zstart = (seg_start + tot).astype(jnp.int32)
    zcount = (seg - tot).astype(jnp.int32)
    first_tile = jnp.sum(jnp.where(own, (seg_start // MOE_CPT)[None, :], 0), axis=1)
    own_chunks = jnp.sum(jnp.where(own, tot[None, :], 0), axis=1)
    filled = own_chunks - (jnp.arange(nt_max, dtype=jnp.int32) - first_tile) * MOE_CPT
    per_part = MOE_CPT // MOE_PARTS
    parts = jnp.clip((filled + per_part - 1) // per_part, 1, MOE_PARTS).astype(jnp.int32)
    expert_tables = (tile_exp, tile_blk, n_used.reshape(1), next_exp, parts)
    return copies, ncopies, zstart, zcount, expert_tables


def _chunk_rows(c, nchunks=1):
    return pl.ds(pl.multiple_of(c * MOE_CH, MOE_CH), nchunks * MOE_CH)


def _for_tile_copies(cp_ref, cn_ref, tile, make, op):
    base = tile * MOE_CPROW

    def run(i, carry):
        op(make(cp_ref[base + i], cp_ref[base + MOE_MAXQ + i], MOE_RUN))
        return carry
    lax.fori_loop(0, cn_ref[2 * tile], run, 0)

    def single(i, carry):
        op(make(cp_ref[base + 2 * MOE_MAXQ + i], cp_ref[base + 2 * MOE_MAXQ + MOE_MAXS + i], 1))
        return carry
    lax.fori_loop(0, cn_ref[2 * tile + 1], single, 0)


def _dispatch_kernel(cp_ref, cn_ref, zstart_ref, zcount_ref, nu_ref, h2_ref, pos_ref, xs_hbm, xl_scr, zero_scr,
                     sems, zsem):
    j = pl.program_id(0)
    n = pl.num_programs(0)
    slot = j % 2
    tt = h2_ref.shape[0]
    nt_max = xs_hbm.shape[0] // MOE_TM

    def tile_copies(tile, slot_, op):
        def make(local, glob, nchunks):
            return pltpu.make_async_copy(xl_scr.at[slot_, _chunk_rows(local, nchunks)],
                                         xs_hbm.at[_chunk_rows(glob, nchunks)], sems.at[slot_])
        _for_tile_copies(cp_ref, cn_ref, tile, make, op)

    def zero_chunk_copy(d, nchunks):
        return pltpu.make_async_copy(zero_scr.at[pl.ds(0, nchunks * MOE_CH)], xs_hbm.at[_chunk_rows(d, nchunks)],
                                     zsem.at[0])

    def zero_tile_copy(t):
        rows = pl.ds(pl.multiple_of(t * MOE_TM, MOE_TM), MOE_TM)
        return pltpu.make_async_copy(zero_scr, xs_hbm.at[rows], zsem.at[0])

    def zero_fill(op):
        def expert(e, carry):
            nruns = zcount_ref[e] // MOE_RUN

            def run(r, carry_):
                op(zero_chunk_copy(zstart_ref[e] + r * MOE_RUN, MOE_RUN))
                return carry_
            carry = lax.fori_loop(0, nruns, run, carry)

            def single(r, carry_):
                op(zero_chunk_copy(zstart_ref[e] + nruns * MOE_RUN + r, 1))
                return carry_
            return lax.fori_loop(0, zcount_ref[e] - nruns * MOE_RUN, single, carry)
        lax.fori_loop(0, N_EXPERTS, expert, 0)

        def tile(t, carry):
            op(zero_tile_copy(t))
            return carry
        lax.fori_loop(nu_ref[0], nt_max, tile, 0)

    def wait_tile(tile, slot_):
        tile_copies(tile, slot_, lambda cp: cp.wait())

    @pl.when(j == 0)
    def _():
        zero_scr[...] = jnp.zeros_like(zero_scr)
        zero_fill(lambda cp: cp.start())

    @pl.when(j >= 2)
    def _():
        wait_tile(j - 2, slot)

    pos = pos_ref[...]
    h2 = h2_ref[...]
    for rb in range(MOE_RLOC // MOE_RB):
        io = lax.broadcasted_iota(jnp.int32, (MOE_RB, tt), 0) + rb * MOE_RB
        onehot = jnp.zeros((MOE_RB, tt), F32)
        for k in range(TOP_K):
            onehot = jnp.where(io == pos[k:k + 1, :], 1.0, onehot)
        xl_scr[slot, rb * MOE_RB:(rb + 1) * MOE_RB, :] = _dot(onehot.astype(BF16), h2).astype(BF16)

    tile_copies(j, slot, lambda cp: cp.start())

    @pl.when(j == n - 1)
    def _():
        @pl.when(j >= 1)
        def _():
            wait_tile(j - 1, 1 - slot)
        wait_tile(j, slot)
        zero_fill(lambda cp: cp.wait())


def _dispatch_call(tables, h2, pos_t, n_slots):
    T = h2.shape[0]
    tt = MOE_TT
    copies, ncopies, zstart, zcount, n_used = tables
    grid_spec = pltpu.PrefetchScalarGridSpec(
        num_scalar_prefetch=5,
        grid=(T // tt,),
        in_specs=[
            pl.BlockSpec((tt, D_MODEL), lambda j, *_: (j, 0)),
            pl.BlockSpec((TOP_K, tt), lambda j, *_: (0, j)),
        ],
        out_specs=pl.BlockSpec(memory_space=pl.ANY),
        scratch_shapes=[
            pltpu.VMEM((2, MOE_RLOC, D_MODEL), BF16),
            pltpu.VMEM((MOE_TM, D_MODEL), BF16),
            pltpu.SemaphoreType.DMA((2,)),
            pltpu.SemaphoreType.DMA((1,)),
        ],
    )
    return pl.pallas_call(
        _dispatch_kernel,
        grid_spec=grid_spec,
        out_shape=jax.ShapeDtypeStruct((n_slots, D_MODEL), BF16),
        compiler_params=_params(("arbitrary",)),
        name="moe_dispatch",
    )(copies, ncopies, zstart, zcount, n_used, h2, pos_t)


def _experts_kernel(te_ref, tb_ref, nu_ref, nx_ref, parts_ref, xs_ref, w1_hbm, b1_ref, w2_hbm, b2_ref, y_ref,
                    w1_f32, w2_f32, w1_scr, w2_scr, sems):
    i = pl.program_id(0)
    valid = i < nu_ref[0]
    new_expert = (i == 0) | (te_ref[i] != te_ref[jnp.maximum(i - 1, 0)])

    def fetch(e):
        return (pltpu.make_async_copy(w1_hbm.at[e], w1_f32, sems.at[0]),
                pltpu.make_async_copy(w2_hbm.at[e], w2_f32, sems.at[1]))

    @pl.when(i == 0)
    def _():
        for cp in fetch(te_ref[0]):
            cp.start()

    @pl.when(valid & new_expert)
    def _():
        for cp in fetch(te_ref[i]):
            cp.wait()
        w1_scr[...] = w1_f32[...].astype(BF16)
        w2_scr[...] = w2_f32[...].astype(BF16)

        @pl.when(nx_ref[i] >= 0)
        def _():
            for cp in fetch(nx_ref[i]):
                cp.start()

    def mlp(rows):
        hh = _dot(xs_ref[rows, :], w1_scr[...]) + b1_ref[...]
        gate = jnp.minimum(hh[:, :D_FF], SWIGLU_LIMIT)
        up = jnp.clip(hh[:, D_FF:], -SWIGLU_LIMIT, SWIGLU_LIMIT)
        act = (up + 1.0) * gate * _sigmoid(SWIGLU_ALPHA * gate)
        y_ref[rows, :] = (_dot(act.astype(BF16), w2_scr[...]) + b2_ref[...]).astype(BF16)

    for nparts in range(1, MOE_PARTS + 1):
        @pl.when(valid & (parts_ref[i] == nparts))
        def _():
            rows = nparts * (MOE_TM // MOE_PARTS)
            mlp(slice(0, rows))
            if rows < MOE_TM:
                y_ref[rows:, :] = jnp.zeros((MOE_TM - rows, D_MODEL), BF16)


def _experts_call(tables, xs, w1, b1, w2, b2):
    tile_exp = tables[0]
    nt_max = tile_exp.shape[0]
    slot_tile = lambda i, te, tb, *_: (tb[i], 0)
    per_e = lambda i, te, *_: (te[i], 0, 0)
    grid_spec = pltpu.PrefetchScalarGridSpec(
        num_scalar_prefetch=len(tables),
        grid=(nt_max,),
        in_specs=[
            pl.BlockSpec((MOE_TM, D_MODEL), slot_tile),
            pl.BlockSpec(memory_space=pl.ANY),
            pl.BlockSpec((None, 1, 2 * D_FF), per_e),
            pl.BlockSpec(memory_space=pl.ANY),
            pl.BlockSpec((None, 1, D_MODEL), per_e),
        ],
        out_specs=pl.BlockSpec((MOE_TM, D_MODEL), slot_tile),
        scratch_shapes=[
            pltpu.VMEM((D_MODEL, 2 * D_FF), F32),
            pltpu.VMEM((D_FF, D_MODEL), F32),
            pltpu.VMEM((D_MODEL, 2 * D_FF), BF16),
            pltpu.VMEM((D_FF, D_MODEL), BF16),
            pltpu.SemaphoreType.DMA((2,)),
        ],
    )
    return pl.pallas_call(
        _experts_kernel,
        grid_spec=grid_spec,
        out_shape=jax.ShapeDtypeStruct(xs.shape, BF16),
        input_output_aliases={len(tables): 0},
        compiler_params=_params(("arbitrary",)),
        name="moe_experts",
    )(*tables, xs, w1, b1, w2, b2)


def _combine_kernel(cp_ref, cn_ref, tok_ref, x1_ref, mod_ref, fw_ref, y_hbm, o_ref, yl_scr, w_scr, sems):
    j = pl.program_id(0)
    n = pl.num_programs(0)
    slot = j % 2
    tt = x1_ref.shape[0]

    def tile_copies(tile, slot_, op):
        def make(local, glob, nchunks):
            return pltpu.make_async_copy(y_hbm.at[_chunk_rows(glob, nchunks)],
                                         yl_scr.at[slot_, _chunk_rows(local, nchunks)], sems.at[slot_])
        _for_tile_copies(cp_ref, cn_ref, tile, make, op)

    def start_tile(tile, slot_):
        tile_copies(tile, slot_, lambda cp: cp.start())

    @pl.when(j == 0)
    def _():
        yl_scr[...] = jnp.zeros_like(yl_scr)
        start_tile(0, 0)

    @pl.when(j + 1 < n)
    def _():
        start_tile(j + 1, 1 - slot)

    tile_copies(j, slot, lambda cp: cp.wait())

    pos = tok_ref[:, 0:TOP_K].astype(jnp.int32)
    p = tok_ref[:, TOP_K:2 * TOP_K]
    for cb in range(MOE_RLOC // MOE_RB):
        io = lax.broadcasted_iota(jnp.int32, (tt, MOE_RB), 1) + cb * MOE_RB
        w = jnp.zeros((tt, MOE_RB), F32)
        for k in range(TOP_K):
            w = jnp.where(io == pos[:, k:k + 1], p[:, k:k + 1], w)
        w_scr[:, cb * MOE_RB:(cb + 1) * MOE_RB] = w.astype(BF16)
    moe = _dot(w_scr[...], yl_scr[slot])
    x2 = x1_ref[...] + _mod_slice(mod_ref, 0, 5) * moe
    o_ref[...] = _rms(x2, fw_ref[...])


def _combine_call(copies, ncopies, tok, x1, mod, fw, y):
    T = x1.shape[0]
    tt = MOE_TT
    row = lambda j, *_: (j, 0)
    fixed = lambda j, *_: (0, 0)
    grid_spec = pltpu.PrefetchScalarGridSpec(
        num_scalar_prefetch=2,
        grid=(T // tt,),
        in_specs=[
            pl.BlockSpec((tt, 128), row),
            pl.BlockSpec((tt, D_MODEL), row),
            pl.BlockSpec(mod.shape, fixed),
            pl.BlockSpec((1, D_MODEL), fixed),
            pl.BlockSpec(memory_space=pl.ANY),
        ],
        out_specs=pl.BlockSpec((tt, D_MODEL), row),
        scratch_shapes=[
            pltpu.VMEM((2, MOE_RLOC, D_MODEL), BF16),
            pltpu.VMEM((tt, MOE_RLOC), BF16),
            pltpu.SemaphoreType.DMA((2,)),
        ],
    )
    return pl.pallas_call(
        _combine_kernel,
        grid_spec=grid_spec,
        out_shape=jax.ShapeDtypeStruct((T, D_MODEL), F32),
        compiler_params=_params(("arbitrary",)),
        name="moe_combine",
    )(copies, ncopies, tok, x1, mod, fw, y)


def _rope_tables(T):
    rows = T // GRID_W
    n_freq = DK // 4
    inv = ROPE_BASE ** (-jnp.arange(n_freq, dtype=F32) / n_freq)
    row_ang = jnp.arange(rows, dtype=F32)[:, None] * inv
    col_ang = jnp.arange(GRID_W, dtype=F32)[:, None] * inv
    rot_row = jnp.pad(jnp.stack([jnp.cos(row_ang), jnp.sin(row_ang)]), ((0, 0), (0, 0), (0, n_freq)))
    rot_col = jnp.pad(jnp.stack([jnp.cos(col_ang), jnp.sin(col_ang)]), ((0, 0), (0, 0), (n_freq, 0)))
    return rot_row, rot_col


def kernel(x, c, ctx, c_ctx, w_mod, b_mod, norm1_w, norm2_w, w_in, sgu_ln_w, sgu_ln_b, sgu_w, sgu_b,
           ret_decay_fwd, ret_decay_bwd, w_proj_a, w_proj_b, w_out, router_w, router_b,
           moe_w1, moe_b1, moe_w2, moe_b2, final_norm_w):
    B, T, D = x.shape
    assert B == 1 and D == D_MODEL and w_mod.shape[0] == 1 and T % 1024 == 0
    x2d = x.reshape(T, D)
    cc = jnp.stack([c.reshape(D), c_ctx.reshape(D)], axis=1)
    mod = _mod_call(cc, w_mod[0], b_mod)
    dec = jnp.stack([ret_decay_fwd[0], ret_decay_bwd[0]], axis=0)
    w_in_bf = w_in[0].astype(BF16)
    s0 = _ctx_call(ctx.reshape(ctx.shape[1], D), norm1_w, mod, w_in_bf, dec)
    rot_row, rot_col = _rope_tables(T)
    proj = _inproj_call(x2d, norm1_w, mod, w_in_bf, rot_row, rot_col, sgu_ln_w, sgu_ln_b)
    za = _sgu_call(proj, sgu_w[0].astype(BF16), sgu_b[0].T, w_proj_a[0].astype(BF16))
    yf, yb = _ret_call(proj, s0, dec)
    x1, h2, logits_t = _merge_call(yf, yb, za, proj, x2d, mod, norm2_w, w_proj_b[0].astype(BF16),
                                   w_out[0].astype(BF16), router_w[0].T, router_b.reshape(N_EXPERTS, 1))
    pos_t, tok, nch = _route_call(logits_t)
    copies, ncopies, zstart, zcount, expert_tables = _routing_tables(nch[:, :, 0])
    n_slots = expert_tables[0].shape[0] * MOE_TM
    xs = _dispatch_call((copies, ncopies, zstart, zcount, expert_tables[2]), h2, pos_t, n_slots)
    y = _experts_call(expert_tables, xs, moe_w1[0], moe_b1[0][:, None, :], moe_w2[0], moe_b2[0][:, None, :])
    out = _combine_call(copies, ncopies, tok, x1, mod, final_norm_w.reshape(1, D), y)
    return out.reshape(B, T, D)
```
